```python
import math
import jax, jax.numpy as jnp
from jax import lax
import numpy as np

D_MODEL = 1024
BATCH = 32
SEQ = 2048
DEPTH = 2

N_A_LAYERS = DEPTH // 2
N_B_LAYERS = DEPTH - N_A_LAYERS
DN_ALPHA = (2 * DEPTH) ** 0.25
DN_BETA = (8 * DEPTH) ** -0.25
SSD_EXPAND = 2
SSD_D_INNER = SSD_EXPAND * D_MODEL
SSD_HEAD_DIM = 64
SSD_N_HEADS = SSD_D_INNER // SSD_HEAD_DIM
SSD_N_GROUPS = 8
SSD_HEADS_PER_GROUP = SSD_N_HEADS // SSD_N_GROUPS
SSD_D_STATE = 128
SSD_CONV = 4
SSD_CHUNK = 128
SSD_GN = SSD_N_GROUPS * SSD_D_STATE
SSD_CONV_DIM = SSD_D_INNER + 2 * SSD_GN
SSD_IN_DIM = 2 * SSD_D_INNER + 2 * SSD_GN + SSD_N_HEADS
SSD_DT_MIN = 0.001
SSD_DT_MAX = 0.1
MLA_N_HEADS = D_MODEL // 128
MLA_Q_RANK = 384
MLA_KV_RANK = 256
MLA_NOPE = 128
MLA_ROPE = 64
MLA_V = 128
ROPE_THETA = 10000.0
Q_BLOCK = 128
MAX_POS_OFFSET = 1024
FFN_HIDDEN = 2816
FFN_CONV = 3
LN_EPS = 1e-5
RMS_EPS = 1e-6

kernel_name = 'yoco_ssd_mla_convffn_deepnorm'


def layer_norm(x, g, b):
    xf = x.astype(jnp.float32)
    mu = jnp.mean(xf, axis=-1, keepdims=True)
    var = jnp.mean(jnp.square(xf - mu), axis=-1, keepdims=True)
    return ((xf - mu) * lax.rsqrt(var + LN_EPS) * g.astype(jnp.float32) + b.astype(jnp.float32)).astype(x.dtype)


def rms_norm(x, g, eps=RMS_EPS):
    xf = x.astype(jnp.float32)
    y = xf * lax.rsqrt(jnp.mean(xf * xf, axis=-1, keepdims=True) + eps)
    return (y * g.astype(jnp.float32)).astype(x.dtype)


def causal_depthwise_conv(x, w, b):
    width, ch = w.shape
    y = lax.conv_general_dilated(x, w[:, None, :].astype(x.dtype), window_strides=(1,),
                                 padding=[(width - 1, 0)],
                                 dimension_numbers=('NWC', 'WIO', 'NWC'),
                                 feature_group_count=ch)
    return y + b.astype(x.dtype)


def rope_tables(positions):
    inv_freq = 1.0 / (ROPE_THETA ** (jnp.arange(0, MLA_ROPE, 2, dtype=jnp.float32) / MLA_ROPE))
    ang = positions.astype(jnp.float32)[..., None] * inv_freq
    return jnp.cos(ang), jnp.sin(ang)


def apply_rope(x, cos, sin):
    xf = x.astype(jnp.float32)
    x1, x2 = jnp.split(xf, 2, axis=-1)
    return jnp.concatenate([x1 * cos - x2 * sin, x2 * cos + x1 * sin], axis=-1).astype(x.dtype)


def ssd_chunked_scan(xs, dt, A, Bm, Cm):
    b, s = xs.shape[:2]
    nc = s // SSD_CHUNK

    def chunks(a):
        return jnp.moveaxis(a.reshape((b, nc, SSD_CHUNK) + a.shape[2:]), 1, 0)

    causal = jnp.tril(jnp.ones((SSD_CHUNK, SSD_CHUNK), dtype=bool))[None, :, :, None, None]

    def step(state, inp):
        xc, dtc, bc, cc = inp
        cum = jnp.cumsum(dtc * A, axis=1)
        seg = cum[:, :, None] - cum[:, None, :]
        decay = jnp.exp(jnp.where(causal, seg, -jnp.inf))
        cb = jnp.einsum('btgn,bsgn->btsg', cc, bc)
        w = cb[..., None] * decay * dtc[:, None]
        y = jnp.einsum('btsgk,bsgkp->btgkp', w, xc)
        y = y + jnp.einsum('btgn,bgkpn->btgkp', cc, state) * jnp.exp(cum)[..., None]
        w_end = jnp.exp(cum[:, -1:] - cum) * dtc
        state = (state * jnp.exp(cum[:, -1])[..., None, None]
                 + jnp.einsum('bsgk,bsgkp,bsgn->bgkpn', w_end, xc, bc))
        return state, y

    init = jnp.zeros((b, SSD_N_GROUPS, SSD_HEADS_PER_GROUP, SSD_HEAD_DIM, SSD_D_STATE), jnp.float32)
    _, y = lax.scan(step, init, (chunks(xs), chunks(dt), chunks(Bm), chunks(Cm)))
    return jnp.moveaxis(y, 0, 1).reshape(xs.shape)


def ssd_mixer(x, in_proj, conv_w, conv_b, dt_bias, A_log, D, norm_g, out_proj):
    b, s, _ = x.shape
    G, K, P, N = SSD_N_GROUPS, SSD_HEADS_PER_GROUP, SSD_HEAD_DIM, SSD_D_STATE
    zxbcdt = x @ in_proj
    z = zxbcdt[..., :SSD_D_INNER]
    xbc = zxbcdt[..., SSD_D_INNER:SSD_D_INNER + SSD_CONV_DIM]
    dt = zxbcdt[..., SSD_D_INNER + SSD_CONV_DIM:]
    xbc = jax.nn.silu(causal_depthwise_conv(xbc, conv_w, conv_b))
    xs = xbc[..., :SSD_D_INNER].reshape(b, s, G, K, P).astype(jnp.float32)
    Bm = xbc[..., SSD_D_INNER:SSD_D_INNER + SSD_GN].reshape(b, s, G, N).astype(jnp.float32)
    Cm = xbc[..., SSD_D_INNER + SSD_GN:].reshape(b, s, G, N).astype(jnp.float32)
    dt = jax.nn.softplus(dt.astype(jnp.float32) + dt_bias.astype(jnp.float32)).reshape(b, s, G, K)
    A = -jnp.exp(A_log.astype(jnp.float32)).reshape(G, K)
    y = ssd_chunked_scan(xs, dt, A, Bm, Cm)
    y = y + D.astype(jnp.float32).reshape(G, K)[..., None] * xs
    y = y.reshape(b, s, SSD_D_INNER) * jax.nn.silu(z.astype(jnp.float32))
    yg = y.reshape(b, s, G, SSD_D_INNER // G)
    yg = yg * lax.rsqrt(jnp.mean(yg * yg, axis=-1, keepdims=True) + LN_EPS)
    y = (yg.reshape(b, s, SSD_D_INNER) * norm_g.astype(jnp.float32)).astype(x.dtype)
    return y @ out_proj


def mla_shared_kv(h, kv_down_proj, kv_norm_g, kv_up_k, kv_up_v, cos, sin):
    b, s, _ = h.shape
    ckv_kr = h @ kv_down_proj
    c_kv = rms_norm(ckv_kr[..., :MLA_KV_RANK], kv_norm_g)
    k_rope = apply_rope(ckv_kr[..., MLA_KV_RANK:], cos, sin)
    k_nope = (c_kv @ kv_up_k).reshape(b, s, MLA_N_HEADS, MLA_NOPE)
    v = (c_kv @ kv_up_v).reshape(b, s, MLA_N_HEADS, MLA_V)
    return k_nope, k_rope, v


def mla_attention(h, q_down, q_norm_g, q_up, out_proj, k_nope, k_rope, v, cos, sin):
    b, s, _ = h.shape
    c_q = rms_norm(h @ q_down, q_norm_g)
    q = (c_q @ q_up).reshape(b, s, MLA_N_HEADS, MLA_NOPE + MLA_ROPE)
    q_nope = q[..., :MLA_NOPE]
    q_rope = apply_rope(q[..., MLA_NOPE:], cos[:, :, None], sin[:, :, None])
    nb = s // Q_BLOCK
    scale = (MLA_NOPE + MLA_ROPE) ** -0.5
    key_idx = jnp.arange(s)

    def to_blocks(a):
        return jnp.moveaxis(a.reshape((b, nb, Q_BLOCK) + a.shape[2:]), 1, 0)

    def attend_block(args):
        qn, qr, blk = args
        scores = (jnp.einsum('bqhd,bkhd->bhqk', qn, k_nope)
                  + jnp.einsum('bqhr,bkr->bhqk', qr, k_rope)).astype(jnp.float32) * scale
        q_idx = blk * Q_BLOCK + jnp.arange(Q_BLOCK)
        scores = jnp.where(key_idx[None, :] <= q_idx[:, None], scores, -jnp.inf)
        p = jax.nn.softmax(scores, axis=-1).astype(v.dtype)
        return jnp.einsum('bhqk,bkhd->bqhd', p, v)

    o = lax.map(attend_block, (to_blocks(q_nope), to_blocks(q_rope), jnp.arange(nb)))
    o = jnp.moveaxis(o, 0, 1).reshape(b, s, MLA_N_HEADS * MLA_V)
    return o @ out_proj


def conv_ffn(h, up, conv_w, conv_b, down):
    u = causal_depthwise_conv(h @ up, conv_w, conv_b)
    g, val = jnp.split(u, 2, axis=-1)
    return (jax.nn.silu(g) * val) @ down


def _fwd_setup_inputs(seed: int = 0) -> dict:
    key = jax.random.key(seed)
    ks = jax.random.split(key, 32)
    f32 = jnp.float32

    def nrm(k, shape, scale):
        return jax.random.normal(k, shape, f32) * scale

    na, nbl, d = N_A_LAYERS, N_B_LAYERS, D_MODEL
    x = nrm(ks[0], (BATCH, SEQ, d), 1.0)
    offset = jax.random.randint(ks[1], (BATCH, 1), 0, MAX_POS_OFFSET, dtype=jnp.int32)
    positions = (offset + jnp.arange(SEQ, dtype=jnp.int32)[None, :]).astype(jnp.int32)

    u = jax.random.uniform(ks[2], (na, SSD_N_HEADS), f32)
    dt0 = jnp.exp(u * (math.log(SSD_DT_MAX) - math.log(SSD_DT_MIN)) + math.log(SSD_DT_MIN))
    dt0 = jnp.maximum(dt0, 1e-4)
    ssd_dt_bias = dt0 + jnp.log(-jnp.expm1(-dt0))
    ssd_A_log = jnp.log(jax.random.uniform(ks[3], (na, SSD_N_HEADS), f32, 1.0, 16.0))

    return {
        'x': x,
        'positions': positions,
        'ssd_in_proj': nrm(ks[4], (na, d, SSD_IN_DIM), d ** -0.5),
        'ssd_conv_w': nrm(ks[5], (na, SSD_CONV, SSD_CONV_DIM), SSD_CONV ** -0.5),
        'ssd_conv_b': nrm(ks[6], (na, SSD_CONV_DIM), 0.02),
        'ssd_dt_bias': ssd_dt_bias,
        'ssd_A_log': ssd_A_log,
        'ssd_D': 1.0 + nrm(ks[7], (na, SSD_N_HEADS), 0.1),
        'ssd_norm_g': 1.0 + nrm(ks[8], (na, SSD_D_INNER), 0.02),
        'ssd_out_proj': nrm(ks[9], (na, SSD_D_INNER, d), DN_BETA * SSD_D_INNER ** -0.5),
        'kv_down_proj': nrm(ks[10], (d, MLA_KV_RANK + MLA_ROPE), d ** -0.5),
        'kv_norm_g': 1.0 + nrm(ks[11], (MLA_KV_RANK,), 0.02),
        'kv_up_k': nrm(ks[12], (MLA_KV_RANK, MLA_N_HEADS * MLA_NOPE), MLA_KV_RANK ** -0.5),
        'kv_up_v': nrm(ks[13], (MLA_KV_RANK, MLA_N_HEADS * MLA_V), DN_BETA * MLA_KV_RANK ** -0.5),
        'q_down_proj': nrm(ks[14], (nbl, d, MLA_Q_RANK), d ** -0.5),
        'q_norm_g': 1.0 + nrm(ks[15], (nbl, MLA_Q_RANK), 0.02),
        'q_up_proj': nrm(ks[16], (nbl, MLA_Q_RANK, MLA_N_HEADS * (MLA_NOPE + MLA_ROPE)), MLA_Q_RANK ** -0.5),
        'attn_out_proj': nrm(ks[17], (nbl, MLA_N_HEADS * MLA_V, d), DN_BETA * (MLA_N_HEADS * MLA_V) ** -0.5),
        'ffn_up': nrm(ks[18], (DEPTH, d, 2 * FFN_HIDDEN), d ** -0.5),
        'ffn_conv_w': nrm(ks[19], (DEPTH, FFN_CONV, 2 * FFN_HIDDEN), FFN_CONV ** -0.5),
        'ffn_conv_b': nrm(ks[20], (DEPTH, 2 * FFN_HIDDEN), 0.02),
        'ffn_down': nrm(ks[21], (DEPTH, FFN_HIDDEN, d), DN_BETA * FFN_HIDDEN ** -0.5),
        'ln_mix_g': 1.0 + nrm(ks[22], (DEPTH, d), 0.02),
        'ln_mix_b': nrm(ks[23], (DEPTH, d), 0.02),
        'ln_ffn_g': 1.0 + nrm(ks[24], (DEPTH, d), 0.02),
        'ln_ffn_b': nrm(ks[25], (DEPTH, d), 0.02),
    }


def _fwd_reference(x, positions, ssd_in_proj, ssd_conv_w, ssd_conv_b, ssd_dt_bias, ssd_A_log, ssd_D,
              ssd_norm_g, ssd_out_proj, kv_down_proj, kv_norm_g, kv_up_k, kv_up_v, q_down_proj,
              q_norm_g, q_up_proj, attn_out_proj, ffn_up, ffn_conv_w, ffn_conv_b, ffn_down,
              ln_mix_g, ln_mix_b, ln_ffn_g, ln_ffn_b):
    cos, sin = rope_tables(positions)
    h = x
    shared_kv = None
    for i in range(DEPTH):
        if i < N_A_LAYERS:
            mix = ssd_mixer(h, ssd_in_proj[i], ssd_conv_w[i], ssd_conv_b[i], ssd_dt_bias[i],
                            ssd_A_log[i], ssd_D[i], ssd_norm_g[i], ssd_out_proj[i])
        else:
            j = i - N_A_LAYERS
            k_nope, k_rope, v = shared_kv
            mix = mla_attention(h, q_down_proj[j], q_norm_g[j], q_up_proj[j], attn_out_proj[j],
                                k_nope, k_rope, v, cos, sin)
        h = layer_norm(DN_ALPHA * h + mix, ln_mix_g[i], ln_mix_b[i])
        ff = conv_ffn(h, ffn_up[i], ffn_conv_w[i], ffn_conv_b[i], ffn_down[i])
        h = layer_norm(DN_ALPHA * h + ff, ln_ffn_g[i], ln_ffn_b[i])
        if i == N_A_LAYERS - 1:
            shared_kv = mla_shared_kv(h, kv_down_proj, kv_norm_g, kv_up_k, kv_up_v, cos, sin)
    return h


import jax as _jax
import jax.numpy as _jnp

TWIN_FORMAT = 'train_step'
FWD_PARAMS = ['x', 'positions', 'ssd_in_proj', 'ssd_conv_w', 'ssd_conv_b', 'ssd_dt_bias', 'ssd_A_log', 'ssd_D', 'ssd_norm_g', 'ssd_out_proj', 'kv_down_proj', 'kv_norm_g', 'kv_up_k', 'kv_up_v', 'q_down_proj', 'q_norm_g', 'q_up_proj', 'attn_out_proj', 'ffn_up', 'ffn_conv_w', 'ffn_conv_b', 'ffn_down', 'ln_mix_g', 'ln_mix_b', 'ln_ffn_g', 'ln_ffn_b']
TWIN_WEIGHTS = ['ssd_in_proj', 'ssd_conv_w', 'ssd_conv_b', 'ssd_dt_bias', 'ssd_A_log', 'ssd_D', 'ssd_norm_g', 'ssd_out_proj', 'kv_down_proj', 'kv_norm_g', 'kv_up_k', 'kv_up_v', 'q_down_proj', 'q_norm_g', 'q_up_proj', 'attn_out_proj', 'ffn_up', 'ffn_conv_w', 'ffn_conv_b', 'ffn_down', 'ln_mix_g', 'ln_mix_b', 'ln_ffn_g', 'ln_ffn_b']
TWIN_DIFF_INPUT = 'x'
TWIN_INPUTS = ['x', 'positions', 'ssd_in_proj', 'ssd_conv_w', 'ssd_conv_b', 'ssd_dt_bias', 'ssd_A_log', 'ssd_D', 'ssd_norm_g', 'ssd_out_proj', 'kv_down_proj', 'kv_norm_g', 'kv_up_k', 'kv_up_v', 'q_down_proj', 'q_norm_g', 'q_up_proj', 'attn_out_proj', 'ffn_up', 'ffn_conv_w', 'ffn_conv_b', 'ffn_down', 'ln_mix_g', 'ln_mix_b', 'ln_ffn_g', 'ln_ffn_b', 'loss_target', 'm_ssd_in_proj', 'm_ssd_conv_w', 'm_ssd_conv_b', 'm_ssd_dt_bias', 'm_ssd_A_log', 'm_ssd_D', 'm_ssd_norm_g', 'm_ssd_out_proj', 'm_kv_down_proj', 'm_kv_norm_g', 'm_kv_up_k', 'm_kv_up_v', 'm_q_down_proj', 'm_q_norm_g', 'm_q_up_proj', 'm_attn_out_proj', 'm_ffn_up', 'm_ffn_conv_w', 'm_ffn_conv_b', 'm_ffn_down', 'm_ln_mix_g', 'm_ln_mix_b', 'm_ln_ffn_g', 'm_ln_ffn_b', 'v_ssd_in_proj', 'v_ssd_conv_w', 'v_ssd_conv_b', 'v_ssd_dt_bias', 'v_ssd_A_log', 'v_ssd_D', 'v_ssd_norm_g', 'v_ssd_out_proj', 'v_kv_down_proj', 'v_kv_norm_g', 'v_kv_up_k', 'v_kv_up_v', 'v_q_down_proj', 'v_q_norm_g', 'v_q_up_proj', 'v_attn_out_proj', 'v_ffn_up', 'v_ffn_conv_w', 'v_ffn_conv_b', 'v_ffn_down', 'v_ln_mix_g', 'v_ln_mix_b', 'v_ln_ffn_g', 'v_ln_ffn_b']
TWIN_OUTPUTS = ['loss', 'grad_x', 'grad_ssd_in_proj', 'grad_ssd_conv_w', 'grad_ssd_conv_b', 'grad_ssd_dt_bias', 'grad_ssd_A_log', 'grad_ssd_D', 'grad_ssd_norm_g', 'grad_ssd_out_proj', 'grad_kv_down_proj', 'grad_kv_norm_g', 'grad_kv_up_k', 'grad_kv_up_v', 'grad_q_down_proj', 'grad_q_norm_g', 'grad_q_up_proj', 'grad_attn_out_proj', 'grad_ffn_up', 'grad_ffn_conv_w', 'grad_ffn_conv_b', 'grad_ffn_down', 'grad_ln_mix_g', 'grad_ln_mix_b', 'grad_ln_ffn_g', 'grad_ln_ffn_b', 'delta_ssd_in_proj', 'delta_ssd_conv_w', 'delta_ssd_conv_b', 'delta_ssd_dt_bias', 'delta_ssd_A_log', 'delta_ssd_D', 'delta_ssd_norm_g', 'delta_ssd_out_proj', 'delta_kv_down_proj', 'delta_kv_norm_g', 'delta_kv_up_k', 'delta_kv_up_v', 'delta_q_down_proj', 'delta_q_norm_g', 'delta_q_up_proj', 'delta_attn_out_proj', 'delta_ffn_up', 'delta_ffn_conv_w', 'delta_ffn_conv_b', 'delta_ffn_down', 'delta_ln_mix_g', 'delta_ln_mix_b', 'delta_ln_ffn_g', 'delta_ln_ffn_b', 'new_m_ssd_in_proj', 'new_m_ssd_conv_w', 'new_m_ssd_conv_b', 'new_m_ssd_dt_bias', 'new_m_ssd_A_log', 'new_m_ssd_D', 'new_m_ssd_norm_g', 'new_m_ssd_out_proj', 'new_m_kv_down_proj', 'new_m_kv_norm_g', 'new_m_kv_up_k', 'new_m_kv_up_v', 'new_m_q_down_proj', 'new_m_q_norm_g', 'new_m_q_up_proj', 'new_m_attn_out_proj', 'new_m_ffn_up', 'new_m_ffn_conv_w', 'new_m_ffn_conv_b', 'new_m_ffn_down', 'new_m_ln_mix_g', 'new_m_ln_mix_b', 'new_m_ln_ffn_g', 'new_m_ln_ffn_b', 'new_v_ssd_in_proj', 'new_v_ssd_conv_w', 'new_v_ssd_conv_b', 'new_v_ssd_dt_bias', 'new_v_ssd_A_log', 'new_v_ssd_D', 'new_v_ssd_norm_g', 'new_v_ssd_out_proj', 'new_v_kv_down_proj', 'new_v_kv_norm_g', 'new_v_kv_up_k', 'new_v_kv_up_v', 'new_v_q_down_proj', 'new_v_q_norm_g', 'new_v_q_up_proj', 'new_v_attn_out_proj', 'new_v_ffn_up', 'new_v_ffn_conv_w', 'new_v_ffn_conv_b', 'new_v_ffn_down', 'new_v_ln_mix_g', 'new_v_ln_mix_b', 'new_v_ln_ffn_g', 'new_v_ln_ffn_b']
TWIN_LEAF_KINDS = {'loss': 'loss', 'grad_x': 'grad_x', 'grad_ssd_in_proj': 'grad_w', 'grad_ssd_conv_w': 'grad_w', 'grad_ssd_conv_b': 'grad_w', 'grad_ssd_dt_bias': 'grad_w', 'grad_ssd_A_log': 'grad_w', 'grad_ssd_D': 'grad_w', 'grad_ssd_norm_g': 'grad_w', 'grad_ssd_out_proj': 'grad_w', 'grad_kv_down_proj': 'grad_w', 'grad_kv_norm_g': 'grad_w', 'grad_kv_up_k': 'grad_w', 'grad_kv_up_v': 'grad_w', 'grad_q_down_proj': 'grad_w', 'grad_q_norm_g': 'grad_w', 'grad_q_up_proj': 'grad_w', 'grad_attn_out_proj': 'grad_w', 'grad_ffn_up': 'grad_w', 'grad_ffn_conv_w': 'grad_w', 'grad_ffn_conv_b': 'grad_w', 'grad_ffn_down': 'grad_w', 'grad_ln_mix_g': 'grad_w', 'grad_ln_mix_b': 'grad_w', 'grad_ln_ffn_g': 'grad_w', 'grad_ln_ffn_b': 'grad_w', 'delta_ssd_in_proj': 'delta_w', 'delta_ssd_conv_w': 'delta_w', 'delta_ssd_conv_b': 'delta_w', 'delta_ssd_dt_bias': 'delta_w', 'delta_ssd_A_log': 'delta_w', 'delta_ssd_D': 'delta_w', 'delta_ssd_norm_g': 'delta_w', 'delta_ssd_out_proj': 'delta_w', 'delta_kv_down_proj': 'delta_w', 'delta_kv_norm_g': 'delta_w', 'delta_kv_up_k': 'delta_w', 'delta_kv_up_v': 'delta_w', 'delta_q_down_proj': 'delta_w', 'delta_q_norm_g': 'delta_w', 'delta_q_up_proj': 'delta_w', 'delta_attn_out_proj': 'delta_w', 'delta_ffn_up': 'delta_w', 'delta_ffn_conv_w': 'delta_w', 'delta_ffn_conv_b': 'delta_w', 'delta_ffn_down': 'delta_w', 'delta_ln_mix_g': 'delta_w', 'delta_ln_mix_b': 'delta_w', 'delta_ln_ffn_g': 'delta_w', 'delta_ln_ffn_b': 'delta_w', 'new_m_ssd_in_proj': 'new_m', 'new_m_ssd_conv_w': 'new_m', 'new_m_ssd_conv_b': 'new_m', 'new_m_ssd_dt_bias': 'new_m', 'new_m_ssd_A_log': 'new_m', 'new_m_ssd_D': 'new_m', 'new_m_ssd_norm_g': 'new_m', 'new_m_ssd_out_proj': 'new_m', 'new_m_kv_down_proj': 'new_m', 'new_m_kv_norm_g': 'new_m', 'new_m_kv_up_k': 'new_m', 'new_m_kv_up_v': 'new_m', 'new_m_q_down_proj': 'new_m', 'new_m_q_norm_g': 'new_m', 'new_m_q_up_proj': 'new_m', 'new_m_attn_out_proj': 'new_m', 'new_m_ffn_up': 'new_m', 'new_m_ffn_conv_w': 'new_m', 'new_m_ffn_conv_b': 'new_m', 'new_m_ffn_down': 'new_m', 'new_m_ln_mix_g': 'new_m', 'new_m_ln_mix_b': 'new_m', 'new_m_ln_ffn_g': 'new_m', 'new_m_ln_ffn_b': 'new_m', 'new_v_ssd_in_proj': 'new_v', 'new_v_ssd_conv_w': 'new_v', 'new_v_ssd_conv_b': 'new_v', 'new_v_ssd_dt_bias': 'new_v', 'new_v_ssd_A_log': 'new_v', 'new_v_ssd_D': 'new_v', 'new_v_ssd_norm_g': 'new_v', 'new_v_ssd_out_proj': 'new_v', 'new_v_kv_down_proj': 'new_v', 'new_v_kv_norm_g': 'new_v', 'new_v_kv_up_k': 'new_v', 'new_v_kv_up_v': 'new_v', 'new_v_q_down_proj': 'new_v', 'new_v_q_norm_g': 'new_v', 'new_v_q_up_proj': 'new_v', 'new_v_attn_out_proj': 'new_v', 'new_v_ffn_up': 'new_v', 'new_v_ffn_conv_w': 'new_v', 'new_v_ffn_conv_b': 'new_v', 'new_v_ffn_down': 'new_v', 'new_v_ln_mix_g': 'new_v', 'new_v_ln_mix_b': 'new_v', 'new_v_ln_ffn_g': 'new_v', 'new_v_ln_ffn_b': 'new_v'}


def _forward(args):
    return _fwd_reference(*[args[k] for k in FWD_PARAMS])


def _output_shape():
    out = _jax.eval_shape(lambda: _forward(_fwd_setup_inputs(0)))
    return out.shape, out.dtype

N_MICROBATCH = 1
ADAM_LR = 0.001
ADAM_B1 = 0.9
ADAM_B2 = 0.999
ADAM_EPS = 1e-08
ADAM_WD = 0.01
ADAM_STEP = 10
PER_EXAMPLE_BATCH_AXIS = {'x': 0, 'positions': 0, 'loss_target': 0}
SHARED_INPUTS = []
_WEIGHT_DTYPES = {'ssd_in_proj': _jnp.float32, 'ssd_conv_w': _jnp.float32, 'ssd_conv_b': _jnp.float32, 'ssd_dt_bias': _jnp.float32, 'ssd_A_log': _jnp.float32, 'ssd_D': _jnp.float32, 'ssd_norm_g': _jnp.float32, 'ssd_out_proj': _jnp.float32, 'kv_down_proj': _jnp.float32, 'kv_norm_g': _jnp.float32, 'kv_up_k': _jnp.float32, 'kv_up_v': _jnp.float32, 'q_down_proj': _jnp.float32, 'q_norm_g': _jnp.float32, 'q_up_proj': _jnp.float32, 'attn_out_proj': _jnp.float32, 'ffn_up': _jnp.float32, 'ffn_conv_w': _jnp.float32, 'ffn_conv_b': _jnp.float32, 'ffn_down': _jnp.float32, 'ln_mix_g': _jnp.float32, 'ln_mix_b': _jnp.float32, 'ln_ffn_g': _jnp.float32, 'ln_ffn_b': _jnp.float32}
MOMENT_SCALE = {'ssd_in_proj': 5.369564e-02, 'ssd_conv_w': 4.623863e-02, 'ssd_conv_b': 8.391476e-02, 'ssd_dt_bias': 1.469999e-01, 'ssd_A_log': 4.766506e-01, 'ssd_D': 3.515323e-01, 'ssd_norm_g': 6.610819e-02, 'ssd_out_proj': 1.899146e-01, 'kv_down_proj': 2.395627e-02, 'kv_norm_g': 2.482299e-02, 'kv_up_k': 6.852163e-03, 'kv_up_v': 2.151082e-02, 'q_down_proj': 1.346525e-02, 'q_norm_g': 1.330270e-02, 'q_up_proj': 6.729673e-03, 'attn_out_proj': 2.144239e-02, 'ffn_up': 3.283407e-02, 'ffn_conv_w': 3.274703e-02, 'ffn_conv_b': 3.757207e-02, 'ffn_down': 1.080613e-01, 'ln_mix_g': 1.787063e+00, 'ln_mix_b': 7.983581e-01, 'ln_ffn_g': 4.530477e+01, 'ln_ffn_b': 3.065537e+00}


def _to_microbatches(a, axis):
    t = _jnp.moveaxis(a, axis, 0)
    t = t.reshape((N_MICROBATCH, t.shape[0] // N_MICROBATCH) + t.shape[1:])
    return _jnp.moveaxis(t, 1, axis + 1)


def setup_inputs(seed: int = 0) -> dict:
    inp = _fwd_setup_inputs(seed)
    key = _jax.random.fold_in(_jax.random.key(seed), 7919)
    shape, _ = _output_shape()
    out = dict(inp)
    out["loss_target"] = _jax.random.normal(_jax.random.fold_in(key, 0), shape, _jnp.float32)
    for i, name in enumerate(TWIN_WEIGHTS):
        w = inp[name].astype(_jnp.float32)
        if MOMENT_SCALE is None:
            s = _jnp.sqrt(_jnp.mean(_jnp.square(w)) + 1e-30)
        else:
            s = MOMENT_SCALE[name]
        km, kv = _jax.random.split(_jax.random.fold_in(key, i + 1))
        out[name] = w
        out["m_" + name] = s * _jax.random.normal(km, w.shape, _jnp.float32)
        out["v_" + name] = (s * s) * _jax.random.uniform(kv, w.shape, _jnp.float32, 0.5, 1.5)
    if N_MICROBATCH > 1:
        for name, axis in PER_EXAMPLE_BATCH_AXIS.items():
            out[name] = _to_microbatches(out[name], axis)
    return {'x': out['x'], 'positions': out['positions'], 'ssd_in_proj': out['ssd_in_proj'], 'ssd_conv_w': out['ssd_conv_w'], 'ssd_conv_b': out['ssd_conv_b'], 'ssd_dt_bias': out['ssd_dt_bias'], 'ssd_A_log': out['ssd_A_log'], 'ssd_D': out['ssd_D'], 'ssd_norm_g': out['ssd_norm_g'], 'ssd_out_proj': out['ssd_out_proj'], 'kv_down_proj': out['kv_down_proj'], 'kv_norm_g': out['kv_norm_g'], 'kv_up_k': out['kv_up_k'], 'kv_up_v': out['kv_up_v'], 'q_down_proj': out['q_down_proj'], 'q_norm_g': out['q_norm_g'], 'q_up_proj': out['q_up_proj'], 'attn_out_proj': out['attn_out_proj'], 'ffn_up': out['ffn_up'], 'ffn_conv_w': out['ffn_conv_w'], 'ffn_conv_b': out['ffn_conv_b'], 'ffn_down': out['ffn_down'], 'ln_mix_g': out['ln_mix_g'], 'ln_mix_b': out['ln_mix_b'], 'ln_ffn_g': out['ln_ffn_g'], 'ln_ffn_b': out['ln_ffn_b'], 'loss_target': out['loss_target'], 'm_ssd_in_proj': out['m_ssd_in_proj'], 'm_ssd_conv_w': out['m_ssd_conv_w'], 'm_ssd_conv_b': out['m_ssd_conv_b'], 'm_ssd_dt_bias': out['m_ssd_dt_bias'], 'm_ssd_A_log': out['m_ssd_A_log'], 'm_ssd_D': out['m_ssd_D'], 'm_ssd_norm_g': out['m_ssd_norm_g'], 'm_ssd_out_proj': out['m_ssd_out_proj'], 'm_kv_down_proj': out['m_kv_down_proj'], 'm_kv_norm_g': out['m_kv_norm_g'], 'm_kv_up_k': out['m_kv_up_k'], 'm_kv_up_v': out['m_kv_up_v'], 'm_q_down_proj': out['m_q_down_proj'], 'm_q_norm_g': out['m_q_norm_g'], 'm_q_up_proj': out['m_q_up_proj'], 'm_attn_out_proj': out['m_attn_out_proj'], 'm_ffn_up': out['m_ffn_up'], 'm_ffn_conv_w': out['m_ffn_conv_w'], 'm_ffn_conv_b': out['m_ffn_conv_b'], 'm_ffn_down': out['m_ffn_down'], 'm_ln_mix_g': out['m_ln_mix_g'], 'm_ln_mix_b': out['m_ln_mix_b'], 'm_ln_ffn_g': out['m_ln_ffn_g'], 'm_ln_ffn_b': out['m_ln_ffn_b'], 'v_ssd_in_proj': out['v_ssd_in_proj'], 'v_ssd_conv_w': out['v_ssd_conv_w'], 'v_ssd_conv_b': out['v_ssd_conv_b'], 'v_ssd_dt_bias': out['v_ssd_dt_bias'], 'v_ssd_A_log': out['v_ssd_A_log'], 'v_ssd_D': out['v_ssd_D'], 'v_ssd_norm_g': out['v_ssd_norm_g'], 'v_ssd_out_proj': out['v_ssd_out_proj'], 'v_kv_down_proj': out['v_kv_down_proj'], 'v_kv_norm_g': out['v_kv_norm_g'], 'v_kv_up_k': out['v_kv_up_k'], 'v_kv_up_v': out['v_kv_up_v'], 'v_q_down_proj': out['v_q_down_proj'], 'v_q_norm_g': out['v_q_norm_g'], 'v_q_up_proj': out['v_q_up_proj'], 'v_attn_out_proj': out['v_attn_out_proj'], 'v_ffn_up': out['v_ffn_up'], 'v_ffn_conv_w': out['v_ffn_conv_w'], 'v_ffn_conv_b': out['v_ffn_conv_b'], 'v_ffn_down': out['v_ffn_down'], 'v_ln_mix_g': out['v_ln_mix_g'], 'v_ln_mix_b': out['v_ln_mix_b'], 'v_ln_ffn_g': out['v_ln_ffn_g'], 'v_ln_ffn_b': out['v_ln_ffn_b']}


def _loss(weights, diff, rest, loss_target):
    with _jax.named_scope("forward"):
        args = {**rest, TWIN_DIFF_INPUT: diff, **{k: w.astype(_WEIGHT_DTYPES[k]) for k, w in weights.items()}}
        y = _forward(args)
    with _jax.named_scope("loss_head"):
        err = _jnp.square(y.astype(_jnp.float32) - loss_target)
        return 0.5 * _jnp.sum(_jnp.mean(err, axis=-1)) if err.ndim else 0.5 * err


def _adamw(w, g, m, v):
    m = ADAM_B1 * m + (1.0 - ADAM_B1) * g
    v = ADAM_B2 * v + (1.0 - ADAM_B2) * _jnp.square(g)
    m_hat = m / (1.0 - ADAM_B1 ** ADAM_STEP)
    v_hat = v / (1.0 - ADAM_B2 ** ADAM_STEP)
    delta = -ADAM_LR * (m_hat / (_jnp.sqrt(v_hat) + ADAM_EPS) + ADAM_WD * w)
    return delta, m, v


def reference(x, positions, ssd_in_proj, ssd_conv_w, ssd_conv_b, ssd_dt_bias, ssd_A_log, ssd_D, ssd_norm_g, ssd_out_proj, kv_down_proj, kv_norm_g, kv_up_k, kv_up_v, q_down_proj, q_norm_g, q_up_proj, attn_out_proj, ffn_up, ffn_conv_w, ffn_conv_b, ffn_down, ln_mix_g, ln_mix_b, ln_ffn_g, ln_ffn_b, loss_target, m_ssd_in_proj, m_ssd_conv_w, m_ssd_conv_b, m_ssd_dt_bias, m_ssd_A_log, m_ssd_D, m_ssd_norm_g, m_ssd_out_proj, m_kv_down_proj, m_kv_norm_g, m_kv_up_k, m_kv_up_v, m_q_down_proj, m_q_norm_g, m_q_up_proj, m_attn_out_proj, m_ffn_up, m_ffn_conv_w, m_ffn_conv_b, m_ffn_down, m_ln_mix_g, m_ln_mix_b, m_ln_ffn_g, m_ln_ffn_b, v_ssd_in_proj, v_ssd_conv_w, v_ssd_conv_b, v_ssd_dt_bias, v_ssd_A_log, v_ssd_D, v_ssd_norm_g, v_ssd_out_proj, v_kv_down_proj, v_kv_norm_g, v_kv_up_k, v_kv_up_v, v_q_down_proj, v_q_norm_g, v_q_up_proj, v_attn_out_proj, v_ffn_up, v_ffn_conv_w, v_ffn_conv_b, v_ffn_down, v_ln_mix_g, v_ln_mix_b, v_ln_ffn_g, v_ln_ffn_b):
    given = dict(x=x, positions=positions, ssd_in_proj=ssd_in_proj, ssd_conv_w=ssd_conv_w, ssd_conv_b=ssd_conv_b, ssd_dt_bias=ssd_dt_bias, ssd_A_log=ssd_A_log, ssd_D=ssd_D, ssd_norm_g=ssd_norm_g, ssd_out_proj=ssd_out_proj, kv_down_proj=kv_down_proj, kv_norm_g=kv_norm_g, kv_up_k=kv_up_k, kv_up_v=kv_up_v, q_down_proj=q_down_proj, q_norm_g=q_norm_g, q_up_proj=q_up_proj, attn_out_proj=attn_out_proj, ffn_up=ffn_up, ffn_conv_w=ffn_conv_w, ffn_conv_b=ffn_conv_b, ffn_down=ffn_down, ln_mix_g=ln_mix_g, ln_mix_b=ln_mix_b, ln_ffn_g=ln_ffn_g, ln_ffn_b=ln_ffn_b, loss_target=loss_target, m_ssd_in_proj=m_ssd_in_proj, m_ssd_conv_w=m_ssd_conv_w, m_ssd_conv_b=m_ssd_conv_b, m_ssd_dt_bias=m_ssd_dt_bias, m_ssd_A_log=m_ssd_A_log, m_ssd_D=m_ssd_D, m_ssd_norm_g=m_ssd_norm_g, m_ssd_out_proj=m_ssd_out_proj, m_kv_down_proj=m_kv_down_proj, m_kv_norm_g=m_kv_norm_g, m_kv_up_k=m_kv_up_k, m_kv_up_v=m_kv_up_v, m_q_down_proj=m_q_down_proj, m_q_norm_g=m_q_norm_g, m_q_up_proj=m_q_up_proj, m_attn_out_proj=m_attn_out_proj, m_ffn_up=m_ffn_up, m_ffn_conv_w=m_ffn_conv_w, m_ffn_conv_b=m_ffn_conv_b, m_ffn_down=m_ffn_down, m_ln_mix_g=m_ln_mix_g, m_ln_mix_b=m_ln_mix_b, m_ln_ffn_g=m_ln_ffn_g, m_ln_ffn_b=m_ln_ffn_b, v_ssd_in_proj=v_ssd_in_proj, v_ssd_conv_w=v_ssd_conv_w, v_ssd_conv_b=v_ssd_conv_b, v_ssd_dt_bias=v_ssd_dt_bias, v_ssd_A_log=v_ssd_A_log, v_ssd_D=v_ssd_D, v_ssd_norm_g=v_ssd_norm_g, v_ssd_out_proj=v_ssd_out_proj, v_kv_down_proj=v_kv_down_proj, v_kv_norm_g=v_kv_norm_g, v_kv_up_k=v_kv_up_k, v_kv_up_v=v_kv_up_v, v_q_down_proj=v_q_down_proj, v_q_norm_g=v_q_norm_g, v_q_up_proj=v_q_up_proj, v_attn_out_proj=v_attn_out_proj, v_ffn_up=v_ffn_up, v_ffn_conv_w=v_ffn_conv_w, v_ffn_conv_b=v_ffn_conv_b, v_ffn_down=v_ffn_down, v_ln_mix_g=v_ln_mix_g, v_ln_mix_b=v_ln_mix_b, v_ln_ffn_g=v_ln_ffn_g, v_ln_ffn_b=v_ln_ffn_b)
    weights = {n: given[n] for n in TWIN_WEIGHTS}
    shared = {n: given[n] for n in SHARED_INPUTS}
    per_example = {n: given[n] for n in ['x', 'positions']}
    grad_fn = _jax.value_and_grad(_loss, argnums=(0, 1))

    def one_microbatch(ex, loss_target):
        ex = dict(ex)
        diff = ex.pop(TWIN_DIFF_INPUT)
        return grad_fn(weights, diff, {**shared, **ex}, loss_target)

    if N_MICROBATCH == 1:
        loss, (grad_w, grad_x) = one_microbatch(per_example, given["loss_target"])
    else:
        def body(carry, xs):
            loss_sum, grad_sum = carry
            l_k, (gw_k, gx_k) = one_microbatch(xs[0], xs[1])
            with _jax.named_scope("update"):
                return (loss_sum + l_k, _jax.tree.map(_jnp.add, grad_sum, gw_k)), gx_k

        init = (_jnp.zeros((), _jnp.float32), _jax.tree.map(_jnp.zeros_like, weights))
        (loss, grad_w), grad_x = _jax.lax.scan(body, init, (per_example, given["loss_target"]))
    with _jax.named_scope("update"):
        delta_w, new_m, new_v = {}, {}, {}
        for n in TWIN_WEIGHTS:
            delta_w[n], new_m[n], new_v[n] = _adamw(weights[n], grad_w[n], given["m_" + n], given["v_" + n])
    return (loss, grad_x, *[grad_w[n] for n in TWIN_WEIGHTS], *[delta_w[n] for n in TWIN_WEIGHTS],
            *[new_m[n] for n in TWIN_WEIGHTS], *[new_v[n] for n in TWIN_WEIGHTS])
```

```python
import functools
import math

import jax
import jax.numpy as jnp
from jax import lax
from jax.experimental import pallas as pl
from jax.experimental.pallas import tpu as pltpu

F32 = jnp.float32
BF16 = jnp.bfloat16
HIGHEST = lax.Precision.HIGHEST
MESH = pl.DeviceIdType.MESH

D_MODEL = 1024
DEPTH = 2
DN_ALPHA = (2 * DEPTH) ** 0.25
SSD_D_INNER = 2048
SSD_P = 64
SSD_H = 32
SSD_G = 8
SSD_K = 4
SSD_N = 128
SSD_L = 128
SSD_GN = SSD_G * SSD_N
MLA_H = 8
MLA_Q_RANK = 384
MLA_KV_RANK = 256
MLA_NOPE = 128
MLA_ROPE = 64
MLA_V = 128
ROPE_THETA = 10000.0
FFN_HIDDEN = 2816
LN_EPS = 1e-5
RMS_EPS = 1e-6
ADAM_LR = 0.001
ADAM_B1 = 0.9
ADAM_B2 = 0.999
ADAM_EPS = 1e-08
ADAM_WD = 0.01
ADAM_STEP = 10

N_CHIPS = 4
N_DEV = 8
LANE = 128
VMEM_LIMIT = 56 * 1024 * 1024


def _cparams(**kw):
    return pltpu.CompilerParams(vmem_limit_bytes=VMEM_LIMIT, **kw)


def _tile(dim, cap):
    best = None
    t = LANE
    while t <= min(dim, cap):
        if dim % t == 0:
            best = t
        t += LANE
    return dim if best is None else best


def _mm(a, b, *, ta=False, tb=False, add=None, name):
    m, k = (a.shape[1], a.shape[0]) if ta else a.shape
    n = b.shape[0] if tb else b.shape[1]
    assert (b.shape[1] if tb else b.shape[0]) == k
    tm, tn, tk = _tile(m, 512), _tile(n, 512), _tile(k, 1024)
    nk = k // tk
    dims = (((0 if ta else 1,), (1 if tb else 0,)), ((), ()))

    def body(*refs):
        if add is None:
            a_ref, b_ref, o_ref, acc = refs
        else:
            a_ref, b_ref, c_ref, o_ref, acc = refs
        kk = pl.program_id(2)

        @pl.when(kk == 0)
        def _():
            acc[...] = jnp.zeros_like(acc) if add is None else c_ref[...]

        acc[...] += lax.dot_general(a_ref[...].astype(BF16), b_ref[...].astype(BF16), dims,
                                    preferred_element_type=F32)

        @pl.when(kk == nk - 1)
        def _():
            o_ref[...] = acc[...]

    a_spec = pl.BlockSpec((tk, tm), lambda i, j, kk: (kk, i)) if ta else pl.BlockSpec((tm, tk), lambda i, j, kk: (i, kk))
    b_spec = pl.BlockSpec((tn, tk), lambda i, j, kk: (j, kk)) if tb else pl.BlockSpec((tk, tn), lambda i, j, kk: (kk, j))
    o_spec = pl.BlockSpec((tm, tn), lambda i, j, kk: (i, j))
    ins, specs = [a, b], [a_spec, b_spec]
    if add is not None:
        ins.append(add)
        specs.append(o_spec)
    return pl.pallas_call(
        body, name=name, grid=(m // tm, n // tn, nk), in_specs=specs, out_specs=o_spec,
        out_shape=jax.ShapeDtypeStruct((m, n), F32), scratch_shapes=[pltpu.VMEM((tm, tn), F32)],
        compiler_params=_cparams(dimension_semantics=("parallel", "parallel", "arbitrary")),
    )(*ins)


def _spec(op):
    return pl.BlockSpec(op[1], op[2])


def _bw_fwd(name, fn, grid, ins, outs):
    n_in = len(ins)

    def body(*refs):
        res = fn(*[r[...] for r in refs[:n_in]])
        for r, v in zip(refs[n_in:], res):
            r[...] = v

    return pl.pallas_call(
        body, name=name, grid=grid, in_specs=[_spec(o) for o in ins],
        out_specs=[pl.BlockSpec(o[1], o[2]) for o in outs],
        out_shape=[jax.ShapeDtypeStruct(o[0], F32) for o in outs], compiler_params=_cparams(),
    )(*[o[0] for o in ins])


def _bw_bwd(name, fn, grid, data, params, consts, cts, red_axes):
    nd, npar, nc, nct = len(data), len(params), len(consts), len(cts)

    def body(*refs):
        first = None
        for ax in red_axes:
            z = pl.program_id(ax) == 0
            first = z if first is None else jnp.logical_and(first, z)
        vals = [r[...] for r in refs[:nd + npar + nc + nct]]
        d, p, c, g = vals[:nd], vals[nd:nd + npar], vals[nd + npar:nd + npar + nc], vals[nd + npar + nc:]
        _, vjp = jax.vjp(lambda dd, pp: tuple(fn(*dd, *pp, *c)), d, p)
        gd, gp = vjp(tuple(g))
        orefs = refs[nd + npar + nc + nct:]
        for r, v in zip(orefs[:nd], gd):
            r[...] = v
        if npar:
            @pl.when(first)
            def _():
                for r in orefs[nd:]:
                    r[...] = jnp.zeros_like(r)

            for r, v in zip(orefs[nd:], gp):
                r[...] += v

    ins = list(data) + list(params) + list(consts) + list(cts)
    outs = list(data) + list(params)
    return pl.pallas_call(
        body, name=name, grid=grid, in_specs=[_spec(o) for o in ins], out_specs=[_spec(o) for o in outs],
        out_shape=[jax.ShapeDtypeStruct(o[0].shape, F32) for o in outs], compiler_params=_cparams(),
    )(*[o[0] for o in ins])


def _shift_down(x, s):
    row = lax.broadcasted_iota(jnp.int32, x.shape, 0)
    return jnp.where(row < s, 0.0, pltpu.roll(x, s, 0))


def _shift_up(x, s):
    n = x.shape[0]
    row = lax.broadcasted_iota(jnp.int32, x.shape, 0)
    return jnp.where(row >= n - s, 0.0, pltpu.roll(x, n - s, 0))


def _time_shift(s):
    if s == 0:
        return lambda x: x

    @jax.custom_vjp
    def shift(x):
        return _shift_down(x, s)

    shift.defvjp(lambda x: (_shift_down(x, s), None), lambda _, g: (_shift_up(g, s),))
    return shift


def _rot_half_raw(x):
    lane = lax.broadcasted_iota(jnp.int32, x.shape, 1)
    return jnp.where(lane % MLA_ROPE < MLA_ROPE // 2, -pltpu.roll(x, LANE - MLA_ROPE // 2, 1), pltpu.roll(x, MLA_ROPE // 2, 1))


@jax.custom_vjp
def _rot_half(x):
    return _rot_half_raw(x)


_rot_half.defvjp(lambda x: (_rot_half_raw(x), None), lambda _, g: (-_rot_half_raw(g),))


@jax.custom_vjp
def _roll_half_lanes(x):
    return pltpu.roll(x, LANE // 2, 1)


_roll_half_lanes.defvjp(lambda x: (pltpu.roll(x, LANE // 2, 1), None), lambda _, g: (pltpu.roll(g, LANE // 2, 1),))


def _causal_conv(u, w, b):
    width = w.shape[0]
    y = b
    for k in range(width):
        y = y + w[k:k + 1, :] * _time_shift(width - 1 - k)(u)
    return y


def _silu(x):
    return x * jax.nn.sigmoid(x)


def _f_ln(h, mix, g, b):
    x = DN_ALPHA * h + mix
    mu = jnp.mean(x, axis=-1, keepdims=True)
    xc = x - mu
    var = jnp.mean(xc * xc, axis=-1, keepdims=True)
    return (xc * lax.rsqrt(var + LN_EPS) * g + b,)


def _f_convsilu(u, w, b):
    return (_silu(_causal_conv(u, w, b)),)


def _f_convffn(ug, uv, wg, wv, bg, bv):
    return (_silu(_causal_conv(ug, wg, bg)) * _causal_conv(uv, wv, bv),)


def _f_dt(dt_raw, bias, a_log):
    x = dt_raw + bias
    dt = jnp.maximum(x, 0.0) + jnp.log(1.0 + jnp.exp(-jnp.abs(x)))
    return dt, dt * (-jnp.exp(a_log))


def _f_ssd_post(y, xs, z, d_exp, ng):
    t = (y + d_exp * xs) * _silu(z)
    return (t * lax.rsqrt(jnp.mean(t * t, axis=-1, keepdims=True) + LN_EPS) * ng,)


def _f_rms(x, g):
    return (x * lax.rsqrt(jnp.mean(x * x, axis=-1, keepdims=True) + RMS_EPS) * g,)


def _f_rope(x, cos, sin):
    return (x * cos + _rot_half(x) * sin,)


def _f_rope_dup(x, cos, sin):
    r = x * cos + _rot_half(x) * sin
    return (r + _roll_half_lanes(r),)


def _rows(t, cap=512):
    for c in (cap, 256, 128, 64, 32, 16, 8):
        if c <= cap and t % c == 0:
            return c
    return t


class _Blockwise:
    def __init__(self, name, fn, grid, data, params, consts, outs, red_axes):
        self.name, self.fn, self.grid = name, fn, grid
        self.data, self.params, self.consts, self.outs, self.red_axes = data, params, consts, outs, red_axes

    def fwd(self):
        return _bw_fwd(self.name + "_fwd", self.fn, self.grid, self.data + self.params + self.consts, self.outs)

    def bwd(self, cts):
        cts = [(c, o[1], o[2]) for c, o in zip(cts, self.outs)]
        res = _bw_bwd(self.name + "_bwd", self.fn, self.grid, self.data, self.params, self.consts, cts, self.red_axes)
        return res[:len(self.data)], res[len(self.data):]


def _op_ln(name, h, mix, g, b):
    t, d = h.shape
    tr = _rows(t, 256)
    row = ((tr, d), lambda i: (i, 0))
    par = ((1, d), lambda i: (0, 0))
    return _Blockwise(name, _f_ln, (t // tr,), [(h, *row), (mix, *row)], [(g, *par), (b, *par)], [],
                      [((t, d), *row)], (0,))


def _op_conv(name, fn, us, ws, bs, nb, seq):
    c = us[0].shape[1]
    ct = _tile(c, 256)
    blk = ((seq, ct), lambda j, bb: (bb, j))
    data = [(u, *blk) for u in us]
    params = [(w, (w.shape[0], ct), lambda j, bb: (0, j)) for w in ws] + [(b, (1, ct), lambda j, bb: (0, j)) for b in bs]
    return _Blockwise(name, fn, (c // ct, nb), data, params, [], [((nb * seq, c), *blk)], (1,))


def _op_dt(name, dt_raw, bias, a_log):
    t = dt_raw.shape[0]
    tr = _rows(t, 1024)
    row = ((tr, LANE), lambda i: (i, 0))
    par = ((1, LANE), lambda i: (0, 0))
    return _Blockwise(name, _f_dt, (t // tr,), [(dt_raw, *row)], [(bias, *par), (a_log, *par)], [],
                      [((t, LANE), *row), ((t, LANE), *row)], (0,))


def _op_ssd_post(name, y, xs, z, d_exp, ng):
    t, c = y.shape
    gw = c // SSD_G
    tr = _rows(t, 512)
    blk = ((tr, gw), lambda g, i: (i, g))
    par = ((1, gw), lambda g, i: (0, g))
    return _Blockwise(name, _f_ssd_post, (SSD_G, t // tr), [(y, *blk), (xs, *blk), (z, *blk)],
                      [(d_exp, *par), (ng, *par)], [], [((t, c), *blk)], (1,))


def _op_rms(name, x, g):
    t, c = x.shape
    tr = _rows(t, 512)
    return _Blockwise(name, _f_rms, (t // tr,), [(x, (tr, c), lambda i: (i, 0))], [(g, (1, c), lambda i: (0, 0))], [],
                      [((t, c), (tr, c), lambda i: (i, 0))], (0,))


def _op_rope(name, fn, x, cos, sin):
    t, c = x.shape
    tr = _rows(t, 512)
    blk = ((tr, LANE), lambda i, j: (i, j))
    cs = ((tr, LANE), lambda i, j: (i, 0))
    return _Blockwise(name, fn, (t // tr, c // LANE), [(x, *blk)], [], [(cos, *cs), (sin, *cs)], [((t, c), *blk)], ())


def _ssd_chunk(x, bc, cc, s, a_mat, dt_mat):
    n = a_mat.shape[0]
    row = lax.broadcasted_iota(jnp.int32, (n, n), 0)
    col = lax.broadcasted_iota(jnp.int32, (n, n), 1)
    upper = (row <= col).astype(F32)
    cumrow = jnp.dot(a_mat, upper, precision=HIGHEST, preferred_element_type=F32)
    cumcol = cumrow.T
    decay = jnp.exp(jnp.where(row >= col, cumcol - cumrow, -jnp.inf))
    g = lax.dot_general(cc.astype(BF16), bc.astype(BF16), (((1,), (1,)), ((), ())), preferred_element_type=F32)
    w = g * decay * dt_mat
    y = jnp.dot(w.astype(BF16), x.astype(BF16), preferred_element_type=F32)
    y = y + lax.dot_general((cc * jnp.exp(cumcol)).astype(BF16), s.astype(BF16), (((1,), (1,)), ((), ())),
                            preferred_element_type=F32)
    cum_last = jnp.sum(a_mat[0:1, :], axis=1, keepdims=True)
    w_end = jnp.exp(cum_last - cumcol) * dt_mat.T
    s_new = s * jnp.exp(cum_last) + lax.dot_general(x.astype(BF16), (bc * w_end).astype(BF16),
                                                    (((0,), (0,)), ((), ())), preferred_element_type=F32)
    return y, s_new


SSD_KPAD = 8


def _ssd_scan_fwd(xs, bm, cm, dt_t, a_t, nb, seq):
    nc = seq // SSD_L
    kp = SSD_K * SSD_P

    def body(xs_ref, b_ref, c_ref, dt_ref, a_ref, y_ref, st_ref):
        def chunk(c, states):
            sl = pl.ds(pl.multiple_of(c * SSD_L, SSD_L), SSD_L)
            bc, cc = b_ref[sl, :], c_ref[sl, :]
            new = []
            for k in range(SSD_K):
                hs = pl.ds(k * SSD_P, SSD_P)
                a_mat = jnp.broadcast_to(a_ref[0, k:k + 1, sl], (SSD_L, SSD_L))
                dt_mat = jnp.broadcast_to(dt_ref[0, k:k + 1, sl], (SSD_L, SSD_L))
                st_ref[0, c * SSD_K + k] = states[k]
                y, s_new = _ssd_chunk(xs_ref[sl, hs], bc, cc, states[k], a_mat, dt_mat)
                y_ref[sl, hs] = y
                new.append(s_new)
            return tuple(new)

        lax.fori_loop(0, nc, chunk, tuple(jnp.zeros((SSD_P, SSD_N), F32) for _ in range(SSD_K)))

    t = xs.shape[0]
    h_spec = pl.BlockSpec((1, SSD_KPAD, seq), lambda b, g: (b * SSD_G + g, 0, 0))
    return pl.pallas_call(
        body, name="ssd_scan_fwd", grid=(nb, SSD_G),
        in_specs=[pl.BlockSpec((seq, kp), lambda b, g: (b, g)),
                  pl.BlockSpec((seq, SSD_N), lambda b, g: (b, g)),
                  pl.BlockSpec((seq, SSD_N), lambda b, g: (b, g)), h_spec, h_spec],
        out_specs=[pl.BlockSpec((seq, kp), lambda b, g: (b, g)),
                   pl.BlockSpec((1, nc * SSD_K, SSD_P, SSD_N), lambda b, g: (b * SSD_G + g, 0, 0, 0))],
        out_shape=[jax.ShapeDtypeStruct((t, SSD_D_INNER), F32),
                   jax.ShapeDtypeStruct((nb * SSD_G, nc * SSD_K, SSD_P, SSD_N), F32)],
        compiler_params=_cparams(),
    )(xs, bm, cm, dt_t, a_t)


def _ssd_scan_bwd(xs, bm, cm, dt_t, a_t, states, dy, nb, seq):
    nc = seq // SSD_L
    kp = SSD_K * SSD_P

    def body(xs_ref, b_ref, c_ref, dt_ref, a_ref, st_ref, dy_ref, dxs_ref, db_ref, dc_ref, ddt_ref, da_ref):
        def chunk(i, dstates):
            c = nc - 1 - i
            sl = pl.ds(pl.multiple_of(c * SSD_L, SSD_L), SSD_L)
            bc, cc = b_ref[sl, :], c_ref[sl, :]
            db = jnp.zeros((SSD_L, SSD_N), F32)
            dc = jnp.zeros((SSD_L, SSD_N), F32)
            new = []
            for k in range(SSD_K):
                hs = pl.ds(k * SSD_P, SSD_P)
                a_mat = jnp.broadcast_to(a_ref[0, k:k + 1, sl], (SSD_L, SSD_L))
                dt_mat = jnp.broadcast_to(dt_ref[0, k:k + 1, sl], (SSD_L, SSD_L))
                _, vjp = jax.vjp(_ssd_chunk, xs_ref[sl, hs], bc, cc, st_ref[0, c * SSD_K + k], a_mat, dt_mat)
                dx, dbk, dck, ds, da_mat, ddt_mat = vjp((dy_ref[sl, hs], dstates[k]))
                dxs_ref[sl, hs] = dx
                db, dc = db + dbk, dc + dck
                da_ref[0, k:k + 1, sl] = jnp.sum(da_mat, axis=0, keepdims=True)
                ddt_ref[0, k:k + 1, sl] = jnp.sum(ddt_mat, axis=0, keepdims=True)
                new.append(ds)
            pad = jnp.zeros((SSD_KPAD - SSD_K, SSD_L), F32)
            da_ref[0, SSD_K:SSD_KPAD, sl] = pad
            ddt_ref[0, SSD_K:SSD_KPAD, sl] = pad
            db_ref[sl, :] = db
            dc_ref[sl, :] = dc
            return tuple(new)

        lax.fori_loop(0, nc, chunk, tuple(jnp.zeros((SSD_P, SSD_N), F32) for _ in range(SSD_K)))

    t = xs.shape[0]
    x_spec = pl.BlockSpec((seq, kp), lambda b, g: (b, g))
    n_spec = pl.BlockSpec((seq, SSD_N), lambda b, g: (b, g))
    h_spec = pl.BlockSpec((1, SSD_KPAD, seq), lambda b, g: (b * SSD_G + g, 0, 0))
    h_shape = jax.ShapeDtypeStruct((nb * SSD_G, SSD_KPAD, seq), F32)
    return pl.pallas_call(
        body, name="ssd_scan_bwd", grid=(nb, SSD_G),
        in_specs=[x_spec, n_spec, n_spec, h_spec, h_spec,
                  pl.BlockSpec((1, nc * SSD_K, SSD_P, SSD_N), lambda b, g: (b * SSD_G + g, 0, 0, 0)), x_spec],
        out_specs=[x_spec, n_spec, n_spec, h_spec, h_spec],
        out_shape=[jax.ShapeDtypeStruct((t, SSD_D_INNER), F32), jax.ShapeDtypeStruct((t, SSD_GN), F32),
                   jax.ShapeDtypeStruct((t, SSD_GN), F32), h_shape, h_shape],
        compiler_params=_cparams(),
    )(xs, bm, cm, dt_t, a_t, states, dy)


ATT_TQ = 256
ATT_SCALE = (MLA_NOPE + MLA_ROPE) ** -0.5


def _attn_pair(qn, qr, kn, kr, v, q0):
    tq, s = qn.shape[0], kn.shape[0]
    q_idx = q0 + lax.broadcasted_iota(jnp.int32, (tq, s), 0)
    k_idx = lax.broadcasted_iota(jnp.int32, (tq, s), 1)
    lane = lax.broadcasted_iota(jnp.int32, qr.shape, 1)
    nt = (((1,), (1,)), ((), ()))
    krb = kr.astype(BF16)
    outs = []
    for j in range(2):
        hs = slice(j * MLA_NOPE, (j + 1) * MLA_NOPE)
        qrj = jnp.where((lane // MLA_ROPE) == j, qr, 0.0)
        sc = lax.dot_general(qn[:, hs].astype(BF16), kn[:, hs].astype(BF16), nt, preferred_element_type=F32)
        sc = sc + lax.dot_general(qrj.astype(BF16), krb, nt, preferred_element_type=F32)
        sc = jnp.where(k_idx <= q_idx, sc * ATT_SCALE, -jnp.inf)
        m = jnp.max(sc, axis=-1, keepdims=True)
        e = jnp.exp(sc - lax.stop_gradient(m))
        p = e / jnp.sum(e, axis=-1, keepdims=True)
        outs.append(jnp.dot(p.astype(BF16), v[:, hs].astype(BF16), preferred_element_type=F32))
    return jnp.concatenate(outs, axis=1)


def _attn_specs(nb, seq):
    nq = seq // ATT_TQ
    qn = pl.BlockSpec((ATT_TQ, 2 * MLA_NOPE), lambda b, hp, qi: (b * nq + qi, hp))
    qr = pl.BlockSpec((ATT_TQ, LANE), lambda b, hp, qi: (b * nq + qi, hp))
    kn = pl.BlockSpec((seq, 2 * MLA_NOPE), lambda b, hp, qi: (b, hp))
    kr = pl.BlockSpec((seq, LANE), lambda b, hp, qi: (b, 0))
    return (nb, MLA_H // 2, nq), qn, qr, kn, kr


def _attn_fwd(qn, qr, kn, kr, v, nb, seq):
    grid, s_qn, s_qr, s_kn, s_kr = _attn_specs(nb, seq)

    def body(qn_ref, qr_ref, kn_ref, kr_ref, v_ref, o_ref):
        q0 = pl.program_id(2) * ATT_TQ
        o_ref[...] = _attn_pair(qn_ref[...], qr_ref[...], kn_ref[...], kr_ref[...], v_ref[...], q0)

    return pl.pallas_call(
        body, name="attn_fwd", grid=grid, in_specs=[s_qn, s_qr, s_kn, s_kr, s_kn], out_specs=s_qn,
        out_shape=jax.ShapeDtypeStruct(qn.shape, F32), compiler_params=_cparams(),
    )(qn, qr, kn, kr, v)


def _attn_bwd(qn, qr, kn, kr, v, do, nb, seq):
    grid, s_qn, s_qr, s_kn, s_kr = _attn_specs(nb, seq)

    def body(qn_ref, qr_ref, kn_ref, kr_ref, v_ref, do_ref, dqn_ref, dqr_ref, dkn_ref, dkr_ref, dv_ref):
        hp, qi = pl.program_id(1), pl.program_id(2)
        q0 = qi * ATT_TQ
        _, vjp = jax.vjp(lambda a, b, c, d, e: _attn_pair(a, b, c, d, e, q0),
                         qn_ref[...], qr_ref[...], kn_ref[...], kr_ref[...], v_ref[...])
        dqn, dqr, dkn, dkr, dv = vjp(do_ref[...])
        dqn_ref[...] = dqn
        dqr_ref[...] = dqr

        @pl.when(qi == 0)
        def _():
            dkn_ref[...] = jnp.zeros_like(dkn_ref)
            dv_ref[...] = jnp.zeros_like(dv_ref)

        @pl.when(jnp.logical_and(qi == 0, hp == 0))
        def _():
            dkr_ref[...] = jnp.zeros_like(dkr_ref)

        dkn_ref[...] += dkn
        dv_ref[...] += dv
        dkr_ref[...] += dkr

    return pl.pallas_call(
        body, name="attn_bwd", grid=grid, in_specs=[s_qn, s_qr, s_kn, s_kr, s_kn, s_qn],
        out_specs=[s_qn, s_qr, s_kn, s_kr, s_kn],
        out_shape=[jax.ShapeDtypeStruct(a.shape, F32) for a in (qn, qr, kn, kr, v)], compiler_params=_cparams(),
    )(qn, qr, kn, kr, v, do)


def _loss_head(y, target):
    t, d = y.shape
    tr = _rows(t, 512)

    def body(y_ref, t_ref, l_ref, dy_ref):
        err = y_ref[...] - t_ref[...]
        dy_ref[...] = err * (1.0 / d)
        part = 0.5 * jnp.sum(jnp.sum(err * err, axis=1, keepdims=True), axis=0, keepdims=True) * (1.0 / d)
        l_ref[...] = jnp.broadcast_to(part, l_ref.shape)

    row = pl.BlockSpec((tr, d), lambda i: (i, 0))
    parts, dy = pl.pallas_call(
        body, name="loss_head", grid=(t // tr,), in_specs=[row, row],
        out_specs=[pl.BlockSpec((8, LANE), lambda i: (i, 0)), row],
        out_shape=[jax.ShapeDtypeStruct((8 * (t // tr), LANE), F32), jax.ShapeDtypeStruct((t, d), F32)],
        compiler_params=_cparams(),
    )(y, target)
    return parts, dy


def _sum_parts(stack, name):
    n, r, c = stack.shape
    tr = _rows(r, 512)

    def body(s_ref, o_ref):
        acc = s_ref[0]
        for i in range(1, n):
            acc = acc + s_ref[i]
        o_ref[...] = acc

    return pl.pallas_call(
        body, name=name, grid=(r // tr,), in_specs=[pl.BlockSpec((n, tr, c), lambda i: (0, i, 0))],
        out_specs=pl.BlockSpec((tr, c), lambda i: (i, 0)), out_shape=jax.ShapeDtypeStruct((r, c), F32),
        compiler_params=_cparams(),
    )(stack)


def _adamw(w, g_mine, g_other, m, v, name):
    r, c = w.shape
    tr = _rows(r, 256)
    bc1 = 1.0 / (1.0 - ADAM_B1 ** ADAM_STEP)
    bc2 = 1.0 / (1.0 - ADAM_B2 ** ADAM_STEP)
    two = g_other is not None

    def body(*refs):
        if two:
            w_ref, g_ref, g2_ref, m_ref, v_ref, go_ref, d_ref, mo_ref, vo_ref = refs
            g = g_ref[...] + g2_ref[...]
        else:
            w_ref, g_ref, m_ref, v_ref, go_ref, d_ref, mo_ref, vo_ref = refs
            g = g_ref[...]
        mn = ADAM_B1 * m_ref[...] + (1.0 - ADAM_B1) * g
        vn = ADAM_B2 * v_ref[...] + (1.0 - ADAM_B2) * (g * g)
        go_ref[...] = g
        mo_ref[...] = mn
        vo_ref[...] = vn
        d_ref[...] = -ADAM_LR * ((mn * bc1) / (jnp.sqrt(vn * bc2) + ADAM_EPS) + ADAM_WD * w_ref[...])

    blk = pl.BlockSpec((tr, c), lambda i: (i, 0))
    ins = [w, g_mine] + ([g_other] if two else []) + [m, v]
    return pl.pallas_call(
        body, name=name, grid=(r // tr,), in_specs=[blk] * len(ins), out_specs=[blk] * 4,
        out_shape=[jax.ShapeDtypeStruct((r, c), F32)] * 4, compiler_params=_cparams(),
    )(*ins)


def _chip_peers():
    x, y, c = lax.axis_index("x"), lax.axis_index("y"), lax.axis_index("c")
    return (x, y, c), [(1 - x, y), (x, 1 - y), (1 - x, 1 - y)]


def _gather_chips(shards):
    n = len(shards)

    def body(*refs):
        ins, outs = refs[:n], refs[n:2 * n]
        send_sems, recv_sems, local_sems = refs[2 * n:]
        (x, y, c), chips = _chip_peers()
        me = 2 * x + y
        copies = []
        for i in range(n):
            loc = pltpu.make_async_copy(ins[i], outs[i].at[me], local_sems.at[i])
            loc.start()
            copies.append(loc)
        rem = []
        for i in range(n):
            for j, (px, py) in enumerate(chips):
                cp = pltpu.make_async_remote_copy(src_ref=ins[i], dst_ref=outs[i].at[me], send_sem=send_sems.at[i, j],
                                                  recv_sem=recv_sems.at[i, j], device_id=(px, py, c), device_id_type=MESH)
                cp.start()
                rem.append(cp)
        for i in range(n):
            for j, (px, py) in enumerate(chips):
                pltpu.make_async_remote_copy(src_ref=ins[i], dst_ref=outs[i].at[2 * px + py], send_sem=send_sems.at[i, j],
                                             recv_sem=recv_sems.at[i, j], device_id=(px, py, c), device_id_type=MESH).wait_recv()
        for cp in rem:
            cp.wait_send()
        for cp in copies:
            cp.wait()

    any_spec = pl.BlockSpec(memory_space=pl.ANY)
    return pl.pallas_call(
        body, name="gather_chips", in_specs=[any_spec] * n, out_specs=[any_spec] * n,
        out_shape=[jax.ShapeDtypeStruct((N_CHIPS,) + s.shape, s.dtype) for s in shards],
        scratch_shapes=[pltpu.SemaphoreType.DMA((n, 3)), pltpu.SemaphoreType.DMA((n, 3)), pltpu.SemaphoreType.DMA((n,))],
    )(*shards)


def _scatter_chips(stacks):
    n = len(stacks)

    def body(*refs):
        ins, outs = refs[:n], refs[n:2 * n]
        send_sems, recv_sems, local_sems = refs[2 * n:]
        (x, y, c), chips = _chip_peers()
        me = 2 * x + y
        copies = []
        for i in range(n):
            loc = pltpu.make_async_copy(ins[i].at[me], outs[i].at[me], local_sems.at[i])
            loc.start()
            copies.append(loc)
        rem = []
        for i in range(n):
            for j, (px, py) in enumerate(chips):
                cp = pltpu.make_async_remote_copy(src_ref=ins[i].at[2 * px + py], dst_ref=outs[i].at[me],
                                                  send_sem=send_sems.at[i, j], recv_sem=recv_sems.at[i, j],
                                                  device_id=(px, py, c), device_id_type=MESH)
                cp.start()
                rem.append(cp)
        for i in range(n):
            for j, (px, py) in enumerate(chips):
                pltpu.make_async_remote_copy(src_ref=ins[i].at[me], dst_ref=outs[i].at[2 * px + py],
                                             send_sem=send_sems.at[i, j], recv_sem=recv_sems.at[i, j],
                                             device_id=(px, py, c), device_id_type=MESH).wait_recv()
        for cp in rem:
            cp.wait_send()
        for cp in copies:
            cp.wait()

    any_spec = pl.BlockSpec(memory_space=pl.ANY)
    return pl.pallas_call(
        body, name="scatter_chips", in_specs=[any_spec] * n, out_specs=[any_spec] * n,
        out_shape=[jax.ShapeDtypeStruct(s.shape, s.dtype) for s in stacks],
        scratch_shapes=[pltpu.SemaphoreType.DMA((n, 3)), pltpu.SemaphoreType.DMA((n, 3)), pltpu.SemaphoreType.DMA((n,))],
    )(*stacks)


def _swap_cores(arrs):
    n = len(arrs)

    def body(*refs):
        ins, outs = refs[:n], refs[n:2 * n]
        send_sems, recv_sems = refs[2 * n:]
        x, y, c = lax.axis_index("x"), lax.axis_index("y"), lax.axis_index("c")
        cps = []
        for i in range(n):
            cp = pltpu.make_async_remote_copy(src_ref=ins[i], dst_ref=outs[i], send_sem=send_sems.at[i],
                                              recv_sem=recv_sems.at[i], device_id=(x, y, 1 - c), device_id_type=MESH)
            cp.start()
            cps.append(cp)
        for cp in cps:
            cp.wait()

    any_spec = pl.BlockSpec(memory_space=pl.ANY)
    return pl.pallas_call(
        body, name="swap_cores", in_specs=[any_spec] * n, out_specs=[any_spec] * n,
        out_shape=[jax.ShapeDtypeStruct(s.shape, s.dtype) for s in arrs],
        scratch_shapes=[pltpu.SemaphoreType.DMA((n,)), pltpu.SemaphoreType.DMA((n,))],
    )(*arrs)


def _gather_all(block):
    m_per, n = block.shape

    def body(x_ref, out_ref, send_sems, recv_sems, local_sem):
        x, y, c = lax.axis_index("x"), lax.axis_index("y"), lax.axis_index("c")
        me, sibling = (x, y, c), (x, y, 1 - c)
        chips = [(1 - x, y), (x, 1 - y), (1 - x, 1 - y)]

        def rows(px, py, pc):
            return out_ref.at[pl.ds((4 * px + 2 * py + pc) * m_per, m_per), :]

        def copy(k, blk, to, src=None):
            return pltpu.make_async_remote_copy(src_ref=rows(*blk) if src is None else src, dst_ref=rows(*blk),
                                                send_sem=send_sems.at[k], recv_sem=recv_sems.at[k], device_id=to,
                                                device_id_type=MESH)

        mine = pltpu.make_async_copy(x_ref, rows(*me), local_sem)
        mine.start()
        first = [copy(0, me, sibling, src=x_ref)]
        first += [copy(1 + j, me, (*chip, c), src=x_ref) for j, chip in enumerate(chips)]
        for cp in first:
            cp.start()
        passed = [copy(4 + j, (*chip, c), sibling) for j, chip in enumerate(chips)]
        for j, chip in enumerate(chips):
            copy(1 + j, (*chip, c), me).wait_recv()
            passed[j].start()
        copy(0, sibling, me).wait_recv()
        for j, chip in enumerate(chips):
            copy(4 + j, (*chip, 1 - c), me).wait_recv()
        for cp in first + passed:
            cp.wait_send()
        mine.wait()

    return pl.pallas_call(
        body, name="gather_all", out_shape=jax.ShapeDtypeStruct((N_DEV * m_per, n), block.dtype),
        in_specs=[pl.BlockSpec(memory_space=pltpu.VMEM)], out_specs=pl.BlockSpec(memory_space=pltpu.VMEM),
        scratch_shapes=[pltpu.SemaphoreType.DMA((7,)), pltpu.SemaphoreType.DMA((7,)), pltpu.SemaphoreType.DMA],
    )(block)


def _rope_tables(positions):
    inv_freq = 1.0 / (ROPE_THETA ** (jnp.arange(0, MLA_ROPE, 2, dtype=F32) / MLA_ROPE))
    ang = positions.astype(F32).reshape(-1, 1) * inv_freq
    return jnp.tile(jnp.cos(ang), (1, 4)), jnp.tile(jnp.sin(ang), (1, 4))


def _pad_cols(a, n):
    return jnp.pad(a, ((0, 0), (0, n - a.shape[1])))


def _ffn_fwd(l, h, w, nb, seq):
    ug = _mm(h, w["ffn_up_g"][l], name=f"ffn{l}_up_g")
    uv = _mm(h, w["ffn_up_v"][l], name=f"ffn{l}_up_v")
    op = _op_conv(f"ffn{l}_conv", _f_convffn, [ug, uv], [w["ffn_cw_g"][l], w["ffn_cw_v"][l]],
                  [w["ffn_cb_g"][l], w["ffn_cb_v"][l]], nb, seq)
    act = op.fwd()[0]
    ff = _mm(act, w["ffn_down"][l], name=f"ffn{l}_down")
    return ff, (h, op, act)


def _ffn_bwd(l, saved, dff, w, grads):
    h, op, act = saved
    grads[f"ffn_down{l}"] = _mm(act, dff, ta=True, name=f"ffn{l}_down_dw")
    dact = _mm(dff, w["ffn_down"][l], tb=True, name=f"ffn{l}_down_dx")
    (dug, duv), (dwg, dwv, dbg, dbv) = op.bwd([dact])
    grads[f"ffn_up_g{l}"] = _mm(h, dug, ta=True, name=f"ffn{l}_up_g_dw")
    grads[f"ffn_up_v{l}"] = _mm(h, duv, ta=True, name=f"ffn{l}_up_v_dw")
    grads[f"ffn_cw_g{l}"], grads[f"ffn_cw_v{l}"], grads[f"ffn_cb_g{l}"], grads[f"ffn_cb_v{l}"] = dwg, dwv, dbg, dbv
    dh = _mm(dug, w["ffn_up_g"][l], tb=True, name=f"ffn{l}_up_g_dx")
    return _mm(duv, w["ffn_up_v"][l], tb=True, add=dh, name=f"ffn{l}_up_v_dx")


def _local_step(x, positions, target, w):
    nb, seq, d = x.shape
    t = nb * seq
    x2, tgt2 = x.reshape(t, d), target.reshape(t, d)
    cos, sin = _rope_tables(positions)
    grads = {}

    z = _mm(x2, w["in_z"], name="ssd_in_z")
    raw = [_mm(x2, w[k], name="ssd_" + k) for k in ("in_x", "in_b", "in_c")]
    dt_raw = _mm(x2, w["in_dt"], name="ssd_in_dt")
    conv_ops = [_op_conv("ssd_conv_" + s, _f_convsilu, [r], [w["conv_w_" + s]], [w["conv_b_" + s]], nb, seq)
                for s, r in zip("xbc", raw)]
    xs, bm, cm = [op.fwd()[0] for op in conv_ops]
    dt_op = _op_dt("ssd_dt", dt_raw, w["dt_bias"], w["a_log"])
    dt, a = dt_op.fwd()

    def to_heads(v):
        v = jnp.swapaxes(v.reshape(nb, seq, LANE), 1, 2)[:, :SSD_H, :].reshape(nb * SSD_G, SSD_K, seq)
        return jnp.pad(v, ((0, 0), (0, SSD_KPAD - SSD_K), (0, 0)))

    def from_heads(v):
        v = v[:, :SSD_K, :].reshape(nb, SSD_H, seq)
        return jnp.swapaxes(jnp.pad(v, ((0, 0), (0, LANE - SSD_H), (0, 0))), 1, 2).reshape(t, LANE)

    dt_t, a_t = to_heads(dt), to_heads(a)
    y, states = _ssd_scan_fwd(xs, bm, cm, dt_t, a_t, nb, seq)
    post_op = _op_ssd_post("ssd_post", y, xs, z, w["d_exp"], w["norm_g"])
    yn = post_op.fwd()[0]
    mix0 = _mm(yn, w["ssd_out"], name="ssd_out")
    ln0m = _op_ln("ln_mix0", x2, mix0, w["ln_mix_g"][0], w["ln_mix_b"][0])
    h0a = ln0m.fwd()[0]
    ff0, ffn0_saved = _ffn_fwd(0, h0a, w, nb, seq)
    ln0f = _op_ln("ln_ffn0", h0a, ff0, w["ln_ffn_g"][0], w["ln_ffn_b"][0])
    h1 = ln0f.fwd()[0]

    ckv = _mm(h1, w["kv_down_c"], name="kv_down_c")
    kr_in = _mm(h1, w["kv_down_r"], name="kv_down_r")
    kvn_op = _op_rms("kv_norm", ckv, w["kv_norm_g"])
    ckvn = kvn_op.fwd()[0]
    kr_op = _op_rope("k_rope", _f_rope_dup, kr_in, cos, sin)
    kr = kr_op.fwd()[0]
    kn = _mm(ckvn, w["kv_up_k"], name="kv_up_k")
    v = _mm(ckvn, w["kv_up_v"], name="kv_up_v")
    cq_raw = _mm(h1, w["q_down"], name="q_down")
    qn_op = _op_rms("q_norm", cq_raw, w["q_norm_g"])
    cq = qn_op.fwd()[0]
    qn = _mm(cq, w["q_up_n"], name="q_up_n")
    qr_raw = _mm(cq, w["q_up_r"], name="q_up_r")
    qr_op = _op_rope("q_rope", _f_rope, qr_raw, cos, sin)
    qr = qr_op.fwd()[0]
    o = _attn_fwd(qn, qr, kn, kr, v, nb, seq)
    mix1 = _mm(o, w["attn_out"], name="attn_out")
    ln1m = _op_ln("ln_mix1", h1, mix1, w["ln_mix_g"][1], w["ln_mix_b"][1])
    h1a = ln1m.fwd()[0]
    ff1, ffn1_saved = _ffn_fwd(1, h1a, w, nb, seq)
    ln1f = _op_ln("ln_ffn1", h1a, ff1, w["ln_ffn_g"][1], w["ln_ffn_b"][1])
    h2 = ln1f.fwd()[0]

    loss_parts, dh2 = _loss_head(h2, tgt2)

    (dh1a, dff1), (grads["ln_ffn_g1"], grads["ln_ffn_b1"]) = ln1f.bwd([dh2])
    dh1a = _add2(dh1a, _ffn_bwd(1, ffn1_saved, dff1, w, grads), "add_h1a")
    (dh1, dmix1), (grads["ln_mix_g1"], grads["ln_mix_b1"]) = ln1m.bwd([dh1a])
    grads["attn_out"] = _mm(o, dmix1, ta=True, name="attn_out_dw")
    do = _mm(dmix1, w["attn_out"], tb=True, name="attn_out_dx")
    dqn, dqr, dkn, dkr, dv = _attn_bwd(qn, qr, kn, kr, v, do, nb, seq)
    (dqr_raw,), _ = qr_op.bwd([dqr])
    grads["q_up_n"] = _mm(cq, dqn, ta=True, name="q_up_n_dw")
    grads["q_up_r"] = _mm(cq, dqr_raw, ta=True, name="q_up_r_dw")
    dcq = _mm(dqn, w["q_up_n"], tb=True, name="q_up_n_dx")
    dcq = _mm(dqr_raw, w["q_up_r"], tb=True, add=dcq, name="q_up_r_dx")
    (dcq_raw,), (grads["q_norm_g"],) = qn_op.bwd([dcq])
    grads["q_down"] = _mm(h1, dcq_raw, ta=True, name="q_down_dw")
    dh1 = _mm(dcq_raw, w["q_down"], tb=True, add=dh1, name="q_down_dx")
    grads["kv_up_k"] = _mm(ckvn, dkn, ta=True, name="kv_up_k_dw")
    grads["kv_up_v"] = _mm(ckvn, dv, ta=True, name="kv_up_v_dw")
    dckvn = _mm(dkn, w["kv_up_k"], tb=True, name="kv_up_k_dx")
    dckvn = _mm(dv, w["kv_up_v"], tb=True, add=dckvn, name="kv_up_v_dx")
    (dckv,), (grads["kv_norm_g"],) = kvn_op.bwd([dckvn])
    (dkr_in,), _ = kr_op.bwd([dkr])
    grads["kv_down_c"] = _mm(h1, dckv, ta=True, name="kv_down_c_dw")
    grads["kv_down_r"] = _mm(h1, dkr_in, ta=True, name="kv_down_r_dw")
    dh1 = _mm(dckv, w["kv_down_c"], tb=True, add=dh1, name="kv_down_c_dx")
    dh1 = _mm(dkr_in, w["kv_down_r"], tb=True, add=dh1, name="kv_down_r_dx")

    (dh0a, dff0), (grads["ln_ffn_g0"], grads["ln_ffn_b0"]) = ln0f.bwd([dh1])
    dh0a = _add2(dh0a, _ffn_bwd(0, ffn0_saved, dff0, w, grads), "add_h0a")
    (dx, dmix0), (grads["ln_mix_g0"], grads["ln_mix_b0"]) = ln0m.bwd([dh0a])
    grads["ssd_out"] = _mm(yn, dmix0, ta=True, name="ssd_out_dw")
    dyn = _mm(dmix0, w["ssd_out"], tb=True, name="ssd_out_dx")
    (dy, dxs_post, dz), (grads["d_exp"], grads["norm_g"]) = post_op.bwd([dyn])
    dxs, dbm, dcm, ddt_t, da_t = _ssd_scan_bwd(xs, bm, cm, dt_t, a_t, states, dy, nb, seq)
    dxs = _add2(dxs, dxs_post, "add_dxs")
    (ddt_raw,), (grads["dt_bias"], grads["a_log"]) = dt_op.bwd([from_heads(ddt_t), from_heads(da_t)])
    draws = []
    for s, op, dout in zip("xbc", conv_ops, (dxs, dbm, dcm)):
        (dr,), (grads["conv_w_" + s], grads["conv_b_" + s]) = op.bwd([dout])
        draws.append(dr)
    for k, dr in zip(("in_x", "in_b", "in_c"), draws):
        grads[k] = _mm(x2, dr, ta=True, name=f"ssd_{k}_dw")
        dx = _mm(dr, w[k], tb=True, add=dx, name=f"ssd_{k}_dx")
    grads["in_z"] = _mm(x2, dz, ta=True, name="ssd_in_z_dw")
    grads["in_dt"] = _mm(x2, ddt_raw, ta=True, name="ssd_in_dt_dw")
    dx = _mm(dz, w["in_z"], tb=True, add=dx, name="ssd_in_z_dx")
    dx = _mm(ddt_raw, w["in_dt"], tb=True, add=dx, name="ssd_in_dt_dx")
    return loss_parts, dx.reshape(nb, seq, d), grads


def _add2(a, b, name):
    t, c = a.shape
    tr = _rows(t, 512)

    def body(a_ref, b_ref, o_ref):
        o_ref[...] = a_ref[...] + b_ref[...]

    blk = pl.BlockSpec((tr, c), lambda i: (i, 0))
    return pl.pallas_call(body, name=name, grid=(t // tr,), in_specs=[blk, blk], out_specs=blk,
                          out_shape=jax.ShapeDtypeStruct((t, c), F32), compiler_params=_cparams())(a, b)


_XE, _BE, _CE = SSD_D_INNER, SSD_D_INNER + SSD_GN, SSD_D_INNER + 2 * SSD_GN


def _prep_weights(fw):
    w = {}
    ip = fw["ssd_in_proj"][0]
    o = SSD_D_INNER
    w["in_z"], w["in_x"], w["in_b"], w["in_c"] = ip[:, :o], ip[:, o:o + _XE], ip[:, o + _XE:o + _BE], ip[:, o + _BE:o + _CE]
    w["in_dt"] = _pad_cols(ip[:, o + _CE:], LANE)
    cw, cb = fw["ssd_conv_w"][0], fw["ssd_conv_b"]
    for s, (lo, hi) in zip("xbc", ((0, _XE), (_XE, _BE), (_BE, _CE))):
        w["conv_w_" + s], w["conv_b_" + s] = cw[:, lo:hi], cb[:, lo:hi]
    w["dt_bias"], w["a_log"] = _pad_cols(fw["ssd_dt_bias"], LANE), _pad_cols(fw["ssd_A_log"], LANE)
    w["d_exp"] = jnp.repeat(fw["ssd_D"][0], SSD_P)[None, :]
    w["norm_g"], w["ssd_out"] = fw["ssd_norm_g"], fw["ssd_out_proj"][0]
    kd = fw["kv_down_proj"]
    w["kv_down_c"], w["kv_down_r"] = kd[:, :MLA_KV_RANK], _pad_cols(kd[:, MLA_KV_RANK:], LANE)
    w["kv_norm_g"] = fw["kv_norm_g"][None, :]
    w["kv_up_k"], w["kv_up_v"] = fw["kv_up_k"], fw["kv_up_v"]
    w["q_down"], w["q_norm_g"] = fw["q_down_proj"][0], fw["q_norm_g"]
    qu = fw["q_up_proj"][0].reshape(MLA_Q_RANK, MLA_H, MLA_NOPE + MLA_ROPE)
    w["q_up_n"] = qu[:, :, :MLA_NOPE].reshape(MLA_Q_RANK, MLA_H * MLA_NOPE)
    w["q_up_r"] = qu[:, :, MLA_NOPE:].reshape(MLA_Q_RANK, MLA_H * MLA_ROPE)
    w["attn_out"] = fw["attn_out_proj"][0]
    f = FFN_HIDDEN
    w["ffn_up_g"] = [fw["ffn_up"][l][:, :f] for l in range(DEPTH)]
    w["ffn_up_v"] = [fw["ffn_up"][l][:, f:] for l in range(DEPTH)]
    w["ffn_cw_g"] = [fw["ffn_conv_w"][l][:, :f] for l in range(DEPTH)]
    w["ffn_cw_v"] = [fw["ffn_conv_w"][l][:, f:] for l in range(DEPTH)]
    w["ffn_cb_g"] = [fw["ffn_conv_b"][l:l + 1, :f] for l in range(DEPTH)]
    w["ffn_cb_v"] = [fw["ffn_conv_b"][l:l + 1, f:] for l in range(DEPTH)]
    w["ffn_down"] = [fw["ffn_down"][l] for l in range(DEPTH)]
    for k in ("ln_mix_g", "ln_mix_b", "ln_ffn_g", "ln_ffn_b"):
        w[k] = [fw[k][l:l + 1] for l in range(DEPTH)]
    return w


def _assemble_grads(g):
    r = {}
    r["ssd_in_proj"] = jnp.concatenate([g["in_z"], g["in_x"], g["in_b"], g["in_c"], g["in_dt"][:, :SSD_H]], axis=1)[None]
    r["ssd_conv_w"] = jnp.concatenate([g["conv_w_" + s] for s in "xbc"], axis=1)[None]
    r["ssd_conv_b"] = jnp.concatenate([g["conv_b_" + s] for s in "xbc"], axis=1)
    r["ssd_dt_bias"], r["ssd_A_log"] = g["dt_bias"][:, :SSD_H], g["a_log"][:, :SSD_H]
    r["ssd_D"] = jnp.sum(g["d_exp"].reshape(SSD_H, SSD_P), axis=1)[None, :]
    r["ssd_norm_g"], r["ssd_out_proj"] = g["norm_g"], g["ssd_out"][None]
    r["kv_down_proj"] = jnp.concatenate([g["kv_down_c"], g["kv_down_r"][:, :MLA_ROPE]], axis=1)
    r["kv_norm_g"] = g["kv_norm_g"][0]
    r["kv_up_k"], r["kv_up_v"] = g["kv_up_k"], g["kv_up_v"]
    r["q_down_proj"], r["q_norm_g"] = g["q_down"][None], g["q_norm_g"]
    r["q_up_proj"] = jnp.concatenate([g["q_up_n"].reshape(MLA_Q_RANK, MLA_H, MLA_NOPE),
                                      g["q_up_r"].reshape(MLA_Q_RANK, MLA_H, MLA_ROPE)], axis=2).reshape(1, MLA_Q_RANK, -1)
    r["attn_out_proj"] = g["attn_out"][None]
    r["ffn_up"] = jnp.stack([jnp.concatenate([g[f"ffn_up_g{l}"], g[f"ffn_up_v{l}"]], axis=1) for l in range(DEPTH)])
    r["ffn_conv_w"] = jnp.stack([jnp.concatenate([g[f"ffn_cw_g{l}"], g[f"ffn_cw_v{l}"]], axis=1) for l in range(DEPTH)])
    r["ffn_conv_b"] = jnp.concatenate([jnp.concatenate([g[f"ffn_cb_g{l}"], g[f"ffn_cb_v{l}"]], axis=1) for l in range(DEPTH)], axis=0)
    r["ffn_down"] = jnp.stack([g[f"ffn_down{l}"] for l in range(DEPTH)])
    for k in ("ln_mix_g", "ln_mix_b", "ln_ffn_g", "ln_ffn_b"):
        r[k] = jnp.concatenate([g[f"{k}{l}"] for l in range(DEPTH)], axis=0)
    return r


_WEIGHTS = ["ssd_in_proj", "ssd_conv_w", "ssd_conv_b", "ssd_dt_bias", "ssd_A_log", "ssd_D", "ssd_norm_g", "ssd_out_proj",
            "kv_down_proj", "kv_norm_g", "kv_up_k", "kv_up_v", "q_down_proj", "q_norm_g", "q_up_proj", "attn_out_proj",
            "ffn_up", "ffn_conv_w", "ffn_conv_b", "ffn_down", "ln_mix_g", "ln_mix_b", "ln_ffn_g", "ln_ffn_b"]
_COL_CUT = ["ssd_in_proj", "ssd_conv_w", "ssd_conv_b", "ssd_norm_g", "kv_up_k", "kv_up_v", "q_up_proj", "ffn_up", "ffn_conv_w"]
_ROW_CUT = ["ssd_out_proj", "kv_down_proj", "q_down_proj", "attn_out_proj", "ffn_down"]
_CUT = _COL_CUT + _ROW_CUT
_WHOLE = [n for n in _WEIGHTS if n not in _CUT]
_MXU_WEIGHTS = ["ssd_in_proj", "ssd_out_proj", "kv_down_proj", "kv_up_k", "kv_up_v", "q_down_proj", "q_up_proj",
                "attn_out_proj", "ffn_up", "ffn_down"]
_PACK_ROWS = 160


def _shard_2d(name, s):
    return s.reshape(-1, s.shape[-1])


def _unstack(name, g, shard_shape):
    if name in _COL_CUT:
        lead = shard_shape[:-1]
        return jnp.swapaxes(g, 0, 1).reshape(*lead, N_CHIPS * shard_shape[-1])
    lead, rs, c = shard_shape[:-2], shard_shape[-2], shard_shape[-1]
    n_lead = math.prod(lead)
    return jnp.swapaxes(g.reshape(N_CHIPS, n_lead, rs, c), 0, 1).reshape(*lead, N_CHIPS * rs, c)


def _stack(name, full, shard_shape):
    if name in _COL_CUT:
        cs = shard_shape[-1]
        return jnp.swapaxes(full.reshape(-1, N_CHIPS, cs), 0, 1)
    lead, rs, c = shard_shape[:-2], shard_shape[-2], shard_shape[-1]
    n_lead = math.prod(lead)
    return jnp.swapaxes(full.reshape(n_lead, N_CHIPS, rs, c), 0, 1).reshape(N_CHIPS, n_lead * rs, c)


def _pack(arrs):
    flat = jnp.concatenate([a.reshape(-1) for a in arrs])
    return jnp.pad(flat, (0, _PACK_ROWS * LANE - flat.shape[0])).reshape(_PACK_ROWS, LANE)


def _unpack(packed, like):
    flat, out, o = packed.reshape(-1), [], 0
    for a in like:
        out.append(flat[o:o + a.size].reshape(a.shape))
        o += a.size
    return out


_ARGS = ["x", "positions"] + _WEIGHTS + ["loss_target"] + ["m_" + n for n in _WEIGHTS] + ["v_" + n for n in _WEIGHTS]


def kernel(x, positions, ssd_in_proj, ssd_conv_w, ssd_conv_b, ssd_dt_bias, ssd_A_log, ssd_D, ssd_norm_g,
           ssd_out_proj, kv_down_proj, kv_norm_g, kv_up_k, kv_up_v, q_down_proj, q_norm_g, q_up_proj,
           attn_out_proj, ffn_up, ffn_conv_w, ffn_conv_b, ffn_down, ln_mix_g, ln_mix_b, ln_ffn_g, ln_ffn_b,
           loss_target, m_ssd_in_proj, m_ssd_conv_w, m_ssd_conv_b, m_ssd_dt_bias, m_ssd_A_log, m_ssd_D,
           m_ssd_norm_g, m_ssd_out_proj, m_kv_down_proj, m_kv_norm_g, m_kv_up_k, m_kv_up_v, m_q_down_proj,
           m_q_norm_g, m_q_up_proj, m_attn_out_proj, m_ffn_up, m_ffn_conv_w, m_ffn_conv_b, m_ffn_down,
           m_ln_mix_g, m_ln_mix_b, m_ln_ffn_g, m_ln_ffn_b, v_ssd_in_proj, v_ssd_conv_w, v_ssd_conv_b,
           v_ssd_dt_bias, v_ssd_A_log, v_ssd_D, v_ssd_norm_g, v_ssd_out_proj, v_kv_down_proj, v_kv_norm_g,
           v_kv_up_k, v_kv_up_v, v_q_down_proj, v_q_norm_g, v_q_up_proj, v_attn_out_proj, v_ffn_up,
           v_ffn_conv_w, v_ffn_conv_b, v_ffn_down, v_ln_mix_g, v_ln_mix_b, v_ln_ffn_g, v_ln_ffn_b):
    args = (x, positions, ssd_in_proj, ssd_conv_w, ssd_conv_b, ssd_dt_bias, ssd_A_log, ssd_D, ssd_norm_g,
            ssd_out_proj, kv_down_proj, kv_norm_g, kv_up_k, kv_up_v, q_down_proj, q_norm_g, q_up_proj,
            attn_out_proj, ffn_up, ffn_conv_w, ffn_conv_b, ffn_down, ln_mix_g, ln_mix_b, ln_ffn_g, ln_ffn_b,
            loss_target, m_ssd_in_proj, m_ssd_conv_w, m_ssd_conv_b, m_ssd_dt_bias, m_ssd_A_log, m_ssd_D,
            m_ssd_norm_g, m_ssd_out_proj, m_kv_down_proj, m_kv_norm_g, m_kv_up_k, m_kv_up_v, m_q_down_proj,
            m_q_norm_g, m_q_up_proj, m_attn_out_proj, m_ffn_up, m_ffn_conv_w, m_ffn_conv_b, m_ffn_down,
            m_ln_mix_g, m_ln_mix_b, m_ln_ffn_g, m_ln_ffn_b, v_ssd_in_proj, v_ssd_conv_w, v_ssd_conv_b,
            v_ssd_dt_bias, v_ssd_A_log, v_ssd_D, v_ssd_norm_g, v_ssd_out_proj, v_kv_down_proj, v_kv_norm_g,
            v_kv_up_k, v_kv_up_v, v_q_down_proj, v_q_norm_g, v_q_up_proj, v_attn_out_proj, v_ffn_up,
            v_ffn_conv_w, v_ffn_conv_b, v_ffn_down, v_ln_mix_g, v_ln_mix_b, v_ln_ffn_g, v_ln_ffn_b)
    a = dict(zip(_ARGS, args, strict=True))

    shards = [_shard_2d(n, a[n]) for n in _CUT]
    shards = [s.astype(BF16) if n in _MXU_WEIGHTS else s for n, s in zip(_CUT, shards)]
    gathered = _gather_chips(shards)
    fw = {n: _unstack(n, g, a[n].shape) for n, g in zip(_CUT, gathered)}
    fw.update({n: a[n] for n in _WHOLE})

    loss_parts, grad_x, pieces = _local_step(a["x"], a["positions"], a["loss_target"], _prep_weights(fw))
    full = _assemble_grads(pieces)
    loss = lax.psum(jnp.sum(loss_parts[::8, 0]), ("x", "y", "c"))

    bufs = _scatter_chips([_stack(n, full[n], a[n].shape) for n in _CUT])
    sums = [_sum_parts(b, "sum_chips_" + n) for n, b in zip(_CUT, bufs)]
    others = _swap_cores(sums)
    every = _gather_all(_pack([full[n] for n in _WHOLE]))
    g_whole = _sum_parts(every.reshape(N_DEV, _PACK_ROWS, LANE), "sum_devices")

    res = {}
    for n, s, o in zip(_CUT, sums, others):
        out = _adamw(_shard_2d(n, a[n]), s, o, _shard_2d(n, a["m_" + n]), _shard_2d(n, a["v_" + n]), "adamw_" + n)
        res[n] = [r.reshape(a[n].shape) for r in out]
    whole = [a[n] for n in _WHOLE]
    out = _adamw(_pack(whole), g_whole, None, _pack([a["m_" + n] for n in _WHOLE]), _pack([a["v_" + n] for n in _WHOLE]),
                 "adamw_whole")
    for k, n in enumerate(_WHOLE):
        res[n] = [_unpack(r, whole)[k] for r in out]
    return (loss, grad_x, *[res[n][0] for n in _WEIGHTS], *[res[n][1] for n in _WEIGHTS],
            *[res[n][2] for n in _WEIGHTS], *[res[n][3] for n in _WEIGHTS])
```

```python
import functools
import math

import jax
import jax.numpy as jnp
from jax import lax
from jax.experimental import pallas as pl
from jax.experimental.pallas import tpu as pltpu

F32 = jnp.float32
BF16 = jnp.bfloat16
HIGHEST = lax.Precision.HIGHEST
MESH = pl.DeviceIdType.MESH

D_MODEL = 1024
DEPTH = 2
DN_ALPHA = (2 * DEPTH) ** 0.25
SSD_D_INNER = 2048
SSD_P = 64
SSD_H = 32
SSD_G = 8
SSD_K = 4
SSD_N = 128
SSD_L = 128
SSD_GN = SSD_G * SSD_N
MLA_H = 8
MLA_Q_RANK = 384
MLA_KV_RANK = 256
MLA_NOPE = 128
MLA_ROPE = 64
MLA_V = 128
ROPE_THETA = 10000.0
FFN_HIDDEN = 2816
LN_EPS = 1e-5
RMS_EPS = 1e-6
ADAM_LR = 0.001
ADAM_B1 = 0.9
ADAM_B2 = 0.999
ADAM_EPS = 1e-08
ADAM_WD = 0.01
ADAM_STEP = 10

N_CHIPS = 4
N_DEV = 8
LANE = 128
VMEM_LIMIT = 56 * 1024 * 1024


def _cparams(**kw):
    return pltpu.CompilerParams(vmem_limit_bytes=VMEM_LIMIT, **kw)


def _tile(dim, cap):
    best = None
    t = LANE
    while t <= min(dim, cap):
        if dim % t == 0:
            best = t
        t += LANE
    return dim if best is None else best


MM_TILE_CAP = 1408
MM_WHOLE_K = 2816
MM_VMEM_BUDGET = 40 * 1024 * 1024


def _mm_tiles(m, n, k, a_bytes, b_bytes, o_bytes, has_add):
    tm, tn = _tile(m, MM_TILE_CAP), _tile(n, MM_TILE_CAP)
    tk = k if k <= MM_WHOLE_K else _tile(k, 1024)

    def need(tm, tn):
        acc = tm * tn * 4 if tk < k else 0
        return 2 * (tm * tk * a_bytes + tk * tn * b_bytes + tm * tn * o_bytes + (tm * tn * 4 if has_add else 0)) + acc

    while need(tm, tn) > MM_VMEM_BUDGET:
        if tm >= tn and _tile(m, tm // 2) < tm:
            tm = _tile(m, tm // 2)
        elif _tile(n, tn // 2) < tn:
            tn = _tile(n, tn // 2)
        else:
            break
    return tm, tn, tk


def _mm(a, b, *, ta=False, tb=False, add=None, out_dtype=F32, name):
    m, k = (a.shape[1], a.shape[0]) if ta else a.shape
    n = b.shape[0] if tb else b.shape[1]
    assert (b.shape[1] if tb else b.shape[0]) == k
    tm, tn, tk = _mm_tiles(m, n, k, a.dtype.itemsize, b.dtype.itemsize, jnp.dtype(out_dtype).itemsize, add is not None)
    nk = k // tk
    dims = (((0 if ta else 1,), (1 if tb else 0,)), ((), ()))

    def partial_product(a_ref, b_ref):
        return lax.dot_general(a_ref[...].astype(BF16), b_ref[...].astype(BF16), dims, preferred_element_type=F32)

    def body_one(*refs):
        if add is None:
            a_ref, b_ref, o_ref = refs
            o_ref[...] = partial_product(a_ref, b_ref).astype(out_dtype)
        else:
            a_ref, b_ref, c_ref, o_ref = refs
            o_ref[...] = (partial_product(a_ref, b_ref) + c_ref[...]).astype(out_dtype)

    def body_acc(*refs):
        if add is None:
            a_ref, b_ref, o_ref, acc = refs
        else:
            a_ref, b_ref, c_ref, o_ref, acc = refs
        kk = pl.program_id(2)

        @pl.when(kk == 0)
        def _():
            acc[...] = jnp.zeros_like(acc) if add is None else c_ref[...]

        acc[...] += partial_product(a_ref, b_ref)

        @pl.when(kk == nk - 1)
        def _():
            o_ref[...] = acc[...].astype(out_dtype)

    a_spec = pl.BlockSpec((tk, tm), lambda i, j, kk: (kk, i)) if ta else pl.BlockSpec((tm, tk), lambda i, j, kk: (i, kk))
    b_spec = pl.BlockSpec((tn, tk), lambda i, j, kk: (j, kk)) if tb else pl.BlockSpec((tk, tn), lambda i, j, kk: (kk, j))
    o_spec = pl.BlockSpec((tm, tn), lambda i, j, kk: (i, j))
    ins, specs = [a, b], [a_spec, b_spec]
    if add is not None:
        ins.append(add)
        specs.append(o_spec)
    return pl.pallas_call(
        body_one if nk == 1 else body_acc, name=name, grid=(m // tm, n // tn, nk), in_specs=specs, out_specs=o_spec,
        out_shape=jax.ShapeDtypeStruct((m, n), out_dtype),
        scratch_shapes=[] if nk == 1 else [pltpu.VMEM((tm, tn), F32)],
        compiler_params=_cparams(dimension_semantics=("parallel", "parallel", "arbitrary")),
    )(*ins)


def _spec(op):
    return pl.BlockSpec(op[1], op[2])


def _bw_fwd(name, fn, grid, ins, outs, out_dtypes):
    n_in = len(ins)
    flat = [(o, dt) for o, dts in zip(outs, out_dtypes) for dt in dts]

    def body(*refs):
        res = fn(*[r[...].astype(F32) for r in refs[:n_in]])
        orefs = iter(refs[n_in:])
        for v, dts in zip(res, out_dtypes):
            for dt in dts:
                next(orefs)[...] = v.astype(dt)

    return pl.pallas_call(
        body, name=name, grid=grid, in_specs=[_spec(o) for o in ins],
        out_specs=[pl.BlockSpec(o[1], o[2]) for o, _ in flat],
        out_shape=[jax.ShapeDtypeStruct(o[0], dt) for o, dt in flat], compiler_params=_cparams(),
    )(*[o[0] for o in ins])


def _bw_bwd(name, fn, grid, data, params, consts, cts, red_axes, grad_dtypes):
    nd, npar, nc, nct = len(data), len(params), len(consts), len(cts)

    def body(*refs):
        first = None
        for ax in red_axes:
            z = pl.program_id(ax) == 0
            first = z if first is None else jnp.logical_and(first, z)
        vals = [r[...].astype(F32) for r in refs[:nd + npar + nc + nct]]
        d, p, c, g = vals[:nd], vals[nd:nd + npar], vals[nd + npar:nd + npar + nc], vals[nd + npar + nc:]
        _, vjp = jax.vjp(lambda dd, pp: tuple(fn(*dd, *pp, *c)), d, p)
        gd, gp = vjp(tuple(g))
        orefs = refs[nd + npar + nc + nct:]
        for r, v in zip(orefs[:nd], gd):
            r[...] = v.astype(r.dtype)
        if npar:
            @pl.when(first)
            def _():
                for r in orefs[nd:]:
                    r[...] = jnp.zeros_like(r)

            for r, v in zip(orefs[nd:], gp):
                r[...] += v

    ins = list(data) + list(params) + list(consts) + list(cts)
    outs = list(data) + list(params)
    dtypes = list(grad_dtypes) + [F32] * npar
    return pl.pallas_call(
        body, name=name, grid=grid, in_specs=[_spec(o) for o in ins], out_specs=[_spec(o) for o in outs],
        out_shape=[jax.ShapeDtypeStruct(o[0].shape, dt) for o, dt in zip(outs, dtypes)], compiler_params=_cparams(),
    )(*[o[0] for o in ins])


def _shift_down(x, s):
    row = lax.broadcasted_iota(jnp.int32, x.shape, 0)
    return jnp.where(row < s, 0.0, pltpu.roll(x, s, 0))


def _shift_up(x, s):
    n = x.shape[0]
    row = lax.broadcasted_iota(jnp.int32, x.shape, 0)
    return jnp.where(row >= n - s, 0.0, pltpu.roll(x, n - s, 0))


def _time_shift(s):
    if s == 0:
        return lambda x: x

    @jax.custom_vjp
    def shift(x):
        return _shift_down(x, s)

    shift.defvjp(lambda x: (_shift_down(x, s), None), lambda _, g: (_shift_up(g, s),))
    return shift


def _rot_half_raw(x):
    lane = lax.broadcasted_iota(jnp.int32, x.shape, 1)
    return jnp.where(lane % MLA_ROPE < MLA_ROPE // 2, -pltpu.roll(x, LANE - MLA_ROPE // 2, 1), pltpu.roll(x, MLA_ROPE // 2, 1))


@jax.custom_vjp
def _rot_half(x):
    return _rot_half_raw(x)


_rot_half.defvjp(lambda x: (_rot_half_raw(x), None), lambda _, g: (-_rot_half_raw(g),))


@jax.custom_vjp
def _roll_half_lanes(x):
    return pltpu.roll(x, LANE // 2, 1)


_roll_half_lanes.defvjp(lambda x: (pltpu.roll(x, LANE // 2, 1), None), lambda _, g: (pltpu.roll(g, LANE // 2, 1),))


def _causal_conv(u, w, b):
    width = w.shape[0]
    y = b
    for k in range(width):
        y = y + w[k:k + 1, :] * _time_shift(width - 1 - k)(u)
    return y


def _silu(x):
    return x * jax.nn.sigmoid(x)


def _f_ln(h, mix, g, b):
    x = DN_ALPHA * h + mix
    mu = jnp.mean(x, axis=-1, keepdims=True)
    xc = x - mu
    var = jnp.mean(xc * xc, axis=-1, keepdims=True)
    return (xc * lax.rsqrt(var + LN_EPS) * g + b,)


def _f_convsilu(u, w, b):
    return (_silu(_causal_conv(u, w, b)),)


def _f_convffn(ug, uv, wg, wv, bg, bv):
    return (_silu(_causal_conv(ug, wg, bg)) * _causal_conv(uv, wv, bv),)


def _f_dt(dt_raw, bias, a_log):
    x = dt_raw + bias
    dt = jnp.maximum(x, 0.0) + jnp.log(1.0 + jnp.exp(-jnp.abs(x)))
    return dt, dt * (-jnp.exp(a_log))


def _f_ssd_post(y, xs, z, d_exp, ng):
    t = (y + d_exp * xs) * _silu(z)
    return (t * lax.rsqrt(jnp.mean(t * t, axis=-1, keepdims=True) + LN_EPS) * ng,)


def _f_rms(x, g):
    return (x * lax.rsqrt(jnp.mean(x * x, axis=-1, keepdims=True) + RMS_EPS) * g,)


def _f_rope(x, cos, sin):
    return (x * cos + _rot_half(x) * sin,)


def _f_rope_dup(x, cos, sin):
    r = x * cos + _rot_half(x) * sin
    return (r + _roll_half_lanes(r),)


def _rows(t, cap=512):
    for c in (cap, 256, 128, 64, 32, 16, 8):
        if c <= cap and t % c == 0:
            return c
    return t


class _Blockwise:
    def __init__(self, name, fn, grid, data, params, consts, outs, red_axes):
        self.name, self.fn, self.grid = name, fn, grid
        self.data, self.params, self.consts, self.outs, self.red_axes = data, params, consts, outs, red_axes

    def fwd(self, *out_dtypes):
        out_dtypes = out_dtypes or tuple((F32,) for _ in self.outs)
        return _bw_fwd(self.name + "_fwd", self.fn, self.grid, self.data + self.params + self.consts, self.outs, out_dtypes)

    def bwd(self, cts, grad_dtypes=None):
        cts = [(c, o[1], o[2]) for c, o in zip(cts, self.outs)]
        grad_dtypes = grad_dtypes or [F32] * len(self.data)
        res = _bw_bwd(self.name + "_bwd", self.fn, self.grid, self.data, self.params, self.consts, cts, self.red_axes,
                      grad_dtypes)
        return res[:len(self.data)], res[len(self.data):]


def _op_ln(name, h, mix, g, b):
    t, d = h.shape
    tr = _rows(t, 256)
    row = ((tr, d), lambda i: (i, 0))
    par = ((1, d), lambda i: (0, 0))
    return _Blockwise(name, _f_ln, (t // tr,), [(h, *row), (mix, *row)], [(g, *par), (b, *par)], [],
                      [((t, d), *row)], (0,))


def _op_conv(name, fn, us, ws, bs, nb, seq):
    c = us[0].shape[1]
    ct = _tile(c, 256)
    blk = ((seq, ct), lambda j, bb: (bb, j))
    data = [(u, *blk) for u in us]
    params = [(w, (w.shape[0], ct), lambda j, bb: (0, j)) for w in ws] + [(b, (1, ct), lambda j, bb: (0, j)) for b in bs]
    return _Blockwise(name, fn, (c // ct, nb), data, params, [], [((nb * seq, c), *blk)], (1,))


def _op_dt(name, dt_raw, bias, a_log):
    t = dt_raw.shape[0]
    tr = _rows(t, 1024)
    row = ((tr, LANE), lambda i: (i, 0))
    par = ((1, LANE), lambda i: (0, 0))
    return _Blockwise(name, _f_dt, (t // tr,), [(dt_raw, *row)], [(bias, *par), (a_log, *par)], [],
                      [((t, LANE), *row), ((t, LANE), *row)], (0,))


def _op_ssd_post(name, y, xs, z, d_exp, ng):
    t, c = y.shape
    gw = c // SSD_G
    tr = _rows(t, 512)
    blk = ((tr, gw), lambda g, i: (i, g))
    par = ((1, gw), lambda g, i: (0, g))
    return _Blockwise(name, _f_ssd_post, (SSD_G, t // tr), [(y, *blk), (xs, *blk), (z, *blk)],
                      [(d_exp, *par), (ng, *par)], [], [((t, c), *blk)], (1,))


def _op_rms(name, x, g):
    t, c = x.shape
    tr = _rows(t, 512)
    return _Blockwise(name, _f_rms, (t // tr,), [(x, (tr, c), lambda i: (i, 0))], [(g, (1, c), lambda i: (0, 0))], [],
                      [((t, c), (tr, c), lambda i: (i, 0))], (0,))


def _op_rope(name, fn, x, cos, sin):
    t, c = x.shape
    tr = _rows(t, 512)
    blk = ((tr, LANE), lambda i, j: (i, j))
    cs = ((tr, LANE), lambda i, j: (i, 0))
    return _Blockwise(name, fn, (t // tr, c // LANE), [(x, *blk)], [], [(cos, *cs), (sin, *cs)], [((t, c), *blk)], ())


def _ssd_chunk(x, bc, cc, s, a_mat, dt_mat):
    n = a_mat.shape[0]
    row = lax.broadcasted_iota(jnp.int32, (n, n), 0)
    col = lax.broadcasted_iota(jnp.int32, (n, n), 1)
    upper = (row <= col).astype(F32)
    cumrow = jnp.dot(a_mat, upper, precision=HIGHEST, preferred_element_type=F32)
    cumcol = cumrow.T
    decay = jnp.exp(jnp.where(row >= col, cumcol - cumrow, -jnp.inf))
    g = lax.dot_general(cc.astype(BF16), bc.astype(BF16), (((1,), (1,)), ((), ())), preferred_element_type=F32)
    w = g * decay * dt_mat
    y = jnp.dot(w.astype(BF16), x.astype(BF16), preferred_element_type=F32)
    y = y + lax.dot_general((cc * jnp.exp(cumcol)).astype(BF16), s.astype(BF16), (((1,), (1,)), ((), ())),
                            preferred_element_type=F32)
    cum_last = jnp.sum(a_mat[0:1, :], axis=1, keepdims=True)
    w_end = jnp.exp(cum_last - cumcol) * dt_mat.T
    s_new = s * jnp.exp(cum_last) + lax.dot_general(x.astype(BF16), (bc * w_end).astype(BF16),
                                                    (((0,), (0,)), ((), ())), preferred_element_type=F32)
    return y, s_new


SSD_KPAD = 8


def _ssd_scan_fwd(xs, bm, cm, dt_t, a_t, nb, seq):
    nc = seq // SSD_L
    kp = SSD_K * SSD_P

    def body(xs_ref, b_ref, c_ref, dt_ref, a_ref, y_ref, st_ref):
        def chunk(c, states):
            sl = pl.ds(pl.multiple_of(c * SSD_L, SSD_L), SSD_L)
            bc, cc = b_ref[sl, :], c_ref[sl, :]
            new = []
            for k in range(SSD_K):
                hs = pl.ds(k * SSD_P, SSD_P)
                a_mat = jnp.broadcast_to(a_ref[0, k:k + 1, sl], (SSD_L, SSD_L))
                dt_mat = jnp.broadcast_to(dt_ref[0, k:k + 1, sl], (SSD_L, SSD_L))
                st_ref[0, c * SSD_K + k] = states[k]
                y, s_new = _ssd_chunk(xs_ref[sl, hs], bc, cc, states[k], a_mat, dt_mat)
                y_ref[sl, hs] = y
                new.append(s_new)
            return tuple(new)

        lax.fori_loop(0, nc, chunk, tuple(jnp.zeros((SSD_P, SSD_N), F32) for _ in range(SSD_K)))

    t = xs.shape[0]
    h_spec = pl.BlockSpec((1, SSD_KPAD, seq), lambda b, g: (b * SSD_G + g, 0, 0))
    return pl.pallas_call(
        body, name="ssd_scan_fwd", grid=(nb, SSD_G),
        in_specs=[pl.BlockSpec((seq, kp), lambda b, g: (b, g)),
                  pl.BlockSpec((seq, SSD_N), lambda b, g: (b, g)),
                  pl.BlockSpec((seq, SSD_N), lambda b, g: (b, g)), h_spec, h_spec],
        out_specs=[pl.BlockSpec((seq, kp), lambda b, g: (b, g)),
                   pl.BlockSpec((1, nc * SSD_K, SSD_P, SSD_N), lambda b, g: (b * SSD_G + g, 0, 0, 0))],
        out_shape=[jax.ShapeDtypeStruct((t, SSD_D_INNER), F32),
                   jax.ShapeDtypeStruct((nb * SSD_G, nc * SSD_K, SSD_P, SSD_N), F32)],
        compiler_params=_cparams(),
    )(xs, bm, cm, dt_t, a_t)


def _ssd_scan_bwd(xs, bm, cm, dt_t, a_t, states, dy, nb, seq):
    nc = seq // SSD_L
    kp = SSD_K * SSD_P

    def body(xs_ref, b_ref, c_ref, dt_ref, a_ref, st_ref, dy_ref, dxs_ref, db_ref, dc_ref, ddt_ref, da_ref):
        def chunk(i, dstates):
            c = nc - 1 - i
            sl = pl.ds(pl.multiple_of(c * SSD_L, SSD_L), SSD_L)
            bc, cc = b_ref[sl, :], c_ref[sl, :]
            db = jnp.zeros((SSD_L, SSD_N), F32)
            dc = jnp.zeros((SSD_L, SSD_N), F32)
            new = []
            for k in range(SSD_K):
                hs = pl.ds(k * SSD_P, SSD_P)
                a_mat = jnp.broadcast_to(a_ref[0, k:k + 1, sl], (SSD_L, SSD_L))
                dt_mat = jnp.broadcast_to(dt_ref[0, k:k + 1, sl], (SSD_L, SSD_L))
                _, vjp = jax.vjp(_ssd_chunk, xs_ref[sl, hs], bc, cc, st_ref[0, c * SSD_K + k], a_mat, dt_mat)
                dx, dbk, dck, ds, da_mat, ddt_mat = vjp((dy_ref[sl, hs], dstates[k]))
                dxs_ref[sl, hs] = dx
                db, dc = db + dbk, dc + dck
                da_ref[0, k:k + 1, sl] = jnp.sum(da_mat, axis=0, keepdims=True)
                ddt_ref[0, k:k + 1, sl] = jnp.sum(ddt_mat, axis=0, keepdims=True)
                new.append(ds)
            pad = jnp.zeros((SSD_KPAD - SSD_K, SSD_L), F32)
            da_ref[0, SSD_K:SSD_KPAD, sl] = pad
            ddt_ref[0, SSD_K:SSD_KPAD, sl] = pad
            db_ref[sl, :] = db
            dc_ref[sl, :] = dc
            return tuple(new)

        lax.fori_loop(0, nc, chunk, tuple(jnp.zeros((SSD_P, SSD_N), F32) for _ in range(SSD_K)))

    t = xs.shape[0]
    x_spec = pl.BlockSpec((seq, kp), lambda b, g: (b, g))
    n_spec = pl.BlockSpec((seq, SSD_N), lambda b, g: (b, g))
    h_spec = pl.BlockSpec((1, SSD_KPAD, seq), lambda b, g: (b * SSD_G + g, 0, 0))
    h_shape = jax.ShapeDtypeStruct((nb * SSD_G, SSD_KPAD, seq), F32)
    return pl.pallas_call(
        body, name="ssd_scan_bwd", grid=(nb, SSD_G),
        in_specs=[x_spec, n_spec, n_spec, h_spec, h_spec,
                  pl.BlockSpec((1, nc * SSD_K, SSD_P, SSD_N), lambda b, g: (b * SSD_G + g, 0, 0, 0)), x_spec],
        out_specs=[x_spec, n_spec, n_spec, h_spec, h_spec],
        out_shape=[jax.ShapeDtypeStruct((t, SSD_D_INNER), F32), jax.ShapeDtypeStruct((t, SSD_GN), F32),
                   jax.ShapeDtypeStruct((t, SSD_GN), F32), h_shape, h_shape],
        compiler_params=_cparams(),
    )(xs, bm, cm, dt_t, a_t, states, dy)


ATT_TQ = 256
ATT_SCALE = (MLA_NOPE + MLA_ROPE) ** -0.5


def _attn_pair(qn, qr, kn, kr, v, q0):
    tq, s = qn.shape[0], kn.shape[0]
    q_idx = q0 + lax.broadcasted_iota(jnp.int32, (tq, s), 0)
    k_idx = lax.broadcasted_iota(jnp.int32, (tq, s), 1)
    lane = lax.broadcasted_iota(jnp.int32, qr.shape, 1)
    nt = (((1,), (1,)), ((), ()))
    krb = kr.astype(BF16)
    outs = []
    for j in range(2):
        hs = slice(j * MLA_NOPE, (j + 1) * MLA_NOPE)
        qrj = jnp.where((lane >= MLA_ROPE) == (j == 1), qr, jnp.zeros_like(qr))
        sc = lax.dot_general(qn[:, hs].astype(BF16), kn[:, hs].astype(BF16), nt, preferred_element_type=F32)
        sc = sc + lax.dot_general(qrj.astype(BF16), krb, nt, preferred_element_type=F32)
        sc = jnp.where(k_idx <= q_idx, sc * ATT_SCALE, -jnp.inf)
        m = jnp.max(sc, axis=-1, keepdims=True)
        e = jnp.exp(sc - lax.stop_gradient(m))
        p = e / jnp.sum(e, axis=-1, keepdims=True)
        outs.append(jnp.dot(p.astype(BF16), v[:, hs].astype(BF16), preferred_element_type=F32))
    return jnp.concatenate(outs, axis=1)


def _attn_specs(nb, seq):
    nq = seq // ATT_TQ
    qn = pl.BlockSpec((ATT_TQ, 2 * MLA_NOPE), lambda b, hp, qi: (b * nq + qi, hp))
    qr = pl.BlockSpec((ATT_TQ, LANE), lambda b, hp, qi: (b * nq + qi, hp))
    kn = pl.BlockSpec((seq, 2 * MLA_NOPE), lambda b, hp, qi: (b, hp))
    kr = pl.BlockSpec((seq, LANE), lambda b, hp, qi: (b, 0))
    return (nb, MLA_H // 2, nq), qn, qr, kn, kr


def _attn_fwd(qn, qr, kn, kr, v, nb, seq):
    grid, s_qn, s_qr, s_kn, s_kr = _attn_specs(nb, seq)

    def body(qn_ref, qr_ref, kn_ref, kr_ref, v_ref, o_ref):
        q0 = pl.program_id(2) * ATT_TQ
        o_ref[...] = _attn_pair(qn_ref[...], qr_ref[...], kn_ref[...], kr_ref[...], v_ref[...], q0).astype(o_ref.dtype)

    return pl.pallas_call(
        body, name="attn_fwd", grid=grid, in_specs=[s_qn, s_qr, s_kn, s_kr, s_kn], out_specs=s_qn,
        out_shape=jax.ShapeDtypeStruct(qn.shape, BF16), compiler_params=_cparams(),
    )(qn, qr, kn, kr, v)


def _attn_bwd(qn, qr, kn, kr, v, do, nb, seq):
    grid, s_qn, s_qr, s_kn, s_kr = _attn_specs(nb, seq)

    def body(qn_ref, qr_ref, kn_ref, kr_ref, v_ref, do_ref, dqn_ref, dqr_ref, dkn_ref, dkr_ref, dv_ref):
        hp, qi = pl.program_id(1), pl.program_id(2)
        q0 = qi * ATT_TQ
        _, vjp = jax.vjp(lambda a, b, c, d, e: _attn_pair(a, b, c, d, e, q0),
                         *[r[...].astype(F32) for r in (qn_ref, qr_ref, kn_ref, kr_ref, v_ref)])
        dqn, dqr, dkn, dkr, dv = vjp(do_ref[...].astype(F32))
        dqn_ref[...] = dqn.astype(dqn_ref.dtype)
        dqr_ref[...] = dqr

        @pl.when(qi == 0)
        def _():
            dkn_ref[...] = jnp.zeros_like(dkn_ref)
            dv_ref[...] = jnp.zeros_like(dv_ref)

        @pl.when(jnp.logical_and(qi == 0, hp == 0))
        def _():
            dkr_ref[...] = jnp.zeros_like(dkr_ref)

        dkn_ref[...] += dkn
        dv_ref[...] += dv
        dkr_ref[...] += dkr

    return pl.pallas_call(
        body, name="attn_bwd", grid=grid, in_specs=[s_qn, s_qr, s_kn, s_kr, s_kn, s_qn],
        out_specs=[s_qn, s_qr, s_kn, s_kr, s_kn],
        out_shape=[jax.ShapeDtypeStruct(qn.shape, BF16)] + [jax.ShapeDtypeStruct(a.shape, F32) for a in (qr, kn, kr, v)],
        compiler_params=_cparams(),
    )(qn, qr, kn, kr, v, do)


def _loss_head(y, target):
    t, d = y.shape
    tr = _rows(t, 512)

    def body(y_ref, t_ref, l_ref, dy_ref):
        err = y_ref[...] - t_ref[...]
        dy_ref[...] = err * (1.0 / d)
        part = 0.5 * jnp.sum(jnp.sum(err * err, axis=1, keepdims=True), axis=0, keepdims=True) * (1.0 / d)
        l_ref[...] = jnp.broadcast_to(part, l_ref.shape)

    row = pl.BlockSpec((tr, d), lambda i: (i, 0))
    parts, dy = pl.pallas_call(
        body, name="loss_head", grid=(t // tr,), in_specs=[row, row],
        out_specs=[pl.BlockSpec((8, LANE), lambda i: (i, 0)), row],
        out_shape=[jax.ShapeDtypeStruct((8 * (t // tr), LANE), F32), jax.ShapeDtypeStruct((t, d), F32)],
        compiler_params=_cparams(),
    )(y, target)
    return parts, dy


def _sum_parts(stack, name):
    n, r, c = stack.shape
    tr = _rows(r, 512)

    def body(s_ref, o_ref):
        acc = s_ref[0].astype(F32)
        for i in range(1, n):
            acc = acc + s_ref[i].astype(F32)
        o_ref[...] = acc

    return pl.pallas_call(
        body, name=name, grid=(r // tr,), in_specs=[pl.BlockSpec((n, tr, c), lambda i: (0, i, 0))],
        out_specs=pl.BlockSpec((tr, c), lambda i: (i, 0)), out_shape=jax.ShapeDtypeStruct((r, c), F32),
        compiler_params=_cparams(),
    )(stack)


def _adamw(w, g_mine, g_other, m, v, name):
    r, c = w.shape
    tr = _rows(r, 256)
    bc1 = 1.0 / (1.0 - ADAM_B1 ** ADAM_STEP)
    bc2 = 1.0 / (1.0 - ADAM_B2 ** ADAM_STEP)
    two = g_other is not None

    def body(*refs):
        if two:
            w_ref, g_ref, g2_ref, m_ref, v_ref, go_ref, d_ref, mo_ref, vo_ref = refs
            g = g_ref[...] + g2_ref[...]
        else:
            w_ref, g_ref, m_ref, v_ref, go_ref, d_ref, mo_ref, vo_ref = refs
            g = g_ref[...]
        mn = ADAM_B1 * m_ref[...] + (1.0 - ADAM_B1) * g
        vn = ADAM_B2 * v_ref[...] + (1.0 - ADAM_B2) * (g * g)
        go_ref[...] = g
        mo_ref[...] = mn
        vo_ref[...] = vn
        d_ref[...] = -ADAM_LR * ((mn * bc1) / (jnp.sqrt(vn * bc2) + ADAM_EPS) + ADAM_WD * w_ref[...])

    blk = pl.BlockSpec((tr, c), lambda i: (i, 0))
    ins = [w, g_mine] + ([g_other] if two else []) + [m, v]
    return pl.pallas_call(
        body, name=name, grid=(r // tr,), in_specs=[blk] * len(ins), out_specs=[blk] * 4,
        out_shape=[jax.ShapeDtypeStruct((r, c), F32)] * 4, compiler_params=_cparams(),
    )(*ins)


def _chip_peers():
    x, y, c = lax.axis_index("x"), lax.axis_index("y"), lax.axis_index("c")
    return (x, y, c), [(1 - x, y), (x, 1 - y), (1 - x, 1 - y)]


def _gather_chips(shards):
    n = len(shards)

    def body(*refs):
        ins, outs = refs[:n], refs[n:2 * n]
        send_sems, recv_sems, local_sems = refs[2 * n:]
        (x, y, c), chips = _chip_peers()
        me = 2 * x + y
        copies = []
        for i in range(n):
            loc = pltpu.make_async_copy(ins[i], outs[i].at[me], local_sems.at[i])
            loc.start()
            copies.append(loc)
        rem = []
        for i in range(n):
            for j, (px, py) in enumerate(chips):
                cp = pltpu.make_async_remote_copy(src_ref=ins[i], dst_ref=outs[i].at[me], send_sem=send_sems.at[i, j],
                                                  recv_sem=recv_sems.at[i, j], device_id=(px, py, c), device_id_type=MESH)
                cp.start()
                rem.append(cp)
        for i in range(n):
            for j, (px, py) in enumerate(chips):
                pltpu.make_async_remote_copy(src_ref=ins[i], dst_ref=outs[i].at[2 * px + py], send_sem=send_sems.at[i, j],
                                             recv_sem=recv_sems.at[i, j], device_id=(px, py, c), device_id_type=MESH).wait_recv()
        for cp in rem:
            cp.wait_send()
        for cp in copies:
            cp.wait()

    any_spec = pl.BlockSpec(memory_space=pl.ANY)
    return pl.pallas_call(
        body, name="gather_chips", in_specs=[any_spec] * n, out_specs=[any_spec] * n,
        out_shape=[jax.ShapeDtypeStruct((N_CHIPS,) + s.shape, s.dtype) for s in shards],
        scratch_shapes=[pltpu.SemaphoreType.DMA((n, 3)), pltpu.SemaphoreType.DMA((n, 3)), pltpu.SemaphoreType.DMA((n,))],
    )(*shards)


def _scatter_chips(stacks):
    n = len(stacks)

    def body(*refs):
        ins, outs = refs[:n], refs[n:2 * n]
        send_sems, recv_sems, local_sems = refs[2 * n:]
        (x, y, c), chips = _chip_peers()
        me = 2 * x + y
        copies = []
        for i in range(n):
            loc = pltpu.make_async_copy(ins[i].at[me], outs[i].at[me], local_sems.at[i])
            loc.start()
            copies.append(loc)
        rem = []
        for i in range(n):
            for j, (px, py) in enumerate(chips):
                cp = pltpu.make_async_remote_copy(src_ref=ins[i].at[2 * px + py], dst_ref=outs[i].at[me],
                                                  send_sem=send_sems.at[i, j], recv_sem=recv_sems.at[i, j],
                                                  device_id=(px, py, c), device_id_type=MESH)
                cp.start()
                rem.append(cp)
        for i in range(n):
            for j, (px, py) in enumerate(chips):
                pltpu.make_async_remote_copy(src_ref=ins[i].at[me], dst_ref=outs[i].at[2 * px + py],
                                             send_sem=send_sems.at[i, j], recv_sem=recv_sems.at[i, j],
                                             device_id=(px, py, c), device_id_type=MESH).wait_recv()
        for cp in rem:
            cp.wait_send()
        for cp in copies:
            cp.wait()

    any_spec = pl.BlockSpec(memory_space=pl.ANY)
    return pl.pallas_call(
        body, name="scatter_chips", in_specs=[any_spec] * n, out_specs=[any_spec] * n,
        out_shape=[jax.ShapeDtypeStruct(s.shape, s.dtype) for s in stacks],
        scratch_shapes=[pltpu.SemaphoreType.DMA((n, 3)), pltpu.SemaphoreType.DMA((n, 3)), pltpu.SemaphoreType.DMA((n,))],
    )(*stacks)


def _swap_cores(arrs):
    n = len(arrs)

    def body(*refs):
        ins, outs = refs[:n], refs[n:2 * n]
        send_sems, recv_sems = refs[2 * n:]
        x, y, c = lax.axis_index("x"), lax.axis_index("y"), lax.axis_index("c")
        cps = []
        for i in range(n):
            cp = pltpu.make_async_remote_copy(src_ref=ins[i], dst_ref=outs[i], send_sem=send_sems.at[i],
                                              recv_sem=recv_sems.at[i], device_id=(x, y, 1 - c), device_id_type=MESH)
            cp.start()
            cps.append(cp)
        for cp in cps:
            cp.wait()

    any_spec = pl.BlockSpec(memory_space=pl.ANY)
    return pl.pallas_call(
        body, name="swap_cores", in_specs=[any_spec] * n, out_specs=[any_spec] * n,
        out_shape=[jax.ShapeDtypeStruct(s.shape, s.dtype) for s in arrs],
        scratch_shapes=[pltpu.SemaphoreType.DMA((n,)), pltpu.SemaphoreType.DMA((n,))],
    )(*arrs)


def _gather_all(block):
    m_per, n = block.shape

    def body(x_ref, out_ref, send_sems, recv_sems, local_sem):
        x, y, c = lax.axis_index("x"), lax.axis_index("y"), lax.axis_index("c")
        me, sibling = (x, y, c), (x, y, 1 - c)
        chips = [(1 - x, y), (x, 1 - y), (1 - x, 1 - y)]

        def rows(px, py, pc):
            return out_ref.at[pl.ds((4 * px + 2 * py + pc) * m_per, m_per), :]

        def copy(k, blk, to, src=None):
            return pltpu.make_async_remote_copy(src_ref=rows(*blk) if src is None else src, dst_ref=rows(*blk),
                                                send_sem=send_sems.at[k], recv_sem=recv_sems.at[k], device_id=to,
                                                device_id_type=MESH)

        mine = pltpu.make_async_copy(x_ref, rows(*me), local_sem)
        mine.start()
        first = [copy(0, me, sibling, src=x_ref)]
        first += [copy(1 + j, me, (*chip, c), src=x_ref) for j, chip in enumerate(chips)]
        for cp in first:
            cp.start()
        passed = [copy(4 + j, (*chip, c), sibling) for j, chip in enumerate(chips)]
        for j, chip in enumerate(chips):
            copy(1 + j, (*chip, c), me).wait_recv()
            passed[j].start()
        copy(0, sibling, me).wait_recv()
        for j, chip in enumerate(chips):
            copy(4 + j, (*chip, 1 - c), me).wait_recv()
        for cp in first + passed:
            cp.wait_send()
        mine.wait()

    return pl.pallas_call(
        body, name="gather_all", out_shape=jax.ShapeDtypeStruct((N_DEV * m_per, n), block.dtype),
        in_specs=[pl.BlockSpec(memory_space=pltpu.VMEM)], out_specs=pl.BlockSpec(memory_space=pltpu.VMEM),
        scratch_shapes=[pltpu.SemaphoreType.DMA((7,)), pltpu.SemaphoreType.DMA((7,)), pltpu.SemaphoreType.DMA],
    )(block)


def _rope_tables(positions):
    inv_freq = 1.0 / (ROPE_THETA ** (jnp.arange(0, MLA_ROPE, 2, dtype=F32) / MLA_ROPE))
    ang = positions.astype(F32).reshape(-1, 1) * inv_freq
    return jnp.tile(jnp.cos(ang), (1, 4)), jnp.tile(jnp.sin(ang), (1, 4))


def _pad_cols(a, n):
    return jnp.pad(a, ((0, 0), (0, n - a.shape[1])))


def _ffn_fwd(l, h, w, nb, seq):
    ug = _mm(h, w["ffn_up_g"][l], name=f"ffn{l}_up_g")
    uv = _mm(h, w["ffn_up_v"][l], name=f"ffn{l}_up_v")
    op = _op_conv(f"ffn{l}_conv", _f_convffn, [ug, uv], [w["ffn_cw_g"][l], w["ffn_cw_v"][l]],
                  [w["ffn_cb_g"][l], w["ffn_cb_v"][l]], nb, seq)
    act = op.fwd((BF16,))[0]
    ff = _mm(act, w["ffn_down"][l], name=f"ffn{l}_down")
    return ff, (h, op, act)


def _ffn_bwd(l, saved, dff, w, grads):
    h, op, act = saved
    grads[f"ffn_down{l}"] = _mm(act, dff, ta=True, name=f"ffn{l}_down_dw")
    dact = _mm(dff, w["ffn_down"][l], tb=True, name=f"ffn{l}_down_dx")
    (dug, duv), (dwg, dwv, dbg, dbv) = op.bwd([dact], [BF16, BF16])
    grads[f"ffn_up_g{l}"] = _mm(h, dug, ta=True, name=f"ffn{l}_up_g_dw")
    grads[f"ffn_up_v{l}"] = _mm(h, duv, ta=True, name=f"ffn{l}_up_v_dw")
    grads[f"ffn_cw_g{l}"], grads[f"ffn_cw_v{l}"], grads[f"ffn_cb_g{l}"], grads[f"ffn_cb_v{l}"] = dwg, dwv, dbg, dbv
    dh = _mm(dug, w["ffn_up_g"][l], tb=True, name=f"ffn{l}_up_g_dx")
    return _mm(duv, w["ffn_up_v"][l], tb=True, add=dh, name=f"ffn{l}_up_v_dx")


def _local_step(x, positions, target, w):
    nb, seq, d = x.shape
    t = nb * seq
    x2, tgt2 = x.reshape(t, d), target.reshape(t, d)
    cos, sin = _rope_tables(positions)
    grads = {}

    xb = x2.astype(BF16)

    z = _mm(xb, w["in_z"], name="ssd_in_z")
    raw = [_mm(xb, w[k], name="ssd_" + k) for k in ("in_x", "in_b", "in_c")]
    dt_raw = _mm(xb, w["in_dt"], name="ssd_in_dt")
    conv_ops = [_op_conv("ssd_conv_" + s, _f_convsilu, [r], [w["conv_w_" + s]], [w["conv_b_" + s]], nb, seq)
                for s, r in zip("xbc", raw)]
    xs, bm, cm = [op.fwd()[0] for op in conv_ops]
    dt_op = _op_dt("ssd_dt", dt_raw, w["dt_bias"], w["a_log"])
    dt, a = dt_op.fwd()

    def to_heads(v):
        v = jnp.swapaxes(v.reshape(nb, seq, LANE), 1, 2)[:, :SSD_H, :].reshape(nb * SSD_G, SSD_K, seq)
        return jnp.pad(v, ((0, 0), (0, SSD_KPAD - SSD_K), (0, 0)))

    def from_heads(v):
        v = v[:, :SSD_K, :].reshape(nb, SSD_H, seq)
        return jnp.swapaxes(jnp.pad(v, ((0, 0), (0, LANE - SSD_H), (0, 0))), 1, 2).reshape(t, LANE)

    dt_t, a_t = to_heads(dt), to_heads(a)
    y, states = _ssd_scan_fwd(xs, bm, cm, dt_t, a_t, nb, seq)
    post_op = _op_ssd_post("ssd_post", y, xs, z, w["d_exp"], w["norm_g"])
    yn = post_op.fwd((BF16,))[0]
    mix0 = _mm(yn, w["ssd_out"], name="ssd_out")
    ln0m = _op_ln("ln_mix0", x2, mix0, w["ln_mix_g"][0], w["ln_mix_b"][0])
    h0a, h0a_b = ln0m.fwd((F32, BF16))
    ff0, ffn0_saved = _ffn_fwd(0, h0a_b, w, nb, seq)
    ln0f = _op_ln("ln_ffn0", h0a, ff0, w["ln_ffn_g"][0], w["ln_ffn_b"][0])
    h1, h1_b = ln0f.fwd((F32, BF16))

    ckv = _mm(h1_b, w["kv_down_c"], name="kv_down_c")
    kr_in = _mm(h1_b, w["kv_down_r"], name="kv_down_r")
    kvn_op = _op_rms("kv_norm", ckv, w["kv_norm_g"])
    ckvn = kvn_op.fwd((BF16,))[0]
    kr_op = _op_rope("k_rope", _f_rope_dup, kr_in, cos, sin)
    kr = kr_op.fwd((BF16,))[0]
    kn = _mm(ckvn, w["kv_up_k"], out_dtype=BF16, name="kv_up_k")
    v = _mm(ckvn, w["kv_up_v"], out_dtype=BF16, name="kv_up_v")
    cq_raw = _mm(h1_b, w["q_down"], name="q_down")
    qn_op = _op_rms("q_norm", cq_raw, w["q_norm_g"])
    cq = qn_op.fwd((BF16,))[0]
    qn = _mm(cq, w["q_up_n"], out_dtype=BF16, name="q_up_n")
    qr_raw = _mm(cq, w["q_up_r"], name="q_up_r")
    qr_op = _op_rope("q_rope", _f_rope, qr_raw, cos, sin)
    qr = qr_op.fwd((BF16,))[0]
    o = _attn_fwd(qn, qr, kn, kr, v, nb, seq)
    mix1 = _mm(o, w["attn_out"], name="attn_out")
    ln1m = _op_ln("ln_mix1", h1, mix1, w["ln_mix_g"][1], w["ln_mix_b"][1])
    h1a, h1a_b = ln1m.fwd((F32, BF16))
    ff1, ffn1_saved = _ffn_fwd(1, h1a_b, w, nb, seq)
    ln1f = _op_ln("ln_ffn1", h1a, ff1, w["ln_ffn_g"][1], w["ln_ffn_b"][1])
    h2 = ln1f.fwd()[0]

    loss_parts, dh2 = _loss_head(h2, tgt2)

    (dh1a, dff1), (grads["ln_ffn_g1"], grads["ln_ffn_b1"]) = ln1f.bwd([dh2], [F32, BF16])
    dh1a = _add2(dh1a, _ffn_bwd(1, ffn1_saved, dff1, w, grads), "add_h1a")
    (dh1, dmix1), (grads["ln_mix_g1"], grads["ln_mix_b1"]) = ln1m.bwd([dh1a], [F32, BF16])
    grads["attn_out"] = _mm(o, dmix1, ta=True, name="attn_out_dw")
    do = _mm(dmix1, w["attn_out"], tb=True, name="attn_out_dx")
    dqn, dqr, dkn, dkr, dv = _attn_bwd(qn, qr, kn, kr, v, do, nb, seq)
    (dqr_raw,), _ = qr_op.bwd([dqr], [BF16])
    grads["q_up_n"] = _mm(cq, dqn, ta=True, name="q_up_n_dw")
    grads["q_up_r"] = _mm(cq, dqr_raw, ta=True, name="q_up_r_dw")
    dcq = _mm(dqn, w["q_up_n"], tb=True, name="q_up_n_dx")
    dcq = _mm(dqr_raw, w["q_up_r"], tb=True, add=dcq, name="q_up_r_dx")
    (dcq_raw,), (grads["q_norm_g"],) = qn_op.bwd([dcq], [BF16])
    grads["q_down"] = _mm(h1_b, dcq_raw, ta=True, name="q_down_dw")
    dh1 = _mm(dcq_raw, w["q_down"], tb=True, add=dh1, name="q_down_dx")
    grads["kv_up_k"] = _mm(ckvn, dkn, ta=True, name="kv_up_k_dw")
    grads["kv_up_v"] = _mm(ckvn, dv, ta=True, name="kv_up_v_dw")
    dckvn = _mm(dkn, w["kv_up_k"], tb=True, name="kv_up_k_dx")
    dckvn = _mm(dv, w["kv_up_v"], tb=True, add=dckvn, name="kv_up_v_dx")
    (dckv,), (grads["kv_norm_g"],) = kvn_op.bwd([dckvn], [BF16])
    (dkr_in,), _ = kr_op.bwd([dkr], [BF16])
    grads["kv_down_c"] = _mm(h1_b, dckv, ta=True, name="kv_down_c_dw")
    grads["kv_down_r"] = _mm(h1_b, dkr_in, ta=True, name="kv_down_r_dw")
    dh1 = _mm(dckv, w["kv_down_c"], tb=True, add=dh1, name="kv_down_c_dx")
    dh1 = _mm(dkr_in, w["kv_down_r"], tb=True, add=dh1, name="kv_down_r_dx")

    (dh0a, dff0), (grads["ln_ffn_g0"], grads["ln_ffn_b0"]) = ln0f.bwd([dh1], [F32, BF16])
    dh0a = _add2(dh0a, _ffn_bwd(0, ffn0_saved, dff0, w, grads), "add_h0a")
    (dx, dmix0), (grads["ln_mix_g0"], grads["ln_mix_b0"]) = ln0m.bwd([dh0a], [F32, BF16])
    grads["ssd_out"] = _mm(yn, dmix0, ta=True, name="ssd_out_dw")
    dyn = _mm(dmix0, w["ssd_out"], tb=True, name="ssd_out_dx")
    (dy, dxs_post, dz), (grads["d_exp"], grads["norm_g"]) = post_op.bwd([dyn], [F32, F32, BF16])
    dxs, dbm, dcm, ddt_t, da_t = _ssd_scan_bwd(xs, bm, cm, dt_t, a_t, states, dy, nb, seq)
    dxs = _add2(dxs, dxs_post, "add_dxs")
    (ddt_raw,), (grads["dt_bias"], grads["a_log"]) = dt_op.bwd([from_heads(ddt_t), from_heads(da_t)], [BF16])
    draws = []
    for s, op, dout in zip("xbc", conv_ops, (dxs, dbm, dcm)):
        (dr,), (grads["conv_w_" + s], grads["conv_b_" + s]) = op.bwd([dout], [BF16])
        draws.append(dr)
    for k, dr in zip(("in_x", "in_b", "in_c"), draws):
        grads[k] = _mm(xb, dr, ta=True, name=f"ssd_{k}_dw")
        dx = _mm(dr, w[k], tb=True, add=dx, name=f"ssd_{k}_dx")
    grads["in_z"] = _mm(xb, dz, ta=True, name="ssd_in_z_dw")
    grads["in_dt"] = _mm(xb, ddt_raw, ta=True, name="ssd_in_dt_dw")
    dx = _mm(dz, w["in_z"], tb=True, add=dx, name="ssd_in_z_dx")
    dx = _mm(ddt_raw, w["in_dt"], tb=True, add=dx, name="ssd_in_dt_dx")
    return loss_parts, dx.reshape(nb, seq, d), grads


def _add2(a, b, name):
    t, c = a.shape
    tr = _rows(t, 512)

    def body(a_ref, b_ref, o_ref):
        o_ref[...] = a_ref[...] + b_ref[...]

    blk = pl.BlockSpec((tr, c), lambda i: (i, 0))
    return pl.pallas_call(body, name=name, grid=(t // tr,), in_specs=[blk, blk], out_specs=blk,
                          out_shape=jax.ShapeDtypeStruct((t, c), F32), compiler_params=_cparams())(a, b)


_XE, _BE, _CE = SSD_D_INNER, SSD_D_INNER + SSD_GN, SSD_D_INNER + 2 * SSD_GN


def _prep_weights(fw):
    w = {}
    ip = fw["ssd_in_proj"][0]
    o = SSD_D_INNER
    w["in_z"], w["in_x"], w["in_b"], w["in_c"] = ip[:, :o], ip[:, o:o + _XE], ip[:, o + _XE:o + _BE], ip[:, o + _BE:o + _CE]
    w["in_dt"] = _pad_cols(ip[:, o + _CE:], LANE)
    cw, cb = fw["ssd_conv_w"][0], fw["ssd_conv_b"]
    for s, (lo, hi) in zip("xbc", ((0, _XE), (_XE, _BE), (_BE, _CE))):
        w["conv_w_" + s], w["conv_b_" + s] = cw[:, lo:hi], cb[:, lo:hi]
    w["dt_bias"], w["a_log"] = _pad_cols(fw["ssd_dt_bias"], LANE), _pad_cols(fw["ssd_A_log"], LANE)
    w["d_exp"] = jnp.repeat(fw["ssd_D"][0], SSD_P)[None, :]
    w["norm_g"], w["ssd_out"] = fw["ssd_norm_g"], fw["ssd_out_proj"][0]
    kd = fw["kv_down_proj"]
    w["kv_down_c"], w["kv_down_r"] = kd[:, :MLA_KV_RANK], _pad_cols(kd[:, MLA_KV_RANK:], LANE)
    w["kv_norm_g"] = fw["kv_norm_g"][None, :]
    w["kv_up_k"], w["kv_up_v"] = fw["kv_up_k"], fw["kv_up_v"]
    w["q_down"], w["q_norm_g"] = fw["q_down_proj"][0], fw["q_norm_g"]
    qu = fw["q_up_proj"][0].reshape(MLA_Q_RANK, MLA_H, MLA_NOPE + MLA_ROPE)
    w["q_up_n"] = qu[:, :, :MLA_NOPE].reshape(MLA_Q_RANK, MLA_H * MLA_NOPE)
    w["q_up_r"] = qu[:, :, MLA_NOPE:].reshape(MLA_Q_RANK, MLA_H * MLA_ROPE)
    w["attn_out"] = fw["attn_out_proj"][0]
    f = FFN_HIDDEN
    w["ffn_up_g"] = [fw["ffn_up"][l][:, :f] for l in range(DEPTH)]
    w["ffn_up_v"] = [fw["ffn_up"][l][:, f:] for l in range(DEPTH)]
    w["ffn_cw_g"] = [fw["ffn_conv_w"][l][:, :f] for l in range(DEPTH)]
    w["ffn_cw_v"] = [fw["ffn_conv_w"][l][:, f:] for l in range(DEPTH)]
    w["ffn_cb_g"] = [fw["ffn_conv_b"][l:l + 1, :f] for l in range(DEPTH)]
    w["ffn_cb_v"] = [fw["ffn_conv_b"][l:l + 1, f:] for l in range(DEPTH)]
    w["ffn_down"] = [fw["ffn_down"][l] for l in range(DEPTH)]
    for k in ("ln_mix_g", "ln_mix_b", "ln_ffn_g", "ln_ffn_b"):
        w[k] = [fw[k][l:l + 1] for l in range(DEPTH)]
    return w


def _assemble_grads(g):
    r = {}
    r["ssd_in_proj"] = jnp.concatenate([g["in_z"], g["in_x"], g["in_b"], g["in_c"], g["in_dt"][:, :SSD_H]], axis=1)[None]
    r["ssd_conv_w"] = jnp.concatenate([g["conv_w_" + s] for s in "xbc"], axis=1)[None]
    r["ssd_conv_b"] = jnp.concatenate([g["conv_b_" + s] for s in "xbc"], axis=1)
    r["ssd_dt_bias"], r["ssd_A_log"] = g["dt_bias"][:, :SSD_H], g["a_log"][:, :SSD_H]
    r["ssd_D"] = jnp.sum(g["d_exp"].reshape(SSD_H, SSD_P), axis=1)[None, :]
    r["ssd_norm_g"], r["ssd_out_proj"] = g["norm_g"], g["ssd_out"][None]
    r["kv_down_proj"] = jnp.concatenate([g["kv_down_c"], g["kv_down_r"][:, :MLA_ROPE]], axis=1)
    r["kv_norm_g"] = g["kv_norm_g"][0]
    r["kv_up_k"], r["kv_up_v"] = g["kv_up_k"], g["kv_up_v"]
    r["q_down_proj"], r["q_norm_g"] = g["q_down"][None], g["q_norm_g"]
    r["q_up_proj"] = jnp.concatenate([g["q_up_n"].reshape(MLA_Q_RANK, MLA_H, MLA_NOPE),
                                      g["q_up_r"].reshape(MLA_Q_RANK, MLA_H, MLA_ROPE)], axis=2).reshape(1, MLA_Q_RANK, -1)
    r["attn_out_proj"] = g["attn_out"][None]
    r["ffn_up"] = jnp.stack([jnp.concatenate([g[f"ffn_up_g{l}"], g[f"ffn_up_v{l}"]], axis=1) for l in range(DEPTH)])
    r["ffn_conv_w"] = jnp.stack([jnp.concatenate([g[f"ffn_cw_g{l}"], g[f"ffn_cw_v{l}"]], axis=1) for l in range(DEPTH)])
    r["ffn_conv_b"] = jnp.concatenate([jnp.concatenate([g[f"ffn_cb_g{l}"], g[f"ffn_cb_v{l}"]], axis=1) for l in range(DEPTH)], axis=0)
    r["ffn_down"] = jnp.stack([g[f"ffn_down{l}"] for l in range(DEPTH)])
    for k in ("ln_mix_g", "ln_mix_b", "ln_ffn_g", "ln_ffn_b"):
        r[k] = jnp.concatenate([g[f"{k}{l}"] for l in range(DEPTH)], axis=0)
    return r


_WEIGHTS = ["ssd_in_proj", "ssd_conv_w", "ssd_conv_b", "ssd_dt_bias", "ssd_A_log", "ssd_D", "ssd_norm_g", "ssd_out_proj",
            "kv_down_proj", "kv_norm_g", "kv_up_k", "kv_up_v", "q_down_proj", "q_norm_g", "q_up_proj", "attn_out_proj",
            "ffn_up", "ffn_conv_w", "ffn_conv_b", "ffn_down", "ln_mix_g", "ln_mix_b", "ln_ffn_g", "ln_ffn_b"]
_COL_CUT = ["ssd_in_proj", "ssd_conv_w", "ssd_conv_b", "ssd_norm_g", "kv_up_k", "kv_up_v", "q_up_proj", "ffn_up", "ffn_conv_w"]
_ROW_CUT = ["ssd_out_proj", "kv_down_proj", "q_down_proj", "attn_out_proj", "ffn_down"]
_CUT = _COL_CUT + _ROW_CUT
_WHOLE = [n for n in _WEIGHTS if n not in _CUT]
_MXU_WEIGHTS = ["ssd_in_proj", "ssd_out_proj", "kv_down_proj", "kv_up_k", "kv_up_v", "q_down_proj", "q_up_proj",
                "attn_out_proj", "ffn_up", "ffn_down"]
_PACK_ROWS = 160


def _shard_2d(name, s):
    return s.reshape(-1, s.shape[-1])


def _unstack(name, g, shard_shape):
    if name in _COL_CUT:
        lead = shard_shape[:-1]
        return jnp.swapaxes(g, 0, 1).reshape(*lead, N_CHIPS * shard_shape[-1])
    lead, rs, c = shard_shape[:-2], shard_shape[-2], shard_shape[-1]
    n_lead = math.prod(lead)
    return jnp.swapaxes(g.reshape(N_CHIPS, n_lead, rs, c), 0, 1).reshape(*lead, N_CHIPS * rs, c)


def _stack(name, full, shard_shape):
    if name in _COL_CUT:
        cs = shard_shape[-1]
        return jnp.swapaxes(full.reshape(-1, N_CHIPS, cs), 0, 1)
    lead, rs, c = shard_shape[:-2], shard_shape[-2], shard_shape[-1]
    n_lead = math.prod(lead)
    return jnp.swapaxes(full.reshape(n_lead, N_CHIPS, rs, c), 0, 1).reshape(N_CHIPS, n_lead * rs, c)


def _pack(arrs):
    flat = jnp.concatenate([a.reshape(-1) for a in arrs])
    return jnp.pad(flat, (0, _PACK_ROWS * LANE - flat.shape[0])).reshape(_PACK_ROWS, LANE)


def _unpack(packed, like):
    flat, out, o = packed.reshape(-1), [], 0
    for a in like:
        out.append(flat[o:o + a.size].reshape(a.shape))
        o += a.size
    return out


_ARGS = ["x", "positions"] + _WEIGHTS + ["loss_target"] + ["m_" + n for n in _WEIGHTS] + ["v_" + n for n in _WEIGHTS]


def kernel(x, positions, ssd_in_proj, ssd_conv_w, ssd_conv_b, ssd_dt_bias, ssd_A_log, ssd_D, ssd_norm_g,
           ssd_out_proj, kv_down_proj, kv_norm_g, kv_up_k, kv_up_v, q_down_proj, q_norm_g, q_up_proj,
           attn_out_proj, ffn_up, ffn_conv_w, ffn_conv_b, ffn_down, ln_mix_g, ln_mix_b, ln_ffn_g, ln_ffn_b,
           loss_target, m_ssd_in_proj, m_ssd_conv_w, m_ssd_conv_b, m_ssd_dt_bias, m_ssd_A_log, m_ssd_D,
           m_ssd_norm_g, m_ssd_out_proj, m_kv_down_proj, m_kv_norm_g, m_kv_up_k, m_kv_up_v, m_q_down_proj,
           m_q_norm_g, m_q_up_proj, m_attn_out_proj, m_ffn_up, m_ffn_conv_w, m_ffn_conv_b, m_ffn_down,
           m_ln_mix_g, m_ln_mix_b, m_ln_ffn_g, m_ln_ffn_b, v_ssd_in_proj, v_ssd_conv_w, v_ssd_conv_b,
           v_ssd_dt_bias, v_ssd_A_log, v_ssd_D, v_ssd_norm_g, v_ssd_out_proj, v_kv_down_proj, v_kv_norm_g,
           v_kv_up_k, v_kv_up_v, v_q_down_proj, v_q_norm_g, v_q_up_proj, v_attn_out_proj, v_ffn_up,
           v_ffn_conv_w, v_ffn_conv_b, v_ffn_down, v_ln_mix_g, v_ln_mix_b, v_ln_ffn_g, v_ln_ffn_b):
    args = (x, positions, ssd_in_proj, ssd_conv_w, ssd_conv_b, ssd_dt_bias, ssd_A_log, ssd_D, ssd_norm_g,
            ssd_out_proj, kv_down_proj, kv_norm_g, kv_up_k, kv_up_v, q_down_proj, q_norm_g, q_up_proj,
            attn_out_proj, ffn_up, ffn_conv_w, ffn_conv_b, ffn_down, ln_mix_g, ln_mix_b, ln_ffn_g, ln_ffn_b,
            loss_target, m_ssd_in_proj, m_ssd_conv_w, m_ssd_conv_b, m_ssd_dt_bias, m_ssd_A_log, m_ssd_D,
            m_ssd_norm_g, m_ssd_out_proj, m_kv_down_proj, m_kv_norm_g, m_kv_up_k, m_kv_up_v, m_q_down_proj,
            m_q_norm_g, m_q_up_proj, m_attn_out_proj, m_ffn_up, m_ffn_conv_w, m_ffn_conv_b, m_ffn_down,
            m_ln_mix_g, m_ln_mix_b, m_ln_ffn_g, m_ln_ffn_b, v_ssd_in_proj, v_ssd_conv_w, v_ssd_conv_b,
            v_ssd_dt_bias, v_ssd_A_log, v_ssd_D, v_ssd_norm_g, v_ssd_out_proj, v_kv_down_proj, v_kv_norm_g,
            v_kv_up_k, v_kv_up_v, v_q_down_proj, v_q_norm_g, v_q_up_proj, v_attn_out_proj, v_ffn_up,
            v_ffn_conv_w, v_ffn_conv_b, v_ffn_down, v_ln_mix_g, v_ln_mix_b, v_ln_ffn_g, v_ln_ffn_b)
    a = dict(zip(_ARGS, args, strict=True))

    shards = [_shard_2d(n, a[n]) for n in _CUT]
    shards = [s.astype(BF16) if n in _MXU_WEIGHTS else s for n, s in zip(_CUT, shards)]
    gathered = _gather_chips(shards)
    fw = {n: _unstack(n, g, a[n].shape) for n, g in zip(_CUT, gathered)}
    fw.update({n: a[n] for n in _WHOLE})

    loss_parts, grad_x, pieces = _local_step(a["x"], a["positions"], a["loss_target"], _prep_weights(fw))
    full = _assemble_grads(pieces)
    loss = lax.psum(jnp.sum(loss_parts[::8, 0]), ("x", "y", "c"))

    bufs = _scatter_chips([_stack(n, full[n], a[n].shape).astype(BF16) for n in _CUT])
    sums = [_sum_parts(b, "sum_chips_" + n) for n, b in zip(_CUT, bufs)]
    others = _swap_cores(sums)
    every = _gather_all(_pack([full[n] for n in _WHOLE]))
    g_whole = _sum_parts(every.reshape(N_DEV, _PACK_ROWS, LANE), "sum_devices")

    res = {}
    for n, s, o in zip(_CUT, sums, others):
        out = _adamw(_shard_2d(n, a[n]), s, o, _shard_2d(n, a["m_" + n]), _shard_2d(n, a["v_" + n]), "adamw_" + n)
        res[n] = [r.reshape(a[n].shape) for r in out]
    whole = [a[n] for n in _WHOLE]
    out = _adamw(_pack(whole), g_whole, None, _pack([a["m_" + n] for n in _WHOLE]), _pack([a["v_" + n] for n in _WHOLE]),
                 "adamw_whole")
    for k, n in enumerate(_WHOLE):
        res[n] = [_unpack(r, whole)[k] for r in out]
    return (loss, grad_x, *[res[n][0] for n in _WEIGHTS], *[res[n][1] for n in _WEIGHTS],
            *[res[n][2] for n in _WEIGHTS], *[res[n][3] for n in _WEIGHTS])
```

```python
import functools
import math

import jax
import jax.numpy as jnp
from jax import lax
from jax.experimental import pallas as pl
from jax.experimental.pallas import tpu as pltpu

F32 = jnp.float32
BF16 = jnp.bfloat16
HIGHEST = lax.Precision.HIGHEST
MESH = pl.DeviceIdType.MESH

D_MODEL = 1024
DEPTH = 2
DN_ALPHA = (2 * DEPTH) ** 0.25
SSD_D_INNER = 2048
SSD_P = 64
SSD_H = 32
SSD_G = 8
SSD_K = 4
SSD_N = 128
SSD_L = 128
SSD_GN = SSD_G * SSD_N
MLA_H = 8
MLA_Q_RANK = 384
MLA_KV_RANK = 256
MLA_NOPE = 128
MLA_ROPE = 64
MLA_V = 128
ROPE_THETA = 10000.0
FFN_HIDDEN = 2816
LN_EPS = 1e-5
RMS_EPS = 1e-6
ADAM_LR = 0.001
ADAM_B1 = 0.9
ADAM_B2 = 0.999
ADAM_EPS = 1e-08
ADAM_WD = 0.01
ADAM_STEP = 10

N_CHIPS = 4
N_DEV = 8
LANE = 128
VMEM_LIMIT = 56 * 1024 * 1024


def _cparams(**kw):
    return pltpu.CompilerParams(vmem_limit_bytes=VMEM_LIMIT, **kw)


def _tile(dim, cap):
    best = None
    t = LANE
    while t <= min(dim, cap):
        if dim % t == 0:
            best = t
        t += LANE
    return dim if best is None else best


MM_TILE_CAP = 1408
MM_WHOLE_K = 2816
MM_VMEM_BUDGET = 40 * 1024 * 1024


def _mm_tiles(m, n, k, a_bytes, b_bytes, o_bytes, has_add):
    tm, tn = _tile(m, MM_TILE_CAP), _tile(n, MM_TILE_CAP)
    tk = k if k <= MM_WHOLE_K else _tile(k, 1024)

    def need(tm, tn):
        acc = tm * tn * 4 if tk < k else 0
        return 2 * (tm * tk * a_bytes + tk * tn * b_bytes + tm * tn * o_bytes + (tm * tn * 4 if has_add else 0)) + acc

    while need(tm, tn) > MM_VMEM_BUDGET:
        if tm >= tn and _tile(m, tm // 2) < tm:
            tm = _tile(m, tm // 2)
        elif _tile(n, tn // 2) < tn:
            tn = _tile(n, tn // 2)
        else:
            break
    return tm, tn, tk


def _mm(a, b, *, ta=False, tb=False, add=None, out_dtype=F32, name):
    m, k = (a.shape[1], a.shape[0]) if ta else a.shape
    n = b.shape[0] if tb else b.shape[1]
    assert (b.shape[1] if tb else b.shape[0]) == k
    tm, tn, tk = _mm_tiles(m, n, k, a.dtype.itemsize, b.dtype.itemsize, jnp.dtype(out_dtype).itemsize, add is not None)
    nk = k // tk
    dims = (((0 if ta else 1,), (1 if tb else 0,)), ((), ()))

    def partial_product(a_ref, b_ref):
        return lax.dot_general(a_ref[...].astype(BF16), b_ref[...].astype(BF16), dims, preferred_element_type=F32)

    def body_one(*refs):
        if add is None:
            a_ref, b_ref, o_ref = refs
            o_ref[...] = partial_product(a_ref, b_ref).astype(out_dtype)
        else:
            a_ref, b_ref, c_ref, o_ref = refs
            o_ref[...] = (partial_product(a_ref, b_ref) + c_ref[...]).astype(out_dtype)

    def body_acc(*refs):
        if add is None:
            a_ref, b_ref, o_ref, acc = refs
        else:
            a_ref, b_ref, c_ref, o_ref, acc = refs
        kk = pl.program_id(2)

        @pl.when(kk == 0)
        def _():
            acc[...] = jnp.zeros_like(acc) if add is None else c_ref[...]

        acc[...] += partial_product(a_ref, b_ref)

        @pl.when(kk == nk - 1)
        def _():
            o_ref[...] = acc[...].astype(out_dtype)

    a_spec = pl.BlockSpec((tk, tm), lambda i, j, kk: (kk, i)) if ta else pl.BlockSpec((tm, tk), lambda i, j, kk: (i, kk))
    b_spec = pl.BlockSpec((tn, tk), lambda i, j, kk: (j, kk)) if tb else pl.BlockSpec((tk, tn), lambda i, j, kk: (kk, j))
    o_spec = pl.BlockSpec((tm, tn), lambda i, j, kk: (i, j))
    ins, specs = [a, b], [a_spec, b_spec]
    if add is not None:
        ins.append(add)
        specs.append(o_spec)
    return pl.pallas_call(
        body_one if nk == 1 else body_acc, name=name, grid=(m // tm, n // tn, nk), in_specs=specs, out_specs=o_spec,
        out_shape=jax.ShapeDtypeStruct((m, n), out_dtype),
        scratch_shapes=[] if nk == 1 else [pltpu.VMEM((tm, tn), F32)],
        compiler_params=_cparams(dimension_semantics=("parallel", "parallel", "arbitrary")),
    )(*ins)


def _spec(op):
    return pl.BlockSpec(op[1], op[2])


def _bw_fwd(name, fn, grid, ins, outs, out_dtypes):
    n_in = len(ins)
    flat = [(o, dt) for o, dts in zip(outs, out_dtypes) for dt in dts]

    def body(*refs):
        res = fn(*[r[...].astype(F32) for r in refs[:n_in]])
        orefs = iter(refs[n_in:])
        for v, dts in zip(res, out_dtypes):
            for dt in dts:
                next(orefs)[...] = v.astype(dt)

    return pl.pallas_call(
        body, name=name, grid=grid, in_specs=[_spec(o) for o in ins],
        out_specs=[pl.BlockSpec(o[1], o[2]) for o, _ in flat],
        out_shape=[jax.ShapeDtypeStruct(o[0], dt) for o, dt in flat], compiler_params=_cparams(),
    )(*[o[0] for o in ins])


def _bw_bwd(name, fn, grid, data, params, consts, cts, red_axes, grad_dtypes):
    nd, npar, nc, nct = len(data), len(params), len(consts), len(cts)

    def body(*refs):
        first = None
        for ax in red_axes:
            z = pl.program_id(ax) == 0
            first = z if first is None else jnp.logical_and(first, z)
        vals = [r[...].astype(F32) for r in refs[:nd + npar + nc + nct]]
        d, p, c, g = vals[:nd], vals[nd:nd + npar], vals[nd + npar:nd + npar + nc], vals[nd + npar + nc:]
        _, vjp = jax.vjp(lambda dd, pp: tuple(fn(*dd, *pp, *c)), d, p)
        gd, gp = vjp(tuple(g))
        orefs = refs[nd + npar + nc + nct:]
        for r, v in zip(orefs[:nd], gd):
            r[...] = v.astype(r.dtype)
        if npar:
            @pl.when(first)
            def _():
                for r in orefs[nd:]:
                    r[...] = jnp.zeros_like(r)

            for r, v in zip(orefs[nd:], gp):
                r[...] += v

    ins = list(data) + list(params) + list(consts) + list(cts)
    outs = list(data) + list(params)
    dtypes = list(grad_dtypes) + [F32] * npar
    return pl.pallas_call(
        body, name=name, grid=grid, in_specs=[_spec(o) for o in ins], out_specs=[_spec(o) for o in outs],
        out_shape=[jax.ShapeDtypeStruct(o[0].shape, dt) for o, dt in zip(outs, dtypes)], compiler_params=_cparams(),
    )(*[o[0] for o in ins])


def _shift_down(x, s):
    row = lax.broadcasted_iota(jnp.int32, x.shape, 0)
    return jnp.where(row < s, 0.0, pltpu.roll(x, s, 0))


def _shift_up(x, s):
    n = x.shape[0]
    row = lax.broadcasted_iota(jnp.int32, x.shape, 0)
    return jnp.where(row >= n - s, 0.0, pltpu.roll(x, n - s, 0))


def _time_shift(s):
    if s == 0:
        return lambda x: x

    @jax.custom_vjp
    def shift(x):
        return _shift_down(x, s)

    shift.defvjp(lambda x: (_shift_down(x, s), None), lambda _, g: (_shift_up(g, s),))
    return shift


def _rot_half_raw(x):
    lane = lax.broadcasted_iota(jnp.int32, x.shape, 1)
    return jnp.where(lane % MLA_ROPE < MLA_ROPE // 2, -pltpu.roll(x, LANE - MLA_ROPE // 2, 1), pltpu.roll(x, MLA_ROPE // 2, 1))


@jax.custom_vjp
def _rot_half(x):
    return _rot_half_raw(x)


_rot_half.defvjp(lambda x: (_rot_half_raw(x), None), lambda _, g: (-_rot_half_raw(g),))


@jax.custom_vjp
def _roll_half_lanes(x):
    return pltpu.roll(x, LANE // 2, 1)


_roll_half_lanes.defvjp(lambda x: (pltpu.roll(x, LANE // 2, 1), None), lambda _, g: (pltpu.roll(g, LANE // 2, 1),))


def _causal_conv(u, w, b):
    width = w.shape[0]
    y = b
    for k in range(width):
        y = y + w[k:k + 1, :] * _time_shift(width - 1 - k)(u)
    return y


def _silu(x):
    return x * jax.nn.sigmoid(x)


def _f_ln(h, mix, g, b):
    x = DN_ALPHA * h + mix
    mu = jnp.mean(x, axis=-1, keepdims=True)
    xc = x - mu
    var = jnp.mean(xc * xc, axis=-1, keepdims=True)
    return (xc * lax.rsqrt(var + LN_EPS) * g + b,)


def _f_convsilu(u, w, b):
    return (_silu(_causal_conv(u, w, b)),)


def _f_convffn(ug, uv, wg, wv, bg, bv):
    return (_silu(_causal_conv(ug, wg, bg)) * _causal_conv(uv, wv, bv),)


def _f_dt(dt_raw, bias, a_log):
    x = dt_raw + bias
    dt = jnp.maximum(x, 0.0) + jnp.log(1.0 + jnp.exp(-jnp.abs(x)))
    a = dt * (-jnp.exp(a_log))
    n = a.shape[0]
    lower = (lax.broadcasted_iota(jnp.int32, (n, n), 0) >= lax.broadcasted_iota(jnp.int32, (n, n), 1)).astype(F32)
    cum = jnp.dot(lower, a, precision=HIGHEST, preferred_element_type=F32)
    cum_last = jnp.sum(a, axis=0, keepdims=True)
    return dt, cum, jnp.exp(cum), jnp.exp(cum_last - cum) * dt


def _f_ssd_post(y, xs, z, d_exp, ng):
    t = (y + d_exp * xs) * _silu(z)
    return (t * lax.rsqrt(jnp.mean(t * t, axis=-1, keepdims=True) + LN_EPS) * ng,)


def _f_rms(x, g):
    return (x * lax.rsqrt(jnp.mean(x * x, axis=-1, keepdims=True) + RMS_EPS) * g,)


def _f_rope(x, cos, sin):
    return (x * cos + _rot_half(x) * sin,)


def _f_rope_dup(x, cos, sin):
    r = x * cos + _rot_half(x) * sin
    return (r + _roll_half_lanes(r),)


def _rows(t, cap=512):
    for c in (cap, 256, 128, 64, 32, 16, 8):
        if c <= cap and t % c == 0:
            return c
    return t


class _Blockwise:
    def __init__(self, name, fn, grid, data, params, consts, outs, red_axes):
        self.name, self.fn, self.grid = name, fn, grid
        self.data, self.params, self.consts, self.outs, self.red_axes = data, params, consts, outs, red_axes

    def fwd(self, *out_dtypes):
        out_dtypes = out_dtypes or tuple((F32,) for _ in self.outs)
        return _bw_fwd(self.name + "_fwd", self.fn, self.grid, self.data + self.params + self.consts, self.outs, out_dtypes)

    def bwd(self, cts, grad_dtypes=None):
        cts = [(c, o[1], o[2]) for c, o in zip(cts, self.outs)]
        grad_dtypes = grad_dtypes or [F32] * len(self.data)
        res = _bw_bwd(self.name + "_bwd", self.fn, self.grid, self.data, self.params, self.consts, cts, self.red_axes,
                      grad_dtypes)
        return res[:len(self.data)], res[len(self.data):]


def _op_ln(name, h, mix, g, b):
    t, d = h.shape
    tr = _rows(t, 256)
    row = ((tr, d), lambda i: (i, 0))
    par = ((1, d), lambda i: (0, 0))
    return _Blockwise(name, _f_ln, (t // tr,), [(h, *row), (mix, *row)], [(g, *par), (b, *par)], [],
                      [((t, d), *row)], (0,))


def _op_conv(name, fn, us, ws, bs, nb, seq):
    c = us[0].shape[1]
    ct = _tile(c, 256)
    blk = ((seq, ct), lambda j, bb: (bb, j))
    data = [(u, *blk) for u in us]
    params = [(w, (w.shape[0], ct), lambda j, bb: (0, j)) for w in ws] + [(b, (1, ct), lambda j, bb: (0, j)) for b in bs]
    return _Blockwise(name, fn, (c // ct, nb), data, params, [], [((nb * seq, c), *blk)], (1,))


def _op_dt(name, dt_raw, bias, a_log):
    t = dt_raw.shape[0]
    row = ((SSD_L, LANE), lambda i: (i, 0))
    par = ((1, LANE), lambda i: (0, 0))
    return _Blockwise(name, _f_dt, (t // SSD_L,), [(dt_raw, *row)], [(bias, *par), (a_log, *par)], [],
                      [((t, LANE), *row)] * 4, (0,))


def _op_ssd_post(name, y, xs, z, d_exp, ng):
    t, c = y.shape
    gw = c // SSD_G
    tr = _rows(t, 512)
    blk = ((tr, gw), lambda g, i: (i, g))
    par = ((1, gw), lambda g, i: (0, g))
    return _Blockwise(name, _f_ssd_post, (SSD_G, t // tr), [(y, *blk), (xs, *blk), (z, *blk)],
                      [(d_exp, *par), (ng, *par)], [], [((t, c), *blk)], (1,))


def _op_rms(name, x, g):
    t, c = x.shape
    tr = _rows(t, 512)
    return _Blockwise(name, _f_rms, (t // tr,), [(x, (tr, c), lambda i: (i, 0))], [(g, (1, c), lambda i: (0, 0))], [],
                      [((t, c), (tr, c), lambda i: (i, 0))], (0,))


def _op_rope(name, fn, x, cos, sin):
    t, c = x.shape
    tr = _rows(t, 512)
    blk = ((tr, LANE), lambda i, j: (i, j))
    cs = ((tr, LANE), lambda i, j: (i, 0))
    return _Blockwise(name, fn, (t // tr, c // LANE), [(x, *blk)], [], [(cos, *cs), (sin, *cs)], [((t, c), *blk)], ())


_NT = (((1,), (1,)), ((), ()))
_TN = (((0,), (0,)), ((), ()))


def _ssd_head(x, g, bc, cc, s, cum_row, dt_row, cum_col, ecum_col, wend_col):
    n = g.shape[0]
    row = lax.broadcasted_iota(jnp.int32, (n, n), 0)
    col = lax.broadcasted_iota(jnp.int32, (n, n), 1)
    decay = jnp.exp(jnp.where(row >= col, cum_col - cum_row, -jnp.inf))
    w = g * decay * dt_row
    y = jnp.dot(w.astype(BF16), x.astype(BF16), preferred_element_type=F32)
    y = y + lax.dot_general((cc * ecum_col).astype(BF16), s.astype(BF16), _NT, preferred_element_type=F32)
    lane = lax.broadcasted_iota(jnp.int32, cum_row.shape, 1)
    cum_last = jnp.sum(jnp.where(lane == n - 1, cum_row, 0.0), axis=1, keepdims=True)
    s_new = s * jnp.exp(cum_last) + lax.dot_general(x.astype(BF16), (bc * wend_col).astype(BF16), _TN,
                                                    preferred_element_type=F32)
    return y, s_new


SSD_KPAD = 8
SSD_ROWS = 2 * SSD_KPAD
SSD_COLS = 3 * SSD_KPAD


def _ssd_head_args(rows, cols, k):
    return (rows[k:k + 1], rows[SSD_KPAD + k:SSD_KPAD + k + 1], cols[:, k:k + 1],
            cols[:, SSD_KPAD + k:SSD_KPAD + k + 1], cols[:, 2 * SSD_KPAD + k:2 * SSD_KPAD + k + 1])


def _carried_exchange(arrs, in_refs, out_refs, sems, scatter, n_steps):
    if not arrs:
        return lambda: None, lambda: None
    first = jnp.logical_and(pl.program_id(0) == 0, pl.program_id(1) == 0)
    last = jnp.logical_and(pl.program_id(0) == n_steps[0] - 1, pl.program_id(1) == n_steps[1] - 1)
    start, wait = _chip_exchange(in_refs, out_refs, sems, scatter)
    return (lambda: pl.when(first)(start)), (lambda: pl.when(last)(wait))


def _ssd_scan_fwd(xs, bm, cm, rows, cols, nb, seq, gather=()):
    nc = seq // SSD_L
    kp = SSD_K * SSD_P
    ng = len(gather)

    def body(*refs):
        xs_ref, b_ref, c_ref, row_ref, col_ref = refs[:5]
        y_ref, st_ref = refs[5 + ng:7 + ng]
        start, wait = _carried_exchange(gather, refs[5:5 + ng], refs[7 + ng:7 + 2 * ng], refs[7 + 2 * ng:], False,
                                        (nb, SSD_G))
        start()

        def chunk(c, states):
            sl = pl.ds(pl.multiple_of(c * SSD_L, SSD_L), SSD_L)
            bc, cc = b_ref[sl, :], c_ref[sl, :]
            g = lax.dot_general(cc.astype(BF16), bc.astype(BF16), _NT, preferred_element_type=F32)
            rows_c, cols_c = row_ref[0, :, sl], col_ref[0, sl, :]
            new = []
            for k in range(SSD_K):
                hs = pl.ds(k * SSD_P, SSD_P)
                st_ref[0, c * SSD_K + k] = states[k]
                y, s_new = _ssd_head(xs_ref[sl, hs], g, bc, cc, states[k], *_ssd_head_args(rows_c, cols_c, k))
                y_ref[sl, hs] = y
                new.append(s_new)
            return tuple(new)

        lax.fori_loop(0, nc, chunk, tuple(jnp.zeros((SSD_P, SSD_N), F32) for _ in range(SSD_K)))
        wait()

    t = xs.shape[0]
    any_spec = pl.BlockSpec(memory_space=pl.ANY)
    return pl.pallas_call(
        body, name="ssd_scan_fwd", grid=(nb, SSD_G),
        in_specs=[pl.BlockSpec((seq, kp), lambda b, g: (b, g)),
                  pl.BlockSpec((seq, SSD_N), lambda b, g: (b, g)),
                  pl.BlockSpec((seq, SSD_N), lambda b, g: (b, g)),
                  pl.BlockSpec((1, SSD_ROWS, seq), lambda b, g: (b * SSD_G + g, 0, 0)),
                  pl.BlockSpec((1, seq, SSD_COLS), lambda b, g: (b * SSD_G + g, 0, 0))] + [any_spec] * ng,
        out_specs=[pl.BlockSpec((seq, kp), lambda b, g: (b, g)),
                   pl.BlockSpec((1, nc * SSD_K, SSD_P, SSD_N), lambda b, g: (b * SSD_G + g, 0, 0, 0))] + [any_spec] * ng,
        out_shape=[jax.ShapeDtypeStruct((t, SSD_D_INNER), F32),
                   jax.ShapeDtypeStruct((nb * SSD_G, nc * SSD_K, SSD_P, SSD_N), F32)] + _exchange_shapes(gather, False),
        scratch_shapes=_exchange_sems(ng) if ng else [],
        compiler_params=_cparams(dimension_semantics=("arbitrary", "arbitrary")),
    )(xs, bm, cm, rows, cols, *gather)


def _ssd_scan_bwd(xs, bm, cm, rows, cols, states, dy, nb, seq, scatter=()):
    nc = seq // SSD_L
    kp = SSD_K * SSD_P
    ns = len(scatter)

    def body(*refs):
        xs_ref, b_ref, c_ref, row_ref, col_ref, st_ref, dy_ref = refs[:7]
        dxs_ref, db_ref, dc_ref, drow_ref, dcol_ref = refs[7 + ns:12 + ns]
        start, wait = _carried_exchange(scatter, refs[7:7 + ns], refs[12 + ns:12 + 2 * ns], refs[12 + 2 * ns:], True,
                                        (nb, SSD_G))
        start()

        def chunk(i, dstates):
            c = nc - 1 - i
            sl = pl.ds(pl.multiple_of(c * SSD_L, SSD_L), SSD_L)
            bc, cc = b_ref[sl, :], c_ref[sl, :]
            bcb, ccb = bc.astype(BF16), cc.astype(BF16)
            g = lax.dot_general(ccb, bcb, _NT, preferred_element_type=F32)
            rows_c, cols_c = row_ref[0, :, sl], col_ref[0, sl, :]
            drow_ref[0, :, sl] = jnp.zeros((SSD_ROWS, SSD_L), F32)
            dcol_ref[0, sl, :] = jnp.zeros((SSD_L, SSD_COLS), F32)
            db = jnp.zeros((SSD_L, SSD_N), F32)
            dc = jnp.zeros((SSD_L, SSD_N), F32)
            dg = jnp.zeros((SSD_L, SSD_L), F32)
            new = []
            for k in range(SSD_K):
                hs = pl.ds(k * SSD_P, SSD_P)
                _, vjp = jax.vjp(_ssd_head, xs_ref[sl, hs], g, bc, cc, st_ref[0, c * SSD_K + k],
                                 *_ssd_head_args(rows_c, cols_c, k))
                dx, dgk, dbk, dck, ds, d_cum_row, d_dt_row, d_cum_col, d_ecum_col, d_wend_col = vjp(
                    (dy_ref[sl, hs], dstates[k]))
                dxs_ref[sl, hs] = dx
                db, dc, dg = db + dbk, dc + dck, dg + dgk
                drow_ref[0, k:k + 1, sl] = d_cum_row
                drow_ref[0, SSD_KPAD + k:SSD_KPAD + k + 1, sl] = d_dt_row
                dcol_ref[0, sl, k:k + 1] = d_cum_col
                dcol_ref[0, sl, SSD_KPAD + k:SSD_KPAD + k + 1] = d_ecum_col
                dcol_ref[0, sl, 2 * SSD_KPAD + k:2 * SSD_KPAD + k + 1] = d_wend_col
                new.append(ds)
            dgb = dg.astype(BF16)
            dc_ref[sl, :] = dc + jnp.dot(dgb, bcb, preferred_element_type=F32)
            db_ref[sl, :] = db + lax.dot_general(dgb, ccb, _TN, preferred_element_type=F32)
            return tuple(new)

        lax.fori_loop(0, nc, chunk, tuple(jnp.zeros((SSD_P, SSD_N), F32) for _ in range(SSD_K)))
        wait()

    t = xs.shape[0]
    x_spec = pl.BlockSpec((seq, kp), lambda b, g: (b, g))
    n_spec = pl.BlockSpec((seq, SSD_N), lambda b, g: (b, g))
    r_spec = pl.BlockSpec((1, SSD_ROWS, seq), lambda b, g: (b * SSD_G + g, 0, 0))
    c_spec = pl.BlockSpec((1, seq, SSD_COLS), lambda b, g: (b * SSD_G + g, 0, 0))
    any_spec = pl.BlockSpec(memory_space=pl.ANY)
    return pl.pallas_call(
        body, name="ssd_scan_bwd", grid=(nb, SSD_G),
        in_specs=[x_spec, n_spec, n_spec, r_spec, c_spec,
                  pl.BlockSpec((1, nc * SSD_K, SSD_P, SSD_N), lambda b, g: (b * SSD_G + g, 0, 0, 0)), x_spec]
        + [any_spec] * ns,
        out_specs=[x_spec, n_spec, n_spec, r_spec, c_spec] + [any_spec] * ns,
        out_shape=[jax.ShapeDtypeStruct((t, SSD_D_INNER), F32), jax.ShapeDtypeStruct((t, SSD_GN), F32),
                   jax.ShapeDtypeStruct((t, SSD_GN), F32), jax.ShapeDtypeStruct(rows.shape, F32),
                   jax.ShapeDtypeStruct(cols.shape, F32)] + _exchange_shapes(scatter, True),
        scratch_shapes=_exchange_sems(ns) if ns else [],
        compiler_params=_cparams(dimension_semantics=("arbitrary", "arbitrary")),
    )(xs, bm, cm, rows, cols, states, dy, *scatter)


ATT_TQ = 256
ATT_SCALE = (MLA_NOPE + MLA_ROPE) ** -0.5


def _attn_pair(qn, qr, kn, kr, v, q0):
    tq, s = qn.shape[0], kn.shape[0]
    q_idx = q0 + lax.broadcasted_iota(jnp.int32, (tq, s), 0)
    k_idx = lax.broadcasted_iota(jnp.int32, (tq, s), 1)
    lane = lax.broadcasted_iota(jnp.int32, qr.shape, 1)
    nt = (((1,), (1,)), ((), ()))
    krb = kr.astype(BF16)
    outs = []
    for j in range(2):
        hs = slice(j * MLA_NOPE, (j + 1) * MLA_NOPE)
        qrj = jnp.where((lane >= MLA_ROPE) == (j == 1), qr, jnp.zeros_like(qr))
        sc = lax.dot_general(qn[:, hs].astype(BF16), kn[:, hs].astype(BF16), nt, preferred_element_type=F32)
        sc = sc + lax.dot_general(qrj.astype(BF16), krb, nt, preferred_element_type=F32)
        sc = jnp.where(k_idx <= q_idx, sc * ATT_SCALE, -jnp.inf)
        m = jnp.max(sc, axis=-1, keepdims=True)
        e = jnp.exp(sc - lax.stop_gradient(m))
        p = e / jnp.sum(e, axis=-1, keepdims=True)
        outs.append(jnp.dot(p.astype(BF16), v[:, hs].astype(BF16), preferred_element_type=F32))
    return jnp.concatenate(outs, axis=1)


def _attn_specs(nb, seq):
    nq = seq // ATT_TQ
    qn = pl.BlockSpec((ATT_TQ, 2 * MLA_NOPE), lambda b, hp, qi: (b * nq + qi, hp))
    qr = pl.BlockSpec((ATT_TQ, LANE), lambda b, hp, qi: (b * nq + qi, hp))
    kn = pl.BlockSpec((seq, 2 * MLA_NOPE), lambda b, hp, qi: (b, hp))
    kr = pl.BlockSpec((seq, LANE), lambda b, hp, qi: (b, 0))
    return (nb, MLA_H // 2, nq), qn, qr, kn, kr


def _attn_fwd(qn, qr, kn, kr, v, nb, seq):
    grid, s_qn, s_qr, s_kn, s_kr = _attn_specs(nb, seq)

    def body(qn_ref, qr_ref, kn_ref, kr_ref, v_ref, o_ref):
        q0 = pl.program_id(2) * ATT_TQ
        o_ref[...] = _attn_pair(qn_ref[...], qr_ref[...], kn_ref[...], kr_ref[...], v_ref[...], q0).astype(o_ref.dtype)

    return pl.pallas_call(
        body, name="attn_fwd", grid=grid, in_specs=[s_qn, s_qr, s_kn, s_kr, s_kn], out_specs=s_qn,
        out_shape=jax.ShapeDtypeStruct(qn.shape, BF16), compiler_params=_cparams(),
    )(qn, qr, kn, kr, v)


def _attn_bwd(qn, qr, kn, kr, v, do, nb, seq):
    grid, s_qn, s_qr, s_kn, s_kr = _attn_specs(nb, seq)

    def body(qn_ref, qr_ref, kn_ref, kr_ref, v_ref, do_ref, dqn_ref, dqr_ref, dkn_ref, dkr_ref, dv_ref):
        hp, qi = pl.program_id(1), pl.program_id(2)
        q0 = qi * ATT_TQ
        _, vjp = jax.vjp(lambda a, b, c, d, e: _attn_pair(a, b, c, d, e, q0),
                         *[r[...].astype(F32) for r in (qn_ref, qr_ref, kn_ref, kr_ref, v_ref)])
        dqn, dqr, dkn, dkr, dv = vjp(do_ref[...].astype(F32))
        dqn_ref[...] = dqn.astype(dqn_ref.dtype)
        dqr_ref[...] = dqr

        @pl.when(qi == 0)
        def _():
            dkn_ref[...] = jnp.zeros_like(dkn_ref)
            dv_ref[...] = jnp.zeros_like(dv_ref)

        @pl.when(jnp.logical_and(qi == 0, hp == 0))
        def _():
            dkr_ref[...] = jnp.zeros_like(dkr_ref)

        dkn_ref[...] += dkn
        dv_ref[...] += dv
        dkr_ref[...] += dkr

    return pl.pallas_call(
        body, name="attn_bwd", grid=grid, in_specs=[s_qn, s_qr, s_kn, s_kr, s_kn, s_qn],
        out_specs=[s_qn, s_qr, s_kn, s_kr, s_kn],
        out_shape=[jax.ShapeDtypeStruct(qn.shape, BF16)] + [jax.ShapeDtypeStruct(a.shape, F32) for a in (qr, kn, kr, v)],
        compiler_params=_cparams(),
    )(qn, qr, kn, kr, v, do)


def _loss_head(y, target):
    t, d = y.shape
    tr = _rows(t, 512)

    def body(y_ref, t_ref, l_ref, dy_ref):
        err = y_ref[...] - t_ref[...]
        dy_ref[...] = err * (1.0 / d)
        part = 0.5 * jnp.sum(jnp.sum(err * err, axis=1, keepdims=True), axis=0, keepdims=True) * (1.0 / d)
        l_ref[...] = jnp.broadcast_to(part, l_ref.shape)

    row = pl.BlockSpec((tr, d), lambda i: (i, 0))
    parts, dy = pl.pallas_call(
        body, name="loss_head", grid=(t // tr,), in_specs=[row, row],
        out_specs=[pl.BlockSpec((8, LANE), lambda i: (i, 0)), row],
        out_shape=[jax.ShapeDtypeStruct((8 * (t // tr), LANE), F32), jax.ShapeDtypeStruct((t, d), F32)],
        compiler_params=_cparams(),
    )(y, target)
    return parts, dy


def _sum_parts(stack, name):
    n, r, c = stack.shape
    tr = _rows(r, 512)

    def body(s_ref, o_ref):
        acc = s_ref[0].astype(F32)
        for i in range(1, n):
            acc = acc + s_ref[i].astype(F32)
        o_ref[...] = acc

    return pl.pallas_call(
        body, name=name, grid=(r // tr,), in_specs=[pl.BlockSpec((n, tr, c), lambda i: (0, i, 0))],
        out_specs=pl.BlockSpec((tr, c), lambda i: (i, 0)), out_shape=jax.ShapeDtypeStruct((r, c), F32),
        compiler_params=_cparams(),
    )(stack)


def _adamw(w, g_mine, g_other, m, v, name):
    r, c = w.shape
    tr = _rows(r, 256)
    bc1 = 1.0 / (1.0 - ADAM_B1 ** ADAM_STEP)
    bc2 = 1.0 / (1.0 - ADAM_B2 ** ADAM_STEP)
    two = g_other is not None

    def body(*refs):
        if two:
            w_ref, g_ref, g2_ref, m_ref, v_ref, go_ref, d_ref, mo_ref, vo_ref = refs
            g = g_ref[...] + g2_ref[...]
        else:
            w_ref, g_ref, m_ref, v_ref, go_ref, d_ref, mo_ref, vo_ref = refs
            g = g_ref[...]
        mn = ADAM_B1 * m_ref[...] + (1.0 - ADAM_B1) * g
        vn = ADAM_B2 * v_ref[...] + (1.0 - ADAM_B2) * (g * g)
        go_ref[...] = g
        mo_ref[...] = mn
        vo_ref[...] = vn
        d_ref[...] = -ADAM_LR * ((mn * bc1) / (jnp.sqrt(vn * bc2) + ADAM_EPS) + ADAM_WD * w_ref[...])

    blk = pl.BlockSpec((tr, c), lambda i: (i, 0))
    ins = [w, g_mine] + ([g_other] if two else []) + [m, v]
    return pl.pallas_call(
        body, name=name, grid=(r // tr,), in_specs=[blk] * len(ins), out_specs=[blk] * 4,
        out_shape=[jax.ShapeDtypeStruct((r, c), F32)] * 4, compiler_params=_cparams(),
    )(*ins)


def _chip_peers():
    x, y, c = lax.axis_index("x"), lax.axis_index("y"), lax.axis_index("c")
    return (x, y, c), [(1 - x, y), (x, 1 - y), (1 - x, 1 - y)]


def _chip_exchange(ins, outs, sems, scatter):
    send_sems, recv_sems, local_sems = sems
    (x, y, c), chips = _chip_peers()
    me = 2 * x + y
    n = len(ins)

    def src(i, chip):
        return ins[i].at[chip] if scatter else ins[i]

    def local(i):
        return pltpu.make_async_copy(src(i, me), outs[i].at[me], local_sems.at[i])

    def remote(i, j, piece, slot):
        px, py = chips[j]
        return pltpu.make_async_remote_copy(src_ref=src(i, piece), dst_ref=outs[i].at[slot], send_sem=send_sems.at[i, j],
                                            recv_sem=recv_sems.at[i, j], device_id=(px, py, c), device_id_type=MESH)

    def start():
        for i in range(n):
            local(i).start()
            for j, (px, py) in enumerate(chips):
                remote(i, j, 2 * px + py, me).start()

    def wait():
        for i in range(n):
            for j, (px, py) in enumerate(chips):
                remote(i, j, me, 2 * px + py).wait_recv()
        for i in range(n):
            for j, (px, py) in enumerate(chips):
                remote(i, j, 2 * px + py, me).wait_send()
            local(i).wait()

    return start, wait


def _exchange_sems(n):
    return [pltpu.SemaphoreType.DMA((n, 3)), pltpu.SemaphoreType.DMA((n, 3)), pltpu.SemaphoreType.DMA((n,))]


def _exchange_shapes(arrs, scatter):
    return [jax.ShapeDtypeStruct(s.shape if scatter else (N_CHIPS,) + s.shape, s.dtype) for s in arrs]


def _exchange_call(name, arrs, scatter):
    n = len(arrs)

    def body(*refs):
        start, wait = _chip_exchange(refs[:n], refs[n:2 * n], refs[2 * n:], scatter)
        start()
        wait()

    any_spec = pl.BlockSpec(memory_space=pl.ANY)
    return pl.pallas_call(body, name=name, in_specs=[any_spec] * n, out_specs=[any_spec] * n,
                          out_shape=_exchange_shapes(arrs, scatter), scratch_shapes=_exchange_sems(n))(*arrs)


def _gather_chips(shards, name="gather_chips"):
    return _exchange_call(name, shards, False)


def _scatter_chips(stacks, name="scatter_chips"):
    return _exchange_call(name, stacks, True)


def _swap_cores(arrs):
    n = len(arrs)

    def body(*refs):
        ins, outs = refs[:n], refs[n:2 * n]
        send_sems, recv_sems = refs[2 * n:]
        x, y, c = lax.axis_index("x"), lax.axis_index("y"), lax.axis_index("c")
        cps = []
        for i in range(n):
            cp = pltpu.make_async_remote_copy(src_ref=ins[i], dst_ref=outs[i], send_sem=send_sems.at[i],
                                              recv_sem=recv_sems.at[i], device_id=(x, y, 1 - c), device_id_type=MESH)
            cp.start()
            cps.append(cp)
        for cp in cps:
            cp.wait()

    any_spec = pl.BlockSpec(memory_space=pl.ANY)
    return pl.pallas_call(
        body, name="swap_cores", in_specs=[any_spec] * n, out_specs=[any_spec] * n,
        out_shape=[jax.ShapeDtypeStruct(s.shape, s.dtype) for s in arrs],
        scratch_shapes=[pltpu.SemaphoreType.DMA((n,)), pltpu.SemaphoreType.DMA((n,))],
    )(*arrs)


def _gather_all(block):
    m_per, n = block.shape

    def body(x_ref, out_ref, send_sems, recv_sems, local_sem):
        x, y, c = lax.axis_index("x"), lax.axis_index("y"), lax.axis_index("c")
        me, sibling = (x, y, c), (x, y, 1 - c)
        chips = [(1 - x, y), (x, 1 - y), (1 - x, 1 - y)]

        def rows(px, py, pc):
            return out_ref.at[pl.ds((4 * px + 2 * py + pc) * m_per, m_per), :]

        def copy(k, blk, to, src=None):
            return pltpu.make_async_remote_copy(src_ref=rows(*blk) if src is None else src, dst_ref=rows(*blk),
                                                send_sem=send_sems.at[k], recv_sem=recv_sems.at[k], device_id=to,
                                                device_id_type=MESH)

        mine = pltpu.make_async_copy(x_ref, rows(*me), local_sem)
        mine.start()
        first = [copy(0, me, sibling, src=x_ref)]
        first += [copy(1 + j, me, (*chip, c), src=x_ref) for j, chip in enumerate(chips)]
        for cp in first:
            cp.start()
        passed = [copy(4 + j, (*chip, c), sibling) for j, chip in enumerate(chips)]
        for j, chip in enumerate(chips):
            copy(1 + j, (*chip, c), me).wait_recv()
            passed[j].start()
        copy(0, sibling, me).wait_recv()
        for j, chip in enumerate(chips):
            copy(4 + j, (*chip, 1 - c), me).wait_recv()
        for cp in first + passed:
            cp.wait_send()
        mine.wait()

    return pl.pallas_call(
        body, name="gather_all", out_shape=jax.ShapeDtypeStruct((N_DEV * m_per, n), block.dtype),
        in_specs=[pl.BlockSpec(memory_space=pltpu.VMEM)], out_specs=pl.BlockSpec(memory_space=pltpu.VMEM),
        scratch_shapes=[pltpu.SemaphoreType.DMA((7,)), pltpu.SemaphoreType.DMA((7,)), pltpu.SemaphoreType.DMA],
    )(block)


def _rope_tables(positions):
    inv_freq = 1.0 / (ROPE_THETA ** (jnp.arange(0, MLA_ROPE, 2, dtype=F32) / MLA_ROPE))
    ang = positions.astype(F32).reshape(-1, 1) * inv_freq
    return jnp.tile(jnp.cos(ang), (1, 4)), jnp.tile(jnp.sin(ang), (1, 4))


def _pad_cols(a, n):
    return jnp.pad(a, ((0, 0), (0, n - a.shape[1])))


def _ffn_fwd(l, h, w, nb, seq):
    ug = _mm(h, w["ffn_up_g"][l], name=f"ffn{l}_up_g")
    uv = _mm(h, w["ffn_up_v"][l], name=f"ffn{l}_up_v")
    op = _op_conv(f"ffn{l}_conv", _f_convffn, [ug, uv], [w["ffn_cw_g"][l], w["ffn_cw_v"][l]],
                  [w["ffn_cb_g"][l], w["ffn_cb_v"][l]], nb, seq)
    act = op.fwd((BF16,))[0]
    ff = _mm(act, w["ffn_down"][l], name=f"ffn{l}_down")
    return ff, (h, op, act)


def _ffn_bwd(l, saved, dff, w, grads):
    h, op, act = saved
    grads[f"ffn_down{l}"] = _mm(act, dff, ta=True, name=f"ffn{l}_down_dw")
    dact = _mm(dff, w["ffn_down"][l], tb=True, name=f"ffn{l}_down_dx")
    (dug, duv), (dwg, dwv, dbg, dbv) = op.bwd([dact], [BF16, BF16])
    grads[f"ffn_up_g{l}"] = _mm(h, dug, ta=True, name=f"ffn{l}_up_g_dw")
    grads[f"ffn_up_v{l}"] = _mm(h, duv, ta=True, name=f"ffn{l}_up_v_dw")
    grads[f"ffn_cw_g{l}"], grads[f"ffn_cw_v{l}"], grads[f"ffn_cb_g{l}"], grads[f"ffn_cb_v{l}"] = dwg, dwv, dbg, dbv
    dh = _mm(dug, w["ffn_up_g"][l], tb=True, name=f"ffn{l}_up_g_dx")
    return _mm(duv, w["ffn_up_v"][l], tb=True, add=dh, name=f"ffn{l}_up_v_dx")


class _Carried:
    def __init__(self, shards, finish_weights, grad_stacks):
        self.shards, self.finish_weights, self.grad_stacks, self.received = shards, finish_weights, grad_stacks, None


def _local_step(x, positions, target, w, carried=None):
    nb, seq, d = x.shape
    t = nb * seq
    x2, tgt2 = x.reshape(t, d), target.reshape(t, d)
    cos, sin = _rope_tables(positions)
    grads = {}

    xb = x2.astype(BF16)

    z = _mm(xb, w["in_z"], name="ssd_in_z")
    raw = [_mm(xb, w[k], name="ssd_" + k) for k in ("in_x", "in_b", "in_c")]
    dt_raw = _mm(xb, w["in_dt"], name="ssd_in_dt")
    conv_ops = [_op_conv("ssd_conv_" + s, _f_convsilu, [r], [w["conv_w_" + s]], [w["conv_b_" + s]], nb, seq)
                for s, r in zip("xbc", raw)]
    xs, bm, cm = [op.fwd()[0] for op in conv_ops]
    dt_op = _op_dt("ssd_dt", dt_raw, w["dt_bias"], w["a_log"])
    dt, cum, ecum, wend = dt_op.fwd()

    def to_rows(v):
        v = jnp.swapaxes(v.reshape(nb, seq, LANE), 1, 2)[:, :SSD_H, :].reshape(nb * SSD_G, SSD_K, seq)
        return jnp.pad(v, ((0, 0), (0, SSD_KPAD - SSD_K), (0, 0)))

    def from_rows(v):
        v = v[:, :SSD_K, :].reshape(nb, SSD_H, seq)
        return jnp.swapaxes(jnp.pad(v, ((0, 0), (0, LANE - SSD_H), (0, 0))), 1, 2).reshape(t, LANE)

    def to_cols(v):
        v = jnp.swapaxes(v[:, :SSD_H].reshape(nb, seq, SSD_G, SSD_K), 1, 2).reshape(nb * SSD_G, seq, SSD_K)
        return jnp.pad(v, ((0, 0), (0, 0), (0, SSD_KPAD - SSD_K)))

    def from_cols(v):
        v = jnp.swapaxes(v[:, :, :SSD_K].reshape(nb, SSD_G, seq, SSD_K), 1, 2).reshape(t, SSD_H)
        return jnp.pad(v, ((0, 0), (0, LANE - SSD_H)))

    ssd_rows = jnp.concatenate([to_rows(cum), to_rows(dt)], axis=1)
    ssd_cols = jnp.concatenate([to_cols(cum), to_cols(ecum), to_cols(wend)], axis=2)
    y, states, *gathered = _ssd_scan_fwd(xs, bm, cm, ssd_rows, ssd_cols, nb, seq, carried.shards if carried else ())
    if carried:
        w = {**w, **carried.finish_weights(gathered)}
    post_op = _op_ssd_post("ssd_post", y, xs, z, w["d_exp"], w["norm_g"])
    yn = post_op.fwd((BF16,))[0]
    mix0 = _mm(yn, w["ssd_out"], name="ssd_out")
    ln0m = _op_ln("ln_mix0", x2, mix0, w["ln_mix_g"][0], w["ln_mix_b"][0])
    h0a, h0a_b = ln0m.fwd((F32, BF16))
    ff0, ffn0_saved = _ffn_fwd(0, h0a_b, w, nb, seq)
    ln0f = _op_ln("ln_ffn0", h0a, ff0, w["ln_ffn_g"][0], w["ln_ffn_b"][0])
    h1, h1_b = ln0f.fwd((F32, BF16))

    ckv = _mm(h1_b, w["kv_down_c"], name="kv_down_c")
    kr_in = _mm(h1_b, w["kv_down_r"], name="kv_down_r")
    kvn_op = _op_rms("kv_norm", ckv, w["kv_norm_g"])
    ckvn = kvn_op.fwd((BF16,))[0]
    kr_op = _op_rope("k_rope", _f_rope_dup, kr_in, cos, sin)
    kr = kr_op.fwd((BF16,))[0]
    kn = _mm(ckvn, w["kv_up_k"], out_dtype=BF16, name="kv_up_k")
    v = _mm(ckvn, w["kv_up_v"], out_dtype=BF16, name="kv_up_v")
    cq_raw = _mm(h1_b, w["q_down"], name="q_down")
    qn_op = _op_rms("q_norm", cq_raw, w["q_norm_g"])
    cq = qn_op.fwd((BF16,))[0]
    qn = _mm(cq, w["q_up_n"], out_dtype=BF16, name="q_up_n")
    qr_raw = _mm(cq, w["q_up_r"], name="q_up_r")
    qr_op = _op_rope("q_rope", _f_rope, qr_raw, cos, sin)
    qr = qr_op.fwd((BF16,))[0]
    o = _attn_fwd(qn, qr, kn, kr, v, nb, seq)
    mix1 = _mm(o, w["attn_out"], name="attn_out")
    ln1m = _op_ln("ln_mix1", h1, mix1, w["ln_mix_g"][1], w["ln_mix_b"][1])
    h1a, h1a_b = ln1m.fwd((F32, BF16))
    ff1, ffn1_saved = _ffn_fwd(1, h1a_b, w, nb, seq)
    ln1f = _op_ln("ln_ffn1", h1a, ff1, w["ln_ffn_g"][1], w["ln_ffn_b"][1])
    h2 = ln1f.fwd()[0]

    loss_parts, dh2 = _loss_head(h2, tgt2)

    (dh1a, dff1), (grads["ln_ffn_g1"], grads["ln_ffn_b1"]) = ln1f.bwd([dh2], [F32, BF16])
    dh1a = _add2(dh1a, _ffn_bwd(1, ffn1_saved, dff1, w, grads), "add_h1a")
    (dh1, dmix1), (grads["ln_mix_g1"], grads["ln_mix_b1"]) = ln1m.bwd([dh1a], [F32, BF16])
    grads["attn_out"] = _mm(o, dmix1, ta=True, name="attn_out_dw")
    do = _mm(dmix1, w["attn_out"], tb=True, name="attn_out_dx")
    dqn, dqr, dkn, dkr, dv = _attn_bwd(qn, qr, kn, kr, v, do, nb, seq)
    (dqr_raw,), _ = qr_op.bwd([dqr], [BF16])
    grads["q_up_n"] = _mm(cq, dqn, ta=True, name="q_up_n_dw")
    grads["q_up_r"] = _mm(cq, dqr_raw, ta=True, name="q_up_r_dw")
    dcq = _mm(dqn, w["q_up_n"], tb=True, name="q_up_n_dx")
    dcq = _mm(dqr_raw, w["q_up_r"], tb=True, add=dcq, name="q_up_r_dx")
    (dcq_raw,), (grads["q_norm_g"],) = qn_op.bwd([dcq], [BF16])
    grads["q_down"] = _mm(h1_b, dcq_raw, ta=True, name="q_down_dw")
    dh1 = _mm(dcq_raw, w["q_down"], tb=True, add=dh1, name="q_down_dx")
    grads["kv_up_k"] = _mm(ckvn, dkn, ta=True, name="kv_up_k_dw")
    grads["kv_up_v"] = _mm(ckvn, dv, ta=True, name="kv_up_v_dw")
    dckvn = _mm(dkn, w["kv_up_k"], tb=True, name="kv_up_k_dx")
    dckvn = _mm(dv, w["kv_up_v"], tb=True, add=dckvn, name="kv_up_v_dx")
    (dckv,), (grads["kv_norm_g"],) = kvn_op.bwd([dckvn], [BF16])
    (dkr_in,), _ = kr_op.bwd([dkr], [BF16])
    grads["kv_down_c"] = _mm(h1_b, dckv, ta=True, name="kv_down_c_dw")
    grads["kv_down_r"] = _mm(h1_b, dkr_in, ta=True, name="kv_down_r_dw")
    dh1 = _mm(dckv, w["kv_down_c"], tb=True, add=dh1, name="kv_down_c_dx")
    dh1 = _mm(dkr_in, w["kv_down_r"], tb=True, add=dh1, name="kv_down_r_dx")

    (dh0a, dff0), (grads["ln_ffn_g0"], grads["ln_ffn_b0"]) = ln0f.bwd([dh1], [F32, BF16])
    dh0a = _add2(dh0a, _ffn_bwd(0, ffn0_saved, dff0, w, grads), "add_h0a")
    (dx, dmix0), (grads["ln_mix_g0"], grads["ln_mix_b0"]) = ln0m.bwd([dh0a], [F32, BF16])
    grads["ssd_out"] = _mm(yn, dmix0, ta=True, name="ssd_out_dw")
    dyn = _mm(dmix0, w["ssd_out"], tb=True, name="ssd_out_dx")
    (dy, dxs_post, dz), (grads["d_exp"], grads["norm_g"]) = post_op.bwd([dyn], [F32, F32, BF16])
    dxs, dbm, dcm, drows, dcols, *received = _ssd_scan_bwd(xs, bm, cm, ssd_rows, ssd_cols, states, dy, nb, seq,
                                                           carried.grad_stacks(grads) if carried else ())
    if carried:
        carried.received = received
    dxs = _add2(dxs, dxs_post, "add_dxs")
    kq = SSD_KPAD
    d_cum = from_rows(drows[:, :kq]) + from_cols(dcols[:, :, :kq])
    (ddt_raw,), (grads["dt_bias"], grads["a_log"]) = dt_op.bwd(
        [from_rows(drows[:, kq:]), d_cum, from_cols(dcols[:, :, kq:2 * kq]), from_cols(dcols[:, :, 2 * kq:])], [BF16])
    draws = []
    for s, op, dout in zip("xbc", conv_ops, (dxs, dbm, dcm)):
        (dr,), (grads["conv_w_" + s], grads["conv_b_" + s]) = op.bwd([dout], [BF16])
        draws.append(dr)
    for k, dr in zip(("in_x", "in_b", "in_c"), draws):
        grads[k] = _mm(xb, dr, ta=True, name=f"ssd_{k}_dw")
        dx = _mm(dr, w[k], tb=True, add=dx, name=f"ssd_{k}_dx")
    grads["in_z"] = _mm(xb, dz, ta=True, name="ssd_in_z_dw")
    grads["in_dt"] = _mm(xb, ddt_raw, ta=True, name="ssd_in_dt_dw")
    dx = _mm(dz, w["in_z"], tb=True, add=dx, name="ssd_in_z_dx")
    dx = _mm(ddt_raw, w["in_dt"], tb=True, add=dx, name="ssd_in_dt_dx")
    return loss_parts, dx.reshape(nb, seq, d), grads


def _add2(a, b, name):
    t, c = a.shape
    tr = _rows(t, 512)

    def body(a_ref, b_ref, o_ref):
        o_ref[...] = a_ref[...] + b_ref[...]

    blk = pl.BlockSpec((tr, c), lambda i: (i, 0))
    return pl.pallas_call(body, name=name, grid=(t // tr,), in_specs=[blk, blk], out_specs=blk,
                          out_shape=jax.ShapeDtypeStruct((t, c), F32), compiler_params=_cparams())(a, b)


_XE, _BE, _CE = SSD_D_INNER, SSD_D_INNER + SSD_GN, SSD_D_INNER + 2 * SSD_GN


def _prep_weights(fw):
    w = {}
    f = FFN_HIDDEN
    if "ssd_in_proj" in fw:
        ip = fw["ssd_in_proj"][0]
        o = SSD_D_INNER
        w["in_z"], w["in_x"], w["in_b"], w["in_c"] = ip[:, :o], ip[:, o:o + _XE], ip[:, o + _XE:o + _BE], ip[:, o + _BE:o + _CE]
        w["in_dt"] = _pad_cols(ip[:, o + _CE:], LANE)
        cw, cb = fw["ssd_conv_w"][0], fw["ssd_conv_b"]
        for s, (lo, hi) in zip("xbc", ((0, _XE), (_XE, _BE), (_BE, _CE))):
            w["conv_w_" + s], w["conv_b_" + s] = cw[:, lo:hi], cb[:, lo:hi]
    if "ssd_dt_bias" in fw:
        w["dt_bias"], w["a_log"] = _pad_cols(fw["ssd_dt_bias"], LANE), _pad_cols(fw["ssd_A_log"], LANE)
        w["d_exp"] = jnp.repeat(fw["ssd_D"][0], SSD_P)[None, :]
        w["kv_norm_g"], w["q_norm_g"] = fw["kv_norm_g"][None, :], fw["q_norm_g"]
        w["ffn_cb_g"] = [fw["ffn_conv_b"][l:l + 1, :f] for l in range(DEPTH)]
        w["ffn_cb_v"] = [fw["ffn_conv_b"][l:l + 1, f:] for l in range(DEPTH)]
        for k in ("ln_mix_g", "ln_mix_b", "ln_ffn_g", "ln_ffn_b"):
            w[k] = [fw[k][l:l + 1] for l in range(DEPTH)]
    if "ssd_out_proj" in fw:
        w["norm_g"], w["ssd_out"] = fw["ssd_norm_g"], fw["ssd_out_proj"][0]
        kd = fw["kv_down_proj"]
        w["kv_down_c"], w["kv_down_r"] = kd[:, :MLA_KV_RANK], _pad_cols(kd[:, MLA_KV_RANK:], LANE)
        w["kv_up_k"], w["kv_up_v"] = fw["kv_up_k"], fw["kv_up_v"]
        w["q_down"] = fw["q_down_proj"][0]
        qu = fw["q_up_proj"][0].reshape(MLA_Q_RANK, MLA_H, MLA_NOPE + MLA_ROPE)
        w["q_up_n"] = qu[:, :, :MLA_NOPE].reshape(MLA_Q_RANK, MLA_H * MLA_NOPE)
        w["q_up_r"] = qu[:, :, MLA_NOPE:].reshape(MLA_Q_RANK, MLA_H * MLA_ROPE)
        w["attn_out"] = fw["attn_out_proj"][0]
        w["ffn_up_g"] = [fw["ffn_up"][l][:, :f] for l in range(DEPTH)]
        w["ffn_up_v"] = [fw["ffn_up"][l][:, f:] for l in range(DEPTH)]
        w["ffn_cw_g"] = [fw["ffn_conv_w"][l][:, :f] for l in range(DEPTH)]
        w["ffn_cw_v"] = [fw["ffn_conv_w"][l][:, f:] for l in range(DEPTH)]
        w["ffn_down"] = [fw["ffn_down"][l] for l in range(DEPTH)]
    return w


def _assemble_grads(g, names):
    make = {
        "ssd_in_proj": lambda: jnp.concatenate([g["in_z"], g["in_x"], g["in_b"], g["in_c"], g["in_dt"][:, :SSD_H]], axis=1)[None],
        "ssd_conv_w": lambda: jnp.concatenate([g["conv_w_" + s] for s in "xbc"], axis=1)[None],
        "ssd_conv_b": lambda: jnp.concatenate([g["conv_b_" + s] for s in "xbc"], axis=1),
        "ssd_dt_bias": lambda: g["dt_bias"][:, :SSD_H],
        "ssd_A_log": lambda: g["a_log"][:, :SSD_H],
        "ssd_D": lambda: jnp.sum(g["d_exp"].reshape(SSD_H, SSD_P), axis=1)[None, :],
        "ssd_norm_g": lambda: g["norm_g"],
        "ssd_out_proj": lambda: g["ssd_out"][None],
        "kv_down_proj": lambda: jnp.concatenate([g["kv_down_c"], g["kv_down_r"][:, :MLA_ROPE]], axis=1),
        "kv_norm_g": lambda: g["kv_norm_g"][0],
        "kv_up_k": lambda: g["kv_up_k"],
        "kv_up_v": lambda: g["kv_up_v"],
        "q_down_proj": lambda: g["q_down"][None],
        "q_norm_g": lambda: g["q_norm_g"],
        "q_up_proj": lambda: jnp.concatenate([g["q_up_n"].reshape(MLA_Q_RANK, MLA_H, MLA_NOPE),
                                              g["q_up_r"].reshape(MLA_Q_RANK, MLA_H, MLA_ROPE)], axis=2).reshape(1, MLA_Q_RANK, -1),
        "attn_out_proj": lambda: g["attn_out"][None],
        "ffn_up": lambda: jnp.stack([jnp.concatenate([g[f"ffn_up_g{l}"], g[f"ffn_up_v{l}"]], axis=1) for l in range(DEPTH)]),
        "ffn_conv_w": lambda: jnp.stack([jnp.concatenate([g[f"ffn_cw_g{l}"], g[f"ffn_cw_v{l}"]], axis=1) for l in range(DEPTH)]),
        "ffn_conv_b": lambda: jnp.concatenate([jnp.concatenate([g[f"ffn_cb_g{l}"], g[f"ffn_cb_v{l}"]], axis=1)
                                               for l in range(DEPTH)], axis=0),
        "ffn_down": lambda: jnp.stack([g[f"ffn_down{l}"] for l in range(DEPTH)]),
    }
    for k in ("ln_mix_g", "ln_mix_b", "ln_ffn_g", "ln_ffn_b"):
        make[k] = lambda k=k: jnp.concatenate([g[f"{k}{l}"] for l in range(DEPTH)], axis=0)
    return {n: make[n]() for n in names}


_WEIGHTS = ["ssd_in_proj", "ssd_conv_w", "ssd_conv_b", "ssd_dt_bias", "ssd_A_log", "ssd_D", "ssd_norm_g", "ssd_out_proj",
            "kv_down_proj", "kv_norm_g", "kv_up_k", "kv_up_v", "q_down_proj", "q_norm_g", "q_up_proj", "attn_out_proj",
            "ffn_up", "ffn_conv_w", "ffn_conv_b", "ffn_down", "ln_mix_g", "ln_mix_b", "ln_ffn_g", "ln_ffn_b"]
_COL_CUT = ["ssd_in_proj", "ssd_conv_w", "ssd_conv_b", "ssd_norm_g", "kv_up_k", "kv_up_v", "q_up_proj", "ffn_up", "ffn_conv_w"]
_ROW_CUT = ["ssd_out_proj", "kv_down_proj", "q_down_proj", "attn_out_proj", "ffn_down"]
_CUT = _COL_CUT + _ROW_CUT
_WHOLE = [n for n in _WEIGHTS if n not in _CUT]
_EARLY = ["ssd_in_proj", "ssd_conv_w", "ssd_conv_b"]
_LATE = [n for n in _CUT if n not in _EARLY]
_MXU_WEIGHTS = ["ssd_in_proj", "ssd_out_proj", "kv_down_proj", "kv_up_k", "kv_up_v", "q_down_proj", "q_up_proj",
                "attn_out_proj", "ffn_up", "ffn_down"]
_PACK_ROWS = 160


def _shard_2d(name, s):
    return s.reshape(-1, s.shape[-1])


def _unstack(name, g, shard_shape):
    if name in _COL_CUT:
        lead = shard_shape[:-1]
        return jnp.swapaxes(g, 0, 1).reshape(*lead, N_CHIPS * shard_shape[-1])
    lead, rs, c = shard_shape[:-2], shard_shape[-2], shard_shape[-1]
    n_lead = math.prod(lead)
    return jnp.swapaxes(g.reshape(N_CHIPS, n_lead, rs, c), 0, 1).reshape(*lead, N_CHIPS * rs, c)


def _stack(name, full, shard_shape):
    if name in _COL_CUT:
        cs = shard_shape[-1]
        return jnp.swapaxes(full.reshape(-1, N_CHIPS, cs), 0, 1)
    lead, rs, c = shard_shape[:-2], shard_shape[-2], shard_shape[-1]
    n_lead = math.prod(lead)
    return jnp.swapaxes(full.reshape(n_lead, N_CHIPS, rs, c), 0, 1).reshape(N_CHIPS, n_lead * rs, c)


def _pack(arrs):
    flat = jnp.concatenate([a.reshape(-1) for a in arrs])
    return jnp.pad(flat, (0, _PACK_ROWS * LANE - flat.shape[0])).reshape(_PACK_ROWS, LANE)


def _unpack(packed, like):
    flat, out, o = packed.reshape(-1), [], 0
    for a in like:
        out.append(flat[o:o + a.size].reshape(a.shape))
        o += a.size
    return out


_ARGS = ["x", "positions"] + _WEIGHTS + ["loss_target"] + ["m_" + n for n in _WEIGHTS] + ["v_" + n for n in _WEIGHTS]


def kernel(x, positions, ssd_in_proj, ssd_conv_w, ssd_conv_b, ssd_dt_bias, ssd_A_log, ssd_D, ssd_norm_g,
           ssd_out_proj, kv_down_proj, kv_norm_g, kv_up_k, kv_up_v, q_down_proj, q_norm_g, q_up_proj,
           attn_out_proj, ffn_up, ffn_conv_w, ffn_conv_b, ffn_down, ln_mix_g, ln_mix_b, ln_ffn_g, ln_ffn_b,
           loss_target, m_ssd_in_proj, m_ssd_conv_w, m_ssd_conv_b, m_ssd_dt_bias, m_ssd_A_log, m_ssd_D,
           m_ssd_norm_g, m_ssd_out_proj, m_kv_down_proj, m_kv_norm_g, m_kv_up_k, m_kv_up_v, m_q_down_proj,
           m_q_norm_g, m_q_up_proj, m_attn_out_proj, m_ffn_up, m_ffn_conv_w, m_ffn_conv_b, m_ffn_down,
           m_ln_mix_g, m_ln_mix_b, m_ln_ffn_g, m_ln_ffn_b, v_ssd_in_proj, v_ssd_conv_w, v_ssd_conv_b,
           v_ssd_dt_bias, v_ssd_A_log, v_ssd_D, v_ssd_norm_g, v_ssd_out_proj, v_kv_down_proj, v_kv_norm_g,
           v_kv_up_k, v_kv_up_v, v_q_down_proj, v_q_norm_g, v_q_up_proj, v_attn_out_proj, v_ffn_up,
           v_ffn_conv_w, v_ffn_conv_b, v_ffn_down, v_ln_mix_g, v_ln_mix_b, v_ln_ffn_g, v_ln_ffn_b):
    args = (x, positions, ssd_in_proj, ssd_conv_w, ssd_conv_b, ssd_dt_bias, ssd_A_log, ssd_D, ssd_norm_g,
            ssd_out_proj, kv_down_proj, kv_norm_g, kv_up_k, kv_up_v, q_down_proj, q_norm_g, q_up_proj,
            attn_out_proj, ffn_up, ffn_conv_w, ffn_conv_b, ffn_down, ln_mix_g, ln_mix_b, ln_ffn_g, ln_ffn_b,
            loss_target, m_ssd_in_proj, m_ssd_conv_w, m_ssd_conv_b, m_ssd_dt_bias, m_ssd_A_log, m_ssd_D,
            m_ssd_norm_g, m_ssd_out_proj, m_kv_down_proj, m_kv_norm_g, m_kv_up_k, m_kv_up_v, m_q_down_proj,
            m_q_norm_g, m_q_up_proj, m_attn_out_proj, m_ffn_up, m_ffn_conv_w, m_ffn_conv_b, m_ffn_down,
            m_ln_mix_g, m_ln_mix_b, m_ln_ffn_g, m_ln_ffn_b, v_ssd_in_proj, v_ssd_conv_w, v_ssd_conv_b,
            v_ssd_dt_bias, v_ssd_A_log, v_ssd_D, v_ssd_norm_g, v_ssd_out_proj, v_kv_down_proj, v_kv_norm_g,
            v_kv_up_k, v_kv_up_v, v_q_down_proj, v_q_norm_g, v_q_up_proj, v_attn_out_proj, v_ffn_up,
            v_ffn_conv_w, v_ffn_conv_b, v_ffn_down, v_ln_mix_g, v_ln_mix_b, v_ln_ffn_g, v_ln_ffn_b)
    a = dict(zip(_ARGS, args, strict=True))

    def shard(n):
        s = _shard_2d(n, a[n])
        return s.astype(BF16) if n in _MXU_WEIGHTS else s

    def full_weights(names, gathered):
        return {n: _unstack(n, g, a[n].shape) for n, g in zip(names, gathered)}

    def grad_stacks(names, pieces):
        full = _assemble_grads(pieces, names)
        return [_stack(n, full[n], a[n].shape).astype(BF16) for n in names]

    fw = full_weights(_EARLY, _gather_chips([shard(n) for n in _EARLY]))
    fw.update({n: a[n] for n in _WHOLE})
    carried = _Carried([shard(n) for n in _LATE], lambda got: _prep_weights(full_weights(_LATE, got)),
                       lambda pieces: grad_stacks(_LATE, pieces))

    loss_parts, grad_x, pieces = _local_step(a["x"], a["positions"], a["loss_target"], _prep_weights(fw), carried)
    loss = lax.psum(jnp.sum(loss_parts[::8, 0]), ("x", "y", "c"))

    bufs = dict(zip(_LATE, carried.received))
    bufs.update(zip(_EARLY, _scatter_chips(grad_stacks(_EARLY, pieces))))
    sums = [_sum_parts(bufs[n], "sum_chips_" + n) for n in _CUT]
    others = _swap_cores(sums)
    full = _assemble_grads(pieces, _WHOLE)
    every = _gather_all(_pack([full[n] for n in _WHOLE]))
    g_whole = _sum_parts(every.reshape(N_DEV, _PACK_ROWS, LANE), "sum_devices")

    res = {}
    for n, s, o in zip(_CUT, sums, others):
        out = _adamw(_shard_2d(n, a[n]), s, o, _shard_2d(n, a["m_" + n]), _shard_2d(n, a["v_" + n]), "adamw_" + n)
        res[n] = [r.reshape(a[n].shape) for r in out]
    whole = [a[n] for n in _WHOLE]
    out = _adamw(_pack(whole), g_whole, None, _pack([a["m_" + n] for n in _WHOLE]), _pack([a["v_" + n] for n in _WHOLE]),
                 "adamw_whole")
    for k, n in enumerate(_WHOLE):
        res[n] = [_unpack(r, whole)[k] for r in out]
    return (loss, grad_x, *[res[n][0] for n in _WEIGHTS], *[res[n][1] for n in _WEIGHTS],
            *[res[n][2] for n in _WEIGHTS], *[res[n][3] for n in _WEIGHTS])
```

```python
import functools
import math

import jax
import jax.numpy as jnp
from jax import lax
from jax.experimental import pallas as pl
from jax.experimental.pallas import tpu as pltpu

F32 = jnp.float32
BF16 = jnp.bfloat16
HIGHEST = lax.Precision.HIGHEST
MESH = pl.DeviceIdType.MESH

D_MODEL = 1024
DEPTH = 2
DN_ALPHA = (2 * DEPTH) ** 0.25
SSD_D_INNER = 2048
SSD_P = 64
SSD_H = 32
SSD_G = 8
SSD_K = 4
SSD_N = 128
SSD_L = 128
SSD_GN = SSD_G * SSD_N
MLA_H = 8
MLA_Q_RANK = 384
MLA_KV_RANK = 256
MLA_NOPE = 128
MLA_ROPE = 64
MLA_V = 128
ROPE_THETA = 10000.0
FFN_HIDDEN = 2816
LN_EPS = 1e-5
RMS_EPS = 1e-6
ADAM_LR = 0.001
ADAM_B1 = 0.9
ADAM_B2 = 0.999
ADAM_EPS = 1e-08
ADAM_WD = 0.01
ADAM_STEP = 10

N_CHIPS = 4
N_DEV = 8
LANE = 128
VMEM_LIMIT = 56 * 1024 * 1024


def _cparams(**kw):
    return pltpu.CompilerParams(vmem_limit_bytes=VMEM_LIMIT, **kw)


def _tile(dim, cap):
    best = None
    t = LANE
    while t <= min(dim, cap):
        if dim % t == 0:
            best = t
        t += LANE
    return dim if best is None else best


MM_TILE_CAP = 1408
MM_WHOLE_K = 2816
MM_VMEM_BUDGET = 40 * 1024 * 1024


def _mm_tiles(m, n, k, a_bytes, b_bytes, o_bytes, has_add):
    tm, tn = _tile(m, MM_TILE_CAP), _tile(n, MM_TILE_CAP)
    tk = k if k <= MM_WHOLE_K else _tile(k, 1024)

    def need(tm, tn):
        acc = tm * tn * 4 if tk < k else 0
        return 2 * (tm * tk * a_bytes + tk * tn * b_bytes + tm * tn * o_bytes + (tm * tn * 4 if has_add else 0)) + acc

    while need(tm, tn) > MM_VMEM_BUDGET:
        if tm >= tn and _tile(m, tm // 2) < tm:
            tm = _tile(m, tm // 2)
        elif _tile(n, tn // 2) < tn:
            tn = _tile(n, tn // 2)
        else:
            break
    return tm, tn, tk


def _mm(a, b, *, ta=False, tb=False, add=None, out_dtype=F32, name):
    m, k = (a.shape[1], a.shape[0]) if ta else a.shape
    n = b.shape[0] if tb else b.shape[1]
    assert (b.shape[1] if tb else b.shape[0]) == k
    tm, tn, tk = _mm_tiles(m, n, k, a.dtype.itemsize, b.dtype.itemsize, jnp.dtype(out_dtype).itemsize, add is not None)
    nk = k // tk
    dims = (((0 if ta else 1,), (1 if tb else 0,)), ((), ()))

    def partial_product(a_ref, b_ref):
        return lax.dot_general(a_ref[...].astype(BF16), b_ref[...].astype(BF16), dims, preferred_element_type=F32)

    def body_one(*refs):
        if add is None:
            a_ref, b_ref, o_ref = refs
            o_ref[...] = partial_product(a_ref, b_ref).astype(out_dtype)
        else:
            a_ref, b_ref, c_ref, o_ref = refs
            o_ref[...] = (partial_product(a_ref, b_ref) + c_ref[...]).astype(out_dtype)

    def body_acc(*refs):
        if add is None:
            a_ref, b_ref, o_ref, acc = refs
        else:
            a_ref, b_ref, c_ref, o_ref, acc = refs
        kk = pl.program_id(2)

        @pl.when(kk == 0)
        def _():
            acc[...] = jnp.zeros_like(acc) if add is None else c_ref[...]

        acc[...] += partial_product(a_ref, b_ref)

        @pl.when(kk == nk - 1)
        def _():
            o_ref[...] = acc[...].astype(out_dtype)

    a_spec = pl.BlockSpec((tk, tm), lambda i, j, kk: (kk, i)) if ta else pl.BlockSpec((tm, tk), lambda i, j, kk: (i, kk))
    b_spec = pl.BlockSpec((tn, tk), lambda i, j, kk: (j, kk)) if tb else pl.BlockSpec((tk, tn), lambda i, j, kk: (kk, j))
    o_spec = pl.BlockSpec((tm, tn), lambda i, j, kk: (i, j))
    ins, specs = [a, b], [a_spec, b_spec]
    if add is not None:
        ins.append(add)
        specs.append(o_spec)
    return pl.pallas_call(
        body_one if nk == 1 else body_acc, name=name, grid=(m // tm, n // tn, nk), in_specs=specs, out_specs=o_spec,
        out_shape=jax.ShapeDtypeStruct((m, n), out_dtype),
        scratch_shapes=[] if nk == 1 else [pltpu.VMEM((tm, tn), F32)],
        compiler_params=_cparams(dimension_semantics=("parallel", "parallel", "arbitrary")),
    )(*ins)


def _spec(op):
    return pl.BlockSpec(op[1], op[2])


def _bw_fwd(name, fn, grid, ins, outs, out_dtypes):
    n_in = len(ins)
    flat = [(o, dt) for o, dts in zip(outs, out_dtypes) for dt in dts]

    def body(*refs):
        res = fn(*[r[...].astype(F32) for r in refs[:n_in]])
        orefs = iter(refs[n_in:])
        for v, dts in zip(res, out_dtypes):
            for dt in dts:
                next(orefs)[...] = v.astype(dt)

    return pl.pallas_call(
        body, name=name, grid=grid, in_specs=[_spec(o) for o in ins],
        out_specs=[pl.BlockSpec(o[1], o[2]) for o, _ in flat],
        out_shape=[jax.ShapeDtypeStruct(o[0], dt) for o, dt in flat], compiler_params=_cparams(),
    )(*[o[0] for o in ins])


def _bw_bwd(name, fn, grid, data, params, consts, cts, red_axes, grad_dtypes):
    nd, npar, nc, nct = len(data), len(params), len(consts), len(cts)

    def body(*refs):
        first = None
        for ax in red_axes:
            z = pl.program_id(ax) == 0
            first = z if first is None else jnp.logical_and(first, z)
        vals = [r[...].astype(F32) for r in refs[:nd + npar + nc + nct]]
        d, p, c, g = vals[:nd], vals[nd:nd + npar], vals[nd + npar:nd + npar + nc], vals[nd + npar + nc:]
        _, vjp = jax.vjp(lambda dd, pp: tuple(fn(*dd, *pp, *c)), d, p)
        gd, gp = vjp(tuple(g))
        orefs = refs[nd + npar + nc + nct:]
        for r, v in zip(orefs[:nd], gd):
            r[...] = v.astype(r.dtype)
        if npar:
            @pl.when(first)
            def _():
                for r in orefs[nd:]:
                    r[...] = jnp.zeros_like(r)

            for r, v in zip(orefs[nd:], gp):
                r[...] += v

    ins = list(data) + list(params) + list(consts) + list(cts)
    outs = list(data) + list(params)
    dtypes = list(grad_dtypes) + [F32] * npar
    return pl.pallas_call(
        body, name=name, grid=grid, in_specs=[_spec(o) for o in ins], out_specs=[_spec(o) for o in outs],
        out_shape=[jax.ShapeDtypeStruct(o[0].shape, dt) for o, dt in zip(outs, dtypes)], compiler_params=_cparams(),
    )(*[o[0] for o in ins])


def _shift_down(x, s):
    row = lax.broadcasted_iota(jnp.int32, x.shape, 0)
    return jnp.where(row < s, 0.0, pltpu.roll(x, s, 0))


def _shift_up(x, s):
    n = x.shape[0]
    row = lax.broadcasted_iota(jnp.int32, x.shape, 0)
    return jnp.where(row >= n - s, 0.0, pltpu.roll(x, n - s, 0))


def _time_shift(s):
    if s == 0:
        return lambda x: x

    @jax.custom_vjp
    def shift(x):
        return _shift_down(x, s)

    shift.defvjp(lambda x: (_shift_down(x, s), None), lambda _, g: (_shift_up(g, s),))
    return shift


def _rot_half_raw(x):
    lane = lax.broadcasted_iota(jnp.int32, x.shape, 1)
    return jnp.where(lane % MLA_ROPE < MLA_ROPE // 2, -pltpu.roll(x, LANE - MLA_ROPE // 2, 1), pltpu.roll(x, MLA_ROPE // 2, 1))


@jax.custom_vjp
def _rot_half(x):
    return _rot_half_raw(x)


_rot_half.defvjp(lambda x: (_rot_half_raw(x), None), lambda _, g: (-_rot_half_raw(g),))


@jax.custom_vjp
def _roll_half_lanes(x):
    return pltpu.roll(x, LANE // 2, 1)


_roll_half_lanes.defvjp(lambda x: (pltpu.roll(x, LANE // 2, 1), None), lambda _, g: (pltpu.roll(g, LANE // 2, 1),))


def _causal_conv(u, w, b):
    width = w.shape[0]
    y = b
    for k in range(width):
        y = y + w[k:k + 1, :] * _time_shift(width - 1 - k)(u)
    return y


def _silu(x):
    return x * jax.nn.sigmoid(x)


def _f_ln(h, mix, g, b):
    x = DN_ALPHA * h + mix
    mu = jnp.mean(x, axis=-1, keepdims=True)
    xc = x - mu
    var = jnp.mean(xc * xc, axis=-1, keepdims=True)
    return (xc * lax.rsqrt(var + LN_EPS) * g + b,)


def _f_convsilu(u, w, b):
    return (_silu(_causal_conv(u, w, b)),)


def _f_convffn(ug, uv, wg, wv, bg, bv):
    return (_silu(_causal_conv(ug, wg, bg)) * _causal_conv(uv, wv, bv),)


def _f_dt(dt_raw, bias, a_log):
    x = dt_raw + bias
    dt = jnp.maximum(x, 0.0) + jnp.log(1.0 + jnp.exp(-jnp.abs(x)))
    a = dt * (-jnp.exp(a_log))
    n = a.shape[0]
    lower = (lax.broadcasted_iota(jnp.int32, (n, n), 0) >= lax.broadcasted_iota(jnp.int32, (n, n), 1)).astype(F32)
    cum = jnp.dot(lower, a, precision=HIGHEST, preferred_element_type=F32)
    cum_last = jnp.sum(a, axis=0, keepdims=True)
    return dt, cum, jnp.exp(cum), jnp.exp(cum_last - cum) * dt


def _f_ssd_post(y, xs, z, d_exp, ng):
    t = (y + d_exp * xs) * _silu(z)
    return (t * lax.rsqrt(jnp.mean(t * t, axis=-1, keepdims=True) + LN_EPS) * ng,)


def _f_rms(x, g):
    return (x * lax.rsqrt(jnp.mean(x * x, axis=-1, keepdims=True) + RMS_EPS) * g,)


def _f_rope(x, cos, sin):
    return (x * cos + _rot_half(x) * sin,)


def _f_rope_dup(x, cos, sin):
    r = x * cos + _rot_half(x) * sin
    return (r + _roll_half_lanes(r),)


def _rows(t, cap=512):
    for c in (cap, 256, 128, 64, 32, 16, 8):
        if c <= cap and t % c == 0:
            return c
    return t


class _Blockwise:
    def __init__(self, name, fn, grid, data, params, consts, outs, red_axes):
        self.name, self.fn, self.grid = name, fn, grid
        self.data, self.params, self.consts, self.outs, self.red_axes = data, params, consts, outs, red_axes

    def fwd(self, *out_dtypes):
        out_dtypes = out_dtypes or tuple((F32,) for _ in self.outs)
        return _bw_fwd(self.name + "_fwd", self.fn, self.grid, self.data + self.params + self.consts, self.outs, out_dtypes)

    def bwd(self, cts, grad_dtypes=None):
        cts = [(c, o[1], o[2]) for c, o in zip(cts, self.outs)]
        grad_dtypes = grad_dtypes or [F32] * len(self.data)
        res = _bw_bwd(self.name + "_bwd", self.fn, self.grid, self.data, self.params, self.consts, cts, self.red_axes,
                      grad_dtypes)
        return res[:len(self.data)], res[len(self.data):]


def _op_ln(name, h, mix, g, b):
    t, d = h.shape
    tr = _rows(t, 256)
    row = ((tr, d), lambda i: (i, 0))
    par = ((1, d), lambda i: (0, 0))
    return _Blockwise(name, _f_ln, (t // tr,), [(h, *row), (mix, *row)], [(g, *par), (b, *par)], [],
                      [((t, d), *row)], (0,))


def _op_conv(name, fn, us, ws, bs, nb, seq):
    c = us[0].shape[1]
    ct = _tile(c, 256)
    blk = ((seq, ct), lambda j, bb: (bb, j))
    data = [(u, *blk) for u in us]
    params = [(w, (w.shape[0], ct), lambda j, bb: (0, j)) for w in ws] + [(b, (1, ct), lambda j, bb: (0, j)) for b in bs]
    return _Blockwise(name, fn, (c // ct, nb), data, params, [], [((nb * seq, c), *blk)], (1,))


def _op_dt(name, dt_raw, bias, a_log):
    t = dt_raw.shape[0]
    row = ((SSD_L, LANE), lambda i: (i, 0))
    par = ((1, LANE), lambda i: (0, 0))
    return _Blockwise(name, _f_dt, (t // SSD_L,), [(dt_raw, *row)], [(bias, *par), (a_log, *par)], [],
                      [((t, LANE), *row)] * 4, (0,))


def _op_ssd_post(name, y, xs, z, d_exp, ng):
    t, c = y.shape
    gw = c // SSD_G
    tr = _rows(t, 512)
    blk = ((tr, gw), lambda g, i: (i, g))
    par = ((1, gw), lambda g, i: (0, g))
    return _Blockwise(name, _f_ssd_post, (SSD_G, t // tr), [(y, *blk), (xs, *blk), (z, *blk)],
                      [(d_exp, *par), (ng, *par)], [], [((t, c), *blk)], (1,))


def _op_rms(name, x, g):
    t, c = x.shape
    tr = _rows(t, 512)
    return _Blockwise(name, _f_rms, (t // tr,), [(x, (tr, c), lambda i: (i, 0))], [(g, (1, c), lambda i: (0, 0))], [],
                      [((t, c), (tr, c), lambda i: (i, 0))], (0,))


def _op_rope(name, fn, x, cos, sin):
    t, c = x.shape
    tr = _rows(t, 512)
    blk = ((tr, LANE), lambda i, j: (i, j))
    cs = ((tr, LANE), lambda i, j: (i, 0))
    return _Blockwise(name, fn, (t // tr, c // LANE), [(x, *blk)], [], [(cos, *cs), (sin, *cs)], [((t, c), *blk)], ())


_NT = (((1,), (1,)), ((), ()))
_TN = (((0,), (0,)), ((), ()))


def _ssd_head(x, g, bc, cc, s, cum_row, dt_row, cum_col, ecum_col, wend_col):
    n = g.shape[0]
    row = lax.broadcasted_iota(jnp.int32, (n, n), 0)
    col = lax.broadcasted_iota(jnp.int32, (n, n), 1)
    decay = jnp.exp(jnp.where(row >= col, cum_col - cum_row, -jnp.inf))
    w = g * decay * dt_row
    y = jnp.dot(w.astype(BF16), x.astype(BF16), preferred_element_type=F32)
    y = y + lax.dot_general((cc * ecum_col).astype(BF16), s.astype(BF16), _NT, preferred_element_type=F32)
    lane = lax.broadcasted_iota(jnp.int32, cum_row.shape, 1)
    cum_last = jnp.sum(jnp.where(lane == n - 1, cum_row, 0.0), axis=1, keepdims=True)
    s_new = s * jnp.exp(cum_last) + lax.dot_general(x.astype(BF16), (bc * wend_col).astype(BF16), _TN,
                                                    preferred_element_type=F32)
    return y, s_new


SSD_KPAD = 8
SSD_ROWS = 2 * SSD_KPAD
SSD_COLS = 3 * SSD_KPAD


def _ssd_head_args(rows, cols, k):
    return (rows[k:k + 1], rows[SSD_KPAD + k:SSD_KPAD + k + 1], cols[:, k:k + 1],
            cols[:, SSD_KPAD + k:SSD_KPAD + k + 1], cols[:, 2 * SSD_KPAD + k:2 * SSD_KPAD + k + 1])


def _carried_exchange(arrs, in_refs, out_refs, sems, scatter, n_steps):
    if not arrs:
        return lambda: None, lambda: None
    first = jnp.logical_and(pl.program_id(0) == 0, pl.program_id(1) == 0)
    last = jnp.logical_and(pl.program_id(0) == n_steps[0] - 1, pl.program_id(1) == n_steps[1] - 1)
    start, wait = _chip_exchange(in_refs, out_refs, sems, scatter)
    return (lambda: pl.when(first)(start)), (lambda: pl.when(last)(wait))


def _ssd_scan_fwd(xs, bm, cm, rows, cols, nb, seq, gather=()):
    nc = seq // SSD_L
    kp = SSD_K * SSD_P
    ng = len(gather)

    def body(*refs):
        xs_ref, b_ref, c_ref, row_ref, col_ref = refs[:5]
        y_ref, st_ref = refs[5 + ng:7 + ng]
        start, wait = _carried_exchange(gather, refs[5:5 + ng], refs[7 + ng:7 + 2 * ng], refs[7 + 2 * ng:], False,
                                        (nb, SSD_G))
        start()

        def chunk(c, states):
            sl = pl.ds(pl.multiple_of(c * SSD_L, SSD_L), SSD_L)
            bc, cc = b_ref[sl, :], c_ref[sl, :]
            g = lax.dot_general(cc.astype(BF16), bc.astype(BF16), _NT, preferred_element_type=F32)
            rows_c, cols_c = row_ref[0, :, sl], col_ref[0, sl, :]
            new = []
            for k in range(SSD_K):
                hs = pl.ds(k * SSD_P, SSD_P)
                st_ref[0, c * SSD_K + k] = states[k]
                y, s_new = _ssd_head(xs_ref[sl, hs], g, bc, cc, states[k], *_ssd_head_args(rows_c, cols_c, k))
                y_ref[sl, hs] = y
                new.append(s_new)
            return tuple(new)

        lax.fori_loop(0, nc, chunk, tuple(jnp.zeros((SSD_P, SSD_N), F32) for _ in range(SSD_K)))
        wait()

    t = xs.shape[0]
    any_spec = pl.BlockSpec(memory_space=pl.ANY)
    return pl.pallas_call(
        body, name="ssd_scan_fwd", grid=(nb, SSD_G),
        in_specs=[pl.BlockSpec((seq, kp), lambda b, g: (b, g)),
                  pl.BlockSpec((seq, SSD_N), lambda b, g: (b, g)),
                  pl.BlockSpec((seq, SSD_N), lambda b, g: (b, g)),
                  pl.BlockSpec((1, SSD_ROWS, seq), lambda b, g: (b * SSD_G + g, 0, 0)),
                  pl.BlockSpec((1, seq, SSD_COLS), lambda b, g: (b * SSD_G + g, 0, 0))] + [any_spec] * ng,
        out_specs=[pl.BlockSpec((seq, kp), lambda b, g: (b, g)),
                   pl.BlockSpec((1, nc * SSD_K, SSD_P, SSD_N), lambda b, g: (b * SSD_G + g, 0, 0, 0))] + [any_spec] * ng,
        out_shape=[jax.ShapeDtypeStruct((t, SSD_D_INNER), F32),
                   jax.ShapeDtypeStruct((nb * SSD_G, nc * SSD_K, SSD_P, SSD_N), F32)] + _exchange_shapes(gather, False),
        scratch_shapes=_exchange_sems(ng) if ng else [],
        compiler_params=_cparams(dimension_semantics=("arbitrary", "arbitrary")),
    )(xs, bm, cm, rows, cols, *gather)


def _ssd_scan_bwd(xs, bm, cm, rows, cols, states, dy, dxs_skip, nb, seq, scatter=()):
    nc = seq // SSD_L
    kp = SSD_K * SSD_P
    ns = len(scatter)

    def body(*refs):
        xs_ref, b_ref, c_ref, row_ref, col_ref, st_ref, dy_ref, skip_ref = refs[:8]
        dxs_ref, db_ref, dc_ref, drow_ref, dcol_ref = refs[8 + ns:13 + ns]
        start, wait = _carried_exchange(scatter, refs[8:8 + ns], refs[13 + ns:13 + 2 * ns], refs[13 + 2 * ns:], True,
                                        (nb, SSD_G))
        start()

        def chunk(i, dstates):
            c = nc - 1 - i
            sl = pl.ds(pl.multiple_of(c * SSD_L, SSD_L), SSD_L)
            bc, cc = b_ref[sl, :], c_ref[sl, :]
            bcb, ccb = bc.astype(BF16), cc.astype(BF16)
            g = lax.dot_general(ccb, bcb, _NT, preferred_element_type=F32)
            rows_c, cols_c = row_ref[0, :, sl], col_ref[0, sl, :]
            drow_ref[0, :, sl] = jnp.zeros((SSD_ROWS, SSD_L), F32)
            dcol_ref[0, sl, :] = jnp.zeros((SSD_L, SSD_COLS), F32)
            db = jnp.zeros((SSD_L, SSD_N), F32)
            dc = jnp.zeros((SSD_L, SSD_N), F32)
            dg = jnp.zeros((SSD_L, SSD_L), F32)
            new = []
            for k in range(SSD_K):
                hs = pl.ds(k * SSD_P, SSD_P)
                _, vjp = jax.vjp(_ssd_head, xs_ref[sl, hs], g, bc, cc, st_ref[0, c * SSD_K + k],
                                 *_ssd_head_args(rows_c, cols_c, k))
                dx, dgk, dbk, dck, ds, d_cum_row, d_dt_row, d_cum_col, d_ecum_col, d_wend_col = vjp(
                    (dy_ref[sl, hs], dstates[k]))
                dxs_ref[sl, hs] = dx + skip_ref[sl, hs]
                db, dc, dg = db + dbk, dc + dck, dg + dgk
                drow_ref[0, k:k + 1, sl] = d_cum_row
                drow_ref[0, SSD_KPAD + k:SSD_KPAD + k + 1, sl] = d_dt_row
                dcol_ref[0, sl, k:k + 1] = d_cum_col
                dcol_ref[0, sl, SSD_KPAD + k:SSD_KPAD + k + 1] = d_ecum_col
                dcol_ref[0, sl, 2 * SSD_KPAD + k:2 * SSD_KPAD + k + 1] = d_wend_col
                new.append(ds)
            dgb = dg.astype(BF16)
            dc_ref[sl, :] = dc + jnp.dot(dgb, bcb, preferred_element_type=F32)
            db_ref[sl, :] = db + lax.dot_general(dgb, ccb, _TN, preferred_element_type=F32)
            return tuple(new)

        lax.fori_loop(0, nc, chunk, tuple(jnp.zeros((SSD_P, SSD_N), F32) for _ in range(SSD_K)))
        wait()

    t = xs.shape[0]
    x_spec = pl.BlockSpec((seq, kp), lambda b, g: (b, g))
    n_spec = pl.BlockSpec((seq, SSD_N), lambda b, g: (b, g))
    r_spec = pl.BlockSpec((1, SSD_ROWS, seq), lambda b, g: (b * SSD_G + g, 0, 0))
    c_spec = pl.BlockSpec((1, seq, SSD_COLS), lambda b, g: (b * SSD_G + g, 0, 0))
    any_spec = pl.BlockSpec(memory_space=pl.ANY)
    return pl.pallas_call(
        body, name="ssd_scan_bwd", grid=(nb, SSD_G),
        in_specs=[x_spec, n_spec, n_spec, r_spec, c_spec,
                  pl.BlockSpec((1, nc * SSD_K, SSD_P, SSD_N), lambda b, g: (b * SSD_G + g, 0, 0, 0)), x_spec, x_spec]
        + [any_spec] * ns,
        out_specs=[x_spec, n_spec, n_spec, r_spec, c_spec] + [any_spec] * ns,
        out_shape=[jax.ShapeDtypeStruct((t, SSD_D_INNER), F32), jax.ShapeDtypeStruct((t, SSD_GN), F32),
                   jax.ShapeDtypeStruct((t, SSD_GN), F32), jax.ShapeDtypeStruct(rows.shape, F32),
                   jax.ShapeDtypeStruct(cols.shape, F32)] + _exchange_shapes(scatter, True),
        scratch_shapes=_exchange_sems(ns) if ns else [],
        compiler_params=_cparams(dimension_semantics=("arbitrary", "arbitrary")),
    )(xs, bm, cm, rows, cols, states, dy, dxs_skip, *scatter)


ATT_TQ = 256
ATT_SCALE = (MLA_NOPE + MLA_ROPE) ** -0.5


def _pair_masks(shape):
    lane = lax.broadcasted_iota(jnp.int32, shape, 1)
    return lane < MLA_ROPE, lane >= MLA_ROPE


def _scores(qn, qr, kn, kr, diagonal):
    s = lax.dot_general(qn, kn, _NT, preferred_element_type=F32) + lax.dot_general(qr, kr, _NT, preferred_element_type=F32)
    s = s * ATT_SCALE
    if diagonal:
        row = lax.broadcasted_iota(jnp.int32, s.shape, 0)
        col = lax.broadcasted_iota(jnp.int32, s.shape, 1)
        s = jnp.where(col <= row, s, -jnp.inf)
    return s


def _attn_specs(nb, seq):
    nq = seq // ATT_TQ
    qn = pl.BlockSpec((ATT_TQ, 2 * MLA_NOPE), lambda b, hp, qi: (b * nq + qi, hp))
    qr = pl.BlockSpec((ATT_TQ, LANE), lambda b, hp, qi: (b * nq + qi, hp))
    kn = pl.BlockSpec((seq, 2 * MLA_NOPE), lambda b, hp, qi: (b, hp))
    kr = pl.BlockSpec((seq, LANE), lambda b, hp, qi: (b, 0))
    return (nb, MLA_H // 2, nq), qn, qr, kn, kr


def _attn_fwd(qn, qr, kn, kr, v, nb, seq):
    grid, s_qn, s_qr, s_kn, s_kr = _attn_specs(nb, seq)
    tq = ATT_TQ

    def body(qn_ref, qr_ref, kn_ref, kr_ref, v_ref, o_ref, lse_ref):
        qi = pl.program_id(2)
        qr = qr_ref[...]
        masks = _pair_masks(qr.shape)
        outs, lses = [], []
        for j in range(2):
            hs = pl.ds(j * MLA_NOPE, MLA_NOPE)
            qn_j, qr_j = qn_ref[:, hs], jnp.where(masks[j], qr, jnp.zeros_like(qr))

            def chunk(kc, carry, diagonal):
                m, l, acc = carry
                ks = pl.ds(pl.multiple_of(kc * tq, tq), tq)
                s = _scores(qn_j, qr_j, kn_ref[ks, hs], kr_ref[ks, :], diagonal)
                m_new = jnp.maximum(m, jnp.max(s, axis=1, keepdims=True))
                alpha, p = jnp.exp(m - m_new), jnp.exp(s - m_new)
                l = alpha * l + jnp.sum(p, axis=1, keepdims=True)
                acc = alpha * acc + jnp.dot(p.astype(BF16), v_ref[ks, hs], preferred_element_type=F32)
                return m_new, l, acc

            init = (jnp.full((tq, 1), -jnp.inf, F32), jnp.zeros((tq, 1), F32), jnp.zeros((tq, MLA_V), F32))
            carry = lax.fori_loop(0, qi, lambda kc, c: chunk(kc, c, False), init)
            m, l, acc = chunk(qi, carry, True)
            outs.append(acc / l)
            lses.append(m + jnp.log(l))
        o_ref[...] = jnp.concatenate(outs, axis=1).astype(o_ref.dtype)
        lse_ref[...] = jnp.where(masks[0], lses[0], lses[1])

    return pl.pallas_call(
        body, name="attn_fwd", grid=grid, in_specs=[s_qn, s_qr, s_kn, s_kr, s_kn], out_specs=[s_qn, s_qr],
        out_shape=[jax.ShapeDtypeStruct(qn.shape, BF16), jax.ShapeDtypeStruct(qr.shape, F32)], compiler_params=_cparams(),
    )(qn, qr, kn, kr, v)


def _attn_bwd(qn, qr, kn, kr, v, o, lse, do, nb, seq):
    grid, s_qn, s_qr, s_kn, s_kr = _attn_specs(nb, seq)
    tq = ATT_TQ

    def body(qn_ref, qr_ref, kn_ref, kr_ref, v_ref, o_ref, lse_ref, do_ref, dqn_ref, dqr_ref, dkn_ref, dkr_ref, dv_ref):
        hp, qi = pl.program_id(1), pl.program_id(2)

        @pl.when(qi == 0)
        def _():
            dkn_ref[...] = jnp.zeros_like(dkn_ref)
            dv_ref[...] = jnp.zeros_like(dv_ref)

        @pl.when(jnp.logical_and(qi == 0, hp == 0))
        def _():
            dkr_ref[...] = jnp.zeros_like(dkr_ref)

        qr, lse = qr_ref[...], lse_ref[...]
        masks = _pair_masks(qr.shape)
        dqr_heads = []
        for j in range(2):
            hs = pl.ds(j * MLA_NOPE, MLA_NOPE)
            qn_j, qr_j = qn_ref[:, hs], jnp.where(masks[j], qr, jnp.zeros_like(qr))
            do_j = do_ref[:, hs]
            dob = do_j.astype(BF16)
            delta = jnp.sum(do_j * o_ref[:, hs].astype(F32), axis=1, keepdims=True)
            lse_j = lse[:, j * MLA_ROPE:j * MLA_ROPE + 1]

            def chunk(kc, carry, diagonal):
                dqn, dqr = carry
                ks = pl.ds(pl.multiple_of(kc * tq, tq), tq)
                kn_c, kr_c, v_c = kn_ref[ks, hs], kr_ref[ks, :], v_ref[ks, hs]
                p = jnp.exp(_scores(qn_j, qr_j, kn_c, kr_c, diagonal) - lse_j)
                dp = lax.dot_general(dob, v_c, _NT, preferred_element_type=F32)
                ds = (p * (dp - delta) * ATT_SCALE).astype(BF16)
                dkn_ref[ks, hs] += lax.dot_general(ds, qn_j, _TN, preferred_element_type=F32)
                dkr_ref[ks, :] += lax.dot_general(ds, qr_j, _TN, preferred_element_type=F32)
                dv_ref[ks, hs] += lax.dot_general(p.astype(BF16), dob, _TN, preferred_element_type=F32)
                return (dqn + jnp.dot(ds, kn_c, preferred_element_type=F32),
                        dqr + jnp.dot(ds, kr_c, preferred_element_type=F32))

            init = (jnp.zeros((tq, MLA_NOPE), F32), jnp.zeros((tq, LANE), F32))
            carry = lax.fori_loop(0, qi, lambda kc, c: chunk(kc, c, False), init)
            dqn, dqr = chunk(qi, carry, True)
            dqn_ref[:, hs] = dqn.astype(dqn_ref.dtype)
            dqr_heads.append(dqr)
        dqr_ref[...] = jnp.where(masks[0], dqr_heads[0], dqr_heads[1])

    return pl.pallas_call(
        body, name="attn_bwd", grid=grid, in_specs=[s_qn, s_qr, s_kn, s_kr, s_kn, s_qn, s_qr, s_qn],
        out_specs=[s_qn, s_qr, s_kn, s_kr, s_kn],
        out_shape=[jax.ShapeDtypeStruct(qn.shape, BF16)] + [jax.ShapeDtypeStruct(a.shape, F32) for a in (qr, kn, kr, v)],
        compiler_params=_cparams(),
    )(qn, qr, kn, kr, v, o, lse, do)


def _loss_head(y, target):
    t, d = y.shape
    tr = _rows(t, 512)

    def body(y_ref, t_ref, l_ref, dy_ref):
        err = y_ref[...] - t_ref[...]
        dy_ref[...] = err * (1.0 / d)
        part = 0.5 * jnp.sum(jnp.sum(err * err, axis=1, keepdims=True), axis=0, keepdims=True) * (1.0 / d)
        l_ref[...] = jnp.broadcast_to(part, l_ref.shape)

    row = pl.BlockSpec((tr, d), lambda i: (i, 0))
    parts, dy = pl.pallas_call(
        body, name="loss_head", grid=(t // tr,), in_specs=[row, row],
        out_specs=[pl.BlockSpec((8, LANE), lambda i: (i, 0)), row],
        out_shape=[jax.ShapeDtypeStruct((8 * (t // tr), LANE), F32), jax.ShapeDtypeStruct((t, d), F32)],
        compiler_params=_cparams(),
    )(y, target)
    return parts, dy


def _sum_parts(stack, name):
    n, r, c = stack.shape
    tr = _rows(r, 512)

    def body(s_ref, o_ref):
        acc = s_ref[0].astype(F32)
        for i in range(1, n):
            acc = acc + s_ref[i].astype(F32)
        o_ref[...] = acc

    return pl.pallas_call(
        body, name=name, grid=(r // tr,), in_specs=[pl.BlockSpec((n, tr, c), lambda i: (0, i, 0))],
        out_specs=pl.BlockSpec((tr, c), lambda i: (i, 0)), out_shape=jax.ShapeDtypeStruct((r, c), F32),
        compiler_params=_cparams(),
    )(stack)


def _adamw(w, g_mine, g_other, m, v, name):
    r, c = w.shape
    tr = _rows(r, 256)
    bc1 = 1.0 / (1.0 - ADAM_B1 ** ADAM_STEP)
    bc2 = 1.0 / (1.0 - ADAM_B2 ** ADAM_STEP)
    two = g_other is not None

    def body(*refs):
        if two:
            w_ref, g_ref, g2_ref, m_ref, v_ref, go_ref, d_ref, mo_ref, vo_ref = refs
            g = g_ref[...] + g2_ref[...]
        else:
            w_ref, g_ref, m_ref, v_ref, go_ref, d_ref, mo_ref, vo_ref = refs
            g = g_ref[...]
        mn = ADAM_B1 * m_ref[...] + (1.0 - ADAM_B1) * g
        vn = ADAM_B2 * v_ref[...] + (1.0 - ADAM_B2) * (g * g)
        go_ref[...] = g
        mo_ref[...] = mn
        vo_ref[...] = vn
        d_ref[...] = -ADAM_LR * ((mn * bc1) / (jnp.sqrt(vn * bc2) + ADAM_EPS) + ADAM_WD * w_ref[...])

    blk = pl.BlockSpec((tr, c), lambda i: (i, 0))
    ins = [w, g_mine] + ([g_other] if two else []) + [m, v]
    return pl.pallas_call(
        body, name=name, grid=(r // tr,), in_specs=[blk] * len(ins), out_specs=[blk] * 4,
        out_shape=[jax.ShapeDtypeStruct((r, c), F32)] * 4, compiler_params=_cparams(),
    )(*ins)


def _chip_peers():
    x, y, c = lax.axis_index("x"), lax.axis_index("y"), lax.axis_index("c")
    return (x, y, c), [(1 - x, y), (x, 1 - y), (1 - x, 1 - y)]


def _chip_exchange(ins, outs, sems, scatter):
    send_sems, recv_sems, local_sems = sems
    (x, y, c), chips = _chip_peers()
    me = 2 * x + y
    n = len(ins)

    def src(i, chip):
        return ins[i].at[chip] if scatter else ins[i]

    def local(i):
        return pltpu.make_async_copy(src(i, me), outs[i].at[me], local_sems.at[i])

    def remote(i, j, piece, slot):
        px, py = chips[j]
        return pltpu.make_async_remote_copy(src_ref=src(i, piece), dst_ref=outs[i].at[slot], send_sem=send_sems.at[i, j],
                                            recv_sem=recv_sems.at[i, j], device_id=(px, py, c), device_id_type=MESH)

    def start():
        for i in range(n):
            local(i).start()
            for j, (px, py) in enumerate(chips):
                remote(i, j, 2 * px + py, me).start()

    def wait():
        for i in range(n):
            for j, (px, py) in enumerate(chips):
                remote(i, j, me, 2 * px + py).wait_recv()
        for i in range(n):
            for j, (px, py) in enumerate(chips):
                remote(i, j, 2 * px + py, me).wait_send()
            local(i).wait()

    return start, wait


def _exchange_sems(n):
    return [pltpu.SemaphoreType.DMA((n, 3)), pltpu.SemaphoreType.DMA((n, 3)), pltpu.SemaphoreType.DMA((n,))]


def _exchange_shapes(arrs, scatter):
    return [jax.ShapeDtypeStruct(s.shape if scatter else (N_CHIPS,) + s.shape, s.dtype) for s in arrs]


def _exchange_call(name, arrs, scatter):
    n = len(arrs)

    def body(*refs):
        start, wait = _chip_exchange(refs[:n], refs[n:2 * n], refs[2 * n:], scatter)
        start()
        wait()

    any_spec = pl.BlockSpec(memory_space=pl.ANY)
    return pl.pallas_call(body, name=name, in_specs=[any_spec] * n, out_specs=[any_spec] * n,
                          out_shape=_exchange_shapes(arrs, scatter), scratch_shapes=_exchange_sems(n))(*arrs)


def _gather_chips(shards, name="gather_chips"):
    return _exchange_call(name, shards, False)


def _scatter_chips(stacks, name="scatter_chips"):
    return _exchange_call(name, stacks, True)


def _swap_cores(arrs):
    n = len(arrs)

    def body(*refs):
        ins, outs = refs[:n], refs[n:2 * n]
        send_sems, recv_sems = refs[2 * n:]
        x, y, c = lax.axis_index("x"), lax.axis_index("y"), lax.axis_index("c")
        cps = []
        for i in range(n):
            cp = pltpu.make_async_remote_copy(src_ref=ins[i], dst_ref=outs[i], send_sem=send_sems.at[i],
                                              recv_sem=recv_sems.at[i], device_id=(x, y, 1 - c), device_id_type=MESH)
            cp.start()
            cps.append(cp)
        for cp in cps:
            cp.wait()

    any_spec = pl.BlockSpec(memory_space=pl.ANY)
    return pl.pallas_call(
        body, name="swap_cores", in_specs=[any_spec] * n, out_specs=[any_spec] * n,
        out_shape=[jax.ShapeDtypeStruct(s.shape, s.dtype) for s in arrs],
        scratch_shapes=[pltpu.SemaphoreType.DMA((n,)), pltpu.SemaphoreType.DMA((n,))],
    )(*arrs)


def _gather_all(block):
    m_per, n = block.shape

    def body(x_ref, out_ref, send_sems, recv_sems, local_sem):
        x, y, c = lax.axis_index("x"), lax.axis_index("y"), lax.axis_index("c")
        me, sibling = (x, y, c), (x, y, 1 - c)
        chips = [(1 - x, y), (x, 1 - y), (1 - x, 1 - y)]

        def rows(px, py, pc):
            return out_ref.at[pl.ds((4 * px + 2 * py + pc) * m_per, m_per), :]

        def copy(k, blk, to, src=None):
            return pltpu.make_async_remote_copy(src_ref=rows(*blk) if src is None else src, dst_ref=rows(*blk),
                                                send_sem=send_sems.at[k], recv_sem=recv_sems.at[k], device_id=to,
                                                device_id_type=MESH)

        mine = pltpu.make_async_copy(x_ref, rows(*me), local_sem)
        mine.start()
        first = [copy(0, me, sibling, src=x_ref)]
        first += [copy(1 + j, me, (*chip, c), src=x_ref) for j, chip in enumerate(chips)]
        for cp in first:
            cp.start()
        passed = [copy(4 + j, (*chip, c), sibling) for j, chip in enumerate(chips)]
        for j, chip in enumerate(chips):
            copy(1 + j, (*chip, c), me).wait_recv()
            passed[j].start()
        copy(0, sibling, me).wait_recv()
        for j, chip in enumerate(chips):
            copy(4 + j, (*chip, 1 - c), me).wait_recv()
        for cp in first + passed:
            cp.wait_send()
        mine.wait()

    return pl.pallas_call(
        body, name="gather_all", out_shape=jax.ShapeDtypeStruct((N_DEV * m_per, n), block.dtype),
        in_specs=[pl.BlockSpec(memory_space=pltpu.VMEM)], out_specs=pl.BlockSpec(memory_space=pltpu.VMEM),
        scratch_shapes=[pltpu.SemaphoreType.DMA((7,)), pltpu.SemaphoreType.DMA((7,)), pltpu.SemaphoreType.DMA],
    )(block)


def _rope_tables(positions):
    inv_freq = 1.0 / (ROPE_THETA ** (jnp.arange(0, MLA_ROPE, 2, dtype=F32) / MLA_ROPE))
    ang = positions.astype(F32).reshape(-1, 1) * inv_freq
    return jnp.tile(jnp.cos(ang), (1, 4)), jnp.tile(jnp.sin(ang), (1, 4))


def _pad_cols(a, n):
    return jnp.pad(a, ((0, 0), (0, n - a.shape[1])))


def _ffn_fwd(l, h, w, nb, seq):
    ug = _mm(h, w["ffn_up_g"][l], name=f"ffn{l}_up_g")
    uv = _mm(h, w["ffn_up_v"][l], name=f"ffn{l}_up_v")
    op = _op_conv(f"ffn{l}_conv", _f_convffn, [ug, uv], [w["ffn_cw_g"][l], w["ffn_cw_v"][l]],
                  [w["ffn_cb_g"][l], w["ffn_cb_v"][l]], nb, seq)
    act = op.fwd((BF16,))[0]
    ff = _mm(act, w["ffn_down"][l], name=f"ffn{l}_down")
    return ff, (h, op, act)


def _ffn_bwd(l, saved, dff, w, grads, dh_skip):
    h, op, act = saved
    grads[f"ffn_down{l}"] = _mm(act, dff, ta=True, name=f"ffn{l}_down_dw")
    dact = _mm(dff, w["ffn_down"][l], tb=True, name=f"ffn{l}_down_dx")
    (dug, duv), (dwg, dwv, dbg, dbv) = op.bwd([dact], [BF16, BF16])
    grads[f"ffn_up_g{l}"] = _mm(h, dug, ta=True, name=f"ffn{l}_up_g_dw")
    grads[f"ffn_up_v{l}"] = _mm(h, duv, ta=True, name=f"ffn{l}_up_v_dw")
    grads[f"ffn_cw_g{l}"], grads[f"ffn_cw_v{l}"], grads[f"ffn_cb_g{l}"], grads[f"ffn_cb_v{l}"] = dwg, dwv, dbg, dbv
    dh = _mm(dug, w["ffn_up_g"][l], tb=True, add=dh_skip, name=f"ffn{l}_up_g_dx")
    return _mm(duv, w["ffn_up_v"][l], tb=True, add=dh, name=f"ffn{l}_up_v_dx")


class _Carried:
    def __init__(self, shards, finish_weights, grad_stacks):
        self.shards, self.finish_weights, self.grad_stacks, self.received = shards, finish_weights, grad_stacks, None


def _local_step(x, positions, target, w, carried=None):
    nb, seq, d = x.shape
    t = nb * seq
    x2, tgt2 = x.reshape(t, d), target.reshape(t, d)
    cos, sin = _rope_tables(positions)
    grads = {}

    xb = x2.astype(BF16)

    z = _mm(xb, w["in_z"], name="ssd_in_z")
    raw = [_mm(xb, w[k], name="ssd_" + k) for k in ("in_x", "in_b", "in_c")]
    dt_raw = _mm(xb, w["in_dt"], name="ssd_in_dt")
    conv_ops = [_op_conv("ssd_conv_" + s, _f_convsilu, [r], [w["conv_w_" + s]], [w["conv_b_" + s]], nb, seq)
                for s, r in zip("xbc", raw)]
    xs, bm, cm = [op.fwd()[0] for op in conv_ops]
    dt_op = _op_dt("ssd_dt", dt_raw, w["dt_bias"], w["a_log"])
    dt, cum, ecum, wend = dt_op.fwd()

    def to_rows(v):
        v = jnp.swapaxes(v.reshape(nb, seq, LANE), 1, 2)[:, :SSD_H, :].reshape(nb * SSD_G, SSD_K, seq)
        return jnp.pad(v, ((0, 0), (0, SSD_KPAD - SSD_K), (0, 0)))

    def from_rows(v):
        v = v[:, :SSD_K, :].reshape(nb, SSD_H, seq)
        return jnp.swapaxes(jnp.pad(v, ((0, 0), (0, LANE - SSD_H), (0, 0))), 1, 2).reshape(t, LANE)

    def to_cols(v):
        v = jnp.swapaxes(v[:, :SSD_H].reshape(nb, seq, SSD_G, SSD_K), 1, 2).reshape(nb * SSD_G, seq, SSD_K)
        return jnp.pad(v, ((0, 0), (0, 0), (0, SSD_KPAD - SSD_K)))

    def from_cols(v):
        v = jnp.swapaxes(v[:, :, :SSD_K].reshape(nb, SSD_G, seq, SSD_K), 1, 2).reshape(t, SSD_H)
        return jnp.pad(v, ((0, 0), (0, LANE - SSD_H)))

    ssd_rows = jnp.concatenate([to_rows(cum), to_rows(dt)], axis=1)
    ssd_cols = jnp.concatenate([to_cols(cum), to_cols(ecum), to_cols(wend)], axis=2)
    y, states, *gathered = _ssd_scan_fwd(xs, bm, cm, ssd_rows, ssd_cols, nb, seq, carried.shards if carried else ())
    if carried:
        w = {**w, **carried.finish_weights(gathered)}
    post_op = _op_ssd_post("ssd_post", y, xs, z, w["d_exp"], w["norm_g"])
    yn = post_op.fwd((BF16,))[0]
    mix0 = _mm(yn, w["ssd_out"], name="ssd_out")
    ln0m = _op_ln("ln_mix0", x2, mix0, w["ln_mix_g"][0], w["ln_mix_b"][0])
    h0a, h0a_b = ln0m.fwd((F32, BF16))
    ff0, ffn0_saved = _ffn_fwd(0, h0a_b, w, nb, seq)
    ln0f = _op_ln("ln_ffn0", h0a, ff0, w["ln_ffn_g"][0], w["ln_ffn_b"][0])
    h1, h1_b = ln0f.fwd((F32, BF16))

    ckv = _mm(h1_b, w["kv_down_c"], name="kv_down_c")
    kr_in = _mm(h1_b, w["kv_down_r"], name="kv_down_r")
    kvn_op = _op_rms("kv_norm", ckv, w["kv_norm_g"])
    ckvn = kvn_op.fwd((BF16,))[0]
    kr_op = _op_rope("k_rope", _f_rope_dup, kr_in, cos, sin)
    kr = kr_op.fwd((BF16,))[0]
    kn = _mm(ckvn, w["kv_up_k"], out_dtype=BF16, name="kv_up_k")
    v = _mm(ckvn, w["kv_up_v"], out_dtype=BF16, name="kv_up_v")
    cq_raw = _mm(h1_b, w["q_down"], name="q_down")
    qn_op = _op_rms("q_norm", cq_raw, w["q_norm_g"])
    cq = qn_op.fwd((BF16,))[0]
    qn = _mm(cq, w["q_up_n"], out_dtype=BF16, name="q_up_n")
    qr_raw = _mm(cq, w["q_up_r"], name="q_up_r")
    qr_op = _op_rope("q_rope", _f_rope, qr_raw, cos, sin)
    qr = qr_op.fwd((BF16,))[0]
    o, lse = _attn_fwd(qn, qr, kn, kr, v, nb, seq)
    mix1 = _mm(o, w["attn_out"], name="attn_out")
    ln1m = _op_ln("ln_mix1", h1, mix1, w["ln_mix_g"][1], w["ln_mix_b"][1])
    h1a, h1a_b = ln1m.fwd((F32, BF16))
    ff1, ffn1_saved = _ffn_fwd(1, h1a_b, w, nb, seq)
    ln1f = _op_ln("ln_ffn1", h1a, ff1, w["ln_ffn_g"][1], w["ln_ffn_b"][1])
    h2 = ln1f.fwd()[0]

    loss_parts, dh2 = _loss_head(h2, tgt2)

    (dh1a, dff1), (grads["ln_ffn_g1"], grads["ln_ffn_b1"]) = ln1f.bwd([dh2], [F32, BF16])
    dh1a = _ffn_bwd(1, ffn1_saved, dff1, w, grads, dh1a)
    (dh1, dmix1), (grads["ln_mix_g1"], grads["ln_mix_b1"]) = ln1m.bwd([dh1a], [F32, BF16])
    grads["attn_out"] = _mm(o, dmix1, ta=True, name="attn_out_dw")
    do = _mm(dmix1, w["attn_out"], tb=True, name="attn_out_dx")
    dqn, dqr, dkn, dkr, dv = _attn_bwd(qn, qr, kn, kr, v, o, lse, do, nb, seq)
    (dqr_raw,), _ = qr_op.bwd([dqr], [BF16])
    grads["q_up_n"] = _mm(cq, dqn, ta=True, name="q_up_n_dw")
    grads["q_up_r"] = _mm(cq, dqr_raw, ta=True, name="q_up_r_dw")
    dcq = _mm(dqn, w["q_up_n"], tb=True, name="q_up_n_dx")
    dcq = _mm(dqr_raw, w["q_up_r"], tb=True, add=dcq, name="q_up_r_dx")
    (dcq_raw,), (grads["q_norm_g"],) = qn_op.bwd([dcq], [BF16])
    grads["q_down"] = _mm(h1_b, dcq_raw, ta=True, name="q_down_dw")
    dh1 = _mm(dcq_raw, w["q_down"], tb=True, add=dh1, name="q_down_dx")
    grads["kv_up_k"] = _mm(ckvn, dkn, ta=True, name="kv_up_k_dw")
    grads["kv_up_v"] = _mm(ckvn, dv, ta=True, name="kv_up_v_dw")
    dckvn = _mm(dkn, w["kv_up_k"], tb=True, name="kv_up_k_dx")
    dckvn = _mm(dv, w["kv_up_v"], tb=True, add=dckvn, name="kv_up_v_dx")
    (dckv,), (grads["kv_norm_g"],) = kvn_op.bwd([dckvn], [BF16])
    (dkr_in,), _ = kr_op.bwd([dkr], [BF16])
    grads["kv_down_c"] = _mm(h1_b, dckv, ta=True, name="kv_down_c_dw")
    grads["kv_down_r"] = _mm(h1_b, dkr_in, ta=True, name="kv_down_r_dw")
    dh1 = _mm(dckv, w["kv_down_c"], tb=True, add=dh1, name="kv_down_c_dx")
    dh1 = _mm(dkr_in, w["kv_down_r"], tb=True, add=dh1, name="kv_down_r_dx")

    (dh0a, dff0), (grads["ln_ffn_g0"], grads["ln_ffn_b0"]) = ln0f.bwd([dh1], [F32, BF16])
    dh0a = _ffn_bwd(0, ffn0_saved, dff0, w, grads, dh0a)
    (dx, dmix0), (grads["ln_mix_g0"], grads["ln_mix_b0"]) = ln0m.bwd([dh0a], [F32, BF16])
    grads["ssd_out"] = _mm(yn, dmix0, ta=True, name="ssd_out_dw")
    dyn = _mm(dmix0, w["ssd_out"], tb=True, name="ssd_out_dx")
    (dy, dxs_post, dz), (grads["d_exp"], grads["norm_g"]) = post_op.bwd([dyn], [F32, F32, BF16])
    dxs, dbm, dcm, drows, dcols, *received = _ssd_scan_bwd(xs, bm, cm, ssd_rows, ssd_cols, states, dy, dxs_post, nb, seq,
                                                           carried.grad_stacks(grads) if carried else ())
    if carried:
        carried.received = received
    kq = SSD_KPAD
    d_cum = from_rows(drows[:, :kq]) + from_cols(dcols[:, :, :kq])
    (ddt_raw,), (grads["dt_bias"], grads["a_log"]) = dt_op.bwd(
        [from_rows(drows[:, kq:]), d_cum, from_cols(dcols[:, :, kq:2 * kq]), from_cols(dcols[:, :, 2 * kq:])], [BF16])
    draws = []
    for s, op, dout in zip("xbc", conv_ops, (dxs, dbm, dcm)):
        (dr,), (grads["conv_w_" + s], grads["conv_b_" + s]) = op.bwd([dout], [BF16])
        draws.append(dr)
    for k, dr in zip(("in_x", "in_b", "in_c"), draws):
        grads[k] = _mm(xb, dr, ta=True, name=f"ssd_{k}_dw")
        dx = _mm(dr, w[k], tb=True, add=dx, name=f"ssd_{k}_dx")
    grads["in_z"] = _mm(xb, dz, ta=True, name="ssd_in_z_dw")
    grads["in_dt"] = _mm(xb, ddt_raw, ta=True, name="ssd_in_dt_dw")
    dx = _mm(dz, w["in_z"], tb=True, add=dx, name="ssd_in_z_dx")
    dx = _mm(ddt_raw, w["in_dt"], tb=True, add=dx, name="ssd_in_dt_dx")
    return loss_parts, dx.reshape(nb, seq, d), grads


_XE, _BE, _CE = SSD_D_INNER, SSD_D_INNER + SSD_GN, SSD_D_INNER + 2 * SSD_GN


def _prep_weights(fw):
    w = {}
    f = FFN_HIDDEN
    if "ssd_in_proj" in fw:
        ip = fw["ssd_in_proj"][0]
        o = SSD_D_INNER
        w["in_z"], w["in_x"], w["in_b"], w["in_c"] = ip[:, :o], ip[:, o:o + _XE], ip[:, o + _XE:o + _BE], ip[:, o + _BE:o + _CE]
        w["in_dt"] = _pad_cols(ip[:, o + _CE:], LANE)
        cw, cb = fw["ssd_conv_w"][0], fw["ssd_conv_b"]
        for s, (lo, hi) in zip("xbc", ((0, _XE), (_XE, _BE), (_BE, _CE))):
            w["conv_w_" + s], w["conv_b_" + s] = cw[:, lo:hi], cb[:, lo:hi]
    if "ssd_dt_bias" in fw:
        w["dt_bias"], w["a_log"] = _pad_cols(fw["ssd_dt_bias"], LANE), _pad_cols(fw["ssd_A_log"], LANE)
        w["d_exp"] = jnp.repeat(fw["ssd_D"][0], SSD_P)[None, :]
        w["kv_norm_g"], w["q_norm_g"] = fw["kv_norm_g"][None, :], fw["q_norm_g"]
        w["ffn_cb_g"] = [fw["ffn_conv_b"][l:l + 1, :f] for l in range(DEPTH)]
        w["ffn_cb_v"] = [fw["ffn_conv_b"][l:l + 1, f:] for l in range(DEPTH)]
        for k in ("ln_mix_g", "ln_mix_b", "ln_ffn_g", "ln_ffn_b"):
            w[k] = [fw[k][l:l + 1] for l in range(DEPTH)]
    if "ssd_out_proj" in fw:
        w["norm_g"], w["ssd_out"] = fw["ssd_norm_g"], fw["ssd_out_proj"][0]
        kd = fw["kv_down_proj"]
        w["kv_down_c"], w["kv_down_r"] = kd[:, :MLA_KV_RANK], _pad_cols(kd[:, MLA_KV_RANK:], LANE)
        w["kv_up_k"], w["kv_up_v"] = fw["kv_up_k"], fw["kv_up_v"]
        w["q_down"] = fw["q_down_proj"][0]
        qu = fw["q_up_proj"][0].reshape(MLA_Q_RANK, MLA_H, MLA_NOPE + MLA_ROPE)
        w["q_up_n"] = qu[:, :, :MLA_NOPE].reshape(MLA_Q_RANK, MLA_H * MLA_NOPE)
        w["q_up_r"] = qu[:, :, MLA_NOPE:].reshape(MLA_Q_RANK, MLA_H * MLA_ROPE)
        w["attn_out"] = fw["attn_out_proj"][0]
        w["ffn_up_g"] = [fw["ffn_up"][l][:, :f] for l in range(DEPTH)]
        w["ffn_up_v"] = [fw["ffn_up"][l][:, f:] for l in range(DEPTH)]
        w["ffn_cw_g"] = [fw["ffn_conv_w"][l][:, :f] for l in range(DEPTH)]
        w["ffn_cw_v"] = [fw["ffn_conv_w"][l][:, f:] for l in range(DEPTH)]
        w["ffn_down"] = [fw["ffn_down"][l] for l in range(DEPTH)]
    return w


def _assemble_grads(g, names):
    make = {
        "ssd_in_proj": lambda: jnp.concatenate([g["in_z"], g["in_x"], g["in_b"], g["in_c"], g["in_dt"][:, :SSD_H]], axis=1)[None],
        "ssd_conv_w": lambda: jnp.concatenate([g["conv_w_" + s] for s in "xbc"], axis=1)[None],
        "ssd_conv_b": lambda: jnp.concatenate([g["conv_b_" + s] for s in "xbc"], axis=1),
        "ssd_dt_bias": lambda: g["dt_bias"][:, :SSD_H],
        "ssd_A_log": lambda: g["a_log"][:, :SSD_H],
        "ssd_D": lambda: jnp.sum(g["d_exp"].reshape(SSD_H, SSD_P), axis=1)[None, :],
        "ssd_norm_g": lambda: g["norm_g"],
        "ssd_out_proj": lambda: g["ssd_out"][None],
        "kv_down_proj": lambda: jnp.concatenate([g["kv_down_c"], g["kv_down_r"][:, :MLA_ROPE]], axis=1),
        "kv_norm_g": lambda: g["kv_norm_g"][0],
        "kv_up_k": lambda: g["kv_up_k"],
        "kv_up_v": lambda: g["kv_up_v"],
        "q_down_proj": lambda: g["q_down"][None],
        "q_norm_g": lambda: g["q_norm_g"],
        "q_up_proj": lambda: jnp.concatenate([g["q_up_n"].reshape(MLA_Q_RANK, MLA_H, MLA_NOPE),
                                              g["q_up_r"].reshape(MLA_Q_RANK, MLA_H, MLA_ROPE)], axis=2).reshape(1, MLA_Q_RANK, -1),
        "attn_out_proj": lambda: g["attn_out"][None],
        "ffn_up": lambda: jnp.stack([jnp.concatenate([g[f"ffn_up_g{l}"], g[f"ffn_up_v{l}"]], axis=1) for l in range(DEPTH)]),
        "ffn_conv_w": lambda: jnp.stack([jnp.concatenate([g[f"ffn_cw_g{l}"], g[f"ffn_cw_v{l}"]], axis=1) for l in range(DEPTH)]),
        "ffn_conv_b": lambda: jnp.concatenate([jnp.concatenate([g[f"ffn_cb_g{l}"], g[f"ffn_cb_v{l}"]], axis=1)
                                               for l in range(DEPTH)], axis=0),
        "ffn_down": lambda: jnp.stack([g[f"ffn_down{l}"] for l in range(DEPTH)]),
    }
    for k in ("ln_mix_g", "ln_mix_b", "ln_ffn_g", "ln_ffn_b"):
        make[k] = lambda k=k: jnp.concatenate([g[f"{k}{l}"] for l in range(DEPTH)], axis=0)
    return {n: make[n]() for n in names}


_WEIGHTS = ["ssd_in_proj", "ssd_conv_w", "ssd_conv_b", "ssd_dt_bias", "ssd_A_log", "ssd_D", "ssd_norm_g", "ssd_out_proj",
            "kv_down_proj", "kv_norm_g", "kv_up_k", "kv_up_v", "q_down_proj", "q_norm_g", "q_up_proj", "attn_out_proj",
            "ffn_up", "ffn_conv_w", "ffn_conv_b", "ffn_down", "ln_mix_g", "ln_mix_b", "ln_ffn_g", "ln_ffn_b"]
_COL_CUT = ["ssd_in_proj", "ssd_conv_w", "ssd_conv_b", "ssd_norm_g", "kv_up_k", "kv_up_v", "q_up_proj", "ffn_up", "ffn_conv_w"]
_ROW_CUT = ["ssd_out_proj", "kv_down_proj", "q_down_proj", "attn_out_proj", "ffn_down"]
_CUT = _COL_CUT + _ROW_CUT
_WHOLE = [n for n in _WEIGHTS if n not in _CUT]
_EARLY = ["ssd_in_proj", "ssd_conv_w", "ssd_conv_b"]
_LATE = [n for n in _CUT if n not in _EARLY]
_MXU_WEIGHTS = ["ssd_in_proj", "ssd_out_proj", "kv_down_proj", "kv_up_k", "kv_up_v", "q_down_proj", "q_up_proj",
                "attn_out_proj", "ffn_up", "ffn_down"]
_PACK_ROWS = 160


def _shard_2d(name, s):
    return s.reshape(-1, s.shape[-1])


def _unstack(name, g, shard_shape):
    if name in _COL_CUT:
        lead = shard_shape[:-1]
        return jnp.swapaxes(g, 0, 1).reshape(*lead, N_CHIPS * shard_shape[-1])
    lead, rs, c = shard_shape[:-2], shard_shape[-2], shard_shape[-1]
    n_lead = math.prod(lead)
    return jnp.swapaxes(g.reshape(N_CHIPS, n_lead, rs, c), 0, 1).reshape(*lead, N_CHIPS * rs, c)


def _stack(name, full, shard_shape):
    if name in _COL_CUT:
        cs = shard_shape[-1]
        return jnp.swapaxes(full.reshape(-1, N_CHIPS, cs), 0, 1)
    lead, rs, c = shard_shape[:-2], shard_shape[-2], shard_shape[-1]
    n_lead = math.prod(lead)
    return jnp.swapaxes(full.reshape(n_lead, N_CHIPS, rs, c), 0, 1).reshape(N_CHIPS, n_lead * rs, c)


def _pack(arrs):
    flat = jnp.concatenate([a.reshape(-1) for a in arrs])
    return jnp.pad(flat, (0, _PACK_ROWS * LANE - flat.shape[0])).reshape(_PACK_ROWS, LANE)


def _unpack(packed, like):
    flat, out, o = packed.reshape(-1), [], 0
    for a in like:
        out.append(flat[o:o + a.size].reshape(a.shape))
        o += a.size
    return out


_ARGS = ["x", "positions"] + _WEIGHTS + ["loss_target"] + ["m_" + n for n in _WEIGHTS] + ["v_" + n for n in _WEIGHTS]


def kernel(x, positions, ssd_in_proj, ssd_conv_w, ssd_conv_b, ssd_dt_bias, ssd_A_log, ssd_D, ssd_norm_g,
           ssd_out_proj, kv_down_proj, kv_norm_g, kv_up_k, kv_up_v, q_down_proj, q_norm_g, q_up_proj,
           attn_out_proj, ffn_up, ffn_conv_w, ffn_conv_b, ffn_down, ln_mix_g, ln_mix_b, ln_ffn_g, ln_ffn_b,
           loss_target, m_ssd_in_proj, m_ssd_conv_w, m_ssd_conv_b, m_ssd_dt_bias, m_ssd_A_log, m_ssd_D,
           m_ssd_norm_g, m_ssd_out_proj, m_kv_down_proj, m_kv_norm_g, m_kv_up_k, m_kv_up_v, m_q_down_proj,
           m_q_norm_g, m_q_up_proj, m_attn_out_proj, m_ffn_up, m_ffn_conv_w, m_ffn_conv_b, m_ffn_down,
           m_ln_mix_g, m_ln_mix_b, m_ln_ffn_g, m_ln_ffn_b, v_ssd_in_proj, v_ssd_conv_w, v_ssd_conv_b,
           v_ssd_dt_bias, v_ssd_A_log, v_ssd_D, v_ssd_norm_g, v_ssd_out_proj, v_kv_down_proj, v_kv_norm_g,
           v_kv_up_k, v_kv_up_v, v_q_down_proj, v_q_norm_g, v_q_up_proj, v_attn_out_proj, v_ffn_up,
           v_ffn_conv_w, v_ffn_conv_b, v_ffn_down, v_ln_mix_g, v_ln_mix_b, v_ln_ffn_g, v_ln_ffn_b):
    args = (x, positions, ssd_in_proj, ssd_conv_w, ssd_conv_b, ssd_dt_bias, ssd_A_log, ssd_D, ssd_norm_g,
            ssd_out_proj, kv_down_proj, kv_norm_g, kv_up_k, kv_up_v, q_down_proj, q_norm_g, q_up_proj,
            attn_out_proj, ffn_up, ffn_conv_w, ffn_conv_b, ffn_down, ln_mix_g, ln_mix_b, ln_ffn_g, ln_ffn_b,
            loss_target, m_ssd_in_proj, m_ssd_conv_w, m_ssd_conv_b, m_ssd_dt_bias, m_ssd_A_log, m_ssd_D,
            m_ssd_norm_g, m_ssd_out_proj, m_kv_down_proj, m_kv_norm_g, m_kv_up_k, m_kv_up_v, m_q_down_proj,
            m_q_norm_g, m_q_up_proj, m_attn_out_proj, m_ffn_up, m_ffn_conv_w, m_ffn_conv_b, m_ffn_down,
            m_ln_mix_g, m_ln_mix_b, m_ln_ffn_g, m_ln_ffn_b, v_ssd_in_proj, v_ssd_conv_w, v_ssd_conv_b,
            v_ssd_dt_bias, v_ssd_A_log, v_ssd_D, v_ssd_norm_g, v_ssd_out_proj, v_kv_down_proj, v_kv_norm_g,
            v_kv_up_k, v_kv_up_v, v_q_down_proj, v_q_norm_g, v_q_up_proj, v_attn_out_proj, v_ffn_up,
            v_ffn_conv_w, v_ffn_conv_b, v_ffn_down, v_ln_mix_g, v_ln_mix_b, v_ln_ffn_g, v_ln_ffn_b)
    a = dict(zip(_ARGS, args, strict=True))

    def shard(n):
        s = _shard_2d(n, a[n])
        return s.astype(BF16) if n in _MXU_WEIGHTS else s

    def full_weights(names, gathered):
        return {n: _unstack(n, g, a[n].shape) for n, g in zip(names, gathered)}

    def grad_stacks(names, pieces):
        full = _assemble_grads(pieces, names)
        return [_stack(n, full[n], a[n].shape).astype(BF16) for n in names]

    fw = full_weights(_EARLY, _gather_chips([shard(n) for n in _EARLY]))
    fw.update({n: a[n] for n in _WHOLE})
    carried = _Carried([shard(n) for n in _LATE], lambda got: _prep_weights(full_weights(_LATE, got)),
                       lambda pieces: grad_stacks(_LATE, pieces))

    loss_parts, grad_x, pieces = _local_step(a["x"], a["positions"], a["loss_target"], _prep_weights(fw), carried)
    loss = lax.psum(jnp.sum(loss_parts[::8, 0]), ("x", "y", "c"))

    bufs = dict(zip(_LATE, carried.received))
    bufs.update(zip(_EARLY, _scatter_chips(grad_stacks(_EARLY, pieces))))
    sums = [_sum_parts(bufs[n], "sum_chips_" + n) for n in _CUT]
    others = _swap_cores(sums)
    full = _assemble_grads(pieces, _WHOLE)
    every = _gather_all(_pack([full[n] for n in _WHOLE]))
    g_whole = _sum_parts(every.reshape(N_DEV, _PACK_ROWS, LANE), "sum_devices")

    res = {}
    for n, s, o in zip(_CUT, sums, others):
        out = _adamw(_shard_2d(n, a[n]), s, o, _shard_2d(n, a["m_" + n]), _shard_2d(n, a["v_" + n]), "adamw_" + n)
        res[n] = [r.reshape(a[n].shape) for r in out]
    whole = [a[n] for n in _WHOLE]
    out = _adamw(_pack(whole), g_whole, None, _pack([a["m_" + n] for n in _WHOLE]), _pack([a["v_" + n] for n in _WHOLE]),
                 "adamw_whole")
    for k, n in enumerate(_WHOLE):
        res[n] = [_unpack(r, whole)[k] for r in out]
    return (loss, grad_x, *[res[n][0] for n in _WEIGHTS], *[res[n][1] for n in _WEIGHTS],
            *[res[n][2] for n in _WEIGHTS], *[res[n][3] for n in _WEIGHTS])
```

```python
import functools
import math

import jax
import jax.numpy as jnp
from jax import lax
from jax.experimental import pallas as pl
from jax.experimental.pallas import tpu as pltpu

F32 = jnp.float32
BF16 = jnp.bfloat16
HIGHEST = lax.Precision.HIGHEST
MESH = pl.DeviceIdType.MESH

D_MODEL = 1024
DEPTH = 2
DN_ALPHA = (2 * DEPTH) ** 0.25
SSD_D_INNER = 2048
SSD_P = 64
SSD_H = 32
SSD_G = 8
SSD_K = 4
SSD_N = 128
SSD_L = 128
SSD_GN = SSD_G * SSD_N
MLA_H = 8
MLA_Q_RANK = 384
MLA_KV_RANK = 256
MLA_NOPE = 128
MLA_ROPE = 64
MLA_V = 128
ROPE_THETA = 10000.0
FFN_HIDDEN = 2816
LN_EPS = 1e-5
RMS_EPS = 1e-6
ADAM_LR = 0.001
ADAM_B1 = 0.9
ADAM_B2 = 0.999
ADAM_EPS = 1e-08
ADAM_WD = 0.01
ADAM_STEP = 10

N_CHIPS = 4
N_DEV = 8
LANE = 128
VMEM_LIMIT = 56 * 1024 * 1024


def _cparams(**kw):
    return pltpu.CompilerParams(vmem_limit_bytes=VMEM_LIMIT, **kw)


def _tile(dim, cap):
    best = None
    t = LANE
    while t <= min(dim, cap):
        if dim % t == 0:
            best = t
        t += LANE
    return dim if best is None else best


MM_TILE_CAP = 1408
MM_WHOLE_K = 2816
MM_VMEM_BUDGET = 40 * 1024 * 1024


def _mm_tiles(m, n, k, a_bytes, b_bytes, o_bytes, has_add):
    tm, tn = _tile(m, MM_TILE_CAP), _tile(n, MM_TILE_CAP)
    tk = k if k <= MM_WHOLE_K else _tile(k, 1024)

    def need(tm, tn):
        acc = tm * tn * 4 if tk < k else 0
        return 2 * (tm * tk * a_bytes + tk * tn * b_bytes + tm * tn * o_bytes + (tm * tn * 4 if has_add else 0)) + acc

    while need(tm, tn) > MM_VMEM_BUDGET:
        if tm >= tn and _tile(m, tm // 2) < tm:
            tm = _tile(m, tm // 2)
        elif _tile(n, tn // 2) < tn:
            tn = _tile(n, tn // 2)
        else:
            break
    return tm, tn, tk


def _mm(a, b, *, ta=False, tb=False, add=None, out_dtype=F32, name):
    m, k = (a.shape[1], a.shape[0]) if ta else a.shape
    n = b.shape[0] if tb else b.shape[1]
    assert (b.shape[1] if tb else b.shape[0]) == k
    tm, tn, tk = _mm_tiles(m, n, k, a.dtype.itemsize, b.dtype.itemsize, jnp.dtype(out_dtype).itemsize, add is not None)
    nk = k // tk
    dims = (((0 if ta else 1,), (1 if tb else 0,)), ((), ()))

    def partial_product(a_ref, b_ref):
        return lax.dot_general(a_ref[...].astype(BF16), b_ref[...].astype(BF16), dims, preferred_element_type=F32)

    def body_one(*refs):
        if add is None:
            a_ref, b_ref, o_ref = refs
            o_ref[...] = partial_product(a_ref, b_ref).astype(out_dtype)
        else:
            a_ref, b_ref, c_ref, o_ref = refs
            o_ref[...] = (partial_product(a_ref, b_ref) + c_ref[...]).astype(out_dtype)

    def body_acc(*refs):
        if add is None:
            a_ref, b_ref, o_ref, acc = refs
        else:
            a_ref, b_ref, c_ref, o_ref, acc = refs
        kk = pl.program_id(2)

        @pl.when(kk == 0)
        def _():
            acc[...] = jnp.zeros_like(acc) if add is None else c_ref[...]

        acc[...] += partial_product(a_ref, b_ref)

        @pl.when(kk == nk - 1)
        def _():
            o_ref[...] = acc[...].astype(out_dtype)

    a_spec = pl.BlockSpec((tk, tm), lambda i, j, kk: (kk, i)) if ta else pl.BlockSpec((tm, tk), lambda i, j, kk: (i, kk))
    b_spec = pl.BlockSpec((tn, tk), lambda i, j, kk: (j, kk)) if tb else pl.BlockSpec((tk, tn), lambda i, j, kk: (kk, j))
    o_spec = pl.BlockSpec((tm, tn), lambda i, j, kk: (i, j))
    ins, specs = [a, b], [a_spec, b_spec]
    if add is not None:
        ins.append(add)
        specs.append(o_spec)
    return pl.pallas_call(
        body_one if nk == 1 else body_acc, name=name, grid=(m // tm, n // tn, nk), in_specs=specs, out_specs=o_spec,
        out_shape=jax.ShapeDtypeStruct((m, n), out_dtype),
        scratch_shapes=[] if nk == 1 else [pltpu.VMEM((tm, tn), F32)],
        compiler_params=_cparams(dimension_semantics=("parallel", "parallel", "arbitrary")),
    )(*ins)


def _spec(op):
    return pl.BlockSpec(op[1], op[2])


def _bw_fwd(name, fn, grid, ins, outs, out_dtypes):
    n_in = len(ins)
    flat = [(o, dt) for o, dts in zip(outs, out_dtypes) for dt in dts]

    def body(*refs):
        res = fn(*[r[...].astype(F32) for r in refs[:n_in]])
        orefs = iter(refs[n_in:])
        for v, dts in zip(res, out_dtypes):
            for dt in dts:
                next(orefs)[...] = v.astype(dt)

    return pl.pallas_call(
        body, name=name, grid=grid, in_specs=[_spec(o) for o in ins],
        out_specs=[pl.BlockSpec(o[1], o[2]) for o, _ in flat],
        out_shape=[jax.ShapeDtypeStruct(o[0], dt) for o, dt in flat], compiler_params=_cparams(),
    )(*[o[0] for o in ins])


def _bw_bwd(name, fn, grid, data, params, consts, cts, red_axes, grad_dtypes):
    nd, npar, nc, nct = len(data), len(params), len(consts), len(cts)

    def body(*refs):
        first = None
        for ax in red_axes:
            z = pl.program_id(ax) == 0
            first = z if first is None else jnp.logical_and(first, z)
        vals = [r[...].astype(F32) for r in refs[:nd + npar + nc + nct]]
        d, p, c, g = vals[:nd], vals[nd:nd + npar], vals[nd + npar:nd + npar + nc], vals[nd + npar + nc:]
        _, vjp = jax.vjp(lambda dd, pp: tuple(fn(*dd, *pp, *c)), d, p)
        gd, gp = vjp(tuple(g))
        orefs = refs[nd + npar + nc + nct:]
        for r, v in zip(orefs[:nd], gd):
            r[...] = v.astype(r.dtype)
        if npar:
            @pl.when(first)
            def _():
                for r in orefs[nd:]:
                    r[...] = jnp.zeros_like(r)

            for r, v in zip(orefs[nd:], gp):
                r[...] += v

    ins = list(data) + list(params) + list(consts) + list(cts)
    outs = list(data) + list(params)
    dtypes = list(grad_dtypes) + [F32] * npar
    return pl.pallas_call(
        body, name=name, grid=grid, in_specs=[_spec(o) for o in ins], out_specs=[_spec(o) for o in outs],
        out_shape=[jax.ShapeDtypeStruct(o[0].shape, dt) for o, dt in zip(outs, dtypes)], compiler_params=_cparams(),
    )(*[o[0] for o in ins])


def _shift_down(x, s):
    row = lax.broadcasted_iota(jnp.int32, x.shape, 0)
    return jnp.where(row < s, 0.0, pltpu.roll(x, s, 0))


def _shift_up(x, s):
    n = x.shape[0]
    row = lax.broadcasted_iota(jnp.int32, x.shape, 0)
    return jnp.where(row >= n - s, 0.0, pltpu.roll(x, n - s, 0))


def _time_shift(s):
    if s == 0:
        return lambda x: x

    @jax.custom_vjp
    def shift(x):
        return _shift_down(x, s)

    shift.defvjp(lambda x: (_shift_down(x, s), None), lambda _, g: (_shift_up(g, s),))
    return shift


def _rot_half_raw(x):
    lane = lax.broadcasted_iota(jnp.int32, x.shape, 1)
    return jnp.where(lane % MLA_ROPE < MLA_ROPE // 2, -pltpu.roll(x, LANE - MLA_ROPE // 2, 1), pltpu.roll(x, MLA_ROPE // 2, 1))


@jax.custom_vjp
def _rot_half(x):
    return _rot_half_raw(x)


_rot_half.defvjp(lambda x: (_rot_half_raw(x), None), lambda _, g: (-_rot_half_raw(g),))


@jax.custom_vjp
def _roll_half_lanes(x):
    return pltpu.roll(x, LANE // 2, 1)


_roll_half_lanes.defvjp(lambda x: (pltpu.roll(x, LANE // 2, 1), None), lambda _, g: (pltpu.roll(g, LANE // 2, 1),))


def _causal_conv(u, w, b):
    width = w.shape[0]
    y = b
    for k in range(width):
        y = y + w[k:k + 1, :] * _time_shift(width - 1 - k)(u)
    return y


def _silu(x):
    return x * jax.nn.sigmoid(x)


def _f_ln(h, mix, g, b):
    x = DN_ALPHA * h + mix
    mu = jnp.mean(x, axis=-1, keepdims=True)
    xc = x - mu
    var = jnp.mean(xc * xc, axis=-1, keepdims=True)
    return (xc * lax.rsqrt(var + LN_EPS) * g + b,)


def _f_convsilu(u, w, b):
    return (_silu(_causal_conv(u, w, b)),)


def _f_convffn(ug, uv, wg, wv, bg, bv):
    return (_silu(_causal_conv(ug, wg, bg)) * _causal_conv(uv, wv, bv),)


def _f_dt(dt_raw, bias, a_log):
    x = dt_raw + bias
    dt = jnp.maximum(x, 0.0) + jnp.log(1.0 + jnp.exp(-jnp.abs(x)))
    a = dt * (-jnp.exp(a_log))
    n = a.shape[0]
    lower = (lax.broadcasted_iota(jnp.int32, (n, n), 0) >= lax.broadcasted_iota(jnp.int32, (n, n), 1)).astype(F32)
    cum = jnp.dot(lower, a, precision=HIGHEST, preferred_element_type=F32)
    cum_last = jnp.sum(a, axis=0, keepdims=True)
    return dt, cum, jnp.exp(cum), jnp.exp(cum_last - cum) * dt


def _f_ssd_post(y, xs, z, d_exp, ng):
    t = (y + d_exp * xs) * _silu(z)
    return (t * lax.rsqrt(jnp.mean(t * t, axis=-1, keepdims=True) + LN_EPS) * ng,)


def _f_rms(x, g):
    return (x * lax.rsqrt(jnp.mean(x * x, axis=-1, keepdims=True) + RMS_EPS) * g,)


def _f_rope(x, cos, sin):
    return (x * cos + _rot_half(x) * sin,)


def _f_rope_dup(x, cos, sin):
    r = x * cos + _rot_half(x) * sin
    return (r + _roll_half_lanes(r),)


def _rows(t, cap=512):
    for c in (cap, 256, 128, 64, 32, 16, 8):
        if c <= cap and t % c == 0:
            return c
    return t


class _Blockwise:
    def __init__(self, name, fn, grid, data, params, consts, outs, red_axes):
        self.name, self.fn, self.grid = name, fn, grid
        self.data, self.params, self.consts, self.outs, self.red_axes = data, params, consts, outs, red_axes

    def fwd(self, *out_dtypes):
        out_dtypes = out_dtypes or tuple((F32,) for _ in self.outs)
        return _bw_fwd(self.name + "_fwd", self.fn, self.grid, self.data + self.params + self.consts, self.outs, out_dtypes)

    def bwd(self, cts, grad_dtypes=None):
        cts = [(c, o[1], o[2]) for c, o in zip(cts, self.outs)]
        grad_dtypes = grad_dtypes or [F32] * len(self.data)
        res = _bw_bwd(self.name + "_bwd", self.fn, self.grid, self.data, self.params, self.consts, cts, self.red_axes,
                      grad_dtypes)
        return res[:len(self.data)], res[len(self.data):]


def _op_ln(name, h, mix, g, b):
    t, d = h.shape
    tr = _rows(t, 256)
    row = ((tr, d), lambda i: (i, 0))
    par = ((1, d), lambda i: (0, 0))
    return _Blockwise(name, _f_ln, (t // tr,), [(h, *row), (mix, *row)], [(g, *par), (b, *par)], [],
                      [((t, d), *row)], (0,))


def _op_conv(name, fn, us, ws, bs, nb, seq):
    c = us[0].shape[1]
    ct = _tile(c, 256)
    blk = ((seq, ct), lambda j, bb: (bb, j))
    data = [(u, *blk) for u in us]
    params = [(w, (w.shape[0], ct), lambda j, bb: (0, j)) for w in ws] + [(b, (1, ct), lambda j, bb: (0, j)) for b in bs]
    return _Blockwise(name, fn, (c // ct, nb), data, params, [], [((nb * seq, c), *blk)], (1,))


def _op_dt(name, dt_raw, bias, a_log):
    t = dt_raw.shape[0]
    row = ((SSD_L, LANE), lambda i: (i, 0))
    par = ((1, LANE), lambda i: (0, 0))
    return _Blockwise(name, _f_dt, (t // SSD_L,), [(dt_raw, *row)], [(bias, *par), (a_log, *par)], [],
                      [((t, LANE), *row)] * 4, (0,))


def _op_ssd_post(name, y, xs, z, d_exp, ng):
    t, c = y.shape
    gw = c // SSD_G
    tr = _rows(t, 512)
    blk = ((tr, gw), lambda g, i: (i, g))
    par = ((1, gw), lambda g, i: (0, g))
    return _Blockwise(name, _f_ssd_post, (SSD_G, t // tr), [(y, *blk), (xs, *blk), (z, *blk)],
                      [(d_exp, *par), (ng, *par)], [], [((t, c), *blk)], (1,))


def _op_rms(name, x, g):
    t, c = x.shape
    tr = _rows(t, 512)
    return _Blockwise(name, _f_rms, (t // tr,), [(x, (tr, c), lambda i: (i, 0))], [(g, (1, c), lambda i: (0, 0))], [],
                      [((t, c), (tr, c), lambda i: (i, 0))], (0,))


def _op_rope(name, fn, x, cos, sin):
    t, c = x.shape
    tr = _rows(t, 512)
    blk = ((tr, LANE), lambda i, j: (i, j))
    cs = ((tr, LANE), lambda i, j: (i, 0))
    return _Blockwise(name, fn, (t // tr, c // LANE), [(x, *blk)], [], [(cos, *cs), (sin, *cs)], [((t, c), *blk)], ())


_NT = (((1,), (1,)), ((), ()))
_TN = (((0,), (0,)), ((), ()))


def _ssd_head(x, g, bc, cc, s, cum_row, dt_row, cum_col, ecum_col, wend_col):
    n = g.shape[0]
    row = lax.broadcasted_iota(jnp.int32, (n, n), 0)
    col = lax.broadcasted_iota(jnp.int32, (n, n), 1)
    decay = jnp.exp(jnp.where(row >= col, cum_col - cum_row, -jnp.inf))
    w = g * decay * dt_row
    y = jnp.dot(w.astype(BF16), x.astype(BF16), preferred_element_type=F32)
    y = y + lax.dot_general((cc * ecum_col).astype(BF16), s.astype(BF16), _NT, preferred_element_type=F32)
    lane = lax.broadcasted_iota(jnp.int32, cum_row.shape, 1)
    cum_last = jnp.sum(jnp.where(lane == n - 1, cum_row, 0.0), axis=1, keepdims=True)
    s_new = s * jnp.exp(cum_last) + lax.dot_general(x.astype(BF16), (bc * wend_col).astype(BF16), _TN,
                                                    preferred_element_type=F32)
    return y, s_new


SSD_KPAD = 8
SSD_ROWS = 4 * SSD_KPAD
_CUM, _DT, _ECUM, _WEND = 0, SSD_KPAD, 2 * SSD_KPAD, 3 * SSD_KPAD


def _ssd_head_args(rows, cols, k):
    return (rows[_CUM + k:_CUM + k + 1], rows[_DT + k:_DT + k + 1], cols[:, _CUM + k:_CUM + k + 1],
            cols[:, _ECUM + k:_ECUM + k + 1], cols[:, _WEND + k:_WEND + k + 1])


def _carried_exchange(arrs, in_refs, out_refs, sems, scatter, n_steps):
    if not arrs:
        return lambda: None, lambda: None
    first = jnp.logical_and(pl.program_id(0) == 0, pl.program_id(1) == 0)
    last = jnp.logical_and(pl.program_id(0) == n_steps[0] - 1, pl.program_id(1) == n_steps[1] - 1)
    start, wait = _chip_exchange(in_refs, out_refs, sems, scatter)
    return (lambda: pl.when(first)(start)), (lambda: pl.when(last)(wait))


def _ssd_scan_fwd(xs, bm, cm, rows, nb, seq, gather=()):
    nc = seq // SSD_L
    kp = SSD_K * SSD_P
    ng = len(gather)

    def body(*refs):
        xs_ref, b_ref, c_ref, row_ref = refs[:4]
        y_ref, st_ref = refs[4 + ng:6 + ng]
        start, wait = _carried_exchange(gather, refs[4:4 + ng], refs[6 + ng:6 + 2 * ng], refs[6 + 2 * ng:], False,
                                        (nb, SSD_G))
        start()

        def chunk(c, states):
            sl = pl.ds(pl.multiple_of(c * SSD_L, SSD_L), SSD_L)
            bc, cc = b_ref[sl, :], c_ref[sl, :]
            g = lax.dot_general(cc.astype(BF16), bc.astype(BF16), _NT, preferred_element_type=F32)
            rows_c = row_ref[0, :, sl]
            cols_c = rows_c.T
            new = []
            for k in range(SSD_K):
                hs = pl.ds(k * SSD_P, SSD_P)
                st_ref[0, c * SSD_K + k] = states[k]
                y, s_new = _ssd_head(xs_ref[sl, hs], g, bc, cc, states[k], *_ssd_head_args(rows_c, cols_c, k))
                y_ref[sl, hs] = y
                new.append(s_new)
            return tuple(new)

        lax.fori_loop(0, nc, chunk, tuple(jnp.zeros((SSD_P, SSD_N), F32) for _ in range(SSD_K)))
        wait()

    t = xs.shape[0]
    any_spec = pl.BlockSpec(memory_space=pl.ANY)
    return pl.pallas_call(
        body, name="ssd_scan_fwd", grid=(nb, SSD_G),
        in_specs=[pl.BlockSpec((seq, kp), lambda b, g: (b, g)),
                  pl.BlockSpec((seq, SSD_N), lambda b, g: (b, g)),
                  pl.BlockSpec((seq, SSD_N), lambda b, g: (b, g)),
                  pl.BlockSpec((1, SSD_ROWS, seq), lambda b, g: (b * SSD_G + g, 0, 0))] + [any_spec] * ng,
        out_specs=[pl.BlockSpec((seq, kp), lambda b, g: (b, g)),
                   pl.BlockSpec((1, nc * SSD_K, SSD_P, SSD_N), lambda b, g: (b * SSD_G + g, 0, 0, 0))] + [any_spec] * ng,
        out_shape=[jax.ShapeDtypeStruct((t, SSD_D_INNER), F32),
                   jax.ShapeDtypeStruct((nb * SSD_G, nc * SSD_K, SSD_P, SSD_N), F32)] + _exchange_shapes(gather, False),
        scratch_shapes=_exchange_sems(ng) if ng else [],
        compiler_params=_cparams(dimension_semantics=("arbitrary", "arbitrary")),
    )(xs, bm, cm, rows, *gather)


def _ssd_scan_bwd(xs, bm, cm, rows, states, dy, dxs_skip, nb, seq, scatter=()):
    nc = seq // SSD_L
    kp = SSD_K * SSD_P
    ns = len(scatter)

    def body(*refs):
        xs_ref, b_ref, c_ref, row_ref, st_ref, dy_ref, skip_ref = refs[:7]
        dxs_ref, db_ref, dc_ref, drow_ref = refs[7 + ns:11 + ns]
        dcol_ref = refs[11 + 2 * ns]
        start, wait = _carried_exchange(scatter, refs[7:7 + ns], refs[11 + ns:11 + 2 * ns], refs[12 + 2 * ns:], True,
                                        (nb, SSD_G))
        start()

        def chunk(i, dstates):
            c = nc - 1 - i
            sl = pl.ds(pl.multiple_of(c * SSD_L, SSD_L), SSD_L)
            bc, cc = b_ref[sl, :], c_ref[sl, :]
            bcb, ccb = bc.astype(BF16), cc.astype(BF16)
            g = lax.dot_general(ccb, bcb, _NT, preferred_element_type=F32)
            rows_c = row_ref[0, :, sl]
            cols_c = rows_c.T
            drow_ref[0, :, sl] = jnp.zeros((SSD_ROWS, SSD_L), F32)
            dcol_ref[...] = jnp.zeros((SSD_L, SSD_ROWS), F32)
            db = jnp.zeros((SSD_L, SSD_N), F32)
            dc = jnp.zeros((SSD_L, SSD_N), F32)
            dg = jnp.zeros((SSD_L, SSD_L), F32)
            new = []
            for k in range(SSD_K):
                hs = pl.ds(k * SSD_P, SSD_P)
                _, vjp = jax.vjp(_ssd_head, xs_ref[sl, hs], g, bc, cc, st_ref[0, c * SSD_K + k],
                                 *_ssd_head_args(rows_c, cols_c, k))
                dx, dgk, dbk, dck, ds, d_cum_row, d_dt_row, d_cum_col, d_ecum_col, d_wend_col = vjp(
                    (dy_ref[sl, hs], dstates[k]))
                dxs_ref[sl, hs] = dx + skip_ref[sl, hs]
                db, dc, dg = db + dbk, dc + dck, dg + dgk
                drow_ref[0, _CUM + k:_CUM + k + 1, sl] = d_cum_row
                drow_ref[0, _DT + k:_DT + k + 1, sl] = d_dt_row
                dcol_ref[:, _CUM + k:_CUM + k + 1] = d_cum_col
                dcol_ref[:, _ECUM + k:_ECUM + k + 1] = d_ecum_col
                dcol_ref[:, _WEND + k:_WEND + k + 1] = d_wend_col
                new.append(ds)
            drow_ref[0, :, sl] += dcol_ref[...].T
            dgb = dg.astype(BF16)
            dc_ref[sl, :] = dc + jnp.dot(dgb, bcb, preferred_element_type=F32)
            db_ref[sl, :] = db + lax.dot_general(dgb, ccb, _TN, preferred_element_type=F32)
            return tuple(new)

        lax.fori_loop(0, nc, chunk, tuple(jnp.zeros((SSD_P, SSD_N), F32) for _ in range(SSD_K)))
        wait()

    t = xs.shape[0]
    x_spec = pl.BlockSpec((seq, kp), lambda b, g: (b, g))
    n_spec = pl.BlockSpec((seq, SSD_N), lambda b, g: (b, g))
    r_spec = pl.BlockSpec((1, SSD_ROWS, seq), lambda b, g: (b * SSD_G + g, 0, 0))
    any_spec = pl.BlockSpec(memory_space=pl.ANY)
    return pl.pallas_call(
        body, name="ssd_scan_bwd", grid=(nb, SSD_G),
        in_specs=[x_spec, n_spec, n_spec, r_spec,
                  pl.BlockSpec((1, nc * SSD_K, SSD_P, SSD_N), lambda b, g: (b * SSD_G + g, 0, 0, 0)), x_spec, x_spec]
        + [any_spec] * ns,
        out_specs=[x_spec, n_spec, n_spec, r_spec] + [any_spec] * ns,
        out_shape=[jax.ShapeDtypeStruct((t, SSD_D_INNER), F32), jax.ShapeDtypeStruct((t, SSD_GN), F32),
                   jax.ShapeDtypeStruct((t, SSD_GN), F32), jax.ShapeDtypeStruct(rows.shape, F32)]
        + _exchange_shapes(scatter, True),
        scratch_shapes=[pltpu.VMEM((SSD_L, SSD_ROWS), F32)] + (_exchange_sems(ns) if ns else []),
        compiler_params=_cparams(dimension_semantics=("arbitrary", "arbitrary")),
    )(xs, bm, cm, rows, states, dy, dxs_skip, *scatter)


ATT_TQ = 256
ATT_TE = 512
ATT_SCALE = (MLA_NOPE + MLA_ROPE) ** -0.5


def _for_key_extent(qi, seq, fn):
    te = min(ATT_TE, seq)
    per = te // ATT_TQ
    for e in range(seq // te):
        pl.when(jnp.logical_and(qi >= e * per, qi < (e + 1) * per))(functools.partial(fn, (e + 1) * te))


def _pair_masks(shape):
    lane = lax.broadcasted_iota(jnp.int32, shape, 1)
    return lane < MLA_ROPE, lane >= MLA_ROPE


def _scores(qn, qr, kn, kr, q0):
    s = lax.dot_general(qn, kn, _NT, preferred_element_type=F32) + lax.dot_general(qr, kr, _NT, preferred_element_type=F32)
    row = lax.broadcasted_iota(jnp.int32, s.shape, 0)
    col = lax.broadcasted_iota(jnp.int32, s.shape, 1)
    return jnp.where(col <= q0 + row, s * ATT_SCALE, -jnp.inf)


def _attn_specs(nb, seq):
    nq = seq // ATT_TQ
    qn = pl.BlockSpec((ATT_TQ, 2 * MLA_NOPE), lambda b, hp, qi: (b * nq + qi, hp))
    qr = pl.BlockSpec((ATT_TQ, LANE), lambda b, hp, qi: (b * nq + qi, hp))
    kn = pl.BlockSpec((seq, 2 * MLA_NOPE), lambda b, hp, qi: (b, hp))
    kr = pl.BlockSpec((seq, LANE), lambda b, hp, qi: (b, 0))
    return (nb, MLA_H // 2, nq), qn, qr, kn, kr


def _attn_fwd(qn, qr, kn, kr, v, nb, seq):
    grid, s_qn, s_qr, s_kn, s_kr = _attn_specs(nb, seq)

    def body(qn_ref, qr_ref, kn_ref, kr_ref, v_ref, o_ref, lse_ref):
        qi = pl.program_id(2)

        def attend(ext):
            qr = qr_ref[...]
            masks = _pair_masks(qr.shape)
            outs, lses = [], []
            for j in range(2):
                hs = pl.ds(j * MLA_NOPE, MLA_NOPE)
                qr_j = jnp.where(masks[j], qr, jnp.zeros_like(qr))
                s = _scores(qn_ref[:, hs], qr_j, kn_ref[0:ext, hs], kr_ref[0:ext, :], qi * ATT_TQ)
                m = jnp.max(s, axis=1, keepdims=True)
                p = jnp.exp(s - m)
                l = jnp.sum(p, axis=1, keepdims=True)
                outs.append(jnp.dot(p.astype(BF16), v_ref[0:ext, hs], preferred_element_type=F32) / l)
                lses.append(m + jnp.log(l))
            o_ref[...] = jnp.concatenate(outs, axis=1).astype(o_ref.dtype)
            lse_ref[...] = jnp.where(masks[0], lses[0], lses[1])

        _for_key_extent(qi, seq, attend)

    return pl.pallas_call(
        body, name="attn_fwd", grid=grid, in_specs=[s_qn, s_qr, s_kn, s_kr, s_kn], out_specs=[s_qn, s_qr],
        out_shape=[jax.ShapeDtypeStruct(qn.shape, BF16), jax.ShapeDtypeStruct(qr.shape, F32)], compiler_params=_cparams(),
    )(qn, qr, kn, kr, v)


def _attn_bwd(qn, qr, kn, kr, v, o, lse, do, nb, seq):
    grid, s_qn, s_qr, s_kn, s_kr = _attn_specs(nb, seq)

    def body(qn_ref, qr_ref, kn_ref, kr_ref, v_ref, o_ref, lse_ref, do_ref, dqn_ref, dqr_ref, dkn_ref, dkr_ref, dv_ref):
        hp, qi = pl.program_id(1), pl.program_id(2)

        @pl.when(qi == 0)
        def _():
            dkn_ref[...] = jnp.zeros_like(dkn_ref)
            dv_ref[...] = jnp.zeros_like(dv_ref)

        @pl.when(jnp.logical_and(qi == 0, hp == 0))
        def _():
            dkr_ref[...] = jnp.zeros_like(dkr_ref)

        def attend(ext):
            qr, lse = qr_ref[...], lse_ref[...]
            masks = _pair_masks(qr.shape)
            dqr_heads = []
            for j in range(2):
                hs = pl.ds(j * MLA_NOPE, MLA_NOPE)
                qn_j, qr_j = qn_ref[:, hs], jnp.where(masks[j], qr, jnp.zeros_like(qr))
                kn_j, kr, v_j = kn_ref[0:ext, hs], kr_ref[0:ext, :], v_ref[0:ext, hs]
                do_j = do_ref[:, hs]
                dob = do_j.astype(BF16)
                delta = jnp.sum(do_j * o_ref[:, hs].astype(F32), axis=1, keepdims=True)
                p = jnp.exp(_scores(qn_j, qr_j, kn_j, kr, qi * ATT_TQ) - lse[:, j * MLA_ROPE:j * MLA_ROPE + 1])
                dp = lax.dot_general(dob, v_j, _NT, preferred_element_type=F32)
                ds = (p * (dp - delta) * ATT_SCALE).astype(BF16)
                dkn_ref[0:ext, hs] += lax.dot_general(ds, qn_j, _TN, preferred_element_type=F32)
                dkr_ref[0:ext, :] += lax.dot_general(ds, qr_j, _TN, preferred_element_type=F32)
                dv_ref[0:ext, hs] += lax.dot_general(p.astype(BF16), dob, _TN, preferred_element_type=F32)
                dqn_ref[:, hs] = jnp.dot(ds, kn_j, preferred_element_type=F32).astype(dqn_ref.dtype)
                dqr_heads.append(jnp.dot(ds, kr, preferred_element_type=F32))
            dqr_ref[...] = jnp.where(masks[0], dqr_heads[0], dqr_heads[1])

        _for_key_extent(qi, seq, attend)

    return pl.pallas_call(
        body, name="attn_bwd", grid=grid, in_specs=[s_qn, s_qr, s_kn, s_kr, s_kn, s_qn, s_qr, s_qn],
        out_specs=[s_qn, s_qr, s_kn, s_kr, s_kn],
        out_shape=[jax.ShapeDtypeStruct(qn.shape, BF16)] + [jax.ShapeDtypeStruct(a.shape, F32) for a in (qr, kn, kr, v)],
        compiler_params=_cparams(),
    )(qn, qr, kn, kr, v, o, lse, do)


def _loss_head(y, target):
    t, d = y.shape
    tr = _rows(t, 512)

    def body(y_ref, t_ref, l_ref, dy_ref):
        err = y_ref[...] - t_ref[...]
        dy_ref[...] = err * (1.0 / d)
        part = 0.5 * jnp.sum(jnp.sum(err * err, axis=1, keepdims=True), axis=0, keepdims=True) * (1.0 / d)
        l_ref[...] = jnp.broadcast_to(part, l_ref.shape)

    row = pl.BlockSpec((tr, d), lambda i: (i, 0))
    parts, dy = pl.pallas_call(
        body, name="loss_head", grid=(t // tr,), in_specs=[row, row],
        out_specs=[pl.BlockSpec((8, LANE), lambda i: (i, 0)), row],
        out_shape=[jax.ShapeDtypeStruct((8 * (t // tr), LANE), F32), jax.ShapeDtypeStruct((t, d), F32)],
        compiler_params=_cparams(),
    )(y, target)
    return parts, dy


def _sum_parts(stack, name):
    n, r, c = stack.shape
    tr = _rows(r, 512)

    def body(s_ref, o_ref):
        acc = s_ref[0].astype(F32)
        for i in range(1, n):
            acc = acc + s_ref[i].astype(F32)
        o_ref[...] = acc

    return pl.pallas_call(
        body, name=name, grid=(r // tr,), in_specs=[pl.BlockSpec((n, tr, c), lambda i: (0, i, 0))],
        out_specs=pl.BlockSpec((tr, c), lambda i: (i, 0)), out_shape=jax.ShapeDtypeStruct((r, c), F32),
        compiler_params=_cparams(),
    )(stack)


def _adamw(w, g_mine, g_other, m, v, name):
    r, c = w.shape
    tr = _rows(r, 256)
    bc1 = 1.0 / (1.0 - ADAM_B1 ** ADAM_STEP)
    bc2 = 1.0 / (1.0 - ADAM_B2 ** ADAM_STEP)
    two = g_other is not None

    def body(*refs):
        if two:
            w_ref, g_ref, g2_ref, m_ref, v_ref, go_ref, d_ref, mo_ref, vo_ref = refs
            g = g_ref[...] + g2_ref[...]
        else:
            w_ref, g_ref, m_ref, v_ref, go_ref, d_ref, mo_ref, vo_ref = refs
            g = g_ref[...]
        mn = ADAM_B1 * m_ref[...] + (1.0 - ADAM_B1) * g
        vn = ADAM_B2 * v_ref[...] + (1.0 - ADAM_B2) * (g * g)
        go_ref[...] = g
        mo_ref[...] = mn
        vo_ref[...] = vn
        d_ref[...] = -ADAM_LR * ((mn * bc1) / (jnp.sqrt(vn * bc2) + ADAM_EPS) + ADAM_WD * w_ref[...])

    blk = pl.BlockSpec((tr, c), lambda i: (i, 0))
    ins = [w, g_mine] + ([g_other] if two else []) + [m, v]
    return pl.pallas_call(
        body, name=name, grid=(r // tr,), in_specs=[blk] * len(ins), out_specs=[blk] * 4,
        out_shape=[jax.ShapeDtypeStruct((r, c), F32)] * 4, compiler_params=_cparams(),
    )(*ins)


def _chip_peers():
    x, y, c = lax.axis_index("x"), lax.axis_index("y"), lax.axis_index("c")
    return (x, y, c), [(1 - x, y), (x, 1 - y), (1 - x, 1 - y)]


def _chip_exchange(ins, outs, sems, scatter):
    send_sems, recv_sems, local_sems = sems
    (x, y, c), chips = _chip_peers()
    me = 2 * x + y
    n = len(ins)

    def src(i, chip):
        return ins[i].at[chip] if scatter else ins[i]

    def local(i):
        return pltpu.make_async_copy(src(i, me), outs[i].at[me], local_sems.at[i])

    def remote(i, j, piece, slot):
        px, py = chips[j]
        return pltpu.make_async_remote_copy(src_ref=src(i, piece), dst_ref=outs[i].at[slot], send_sem=send_sems.at[i, j],
                                            recv_sem=recv_sems.at[i, j], device_id=(px, py, c), device_id_type=MESH)

    def start():
        for i in range(n):
            local(i).start()
            for j, (px, py) in enumerate(chips):
                remote(i, j, 2 * px + py, me).start()

    def wait():
        for i in range(n):
            for j, (px, py) in enumerate(chips):
                remote(i, j, me, 2 * px + py).wait_recv()
        for i in range(n):
            for j, (px, py) in enumerate(chips):
                remote(i, j, 2 * px + py, me).wait_send()
            local(i).wait()

    return start, wait


def _exchange_sems(n):
    return [pltpu.SemaphoreType.DMA((n, 3)), pltpu.SemaphoreType.DMA((n, 3)), pltpu.SemaphoreType.DMA((n,))]


def _exchange_shapes(arrs, scatter):
    return [jax.ShapeDtypeStruct(s.shape if scatter else (N_CHIPS,) + s.shape, s.dtype) for s in arrs]


def _exchange_call(name, arrs, scatter):
    n = len(arrs)

    def body(*refs):
        start, wait = _chip_exchange(refs[:n], refs[n:2 * n], refs[2 * n:], scatter)
        start()
        wait()

    any_spec = pl.BlockSpec(memory_space=pl.ANY)
    return pl.pallas_call(body, name=name, in_specs=[any_spec] * n, out_specs=[any_spec] * n,
                          out_shape=_exchange_shapes(arrs, scatter), scratch_shapes=_exchange_sems(n))(*arrs)


def _gather_chips(shards, name="gather_chips"):
    return _exchange_call(name, shards, False)


def _scatter_chips(stacks, name="scatter_chips"):
    return _exchange_call(name, stacks, True)


def _swap_cores(arrs):
    n = len(arrs)

    def body(*refs):
        ins, outs = refs[:n], refs[n:2 * n]
        send_sems, recv_sems = refs[2 * n:]
        x, y, c = lax.axis_index("x"), lax.axis_index("y"), lax.axis_index("c")
        cps = []
        for i in range(n):
            cp = pltpu.make_async_remote_copy(src_ref=ins[i], dst_ref=outs[i], send_sem=send_sems.at[i],
                                              recv_sem=recv_sems.at[i], device_id=(x, y, 1 - c), device_id_type=MESH)
            cp.start()
            cps.append(cp)
        for cp in cps:
            cp.wait()

    any_spec = pl.BlockSpec(memory_space=pl.ANY)
    return pl.pallas_call(
        body, name="swap_cores", in_specs=[any_spec] * n, out_specs=[any_spec] * n,
        out_shape=[jax.ShapeDtypeStruct(s.shape, s.dtype) for s in arrs],
        scratch_shapes=[pltpu.SemaphoreType.DMA((n,)), pltpu.SemaphoreType.DMA((n,))],
    )(*arrs)


def _gather_all(block):
    m_per, n = block.shape

    def body(x_ref, out_ref, send_sems, recv_sems, local_sem):
        x, y, c = lax.axis_index("x"), lax.axis_index("y"), lax.axis_index("c")
        me, sibling = (x, y, c), (x, y, 1 - c)
        chips = [(1 - x, y), (x, 1 - y), (1 - x, 1 - y)]

        def rows(px, py, pc):
            return out_ref.at[pl.ds((4 * px + 2 * py + pc) * m_per, m_per), :]

        def copy(k, blk, to, src=None):
            return pltpu.make_async_remote_copy(src_ref=rows(*blk) if src is None else src, dst_ref=rows(*blk),
                                                send_sem=send_sems.at[k], recv_sem=recv_sems.at[k], device_id=to,
                                                device_id_type=MESH)

        mine = pltpu.make_async_copy(x_ref, rows(*me), local_sem)
        mine.start()
        first = [copy(0, me, sibling, src=x_ref)]
        first += [copy(1 + j, me, (*chip, c), src=x_ref) for j, chip in enumerate(chips)]
        for cp in first:
            cp.start()
        passed = [copy(4 + j, (*chip, c), sibling) for j, chip in enumerate(chips)]
        for j, chip in enumerate(chips):
            copy(1 + j, (*chip, c), me).wait_recv()
            passed[j].start()
        copy(0, sibling, me).wait_recv()
        for j, chip in enumerate(chips):
            copy(4 + j, (*chip, 1 - c), me).wait_recv()
        for cp in first + passed:
            cp.wait_send()
        mine.wait()

    return pl.pallas_call(
        body, name="gather_all", out_shape=jax.ShapeDtypeStruct((N_DEV * m_per, n), block.dtype),
        in_specs=[pl.BlockSpec(memory_space=pltpu.VMEM)], out_specs=pl.BlockSpec(memory_space=pltpu.VMEM),
        scratch_shapes=[pltpu.SemaphoreType.DMA((7,)), pltpu.SemaphoreType.DMA((7,)), pltpu.SemaphoreType.DMA],
    )(block)


def _rope_tables(positions):
    inv_freq = 1.0 / (ROPE_THETA ** (jnp.arange(0, MLA_ROPE, 2, dtype=F32) / MLA_ROPE))
    ang = positions.astype(F32).reshape(-1, 1) * inv_freq
    return jnp.tile(jnp.cos(ang), (1, 4)), jnp.tile(jnp.sin(ang), (1, 4))


def _pad_cols(a, n):
    return jnp.pad(a, ((0, 0), (0, n - a.shape[1])))


def _ffn_fwd(l, h, w, nb, seq):
    ug = _mm(h, w["ffn_up_g"][l], name=f"ffn{l}_up_g")
    uv = _mm(h, w["ffn_up_v"][l], name=f"ffn{l}_up_v")
    op = _op_conv(f"ffn{l}_conv", _f_convffn, [ug, uv], [w["ffn_cw_g"][l], w["ffn_cw_v"][l]],
                  [w["ffn_cb_g"][l], w["ffn_cb_v"][l]], nb, seq)
    act = op.fwd((BF16,))[0]
    ff = _mm(act, w["ffn_down"][l], name=f"ffn{l}_down")
    return ff, (h, op, act)


def _ffn_bwd(l, saved, dff, w, grads, dh_skip):
    h, op, act = saved
    grads[f"ffn_down{l}"] = _mm(act, dff, ta=True, name=f"ffn{l}_down_dw")
    dact = _mm(dff, w["ffn_down"][l], tb=True, name=f"ffn{l}_down_dx")
    (dug, duv), (dwg, dwv, dbg, dbv) = op.bwd([dact], [BF16, BF16])
    grads[f"ffn_up_g{l}"] = _mm(h, dug, ta=True, name=f"ffn{l}_up_g_dw")
    grads[f"ffn_up_v{l}"] = _mm(h, duv, ta=True, name=f"ffn{l}_up_v_dw")
    grads[f"ffn_cw_g{l}"], grads[f"ffn_cw_v{l}"], grads[f"ffn_cb_g{l}"], grads[f"ffn_cb_v{l}"] = dwg, dwv, dbg, dbv
    dh = _mm(dug, w["ffn_up_g"][l], tb=True, add=dh_skip, name=f"ffn{l}_up_g_dx")
    return _mm(duv, w["ffn_up_v"][l], tb=True, add=dh, name=f"ffn{l}_up_v_dx")


class _Carried:
    def __init__(self, shards, finish_weights, grad_stacks):
        self.shards, self.finish_weights, self.grad_stacks, self.received = shards, finish_weights, grad_stacks, None


def _local_step(x, positions, target, w, carried=None):
    nb, seq, d = x.shape
    t = nb * seq
    x2, tgt2 = x.reshape(t, d), target.reshape(t, d)
    cos, sin = _rope_tables(positions)
    grads = {}

    xb = x2.astype(BF16)

    z = _mm(xb, w["in_z"], name="ssd_in_z")
    raw = [_mm(xb, w[k], name="ssd_" + k) for k in ("in_x", "in_b", "in_c")]
    dt_raw = _mm(xb, w["in_dt"], name="ssd_in_dt")
    conv_ops = [_op_conv("ssd_conv_" + s, _f_convsilu, [r], [w["conv_w_" + s]], [w["conv_b_" + s]], nb, seq)
                for s, r in zip("xbc", raw)]
    xs, bm, cm = [op.fwd()[0] for op in conv_ops]
    dt_op = _op_dt("ssd_dt", dt_raw, w["dt_bias"], w["a_log"])
    dt, cum, ecum, wend = dt_op.fwd()

    def to_rows(v):
        v = jnp.swapaxes(v.reshape(nb, seq, LANE), 1, 2)[:, :SSD_H, :].reshape(nb * SSD_G, SSD_K, seq)
        return jnp.pad(v, ((0, 0), (0, SSD_KPAD - SSD_K), (0, 0)))

    def from_rows(v):
        v = v[:, :SSD_K, :].reshape(nb, SSD_H, seq)
        return jnp.swapaxes(jnp.pad(v, ((0, 0), (0, LANE - SSD_H), (0, 0))), 1, 2).reshape(t, LANE)

    ssd_rows = jnp.concatenate([to_rows(cum), to_rows(dt), to_rows(ecum), to_rows(wend)], axis=1)
    y, states, *gathered = _ssd_scan_fwd(xs, bm, cm, ssd_rows, nb, seq, carried.shards if carried else ())
    if carried:
        w = {**w, **carried.finish_weights(gathered)}
    post_op = _op_ssd_post("ssd_post", y, xs, z, w["d_exp"], w["norm_g"])
    yn = post_op.fwd((BF16,))[0]
    mix0 = _mm(yn, w["ssd_out"], name="ssd_out")
    ln0m = _op_ln("ln_mix0", x2, mix0, w["ln_mix_g"][0], w["ln_mix_b"][0])
    h0a, h0a_b = ln0m.fwd((F32, BF16))
    ff0, ffn0_saved = _ffn_fwd(0, h0a_b, w, nb, seq)
    ln0f = _op_ln("ln_ffn0", h0a, ff0, w["ln_ffn_g"][0], w["ln_ffn_b"][0])
    h1, h1_b = ln0f.fwd((F32, BF16))

    ckv = _mm(h1_b, w["kv_down_c"], name="kv_down_c")
    kr_in = _mm(h1_b, w["kv_down_r"], name="kv_down_r")
    kvn_op = _op_rms("kv_norm", ckv, w["kv_norm_g"])
    ckvn = kvn_op.fwd((BF16,))[0]
    kr_op = _op_rope("k_rope", _f_rope_dup, kr_in, cos, sin)
    kr = kr_op.fwd((BF16,))[0]
    kn = _mm(ckvn, w["kv_up_k"], out_dtype=BF16, name="kv_up_k")
    v = _mm(ckvn, w["kv_up_v"], out_dtype=BF16, name="kv_up_v")
    cq_raw = _mm(h1_b, w["q_down"], name="q_down")
    qn_op = _op_rms("q_norm", cq_raw, w["q_norm_g"])
    cq = qn_op.fwd((BF16,))[0]
    qn = _mm(cq, w["q_up_n"], out_dtype=BF16, name="q_up_n")
    qr_raw = _mm(cq, w["q_up_r"], name="q_up_r")
    qr_op = _op_rope("q_rope", _f_rope, qr_raw, cos, sin)
    qr = qr_op.fwd((BF16,))[0]
    o, lse = _attn_fwd(qn, qr, kn, kr, v, nb, seq)
    mix1 = _mm(o, w["attn_out"], name="attn_out")
    ln1m = _op_ln("ln_mix1", h1, mix1, w["ln_mix_g"][1], w["ln_mix_b"][1])
    h1a, h1a_b = ln1m.fwd((F32, BF16))
    ff1, ffn1_saved = _ffn_fwd(1, h1a_b, w, nb, seq)
    ln1f = _op_ln("ln_ffn1", h1a, ff1, w["ln_ffn_g"][1], w["ln_ffn_b"][1])
    h2 = ln1f.fwd()[0]

    loss_parts, dh2 = _loss_head(h2, tgt2)

    (dh1a, dff1), (grads["ln_ffn_g1"], grads["ln_ffn_b1"]) = ln1f.bwd([dh2], [F32, BF16])
    dh1a = _ffn_bwd(1, ffn1_saved, dff1, w, grads, dh1a)
    (dh1, dmix1), (grads["ln_mix_g1"], grads["ln_mix_b1"]) = ln1m.bwd([dh1a], [F32, BF16])
    grads["attn_out"] = _mm(o, dmix1, ta=True, name="attn_out_dw")
    do = _mm(dmix1, w["attn_out"], tb=True, name="attn_out_dx")
    dqn, dqr, dkn, dkr, dv = _attn_bwd(qn, qr, kn, kr, v, o, lse, do, nb, seq)
    (dqr_raw,), _ = qr_op.bwd([dqr], [BF16])
    grads["q_up_n"] = _mm(cq, dqn, ta=True, name="q_up_n_dw")
    grads["q_up_r"] = _mm(cq, dqr_raw, ta=True, name="q_up_r_dw")
    dcq = _mm(dqn, w["q_up_n"], tb=True, name="q_up_n_dx")
    dcq = _mm(dqr_raw, w["q_up_r"], tb=True, add=dcq, name="q_up_r_dx")
    (dcq_raw,), (grads["q_norm_g"],) = qn_op.bwd([dcq], [BF16])
    grads["q_down"] = _mm(h1_b, dcq_raw, ta=True, name="q_down_dw")
    dh1 = _mm(dcq_raw, w["q_down"], tb=True, add=dh1, name="q_down_dx")
    grads["kv_up_k"] = _mm(ckvn, dkn, ta=True, name="kv_up_k_dw")
    grads["kv_up_v"] = _mm(ckvn, dv, ta=True, name="kv_up_v_dw")
    dckvn = _mm(dkn, w["kv_up_k"], tb=True, name="kv_up_k_dx")
    dckvn = _mm(dv, w["kv_up_v"], tb=True, add=dckvn, name="kv_up_v_dx")
    (dckv,), (grads["kv_norm_g"],) = kvn_op.bwd([dckvn], [BF16])
    (dkr_in,), _ = kr_op.bwd([dkr], [BF16])
    grads["kv_down_c"] = _mm(h1_b, dckv, ta=True, name="kv_down_c_dw")
    grads["kv_down_r"] = _mm(h1_b, dkr_in, ta=True, name="kv_down_r_dw")
    dh1 = _mm(dckv, w["kv_down_c"], tb=True, add=dh1, name="kv_down_c_dx")
    dh1 = _mm(dkr_in, w["kv_down_r"], tb=True, add=dh1, name="kv_down_r_dx")

    (dh0a, dff0), (grads["ln_ffn_g0"], grads["ln_ffn_b0"]) = ln0f.bwd([dh1], [F32, BF16])
    dh0a = _ffn_bwd(0, ffn0_saved, dff0, w, grads, dh0a)
    (dx, dmix0), (grads["ln_mix_g0"], grads["ln_mix_b0"]) = ln0m.bwd([dh0a], [F32, BF16])
    grads["ssd_out"] = _mm(yn, dmix0, ta=True, name="ssd_out_dw")
    dyn = _mm(dmix0, w["ssd_out"], tb=True, name="ssd_out_dx")
    (dy, dxs_post, dz), (grads["d_exp"], grads["norm_g"]) = post_op.bwd([dyn], [F32, F32, BF16])
    dxs, dbm, dcm, drows, *received = _ssd_scan_bwd(xs, bm, cm, ssd_rows, states, dy, dxs_post, nb, seq,
                                                    carried.grad_stacks(grads) if carried else ())
    if carried:
        carried.received = received
    d_dt, d_cum, d_ecum, d_wend = [from_rows(drows[:, o:o + SSD_KPAD]) for o in (_DT, _CUM, _ECUM, _WEND)]
    (ddt_raw,), (grads["dt_bias"], grads["a_log"]) = dt_op.bwd([d_dt, d_cum, d_ecum, d_wend], [BF16])
    draws = []
    for s, op, dout in zip("xbc", conv_ops, (dxs, dbm, dcm)):
        (dr,), (grads["conv_w_" + s], grads["conv_b_" + s]) = op.bwd([dout], [BF16])
        draws.append(dr)
    for k, dr in zip(("in_x", "in_b", "in_c"), draws):
        grads[k] = _mm(xb, dr, ta=True, name=f"ssd_{k}_dw")
        dx = _mm(dr, w[k], tb=True, add=dx, name=f"ssd_{k}_dx")
    grads["in_z"] = _mm(xb, dz, ta=True, name="ssd_in_z_dw")
    grads["in_dt"] = _mm(xb, ddt_raw, ta=True, name="ssd_in_dt_dw")
    dx = _mm(dz, w["in_z"], tb=True, add=dx, name="ssd_in_z_dx")
    dx = _mm(ddt_raw, w["in_dt"], tb=True, add=dx, name="ssd_in_dt_dx")
    return loss_parts, dx.reshape(nb, seq, d), grads


_XE, _BE, _CE = SSD_D_INNER, SSD_D_INNER + SSD_GN, SSD_D_INNER + 2 * SSD_GN


def _prep_weights(fw):
    w = {}
    f = FFN_HIDDEN
    if "ssd_in_proj" in fw:
        ip = fw["ssd_in_proj"][0]
        o = SSD_D_INNER
        w["in_z"], w["in_x"], w["in_b"], w["in_c"] = ip[:, :o], ip[:, o:o + _XE], ip[:, o + _XE:o + _BE], ip[:, o + _BE:o + _CE]
        w["in_dt"] = _pad_cols(ip[:, o + _CE:], LANE)
        cw, cb = fw["ssd_conv_w"][0], fw["ssd_conv_b"]
        for s, (lo, hi) in zip("xbc", ((0, _XE), (_XE, _BE), (_BE, _CE))):
            w["conv_w_" + s], w["conv_b_" + s] = cw[:, lo:hi], cb[:, lo:hi]
    if "ssd_dt_bias" in fw:
        w["dt_bias"], w["a_log"] = _pad_cols(fw["ssd_dt_bias"], LANE), _pad_cols(fw["ssd_A_log"], LANE)
        w["d_exp"] = jnp.repeat(fw["ssd_D"][0], SSD_P)[None, :]
        w["kv_norm_g"], w["q_norm_g"] = fw["kv_norm_g"][None, :], fw["q_norm_g"]
        w["ffn_cb_g"] = [fw["ffn_conv_b"][l:l + 1, :f] for l in range(DEPTH)]
        w["ffn_cb_v"] = [fw["ffn_conv_b"][l:l + 1, f:] for l in range(DEPTH)]
        for k in ("ln_mix_g", "ln_mix_b", "ln_ffn_g", "ln_ffn_b"):
            w[k] = [fw[k][l:l + 1] for l in range(DEPTH)]
    if "ssd_out_proj" in fw:
        w["norm_g"], w["ssd_out"] = fw["ssd_norm_g"], fw["ssd_out_proj"][0]
        kd = fw["kv_down_proj"]
        w["kv_down_c"], w["kv_down_r"] = kd[:, :MLA_KV_RANK], _pad_cols(kd[:, MLA_KV_RANK:], LANE)
        w["kv_up_k"], w["kv_up_v"] = fw["kv_up_k"], fw["kv_up_v"]
        w["q_down"] = fw["q_down_proj"][0]
        qu = fw["q_up_proj"][0].reshape(MLA_Q_RANK, MLA_H, MLA_NOPE + MLA_ROPE)
        w["q_up_n"] = qu[:, :, :MLA_NOPE].reshape(MLA_Q_RANK, MLA_H * MLA_NOPE)
        w["q_up_r"] = qu[:, :, MLA_NOPE:].reshape(MLA_Q_RANK, MLA_H * MLA_ROPE)
        w["attn_out"] = fw["attn_out_proj"][0]
        w["ffn_up_g"] = [fw["ffn_up"][l][:, :f] for l in range(DEPTH)]
        w["ffn_up_v"] = [fw["ffn_up"][l][:, f:] for l in range(DEPTH)]
        w["ffn_cw_g"] = [fw["ffn_conv_w"][l][:, :f] for l in range(DEPTH)]
        w["ffn_cw_v"] = [fw["ffn_conv_w"][l][:, f:] for l in range(DEPTH)]
        w["ffn_down"] = [fw["ffn_down"][l] for l in range(DEPTH)]
    return w


def _assemble_grads(g, names):
    make = {
        "ssd_in_proj": lambda: jnp.concatenate([g["in_z"], g["in_x"], g["in_b"], g["in_c"], g["in_dt"][:, :SSD_H]], axis=1)[None],
        "ssd_conv_w": lambda: jnp.concatenate([g["conv_w_" + s] for s in "xbc"], axis=1)[None],
        "ssd_conv_b": lambda: jnp.concatenate([g["conv_b_" + s] for s in "xbc"], axis=1),
        "ssd_dt_bias": lambda: g["dt_bias"][:, :SSD_H],
        "ssd_A_log": lambda: g["a_log"][:, :SSD_H],
        "ssd_D": lambda: jnp.sum(g["d_exp"].reshape(SSD_H, SSD_P), axis=1)[None, :],
        "ssd_norm_g": lambda: g["norm_g"],
        "ssd_out_proj": lambda: g["ssd_out"][None],
        "kv_down_proj": lambda: jnp.concatenate([g["kv_down_c"], g["kv_down_r"][:, :MLA_ROPE]], axis=1),
        "kv_norm_g": lambda: g["kv_norm_g"][0],
        "kv_up_k": lambda: g["kv_up_k"],
        "kv_up_v": lambda: g["kv_up_v"],
        "q_down_proj": lambda: g["q_down"][None],
        "q_norm_g": lambda: g["q_norm_g"],
        "q_up_proj": lambda: jnp.concatenate([g["q_up_n"].reshape(MLA_Q_RANK, MLA_H, MLA_NOPE),
                                              g["q_up_r"].reshape(MLA_Q_RANK, MLA_H, MLA_ROPE)], axis=2).reshape(1, MLA_Q_RANK, -1),
        "attn_out_proj": lambda: g["attn_out"][None],
        "ffn_up": lambda: jnp.stack([jnp.concatenate([g[f"ffn_up_g{l}"], g[f"ffn_up_v{l}"]], axis=1) for l in range(DEPTH)]),
        "ffn_conv_w": lambda: jnp.stack([jnp.concatenate([g[f"ffn_cw_g{l}"], g[f"ffn_cw_v{l}"]], axis=1) for l in range(DEPTH)]),
        "ffn_conv_b": lambda: jnp.concatenate([jnp.concatenate([g[f"ffn_cb_g{l}"], g[f"ffn_cb_v{l}"]], axis=1)
                                               for l in range(DEPTH)], axis=0),
        "ffn_down": lambda: jnp.stack([g[f"ffn_down{l}"] for l in range(DEPTH)]),
    }
    for k in ("ln_mix_g", "ln_mix_b", "ln_ffn_g", "ln_ffn_b"):
        make[k] = lambda k=k: jnp.concatenate([g[f"{k}{l}"] for l in range(DEPTH)], axis=0)
    return {n: make[n]() for n in names}


_WEIGHTS = ["ssd_in_proj", "ssd_conv_w", "ssd_conv_b", "ssd_dt_bias", "ssd_A_log", "ssd_D", "ssd_norm_g", "ssd_out_proj",
            "kv_down_proj", "kv_norm_g", "kv_up_k", "kv_up_v", "q_down_proj", "q_norm_g", "q_up_proj", "attn_out_proj",
            "ffn_up", "ffn_conv_w", "ffn_conv_b", "ffn_down", "ln_mix_g", "ln_mix_b", "ln_ffn_g", "ln_ffn_b"]
_COL_CUT = ["ssd_in_proj", "ssd_conv_w", "ssd_conv_b", "ssd_norm_g", "kv_up_k", "kv_up_v", "q_up_proj", "ffn_up", "ffn_conv_w"]
_ROW_CUT = ["ssd_out_proj", "kv_down_proj", "q_down_proj", "attn_out_proj", "ffn_down"]
_CUT = _COL_CUT + _ROW_CUT
_WHOLE = [n for n in _WEIGHTS if n not in _CUT]
_EARLY = ["ssd_in_proj", "ssd_conv_w", "ssd_conv_b"]
_LATE = [n for n in _CUT if n not in _EARLY]
_MXU_WEIGHTS = ["ssd_in_proj", "ssd_out_proj", "kv_down_proj", "kv_up_k", "kv_up_v", "q_down_proj", "q_up_proj",
                "attn_out_proj", "ffn_up", "ffn_down"]
_PACK_ROWS = 160


def _shard_2d(name, s):
    return s.reshape(-1, s.shape[-1])


def _unstack(name, g, shard_shape):
    if name in _COL_CUT:
        lead = shard_shape[:-1]
        return jnp.swapaxes(g, 0, 1).reshape(*lead, N_CHIPS * shard_shape[-1])
    lead, rs, c = shard_shape[:-2], shard_shape[-2], shard_shape[-1]
    n_lead = math.prod(lead)
    return jnp.swapaxes(g.reshape(N_CHIPS, n_lead, rs, c), 0, 1).reshape(*lead, N_CHIPS * rs, c)


def _stack(name, full, shard_shape):
    if name in _COL_CUT:
        cs = shard_shape[-1]
        return jnp.swapaxes(full.reshape(-1, N_CHIPS, cs), 0, 1)
    lead, rs, c = shard_shape[:-2], shard_shape[-2], shard_shape[-1]
    n_lead = math.prod(lead)
    return jnp.swapaxes(full.reshape(n_lead, N_CHIPS, rs, c), 0, 1).reshape(N_CHIPS, n_lead * rs, c)


def _pack(arrs):
    flat = jnp.concatenate([a.reshape(-1) for a in arrs])
    return jnp.pad(flat, (0, _PACK_ROWS * LANE - flat.shape[0])).reshape(_PACK_ROWS, LANE)


def _unpack(packed, like):
    flat, out, o = packed.reshape(-1), [], 0
    for a in like:
        out.append(flat[o:o + a.size].reshape(a.shape))
        o += a.size
    return out


_ARGS = ["x", "positions"] + _WEIGHTS + ["loss_target"] + ["m_" + n for n in _WEIGHTS] + ["v_" + n for n in _WEIGHTS]


def kernel(x, positions, ssd_in_proj, ssd_conv_w, ssd_conv_b, ssd_dt_bias, ssd_A_log, ssd_D, ssd_norm_g,
           ssd_out_proj, kv_down_proj, kv_norm_g, kv_up_k, kv_up_v, q_down_proj, q_norm_g, q_up_proj,
           attn_out_proj, ffn_up, ffn_conv_w, ffn_conv_b, ffn_down, ln_mix_g, ln_mix_b, ln_ffn_g, ln_ffn_b,
           loss_target, m_ssd_in_proj, m_ssd_conv_w, m_ssd_conv_b, m_ssd_dt_bias, m_ssd_A_log, m_ssd_D,
           m_ssd_norm_g, m_ssd_out_proj, m_kv_down_proj, m_kv_norm_g, m_kv_up_k, m_kv_up_v, m_q_down_proj,
           m_q_norm_g, m_q_up_proj, m_attn_out_proj, m_ffn_up, m_ffn_conv_w, m_ffn_conv_b, m_ffn_down,
           m_ln_mix_g, m_ln_mix_b, m_ln_ffn_g, m_ln_ffn_b, v_ssd_in_proj, v_ssd_conv_w, v_ssd_conv_b,
           v_ssd_dt_bias, v_ssd_A_log, v_ssd_D, v_ssd_norm_g, v_ssd_out_proj, v_kv_down_proj, v_kv_norm_g,
           v_kv_up_k, v_kv_up_v, v_q_down_proj, v_q_norm_g, v_q_up_proj, v_attn_out_proj, v_ffn_up,
           v_ffn_conv_w, v_ffn_conv_b, v_ffn_down, v_ln_mix_g, v_ln_mix_b, v_ln_ffn_g, v_ln_ffn_b):
    args = (x, positions, ssd_in_proj, ssd_conv_w, ssd_conv_b, ssd_dt_bias, ssd_A_log, ssd_D, ssd_norm_g,
            ssd_out_proj, kv_down_proj, kv_norm_g, kv_up_k, kv_up_v, q_down_proj, q_norm_g, q_up_proj,
            attn_out_proj, ffn_up, ffn_conv_w, ffn_conv_b, ffn_down, ln_mix_g, ln_mix_b, ln_ffn_g, ln_ffn_b,
            loss_target, m_ssd_in_proj, m_ssd_conv_w, m_ssd_conv_b, m_ssd_dt_bias, m_ssd_A_log, m_ssd_D,
            m_ssd_norm_g, m_ssd_out_proj, m_kv_down_proj, m_kv_norm_g, m_kv_up_k, m_kv_up_v, m_q_down_proj,
            m_q_norm_g, m_q_up_proj, m_attn_out_proj, m_ffn_up, m_ffn_conv_w, m_ffn_conv_b, m_ffn_down,
            m_ln_mix_g, m_ln_mix_b, m_ln_ffn_g, m_ln_ffn_b, v_ssd_in_proj, v_ssd_conv_w, v_ssd_conv_b,
            v_ssd_dt_bias, v_ssd_A_log, v_ssd_D, v_ssd_norm_g, v_ssd_out_proj, v_kv_down_proj, v_kv_norm_g,
            v_kv_up_k, v_kv_up_v, v_q_down_proj, v_q_norm_g, v_q_up_proj, v_attn_out_proj, v_ffn_up,
            v_ffn_conv_w, v_ffn_conv_b, v_ffn_down, v_ln_mix_g, v_ln_mix_b, v_ln_ffn_g, v_ln_ffn_b)
    a = dict(zip(_ARGS, args, strict=True))

    def shard(n):
        s = _shard_2d(n, a[n])
        return s.astype(BF16) if n in _MXU_WEIGHTS else s

    def full_weights(names, gathered):
        return {n: _unstack(n, g, a[n].shape) for n, g in zip(names, gathered)}

    def grad_stacks(names, pieces):
        full = _assemble_grads(pieces, names)
        return [_stack(n, full[n], a[n].shape).astype(BF16) for n in names]

    fw = full_weights(_EARLY, _gather_chips([shard(n) for n in _EARLY]))
    fw.update({n: a[n] for n in _WHOLE})
    carried = _Carried([shard(n) for n in _LATE], lambda got: _prep_weights(full_weights(_LATE, got)),
                       lambda pieces: grad_stacks(_LATE, pieces))

    loss_parts, grad_x, pieces = _local_step(a["x"], a["positions"], a["loss_target"], _prep_weights(fw), carried)
    loss = lax.psum(jnp.sum(loss_parts[::8, 0]), ("x", "y", "c"))

    bufs = dict(zip(_LATE, carried.received))
    bufs.update(zip(_EARLY, _scatter_chips(grad_stacks(_EARLY, pieces))))
    sums = [_sum_parts(bufs[n], "sum_chips_" + n) for n in _CUT]
    others = _swap_cores(sums)
    full = _assemble_grads(pieces, _WHOLE)
    every = _gather_all(_pack([full[n] for n in _WHOLE]))
    g_whole = _sum_parts(every.reshape(N_DEV, _PACK_ROWS, LANE), "sum_devices")

    res = {}
    for n, s, o in zip(_CUT, sums, others):
        out = _adamw(_shard_2d(n, a[n]), s, o, _shard_2d(n, a["m_" + n]), _shard_2d(n, a["v_" + n]), "adamw_" + n)
        res[n] = [r.reshape(a[n].shape) for r in out]
    whole = [a[n] for n in _WHOLE]
    out = _adamw(_pack(whole), g_whole, None, _pack([a["m_" + n] for n in _WHOLE]), _pack([a["v_" + n] for n in _WHOLE]),
                 "adamw_whole")
    for k, n in enumerate(_WHOLE):
        res[n] = [_unpack(r, whole)[k] for r in out]
    return (loss, grad_x, *[res[n][0] for n in _WEIGHTS], *[res[n][1] for n in _WEIGHTS],
            *[res[n][2] for n in _WEIGHTS], *[res[n][3] for n in _WEIGHTS])
```

```python
import functools
import math

import jax
import jax.numpy as jnp
from jax import lax
from jax.experimental import pallas as pl
from jax.experimental.pallas import tpu as pltpu

F32 = jnp.float32
BF16 = jnp.bfloat16
HIGHEST = lax.Precision.HIGHEST
MESH = pl.DeviceIdType.MESH

D_MODEL = 1024
DEPTH = 2
DN_ALPHA = (2 * DEPTH) ** 0.25
SSD_D_INNER = 2048
SSD_P = 64
SSD_H = 32
SSD_G = 8
SSD_K = 4
SSD_N = 128
SSD_L = 128
SSD_GN = SSD_G * SSD_N
MLA_H = 8
MLA_Q_RANK = 384
MLA_KV_RANK = 256
MLA_NOPE = 128
MLA_ROPE = 64
MLA_V = 128
ROPE_THETA = 10000.0
FFN_HIDDEN = 2816
LN_EPS = 1e-5
RMS_EPS = 1e-6
ADAM_LR = 0.001
ADAM_B1 = 0.9
ADAM_B2 = 0.999
ADAM_EPS = 1e-08
ADAM_WD = 0.01
ADAM_STEP = 10

N_CHIPS = 4
N_DEV = 8
LANE = 128
VMEM_LIMIT = 56 * 1024 * 1024


def _cparams(**kw):
    return pltpu.CompilerParams(vmem_limit_bytes=VMEM_LIMIT, **kw)


def _tile(dim, cap):
    best = None
    t = LANE
    while t <= min(dim, cap):
        if dim % t == 0:
            best = t
        t += LANE
    return dim if best is None else best


MM_TILE_CAP = 1408
MM_WHOLE_K = 2816
MM_VMEM_BUDGET = 40 * 1024 * 1024


def _mm_tiles(m, n, k, a_bytes, b_bytes, o_bytes, has_add):
    tm, tn = _tile(m, MM_TILE_CAP), _tile(n, MM_TILE_CAP)
    tk = k if k <= MM_WHOLE_K else _tile(k, 1024)

    def need(tm, tn):
        acc = tm * tn * 4 if tk < k else 0
        return 2 * (tm * tk * a_bytes + tk * tn * b_bytes + tm * tn * o_bytes + (tm * tn * 4 if has_add else 0)) + acc

    while need(tm, tn) > MM_VMEM_BUDGET:
        if tm >= tn and _tile(m, tm // 2) < tm:
            tm = _tile(m, tm // 2)
        elif _tile(n, tn // 2) < tn:
            tn = _tile(n, tn // 2)
        else:
            break
    return tm, tn, tk


def _mm(a, b, *, ta=False, tb=False, add=None, out_dtype=F32, name):
    m, k = (a.shape[1], a.shape[0]) if ta else a.shape
    n = b.shape[0] if tb else b.shape[1]
    assert (b.shape[1] if tb else b.shape[0]) == k
    tm, tn, tk = _mm_tiles(m, n, k, a.dtype.itemsize, b.dtype.itemsize, jnp.dtype(out_dtype).itemsize, add is not None)
    nk = k // tk
    dims = (((0 if ta else 1,), (1 if tb else 0,)), ((), ()))

    def partial_product(a_ref, b_ref):
        return lax.dot_general(a_ref[...].astype(BF16), b_ref[...].astype(BF16), dims, preferred_element_type=F32)

    def body_one(*refs):
        if add is None:
            a_ref, b_ref, o_ref = refs
            o_ref[...] = partial_product(a_ref, b_ref).astype(out_dtype)
        else:
            a_ref, b_ref, c_ref, o_ref = refs
            o_ref[...] = (partial_product(a_ref, b_ref) + c_ref[...]).astype(out_dtype)

    def body_acc(*refs):
        if add is None:
            a_ref, b_ref, o_ref, acc = refs
        else:
            a_ref, b_ref, c_ref, o_ref, acc = refs
        kk = pl.program_id(2)

        @pl.when(kk == 0)
        def _():
            acc[...] = jnp.zeros_like(acc) if add is None else c_ref[...]

        acc[...] += partial_product(a_ref, b_ref)

        @pl.when(kk == nk - 1)
        def _():
            o_ref[...] = acc[...].astype(out_dtype)

    a_spec = pl.BlockSpec((tk, tm), lambda i, j, kk: (kk, i)) if ta else pl.BlockSpec((tm, tk), lambda i, j, kk: (i, kk))
    b_spec = pl.BlockSpec((tn, tk), lambda i, j, kk: (j, kk)) if tb else pl.BlockSpec((tk, tn), lambda i, j, kk: (kk, j))
    o_spec = pl.BlockSpec((tm, tn), lambda i, j, kk: (i, j))
    ins, specs = [a, b], [a_spec, b_spec]
    if add is not None:
        ins.append(add)
        specs.append(o_spec)
    return pl.pallas_call(
        body_one if nk == 1 else body_acc, name=name, grid=(m // tm, n // tn, nk), in_specs=specs, out_specs=o_spec,
        out_shape=jax.ShapeDtypeStruct((m, n), out_dtype),
        scratch_shapes=[] if nk == 1 else [pltpu.VMEM((tm, tn), F32)],
        compiler_params=_cparams(dimension_semantics=("parallel", "parallel", "arbitrary")),
    )(*ins)


def _spec(op):
    return pl.BlockSpec(op[1], op[2])


def _bw_fwd(name, fn, grid, ins, outs, out_dtypes):
    n_in = len(ins)
    flat = [(o, dt) for o, dts in zip(outs, out_dtypes) for dt in dts]

    def body(*refs):
        res = fn(*[r[...].astype(F32) for r in refs[:n_in]])
        orefs = iter(refs[n_in:])
        for v, dts in zip(res, out_dtypes):
            for dt in dts:
                next(orefs)[...] = v.astype(dt)

    return pl.pallas_call(
        body, name=name, grid=grid, in_specs=[_spec(o) for o in ins],
        out_specs=[pl.BlockSpec(o[1], o[2]) for o, _ in flat],
        out_shape=[jax.ShapeDtypeStruct(o[0], dt) for o, dt in flat], compiler_params=_cparams(),
    )(*[o[0] for o in ins])


def _bw_bwd(name, fn, grid, data, params, consts, cts, red_axes, grad_dtypes):
    nd, npar, nc, nct = len(data), len(params), len(consts), len(cts)

    def body(*refs):
        first = None
        for ax in red_axes:
            z = pl.program_id(ax) == 0
            first = z if first is None else jnp.logical_and(first, z)
        vals = [r[...].astype(F32) for r in refs[:nd + npar + nc + nct]]
        d, p, c, g = vals[:nd], vals[nd:nd + npar], vals[nd + npar:nd + npar + nc], vals[nd + npar + nc:]
        _, vjp = jax.vjp(lambda dd, pp: tuple(fn(*dd, *pp, *c)), d, p)
        gd, gp = vjp(tuple(g))
        orefs = refs[nd + npar + nc + nct:]
        for r, v in zip(orefs[:nd], gd):
            r[...] = v.astype(r.dtype)
        if npar:
            @pl.when(first)
            def _():
                for r in orefs[nd:]:
                    r[...] = jnp.zeros_like(r)

            for r, v in zip(orefs[nd:], gp):
                r[...] += v

    ins = list(data) + list(params) + list(consts) + list(cts)
    outs = list(data) + list(params)
    dtypes = list(grad_dtypes) + [F32] * npar
    return pl.pallas_call(
        body, name=name, grid=grid, in_specs=[_spec(o) for o in ins], out_specs=[_spec(o) for o in outs],
        out_shape=[jax.ShapeDtypeStruct(o[0].shape, dt) for o, dt in zip(outs, dtypes)], compiler_params=_cparams(),
    )(*[o[0] for o in ins])


def _shift_down(x, s):
    row = lax.broadcasted_iota(jnp.int32, x.shape, 0)
    return jnp.where(row < s, 0.0, pltpu.roll(x, s, 0))


def _shift_up(x, s):
    n = x.shape[0]
    row = lax.broadcasted_iota(jnp.int32, x.shape, 0)
    return jnp.where(row >= n - s, 0.0, pltpu.roll(x, n - s, 0))


def _time_shift(s):
    if s == 0:
        return lambda x: x

    @jax.custom_vjp
    def shift(x):
        return _shift_down(x, s)

    shift.defvjp(lambda x: (_shift_down(x, s), None), lambda _, g: (_shift_up(g, s),))
    return shift


def _rot_half_raw(x):
    lane = lax.broadcasted_iota(jnp.int32, x.shape, 1)
    return jnp.where(lane % MLA_ROPE < MLA_ROPE // 2, -pltpu.roll(x, LANE - MLA_ROPE // 2, 1), pltpu.roll(x, MLA_ROPE // 2, 1))


@jax.custom_vjp
def _rot_half(x):
    return _rot_half_raw(x)


_rot_half.defvjp(lambda x: (_rot_half_raw(x), None), lambda _, g: (-_rot_half_raw(g),))


@jax.custom_vjp
def _roll_half_lanes(x):
    return pltpu.roll(x, LANE // 2, 1)


_roll_half_lanes.defvjp(lambda x: (pltpu.roll(x, LANE // 2, 1), None), lambda _, g: (pltpu.roll(g, LANE // 2, 1),))


def _causal_conv(u, w, b):
    width = w.shape[0]
    y = b
    for k in range(width):
        y = y + w[k:k + 1, :] * _time_shift(width - 1 - k)(u)
    return y


def _silu(x):
    return x * jax.nn.sigmoid(x)


def _f_ln(h, mix, g, b):
    x = DN_ALPHA * h + mix
    mu = jnp.mean(x, axis=-1, keepdims=True)
    xc = x - mu
    var = jnp.mean(xc * xc, axis=-1, keepdims=True)
    return (xc * lax.rsqrt(var + LN_EPS) * g + b,)


def _f_convsilu(u, w, b):
    return (_silu(_causal_conv(u, w, b)),)


def _f_convffn(ug, uv, wg, wv, bg, bv):
    return (_silu(_causal_conv(ug, wg, bg)) * _causal_conv(uv, wv, bv),)


def _f_dt(dt_raw, bias, a_log):
    x = dt_raw + bias
    dt = jnp.maximum(x, 0.0) + jnp.log(1.0 + jnp.exp(-jnp.abs(x)))
    a = dt * (-jnp.exp(a_log))
    n = a.shape[0]
    lower = (lax.broadcasted_iota(jnp.int32, (n, n), 0) >= lax.broadcasted_iota(jnp.int32, (n, n), 1)).astype(F32)
    cum = jnp.dot(lower, a, precision=HIGHEST, preferred_element_type=F32)
    cum_last = jnp.sum(a, axis=0, keepdims=True)
    return dt, cum, jnp.exp(cum), jnp.exp(cum_last - cum) * dt


def _f_ssd_post(y, xs, z, d_exp, ng):
    t = (y + d_exp * xs) * _silu(z)
    return (t * lax.rsqrt(jnp.mean(t * t, axis=-1, keepdims=True) + LN_EPS) * ng,)


def _f_rms(x, g):
    return (x * lax.rsqrt(jnp.mean(x * x, axis=-1, keepdims=True) + RMS_EPS) * g,)


def _f_rope(x, cos, sin):
    return (x * cos + _rot_half(x) * sin,)


def _f_rope_dup(x, cos, sin):
    r = x * cos + _rot_half(x) * sin
    return (r + _roll_half_lanes(r),)


def _rows(t, cap=512):
    for c in (cap, 256, 128, 64, 32, 16, 8):
        if c <= cap and t % c == 0:
            return c
    return t


class _Blockwise:
    def __init__(self, name, fn, grid, data, params, consts, outs, red_axes):
        self.name, self.fn, self.grid = name, fn, grid
        self.data, self.params, self.consts, self.outs, self.red_axes = data, params, consts, outs, red_axes

    def fwd(self, *out_dtypes):
        out_dtypes = out_dtypes or tuple((F32,) for _ in self.outs)
        return _bw_fwd(self.name + "_fwd", self.fn, self.grid, self.data + self.params + self.consts, self.outs, out_dtypes)

    def bwd(self, cts, grad_dtypes=None):
        cts = [(c, o[1], o[2]) for c, o in zip(cts, self.outs)]
        grad_dtypes = grad_dtypes or [F32] * len(self.data)
        res = _bw_bwd(self.name + "_bwd", self.fn, self.grid, self.data, self.params, self.consts, cts, self.red_axes,
                      grad_dtypes)
        return res[:len(self.data)], res[len(self.data):]


def _op_ln(name, h, mix, g, b):
    t, d = h.shape
    tr = _rows(t, 256)
    row = ((tr, d), lambda i: (i, 0))
    par = ((1, d), lambda i: (0, 0))
    return _Blockwise(name, _f_ln, (t // tr,), [(h, *row), (mix, *row)], [(g, *par), (b, *par)], [],
                      [((t, d), *row)], (0,))


def _op_conv(name, fn, us, ws, bs, nb, seq):
    c = us[0].shape[1]
    ct = _tile(c, 256)
    blk = ((seq, ct), lambda j, bb: (bb, j))
    data = [(u, *blk) for u in us]
    params = [(w, (w.shape[0], ct), lambda j, bb: (0, j)) for w in ws] + [(b, (1, ct), lambda j, bb: (0, j)) for b in bs]
    return _Blockwise(name, fn, (c // ct, nb), data, params, [], [((nb * seq, c), *blk)], (1,))


def _op_dt(name, dt_raw, bias, a_log):
    t = dt_raw.shape[0]
    row = ((SSD_L, LANE), lambda i: (i, 0))
    par = ((1, LANE), lambda i: (0, 0))
    return _Blockwise(name, _f_dt, (t // SSD_L,), [(dt_raw, *row)], [(bias, *par), (a_log, *par)], [],
                      [((t, LANE), *row)] * 4, (0,))


def _op_ssd_post(name, y, xs, z, d_exp, ng):
    t, c = y.shape
    gw = c // SSD_G
    tr = _rows(t, 512)
    blk = ((tr, gw), lambda g, i: (i, g))
    par = ((1, gw), lambda g, i: (0, g))
    return _Blockwise(name, _f_ssd_post, (SSD_G, t // tr), [(y, *blk), (xs, *blk), (z, *blk)],
                      [(d_exp, *par), (ng, *par)], [], [((t, c), *blk)], (1,))


def _op_rms(name, x, g):
    t, c = x.shape
    tr = _rows(t, 512)
    return _Blockwise(name, _f_rms, (t // tr,), [(x, (tr, c), lambda i: (i, 0))], [(g, (1, c), lambda i: (0, 0))], [],
                      [((t, c), (tr, c), lambda i: (i, 0))], (0,))


def _op_rope(name, fn, x, cos, sin):
    t, c = x.shape
    tr = _rows(t, 512)
    blk = ((tr, LANE), lambda i, j: (i, j))
    cs = ((tr, LANE), lambda i, j: (i, 0))
    return _Blockwise(name, fn, (t // tr, c // LANE), [(x, *blk)], [], [(cos, *cs), (sin, *cs)], [((t, c), *blk)], ())


_NT = (((1,), (1,)), ((), ()))
_TN = (((0,), (0,)), ((), ()))


def _ssd_head(x, g, bc, cc, s, cum_row, dt_row, cum_col, ecum_col, wend_col):
    n = g.shape[0]
    row = lax.broadcasted_iota(jnp.int32, (n, n), 0)
    col = lax.broadcasted_iota(jnp.int32, (n, n), 1)
    decay = jnp.exp(jnp.where(row >= col, cum_col - cum_row, -jnp.inf))
    w = g * decay * dt_row
    y = jnp.dot(w.astype(BF16), x.astype(BF16), preferred_element_type=F32)
    y = y + lax.dot_general((cc * ecum_col).astype(BF16), s.astype(BF16), _NT, preferred_element_type=F32)
    lane = lax.broadcasted_iota(jnp.int32, cum_row.shape, 1)
    cum_last = jnp.sum(jnp.where(lane == n - 1, cum_row, 0.0), axis=1, keepdims=True)
    s_new = s * jnp.exp(cum_last) + lax.dot_general(x.astype(BF16), (bc * wend_col).astype(BF16), _TN,
                                                    preferred_element_type=F32)
    return y, s_new


SSD_KPAD = 8
SSD_ROWS = 4 * SSD_KPAD
_CUM, _DT, _ECUM, _WEND = 0, SSD_KPAD, 2 * SSD_KPAD, 3 * SSD_KPAD


def _ssd_head_args(rows, cols, k):
    return (rows[_CUM + k:_CUM + k + 1], rows[_DT + k:_DT + k + 1], cols[:, _CUM + k:_CUM + k + 1],
            cols[:, _ECUM + k:_ECUM + k + 1], cols[:, _WEND + k:_WEND + k + 1])


def _carried_exchange(arrs, in_refs, out_refs, sems, scatter, n_steps):
    if not arrs:
        return lambda: None, lambda: None
    first = functools.reduce(jnp.logical_and, [pl.program_id(ax) == 0 for ax in range(len(n_steps))])
    last = functools.reduce(jnp.logical_and, [pl.program_id(ax) == n - 1 for ax, n in enumerate(n_steps)])
    start, wait = _chip_exchange(in_refs, out_refs, sems, scatter)
    return (lambda: pl.when(first)(start)), (lambda: pl.when(last)(wait))


def _ssd_scan_fwd(xs, bm, cm, rows, nb, seq, gather=()):
    nc = seq // SSD_L
    kp = SSD_K * SSD_P
    ng = len(gather)

    def body(*refs):
        xs_ref, b_ref, c_ref, row_ref = refs[:4]
        y_ref, st_ref = refs[4 + ng:6 + ng]
        start, wait = _carried_exchange(gather, refs[4:4 + ng], refs[6 + ng:6 + 2 * ng], refs[6 + 2 * ng:], False,
                                        (nb, SSD_G))
        start()

        def chunk(c, states):
            sl = pl.ds(pl.multiple_of(c * SSD_L, SSD_L), SSD_L)
            bc, cc = b_ref[sl, :], c_ref[sl, :]
            g = lax.dot_general(cc.astype(BF16), bc.astype(BF16), _NT, preferred_element_type=F32)
            rows_c = row_ref[0, :, sl]
            cols_c = rows_c.T
            new = []
            for k in range(SSD_K):
                hs = pl.ds(k * SSD_P, SSD_P)
                st_ref[0, c * SSD_K + k] = states[k]
                y, s_new = _ssd_head(xs_ref[sl, hs], g, bc, cc, states[k], *_ssd_head_args(rows_c, cols_c, k))
                y_ref[sl, hs] = y
                new.append(s_new)
            return tuple(new)

        lax.fori_loop(0, nc, chunk, tuple(jnp.zeros((SSD_P, SSD_N), F32) for _ in range(SSD_K)))
        wait()

    t = xs.shape[0]
    any_spec = pl.BlockSpec(memory_space=pl.ANY)
    return pl.pallas_call(
        body, name="ssd_scan_fwd", grid=(nb, SSD_G),
        in_specs=[pl.BlockSpec((seq, kp), lambda b, g: (b, g)),
                  pl.BlockSpec((seq, SSD_N), lambda b, g: (b, g)),
                  pl.BlockSpec((seq, SSD_N), lambda b, g: (b, g)),
                  pl.BlockSpec((1, SSD_ROWS, seq), lambda b, g: (b * SSD_G + g, 0, 0))] + [any_spec] * ng,
        out_specs=[pl.BlockSpec((seq, kp), lambda b, g: (b, g)),
                   pl.BlockSpec((1, nc * SSD_K, SSD_P, SSD_N), lambda b, g: (b * SSD_G + g, 0, 0, 0))] + [any_spec] * ng,
        out_shape=[jax.ShapeDtypeStruct((t, SSD_D_INNER), F32),
                   jax.ShapeDtypeStruct((nb * SSD_G, nc * SSD_K, SSD_P, SSD_N), F32)] + _exchange_shapes(gather, False),
        scratch_shapes=_exchange_sems(ng) if ng else [],
        compiler_params=_cparams(dimension_semantics=("arbitrary", "arbitrary")),
    )(xs, bm, cm, rows, *gather)


def _ssd_scan_bwd(xs, bm, cm, rows, states, dy, dxs_skip, nb, seq, scatter=()):
    nc = seq // SSD_L
    kp = SSD_K * SSD_P
    ns = len(scatter)

    def body(*refs):
        xs_ref, b_ref, c_ref, row_ref, st_ref, dy_ref, skip_ref = refs[:7]
        dxs_ref, db_ref, dc_ref, drow_ref = refs[7 + ns:11 + ns]
        dcol_ref = refs[11 + 2 * ns]
        start, wait = _carried_exchange(scatter, refs[7:7 + ns], refs[11 + ns:11 + 2 * ns], refs[12 + 2 * ns:], True,
                                        (nb, SSD_G))
        start()

        def chunk(i, dstates):
            c = nc - 1 - i
            sl = pl.ds(pl.multiple_of(c * SSD_L, SSD_L), SSD_L)
            bc, cc = b_ref[sl, :], c_ref[sl, :]
            bcb, ccb = bc.astype(BF16), cc.astype(BF16)
            g = lax.dot_general(ccb, bcb, _NT, preferred_element_type=F32)
            rows_c = row_ref[0, :, sl]
            cols_c = rows_c.T
            drow_ref[0, :, sl] = jnp.zeros((SSD_ROWS, SSD_L), F32)
            dcol_ref[...] = jnp.zeros((SSD_L, SSD_ROWS), F32)
            db = jnp.zeros((SSD_L, SSD_N), F32)
            dc = jnp.zeros((SSD_L, SSD_N), F32)
            dg = jnp.zeros((SSD_L, SSD_L), F32)
            new = []
            for k in range(SSD_K):
                hs = pl.ds(k * SSD_P, SSD_P)
                _, vjp = jax.vjp(_ssd_head, xs_ref[sl, hs], g, bc, cc, st_ref[0, c * SSD_K + k],
                                 *_ssd_head_args(rows_c, cols_c, k))
                dx, dgk, dbk, dck, ds, d_cum_row, d_dt_row, d_cum_col, d_ecum_col, d_wend_col = vjp(
                    (dy_ref[sl, hs], dstates[k]))
                dxs_ref[sl, hs] = dx + skip_ref[sl, hs]
                db, dc, dg = db + dbk, dc + dck, dg + dgk
                drow_ref[0, _CUM + k:_CUM + k + 1, sl] = d_cum_row
                drow_ref[0, _DT + k:_DT + k + 1, sl] = d_dt_row
                dcol_ref[:, _CUM + k:_CUM + k + 1] = d_cum_col
                dcol_ref[:, _ECUM + k:_ECUM + k + 1] = d_ecum_col
                dcol_ref[:, _WEND + k:_WEND + k + 1] = d_wend_col
                new.append(ds)
            drow_ref[0, :, sl] += dcol_ref[...].T
            dgb = dg.astype(BF16)
            dc_ref[sl, :] = dc + jnp.dot(dgb, bcb, preferred_element_type=F32)
            db_ref[sl, :] = db + lax.dot_general(dgb, ccb, _TN, preferred_element_type=F32)
            return tuple(new)

        lax.fori_loop(0, nc, chunk, tuple(jnp.zeros((SSD_P, SSD_N), F32) for _ in range(SSD_K)))
        wait()

    t = xs.shape[0]
    x_spec = pl.BlockSpec((seq, kp), lambda b, g: (b, g))
    n_spec = pl.BlockSpec((seq, SSD_N), lambda b, g: (b, g))
    r_spec = pl.BlockSpec((1, SSD_ROWS, seq), lambda b, g: (b * SSD_G + g, 0, 0))
    any_spec = pl.BlockSpec(memory_space=pl.ANY)
    return pl.pallas_call(
        body, name="ssd_scan_bwd", grid=(nb, SSD_G),
        in_specs=[x_spec, n_spec, n_spec, r_spec,
                  pl.BlockSpec((1, nc * SSD_K, SSD_P, SSD_N), lambda b, g: (b * SSD_G + g, 0, 0, 0)), x_spec, x_spec]
        + [any_spec] * ns,
        out_specs=[x_spec, n_spec, n_spec, r_spec] + [any_spec] * ns,
        out_shape=[jax.ShapeDtypeStruct((t, SSD_D_INNER), F32), jax.ShapeDtypeStruct((t, SSD_GN), F32),
                   jax.ShapeDtypeStruct((t, SSD_GN), F32), jax.ShapeDtypeStruct(rows.shape, F32)]
        + _exchange_shapes(scatter, True),
        scratch_shapes=[pltpu.VMEM((SSD_L, SSD_ROWS), F32)] + (_exchange_sems(ns) if ns else []),
        compiler_params=_cparams(dimension_semantics=("arbitrary", "arbitrary")),
    )(xs, bm, cm, rows, states, dy, dxs_skip, *scatter)


ATT_TQ = 256
ATT_TE = 512
ATT_SCALE = (MLA_NOPE + MLA_ROPE) ** -0.5


def _for_key_extent(qi, seq, fn):
    te = min(ATT_TE, seq)
    per = te // ATT_TQ
    for e in range(seq // te):
        pl.when(jnp.logical_and(qi >= e * per, qi < (e + 1) * per))(functools.partial(fn, (e + 1) * te))


def _pair_masks(shape):
    lane = lax.broadcasted_iota(jnp.int32, shape, 1)
    return lane < MLA_ROPE, lane >= MLA_ROPE


def _scores(qn, qr, kn, kr, q0):
    s = lax.dot_general(qn, kn, _NT, preferred_element_type=F32) + lax.dot_general(qr, kr, _NT, preferred_element_type=F32)
    row = lax.broadcasted_iota(jnp.int32, s.shape, 0)
    col = lax.broadcasted_iota(jnp.int32, s.shape, 1)
    return jnp.where(col <= q0 + row, s * ATT_SCALE, -jnp.inf)


def _attn_specs(nb, seq):
    nq = seq // ATT_TQ
    qn = pl.BlockSpec((ATT_TQ, 2 * MLA_NOPE), lambda b, hp, qi: (b * nq + qi, hp))
    qr = pl.BlockSpec((ATT_TQ, LANE), lambda b, hp, qi: (b * nq + qi, hp))
    kn = pl.BlockSpec((seq, 2 * MLA_NOPE), lambda b, hp, qi: (b, hp))
    kr = pl.BlockSpec((seq, LANE), lambda b, hp, qi: (b, 0))
    return (nb, MLA_H // 2, nq), qn, qr, kn, kr


def _attn_fwd(qn, qr, kn, kr, v, nb, seq, gather=()):
    grid, s_qn, s_qr, s_kn, s_kr = _attn_specs(nb, seq)
    ng = len(gather)

    def body(*refs):
        qn_ref, qr_ref, kn_ref, kr_ref, v_ref = refs[:5]
        o_ref, lse_ref = refs[5 + ng:7 + ng]
        start, wait = _carried_exchange(gather, refs[5:5 + ng], refs[7 + ng:7 + 2 * ng], refs[7 + 2 * ng:], False, grid)
        start()
        qi = pl.program_id(2)

        def attend(ext):
            qr = qr_ref[...]
            masks = _pair_masks(qr.shape)
            outs, lses = [], []
            for j in range(2):
                hs = pl.ds(j * MLA_NOPE, MLA_NOPE)
                qr_j = jnp.where(masks[j], qr, jnp.zeros_like(qr))
                s = _scores(qn_ref[:, hs], qr_j, kn_ref[0:ext, hs], kr_ref[0:ext, :], qi * ATT_TQ)
                m = jnp.max(s, axis=1, keepdims=True)
                p = jnp.exp(s - m)
                l = jnp.sum(p, axis=1, keepdims=True)
                outs.append(jnp.dot(p.astype(BF16), v_ref[0:ext, hs], preferred_element_type=F32) / l)
                lses.append(m + jnp.log(l))
            o_ref[...] = jnp.concatenate(outs, axis=1).astype(o_ref.dtype)
            lse_ref[...] = jnp.where(masks[0], lses[0], lses[1])

        _for_key_extent(qi, seq, attend)
        wait()

    any_spec = pl.BlockSpec(memory_space=pl.ANY)
    return pl.pallas_call(
        body, name="attn_fwd", grid=grid, in_specs=[s_qn, s_qr, s_kn, s_kr, s_kn] + [any_spec] * ng,
        out_specs=[s_qn, s_qr] + [any_spec] * ng,
        out_shape=[jax.ShapeDtypeStruct(qn.shape, BF16), jax.ShapeDtypeStruct(qr.shape, F32)] + _exchange_shapes(gather, False),
        scratch_shapes=_exchange_sems(ng) if ng else [],
        compiler_params=_cparams(dimension_semantics=("arbitrary", "arbitrary", "arbitrary")),
    )(qn, qr, kn, kr, v, *gather)


def _attn_bwd(qn, qr, kn, kr, v, o, lse, do, nb, seq):
    grid, s_qn, s_qr, s_kn, s_kr = _attn_specs(nb, seq)

    def body(qn_ref, qr_ref, kn_ref, kr_ref, v_ref, o_ref, lse_ref, do_ref, dqn_ref, dqr_ref, dkn_ref, dkr_ref, dv_ref):
        hp, qi = pl.program_id(1), pl.program_id(2)

        @pl.when(qi == 0)
        def _():
            dkn_ref[...] = jnp.zeros_like(dkn_ref)
            dv_ref[...] = jnp.zeros_like(dv_ref)

        @pl.when(jnp.logical_and(qi == 0, hp == 0))
        def _():
            dkr_ref[...] = jnp.zeros_like(dkr_ref)

        def attend(ext):
            qr, lse = qr_ref[...], lse_ref[...]
            masks = _pair_masks(qr.shape)
            dqr_heads = []
            for j in range(2):
                hs = pl.ds(j * MLA_NOPE, MLA_NOPE)
                qn_j, qr_j = qn_ref[:, hs], jnp.where(masks[j], qr, jnp.zeros_like(qr))
                kn_j, kr, v_j = kn_ref[0:ext, hs], kr_ref[0:ext, :], v_ref[0:ext, hs]
                do_j = do_ref[:, hs]
                dob = do_j.astype(BF16)
                delta = jnp.sum(do_j * o_ref[:, hs].astype(F32), axis=1, keepdims=True)
                p = jnp.exp(_scores(qn_j, qr_j, kn_j, kr, qi * ATT_TQ) - lse[:, j * MLA_ROPE:j * MLA_ROPE + 1])
                dp = lax.dot_general(dob, v_j, _NT, preferred_element_type=F32)
                ds = (p * (dp - delta) * ATT_SCALE).astype(BF16)
                dkn_ref[0:ext, hs] += lax.dot_general(ds, qn_j, _TN, preferred_element_type=F32)
                dkr_ref[0:ext, :] += lax.dot_general(ds, qr_j, _TN, preferred_element_type=F32)
                dv_ref[0:ext, hs] += lax.dot_general(p.astype(BF16), dob, _TN, preferred_element_type=F32)
                dqn_ref[:, hs] = jnp.dot(ds, kn_j, preferred_element_type=F32).astype(dqn_ref.dtype)
                dqr_heads.append(jnp.dot(ds, kr, preferred_element_type=F32))
            dqr_ref[...] = jnp.where(masks[0], dqr_heads[0], dqr_heads[1])

        _for_key_extent(qi, seq, attend)

    return pl.pallas_call(
        body, name="attn_bwd", grid=grid, in_specs=[s_qn, s_qr, s_kn, s_kr, s_kn, s_qn, s_qr, s_qn],
        out_specs=[s_qn, s_qr, s_kn, s_kr, s_kn],
        out_shape=[jax.ShapeDtypeStruct(qn.shape, BF16)] + [jax.ShapeDtypeStruct(a.shape, F32) for a in (qr, kn, kr, v)],
        compiler_params=_cparams(),
    )(qn, qr, kn, kr, v, o, lse, do)


def _loss_head(y, target):
    t, d = y.shape
    tr = _rows(t, 512)

    def body(y_ref, t_ref, l_ref, dy_ref):
        err = y_ref[...] - t_ref[...]
        dy_ref[...] = err * (1.0 / d)
        part = 0.5 * jnp.sum(jnp.sum(err * err, axis=1, keepdims=True), axis=0, keepdims=True) * (1.0 / d)
        l_ref[...] = jnp.broadcast_to(part, l_ref.shape)

    row = pl.BlockSpec((tr, d), lambda i: (i, 0))
    parts, dy = pl.pallas_call(
        body, name="loss_head", grid=(t // tr,), in_specs=[row, row],
        out_specs=[pl.BlockSpec((8, LANE), lambda i: (i, 0)), row],
        out_shape=[jax.ShapeDtypeStruct((8 * (t // tr), LANE), F32), jax.ShapeDtypeStruct((t, d), F32)],
        compiler_params=_cparams(),
    )(y, target)
    return parts, dy


def _sum_parts(stack, name):
    n, r, c = stack.shape
    tr = _rows(r, 512)

    def body(s_ref, o_ref):
        acc = s_ref[0].astype(F32)
        for i in range(1, n):
            acc = acc + s_ref[i].astype(F32)
        o_ref[...] = acc

    return pl.pallas_call(
        body, name=name, grid=(r // tr,), in_specs=[pl.BlockSpec((n, tr, c), lambda i: (0, i, 0))],
        out_specs=pl.BlockSpec((tr, c), lambda i: (i, 0)), out_shape=jax.ShapeDtypeStruct((r, c), F32),
        compiler_params=_cparams(),
    )(stack)


def _adamw(w, g_mine, g_other, m, v, name):
    r, c = w.shape
    tr = _rows(r, 256)
    bc1 = 1.0 / (1.0 - ADAM_B1 ** ADAM_STEP)
    bc2 = 1.0 / (1.0 - ADAM_B2 ** ADAM_STEP)
    two = g_other is not None

    def body(*refs):
        if two:
            w_ref, g_ref, g2_ref, m_ref, v_ref, go_ref, d_ref, mo_ref, vo_ref = refs
            g = g_ref[...] + g2_ref[...]
        else:
            w_ref, g_ref, m_ref, v_ref, go_ref, d_ref, mo_ref, vo_ref = refs
            g = g_ref[...]
        mn = ADAM_B1 * m_ref[...] + (1.0 - ADAM_B1) * g
        vn = ADAM_B2 * v_ref[...] + (1.0 - ADAM_B2) * (g * g)
        go_ref[...] = g
        mo_ref[...] = mn
        vo_ref[...] = vn
        d_ref[...] = -ADAM_LR * ((mn * bc1) / (jnp.sqrt(vn * bc2) + ADAM_EPS) + ADAM_WD * w_ref[...])

    blk = pl.BlockSpec((tr, c), lambda i: (i, 0))
    ins = [w, g_mine] + ([g_other] if two else []) + [m, v]
    return pl.pallas_call(
        body, name=name, grid=(r // tr,), in_specs=[blk] * len(ins), out_specs=[blk] * 4,
        out_shape=[jax.ShapeDtypeStruct((r, c), F32)] * 4, compiler_params=_cparams(),
    )(*ins)


def _chip_peers():
    x, y, c = lax.axis_index("x"), lax.axis_index("y"), lax.axis_index("c")
    return (x, y, c), [(1 - x, y), (x, 1 - y), (1 - x, 1 - y)]


def _chip_exchange(ins, outs, sems, scatter):
    send_sems, recv_sems, local_sems = sems
    (x, y, c), chips = _chip_peers()
    me = 2 * x + y
    n = len(ins)

    def src(i, chip):
        return ins[i].at[chip] if scatter else ins[i]

    def local(i):
        return pltpu.make_async_copy(src(i, me), outs[i].at[me], local_sems.at[i])

    def remote(i, j, piece, slot):
        px, py = chips[j]
        return pltpu.make_async_remote_copy(src_ref=src(i, piece), dst_ref=outs[i].at[slot], send_sem=send_sems.at[i, j],
                                            recv_sem=recv_sems.at[i, j], device_id=(px, py, c), device_id_type=MESH)

    def start():
        for i in range(n):
            local(i).start()
            for j, (px, py) in enumerate(chips):
                remote(i, j, 2 * px + py, me).start()

    def wait():
        for i in range(n):
            for j, (px, py) in enumerate(chips):
                remote(i, j, me, 2 * px + py).wait_recv()
        for i in range(n):
            for j, (px, py) in enumerate(chips):
                remote(i, j, 2 * px + py, me).wait_send()
            local(i).wait()

    return start, wait


def _exchange_sems(n):
    return [pltpu.SemaphoreType.DMA((n, 3)), pltpu.SemaphoreType.DMA((n, 3)), pltpu.SemaphoreType.DMA((n,))]


def _exchange_shapes(arrs, scatter):
    return [jax.ShapeDtypeStruct(s.shape if scatter else (N_CHIPS,) + s.shape, s.dtype) for s in arrs]


def _exchange_call(name, arrs, scatter):
    n = len(arrs)

    def body(*refs):
        start, wait = _chip_exchange(refs[:n], refs[n:2 * n], refs[2 * n:], scatter)
        start()
        wait()

    any_spec = pl.BlockSpec(memory_space=pl.ANY)
    return pl.pallas_call(body, name=name, in_specs=[any_spec] * n, out_specs=[any_spec] * n,
                          out_shape=_exchange_shapes(arrs, scatter), scratch_shapes=_exchange_sems(n))(*arrs)


def _gather_chips(shards, name="gather_chips"):
    return _exchange_call(name, shards, False)


def _scatter_chips(stacks, name="scatter_chips"):
    return _exchange_call(name, stacks, True)


def _swap_cores(arrs):
    n = len(arrs)

    def body(*refs):
        ins, outs = refs[:n], refs[n:2 * n]
        send_sems, recv_sems = refs[2 * n:]
        x, y, c = lax.axis_index("x"), lax.axis_index("y"), lax.axis_index("c")
        cps = []
        for i in range(n):
            cp = pltpu.make_async_remote_copy(src_ref=ins[i], dst_ref=outs[i], send_sem=send_sems.at[i],
                                              recv_sem=recv_sems.at[i], device_id=(x, y, 1 - c), device_id_type=MESH)
            cp.start()
            cps.append(cp)
        for cp in cps:
            cp.wait()

    any_spec = pl.BlockSpec(memory_space=pl.ANY)
    return pl.pallas_call(
        body, name="swap_cores", in_specs=[any_spec] * n, out_specs=[any_spec] * n,
        out_shape=[jax.ShapeDtypeStruct(s.shape, s.dtype) for s in arrs],
        scratch_shapes=[pltpu.SemaphoreType.DMA((n,)), pltpu.SemaphoreType.DMA((n,))],
    )(*arrs)


def _gather_all(block):
    m_per, n = block.shape

    def body(x_ref, out_ref, send_sems, recv_sems, local_sem):
        x, y, c = lax.axis_index("x"), lax.axis_index("y"), lax.axis_index("c")
        me, sibling = (x, y, c), (x, y, 1 - c)
        chips = [(1 - x, y), (x, 1 - y), (1 - x, 1 - y)]

        def rows(px, py, pc):
            return out_ref.at[pl.ds((4 * px + 2 * py + pc) * m_per, m_per), :]

        def copy(k, blk, to, src=None):
            return pltpu.make_async_remote_copy(src_ref=rows(*blk) if src is None else src, dst_ref=rows(*blk),
                                                send_sem=send_sems.at[k], recv_sem=recv_sems.at[k], device_id=to,
                                                device_id_type=MESH)

        mine = pltpu.make_async_copy(x_ref, rows(*me), local_sem)
        mine.start()
        first = [copy(0, me, sibling, src=x_ref)]
        first += [copy(1 + j, me, (*chip, c), src=x_ref) for j, chip in enumerate(chips)]
        for cp in first:
            cp.start()
        passed = [copy(4 + j, (*chip, c), sibling) for j, chip in enumerate(chips)]
        for j, chip in enumerate(chips):
            copy(1 + j, (*chip, c), me).wait_recv()
            passed[j].start()
        copy(0, sibling, me).wait_recv()
        for j, chip in enumerate(chips):
            copy(4 + j, (*chip, 1 - c), me).wait_recv()
        for cp in first + passed:
            cp.wait_send()
        mine.wait()

    return pl.pallas_call(
        body, name="gather_all", out_shape=jax.ShapeDtypeStruct((N_DEV * m_per, n), block.dtype),
        in_specs=[pl.BlockSpec(memory_space=pltpu.VMEM)], out_specs=pl.BlockSpec(memory_space=pltpu.VMEM),
        scratch_shapes=[pltpu.SemaphoreType.DMA((7,)), pltpu.SemaphoreType.DMA((7,)), pltpu.SemaphoreType.DMA],
    )(block)


def _rope_tables(positions):
    inv_freq = 1.0 / (ROPE_THETA ** (jnp.arange(0, MLA_ROPE, 2, dtype=F32) / MLA_ROPE))
    ang = positions.astype(F32).reshape(-1, 1) * inv_freq
    return jnp.tile(jnp.cos(ang), (1, 4)), jnp.tile(jnp.sin(ang), (1, 4))


def _pad_cols(a, n):
    return jnp.pad(a, ((0, 0), (0, n - a.shape[1])))


def _ffn_fwd(l, h, w, nb, seq):
    ug = _mm(h, w["ffn_up_g"][l], name=f"ffn{l}_up_g")
    uv = _mm(h, w["ffn_up_v"][l], name=f"ffn{l}_up_v")
    op = _op_conv(f"ffn{l}_conv", _f_convffn, [ug, uv], [w["ffn_cw_g"][l], w["ffn_cw_v"][l]],
                  [w["ffn_cb_g"][l], w["ffn_cb_v"][l]], nb, seq)
    act = op.fwd((BF16,))[0]
    ff = _mm(act, w["ffn_down"][l], name=f"ffn{l}_down")
    return ff, (h, op, act)


def _ffn_bwd(l, saved, dff, w, grads, dh_skip):
    h, op, act = saved
    grads[f"ffn_down{l}"] = _mm(act, dff, ta=True, name=f"ffn{l}_down_dw")
    dact = _mm(dff, w["ffn_down"][l], tb=True, name=f"ffn{l}_down_dx")
    (dug, duv), (dwg, dwv, dbg, dbv) = op.bwd([dact], [BF16, BF16])
    grads[f"ffn_up_g{l}"] = _mm(h, dug, ta=True, name=f"ffn{l}_up_g_dw")
    grads[f"ffn_up_v{l}"] = _mm(h, duv, ta=True, name=f"ffn{l}_up_v_dw")
    grads[f"ffn_cw_g{l}"], grads[f"ffn_cw_v{l}"], grads[f"ffn_cb_g{l}"], grads[f"ffn_cb_v{l}"] = dwg, dwv, dbg, dbv
    dh = _mm(dug, w["ffn_up_g"][l], tb=True, add=dh_skip, name=f"ffn{l}_up_g_dx")
    return _mm(duv, w["ffn_up_v"][l], tb=True, add=dh, name=f"ffn{l}_up_v_dx")


class _Carried:
    def __init__(self, shards, finish_weights, ffn_shards, set_ffn, grad_stacks):
        self.shards, self.finish_weights, self.ffn_shards, self.set_ffn = shards, finish_weights, ffn_shards, set_ffn
        self.grad_stacks, self.received = grad_stacks, None


def _local_step(x, positions, target, w, carried=None):
    nb, seq, d = x.shape
    t = nb * seq
    x2, tgt2 = x.reshape(t, d), target.reshape(t, d)
    cos, sin = _rope_tables(positions)
    grads = {}

    xb = x2.astype(BF16)

    z = _mm(xb, w["in_z"], name="ssd_in_z")
    raw = [_mm(xb, w[k], name="ssd_" + k) for k in ("in_x", "in_b", "in_c")]
    dt_raw = _mm(xb, w["in_dt"], name="ssd_in_dt")
    conv_ops = [_op_conv("ssd_conv_" + s, _f_convsilu, [r], [w["conv_w_" + s]], [w["conv_b_" + s]], nb, seq)
                for s, r in zip("xbc", raw)]
    xs, bm, cm = [op.fwd()[0] for op in conv_ops]
    dt_op = _op_dt("ssd_dt", dt_raw, w["dt_bias"], w["a_log"])
    dt, cum, ecum, wend = dt_op.fwd()

    def to_rows(v):
        v = jnp.swapaxes(v.reshape(nb, seq, LANE), 1, 2)[:, :SSD_H, :].reshape(nb * SSD_G, SSD_K, seq)
        return jnp.pad(v, ((0, 0), (0, SSD_KPAD - SSD_K), (0, 0)))

    def from_rows(v):
        v = v[:, :SSD_K, :].reshape(nb, SSD_H, seq)
        return jnp.swapaxes(jnp.pad(v, ((0, 0), (0, LANE - SSD_H), (0, 0))), 1, 2).reshape(t, LANE)

    ssd_rows = jnp.concatenate([to_rows(cum), to_rows(dt), to_rows(ecum), to_rows(wend)], axis=1)
    y, states, *gathered = _ssd_scan_fwd(xs, bm, cm, ssd_rows, nb, seq, carried.shards if carried else ())
    if carried:
        w = {**w, **carried.finish_weights(gathered)}
    post_op = _op_ssd_post("ssd_post", y, xs, z, w["d_exp"], w["norm_g"])
    yn = post_op.fwd((BF16,))[0]
    mix0 = _mm(yn, w["ssd_out"], name="ssd_out")
    ln0m = _op_ln("ln_mix0", x2, mix0, w["ln_mix_g"][0], w["ln_mix_b"][0])
    h0a, h0a_b = ln0m.fwd((F32, BF16))
    ff0, ffn0_saved = _ffn_fwd(0, h0a_b, w, nb, seq)
    ln0f = _op_ln("ln_ffn0", h0a, ff0, w["ln_ffn_g"][0], w["ln_ffn_b"][0])
    h1, h1_b = ln0f.fwd((F32, BF16))

    ckv = _mm(h1_b, w["kv_down_c"], name="kv_down_c")
    kr_in = _mm(h1_b, w["kv_down_r"], name="kv_down_r")
    kvn_op = _op_rms("kv_norm", ckv, w["kv_norm_g"])
    ckvn = kvn_op.fwd((BF16,))[0]
    kr_op = _op_rope("k_rope", _f_rope_dup, kr_in, cos, sin)
    kr = kr_op.fwd((BF16,))[0]
    kn = _mm(ckvn, w["kv_up_k"], out_dtype=BF16, name="kv_up_k")
    v = _mm(ckvn, w["kv_up_v"], out_dtype=BF16, name="kv_up_v")
    cq_raw = _mm(h1_b, w["q_down"], name="q_down")
    qn_op = _op_rms("q_norm", cq_raw, w["q_norm_g"])
    cq = qn_op.fwd((BF16,))[0]
    qn = _mm(cq, w["q_up_n"], out_dtype=BF16, name="q_up_n")
    qr_raw = _mm(cq, w["q_up_r"], name="q_up_r")
    qr_op = _op_rope("q_rope", _f_rope, qr_raw, cos, sin)
    qr = qr_op.fwd((BF16,))[0]
    o, lse, *gathered = _attn_fwd(qn, qr, kn, kr, v, nb, seq, carried.ffn_shards if carried else ())
    if carried:
        carried.set_ffn(w, gathered)
    mix1 = _mm(o, w["attn_out"], name="attn_out")
    ln1m = _op_ln("ln_mix1", h1, mix1, w["ln_mix_g"][1], w["ln_mix_b"][1])
    h1a, h1a_b = ln1m.fwd((F32, BF16))
    ff1, ffn1_saved = _ffn_fwd(1, h1a_b, w, nb, seq)
    ln1f = _op_ln("ln_ffn1", h1a, ff1, w["ln_ffn_g"][1], w["ln_ffn_b"][1])
    h2 = ln1f.fwd()[0]

    loss_parts, dh2 = _loss_head(h2, tgt2)

    (dh1a, dff1), (grads["ln_ffn_g1"], grads["ln_ffn_b1"]) = ln1f.bwd([dh2], [F32, BF16])
    dh1a = _ffn_bwd(1, ffn1_saved, dff1, w, grads, dh1a)
    (dh1, dmix1), (grads["ln_mix_g1"], grads["ln_mix_b1"]) = ln1m.bwd([dh1a], [F32, BF16])
    grads["attn_out"] = _mm(o, dmix1, ta=True, name="attn_out_dw")
    do = _mm(dmix1, w["attn_out"], tb=True, name="attn_out_dx")
    dqn, dqr, dkn, dkr, dv = _attn_bwd(qn, qr, kn, kr, v, o, lse, do, nb, seq)
    (dqr_raw,), _ = qr_op.bwd([dqr], [BF16])
    grads["q_up_n"] = _mm(cq, dqn, ta=True, name="q_up_n_dw")
    grads["q_up_r"] = _mm(cq, dqr_raw, ta=True, name="q_up_r_dw")
    dcq = _mm(dqn, w["q_up_n"], tb=True, name="q_up_n_dx")
    dcq = _mm(dqr_raw, w["q_up_r"], tb=True, add=dcq, name="q_up_r_dx")
    (dcq_raw,), (grads["q_norm_g"],) = qn_op.bwd([dcq], [BF16])
    grads["q_down"] = _mm(h1_b, dcq_raw, ta=True, name="q_down_dw")
    dh1 = _mm(dcq_raw, w["q_down"], tb=True, add=dh1, name="q_down_dx")
    grads["kv_up_k"] = _mm(ckvn, dkn, ta=True, name="kv_up_k_dw")
    grads["kv_up_v"] = _mm(ckvn, dv, ta=True, name="kv_up_v_dw")
    dckvn = _mm(dkn, w["kv_up_k"], tb=True, name="kv_up_k_dx")
    dckvn = _mm(dv, w["kv_up_v"], tb=True, add=dckvn, name="kv_up_v_dx")
    (dckv,), (grads["kv_norm_g"],) = kvn_op.bwd([dckvn], [BF16])
    (dkr_in,), _ = kr_op.bwd([dkr], [BF16])
    grads["kv_down_c"] = _mm(h1_b, dckv, ta=True, name="kv_down_c_dw")
    grads["kv_down_r"] = _mm(h1_b, dkr_in, ta=True, name="kv_down_r_dw")
    dh1 = _mm(dckv, w["kv_down_c"], tb=True, add=dh1, name="kv_down_c_dx")
    dh1 = _mm(dkr_in, w["kv_down_r"], tb=True, add=dh1, name="kv_down_r_dx")

    (dh0a, dff0), (grads["ln_ffn_g0"], grads["ln_ffn_b0"]) = ln0f.bwd([dh1], [F32, BF16])
    dh0a = _ffn_bwd(0, ffn0_saved, dff0, w, grads, dh0a)
    (dx, dmix0), (grads["ln_mix_g0"], grads["ln_mix_b0"]) = ln0m.bwd([dh0a], [F32, BF16])
    grads["ssd_out"] = _mm(yn, dmix0, ta=True, name="ssd_out_dw")
    dyn = _mm(dmix0, w["ssd_out"], tb=True, name="ssd_out_dx")
    (dy, dxs_post, dz), (grads["d_exp"], grads["norm_g"]) = post_op.bwd([dyn], [F32, F32, BF16])
    dxs, dbm, dcm, drows, *received = _ssd_scan_bwd(xs, bm, cm, ssd_rows, states, dy, dxs_post, nb, seq,
                                                    carried.grad_stacks(grads) if carried else ())
    if carried:
        carried.received = received
    d_dt, d_cum, d_ecum, d_wend = [from_rows(drows[:, o:o + SSD_KPAD]) for o in (_DT, _CUM, _ECUM, _WEND)]
    (ddt_raw,), (grads["dt_bias"], grads["a_log"]) = dt_op.bwd([d_dt, d_cum, d_ecum, d_wend], [BF16])
    draws = []
    for s, op, dout in zip("xbc", conv_ops, (dxs, dbm, dcm)):
        (dr,), (grads["conv_w_" + s], grads["conv_b_" + s]) = op.bwd([dout], [BF16])
        draws.append(dr)
    for k, dr in zip(("in_x", "in_b", "in_c"), draws):
        grads[k] = _mm(xb, dr, ta=True, name=f"ssd_{k}_dw")
        dx = _mm(dr, w[k], tb=True, add=dx, name=f"ssd_{k}_dx")
    grads["in_z"] = _mm(xb, dz, ta=True, name="ssd_in_z_dw")
    grads["in_dt"] = _mm(xb, ddt_raw, ta=True, name="ssd_in_dt_dw")
    dx = _mm(dz, w["in_z"], tb=True, add=dx, name="ssd_in_z_dx")
    dx = _mm(ddt_raw, w["in_dt"], tb=True, add=dx, name="ssd_in_dt_dx")
    return loss_parts, dx.reshape(nb, seq, d), grads


_XE, _BE, _CE = SSD_D_INNER, SSD_D_INNER + SSD_GN, SSD_D_INNER + 2 * SSD_GN


def _prep_weights(fw):
    w = {}
    f = FFN_HIDDEN
    if "ssd_in_proj" in fw:
        ip = fw["ssd_in_proj"][0]
        o = SSD_D_INNER
        w["in_z"], w["in_x"], w["in_b"], w["in_c"] = ip[:, :o], ip[:, o:o + _XE], ip[:, o + _XE:o + _BE], ip[:, o + _BE:o + _CE]
        w["in_dt"] = _pad_cols(ip[:, o + _CE:], LANE)
        cw, cb = fw["ssd_conv_w"][0], fw["ssd_conv_b"]
        for s, (lo, hi) in zip("xbc", ((0, _XE), (_XE, _BE), (_BE, _CE))):
            w["conv_w_" + s], w["conv_b_" + s] = cw[:, lo:hi], cb[:, lo:hi]
    if "ssd_dt_bias" in fw:
        w["dt_bias"], w["a_log"] = _pad_cols(fw["ssd_dt_bias"], LANE), _pad_cols(fw["ssd_A_log"], LANE)
        w["d_exp"] = jnp.repeat(fw["ssd_D"][0], SSD_P)[None, :]
        w["kv_norm_g"], w["q_norm_g"] = fw["kv_norm_g"][None, :], fw["q_norm_g"]
        w["ffn_cb_g"] = [fw["ffn_conv_b"][l:l + 1, :f] for l in range(DEPTH)]
        w["ffn_cb_v"] = [fw["ffn_conv_b"][l:l + 1, f:] for l in range(DEPTH)]
        for k in ("ln_mix_g", "ln_mix_b", "ln_ffn_g", "ln_ffn_b"):
            w[k] = [fw[k][l:l + 1] for l in range(DEPTH)]
    if "ssd_out_proj" in fw:
        w["norm_g"], w["ssd_out"] = fw["ssd_norm_g"], fw["ssd_out_proj"][0]
        kd = fw["kv_down_proj"]
        w["kv_down_c"], w["kv_down_r"] = kd[:, :MLA_KV_RANK], _pad_cols(kd[:, MLA_KV_RANK:], LANE)
        w["kv_up_k"], w["kv_up_v"] = fw["kv_up_k"], fw["kv_up_v"]
        w["q_down"] = fw["q_down_proj"][0]
        qu = fw["q_up_proj"][0].reshape(MLA_Q_RANK, MLA_H, MLA_NOPE + MLA_ROPE)
        w["q_up_n"] = qu[:, :, :MLA_NOPE].reshape(MLA_Q_RANK, MLA_H * MLA_NOPE)
        w["q_up_r"] = qu[:, :, MLA_NOPE:].reshape(MLA_Q_RANK, MLA_H * MLA_ROPE)
        w["attn_out"] = fw["attn_out_proj"][0]
        w["ffn_up_g"], w["ffn_up_v"], w["ffn_down"] = [None] * DEPTH, [None] * DEPTH, [None] * DEPTH
        for l in range(DEPTH):
            if fw["ffn_up"][l] is not None:
                _set_ffn_weights(w, l, fw["ffn_up"][l], fw["ffn_down"][l])
        w["ffn_cw_g"] = [fw["ffn_conv_w"][l][:, :f] for l in range(DEPTH)]
        w["ffn_cw_v"] = [fw["ffn_conv_w"][l][:, f:] for l in range(DEPTH)]
    return w


def _set_ffn_weights(w, l, up, down):
    w["ffn_up_g"][l], w["ffn_up_v"][l], w["ffn_down"][l] = up[:, :FFN_HIDDEN], up[:, FFN_HIDDEN:], down


def _assemble_grads(g, names):
    make = {
        "ssd_in_proj": lambda: jnp.concatenate([g["in_z"], g["in_x"], g["in_b"], g["in_c"], g["in_dt"][:, :SSD_H]], axis=1)[None],
        "ssd_conv_w": lambda: jnp.concatenate([g["conv_w_" + s] for s in "xbc"], axis=1)[None],
        "ssd_conv_b": lambda: jnp.concatenate([g["conv_b_" + s] for s in "xbc"], axis=1),
        "ssd_dt_bias": lambda: g["dt_bias"][:, :SSD_H],
        "ssd_A_log": lambda: g["a_log"][:, :SSD_H],
        "ssd_D": lambda: jnp.sum(g["d_exp"].reshape(SSD_H, SSD_P), axis=1)[None, :],
        "ssd_norm_g": lambda: g["norm_g"],
        "ssd_out_proj": lambda: g["ssd_out"][None],
        "kv_down_proj": lambda: jnp.concatenate([g["kv_down_c"], g["kv_down_r"][:, :MLA_ROPE]], axis=1),
        "kv_norm_g": lambda: g["kv_norm_g"][0],
        "kv_up_k": lambda: g["kv_up_k"],
        "kv_up_v": lambda: g["kv_up_v"],
        "q_down_proj": lambda: g["q_down"][None],
        "q_norm_g": lambda: g["q_norm_g"],
        "q_up_proj": lambda: jnp.concatenate([g["q_up_n"].reshape(MLA_Q_RANK, MLA_H, MLA_NOPE),
                                              g["q_up_r"].reshape(MLA_Q_RANK, MLA_H, MLA_ROPE)], axis=2).reshape(1, MLA_Q_RANK, -1),
        "attn_out_proj": lambda: g["attn_out"][None],
        "ffn_up": lambda: jnp.stack([jnp.concatenate([g[f"ffn_up_g{l}"], g[f"ffn_up_v{l}"]], axis=1) for l in range(DEPTH)]),
        "ffn_conv_w": lambda: jnp.stack([jnp.concatenate([g[f"ffn_cw_g{l}"], g[f"ffn_cw_v{l}"]], axis=1) for l in range(DEPTH)]),
        "ffn_conv_b": lambda: jnp.concatenate([jnp.concatenate([g[f"ffn_cb_g{l}"], g[f"ffn_cb_v{l}"]], axis=1)
                                               for l in range(DEPTH)], axis=0),
        "ffn_down": lambda: jnp.stack([g[f"ffn_down{l}"] for l in range(DEPTH)]),
    }
    for k in ("ln_mix_g", "ln_mix_b", "ln_ffn_g", "ln_ffn_b"):
        make[k] = lambda k=k: jnp.concatenate([g[f"{k}{l}"] for l in range(DEPTH)], axis=0)
    return {n: make[n]() for n in names}


_WEIGHTS = ["ssd_in_proj", "ssd_conv_w", "ssd_conv_b", "ssd_dt_bias", "ssd_A_log", "ssd_D", "ssd_norm_g", "ssd_out_proj",
            "kv_down_proj", "kv_norm_g", "kv_up_k", "kv_up_v", "q_down_proj", "q_norm_g", "q_up_proj", "attn_out_proj",
            "ffn_up", "ffn_conv_w", "ffn_conv_b", "ffn_down", "ln_mix_g", "ln_mix_b", "ln_ffn_g", "ln_ffn_b"]
_COL_CUT = ["ssd_in_proj", "ssd_conv_w", "ssd_conv_b", "ssd_norm_g", "kv_up_k", "kv_up_v", "q_up_proj", "ffn_up", "ffn_conv_w"]
_ROW_CUT = ["ssd_out_proj", "kv_down_proj", "q_down_proj", "attn_out_proj", "ffn_down"]
_CUT = _COL_CUT + _ROW_CUT
_WHOLE = [n for n in _WEIGHTS if n not in _CUT]
_EARLY = ["ssd_in_proj", "ssd_conv_w", "ssd_conv_b"]
_LATE = [n for n in _CUT if n not in _EARLY]
_FFN_MATRICES = ["ffn_up", "ffn_down"]
_MXU_WEIGHTS = ["ssd_in_proj", "ssd_out_proj", "kv_down_proj", "kv_up_k", "kv_up_v", "q_down_proj", "q_up_proj",
                "attn_out_proj", "ffn_up", "ffn_down"]
_PACK_ROWS = 160


def _shard_2d(name, s):
    return s.reshape(-1, s.shape[-1])


def _unstack(name, g, shard_shape):
    if name in _COL_CUT:
        lead = shard_shape[:-1]
        return jnp.swapaxes(g, 0, 1).reshape(*lead, N_CHIPS * shard_shape[-1])
    lead, rs, c = shard_shape[:-2], shard_shape[-2], shard_shape[-1]
    n_lead = math.prod(lead)
    return jnp.swapaxes(g.reshape(N_CHIPS, n_lead, rs, c), 0, 1).reshape(*lead, N_CHIPS * rs, c)


def _stack(name, full, shard_shape):
    if name in _COL_CUT:
        cs = shard_shape[-1]
        return jnp.swapaxes(full.reshape(-1, N_CHIPS, cs), 0, 1)
    lead, rs, c = shard_shape[:-2], shard_shape[-2], shard_shape[-1]
    n_lead = math.prod(lead)
    return jnp.swapaxes(full.reshape(n_lead, N_CHIPS, rs, c), 0, 1).reshape(N_CHIPS, n_lead * rs, c)


def _pack(arrs):
    flat = jnp.concatenate([a.reshape(-1) for a in arrs])
    return jnp.pad(flat, (0, _PACK_ROWS * LANE - flat.shape[0])).reshape(_PACK_ROWS, LANE)


def _unpack(packed, like):
    flat, out, o = packed.reshape(-1), [], 0
    for a in like:
        out.append(flat[o:o + a.size].reshape(a.shape))
        o += a.size
    return out


_ARGS = ["x", "positions"] + _WEIGHTS + ["loss_target"] + ["m_" + n for n in _WEIGHTS] + ["v_" + n for n in _WEIGHTS]


def kernel(x, positions, ssd_in_proj, ssd_conv_w, ssd_conv_b, ssd_dt_bias, ssd_A_log, ssd_D, ssd_norm_g,
           ssd_out_proj, kv_down_proj, kv_norm_g, kv_up_k, kv_up_v, q_down_proj, q_norm_g, q_up_proj,
           attn_out_proj, ffn_up, ffn_conv_w, ffn_conv_b, ffn_down, ln_mix_g, ln_mix_b, ln_ffn_g, ln_ffn_b,
           loss_target, m_ssd_in_proj, m_ssd_conv_w, m_ssd_conv_b, m_ssd_dt_bias, m_ssd_A_log, m_ssd_D,
           m_ssd_norm_g, m_ssd_out_proj, m_kv_down_proj, m_kv_norm_g, m_kv_up_k, m_kv_up_v, m_q_down_proj,
           m_q_norm_g, m_q_up_proj, m_attn_out_proj, m_ffn_up, m_ffn_conv_w, m_ffn_conv_b, m_ffn_down,
           m_ln_mix_g, m_ln_mix_b, m_ln_ffn_g, m_ln_ffn_b, v_ssd_in_proj, v_ssd_conv_w, v_ssd_conv_b,
           v_ssd_dt_bias, v_ssd_A_log, v_ssd_D, v_ssd_norm_g, v_ssd_out_proj, v_kv_down_proj, v_kv_norm_g,
           v_kv_up_k, v_kv_up_v, v_q_down_proj, v_q_norm_g, v_q_up_proj, v_attn_out_proj, v_ffn_up,
           v_ffn_conv_w, v_ffn_conv_b, v_ffn_down, v_ln_mix_g, v_ln_mix_b, v_ln_ffn_g, v_ln_ffn_b):
    args = (x, positions, ssd_in_proj, ssd_conv_w, ssd_conv_b, ssd_dt_bias, ssd_A_log, ssd_D, ssd_norm_g,
            ssd_out_proj, kv_down_proj, kv_norm_g, kv_up_k, kv_up_v, q_down_proj, q_norm_g, q_up_proj,
            attn_out_proj, ffn_up, ffn_conv_w, ffn_conv_b, ffn_down, ln_mix_g, ln_mix_b, ln_ffn_g, ln_ffn_b,
            loss_target, m_ssd_in_proj, m_ssd_conv_w, m_ssd_conv_b, m_ssd_dt_bias, m_ssd_A_log, m_ssd_D,
            m_ssd_norm_g, m_ssd_out_proj, m_kv_down_proj, m_kv_norm_g, m_kv_up_k, m_kv_up_v, m_q_down_proj,
            m_q_norm_g, m_q_up_proj, m_attn_out_proj, m_ffn_up, m_ffn_conv_w, m_ffn_conv_b, m_ffn_down,
            m_ln_mix_g, m_ln_mix_b, m_ln_ffn_g, m_ln_ffn_b, v_ssd_in_proj, v_ssd_conv_w, v_ssd_conv_b,
            v_ssd_dt_bias, v_ssd_A_log, v_ssd_D, v_ssd_norm_g, v_ssd_out_proj, v_kv_down_proj, v_kv_norm_g,
            v_kv_up_k, v_kv_up_v, v_q_down_proj, v_q_norm_g, v_q_up_proj, v_attn_out_proj, v_ffn_up,
            v_ffn_conv_w, v_ffn_conv_b, v_ffn_down, v_ln_mix_g, v_ln_mix_b, v_ln_ffn_g, v_ln_ffn_b)
    a = dict(zip(_ARGS, args, strict=True))

    last = DEPTH - 1

    def layers(n, late):
        if n not in _FFN_MATRICES:
            return a[n]
        return a[n][last:] if late else a[n][:last]

    def shard(n, late=False):
        s = _shard_2d(n, layers(n, late))
        return s.astype(BF16) if n in _MXU_WEIGHTS else s

    def full_weights(names, gathered, late=False):
        fw = {n: _unstack(n, g, layers(n, late).shape) for n, g in zip(names, gathered)}
        for n in _FFN_MATRICES:
            if n in fw:
                fw[n] = [None] * last + list(fw[n]) if late else list(fw[n]) + [None]
        return fw

    def set_last_ffn(w, gathered):
        fw = full_weights(_FFN_MATRICES, gathered, late=True)
        _set_ffn_weights(w, last, fw["ffn_up"][last], fw["ffn_down"][last])

    def grad_stacks(names, pieces):
        full = _assemble_grads(pieces, names)
        return [_stack(n, full[n], a[n].shape).astype(BF16) for n in names]

    fw = full_weights(_EARLY, _gather_chips([shard(n) for n in _EARLY]))
    fw.update({n: a[n] for n in _WHOLE})
    carried = _Carried([shard(n) for n in _LATE], lambda got: _prep_weights(full_weights(_LATE, got)),
                       [shard(n, late=True) for n in _FFN_MATRICES], set_last_ffn,
                       lambda pieces: grad_stacks(_LATE, pieces))

    loss_parts, grad_x, pieces = _local_step(a["x"], a["positions"], a["loss_target"], _prep_weights(fw), carried)
    loss = lax.psum(jnp.sum(loss_parts[::8, 0]), ("x", "y", "c"))

    bufs = dict(zip(_LATE, carried.received))
    bufs.update(zip(_EARLY, _scatter_chips(grad_stacks(_EARLY, pieces))))
    sums = [_sum_parts(bufs[n], "sum_chips_" + n) for n in _CUT]
    others = _swap_cores(sums)
    full = _assemble_grads(pieces, _WHOLE)
    every = _gather_all(_pack([full[n] for n in _WHOLE]))
    g_whole = _sum_parts(every.reshape(N_DEV, _PACK_ROWS, LANE), "sum_devices")

    res = {}
    for n, s, o in zip(_CUT, sums, others):
        out = _adamw(_shard_2d(n, a[n]), s, o, _shard_2d(n, a["m_" + n]), _shard_2d(n, a["v_" + n]), "adamw_" + n)
        res[n] = [r.reshape(a[n].shape) for r in out]
    whole = [a[n] for n in _WHOLE]
    out = _adamw(_pack(whole), g_whole, None, _pack([a["m_" + n] for n in _WHOLE]), _pack([a["v_" + n] for n in _WHOLE]),
                 "adamw_whole")
    for k, n in enumerate(_WHOLE):
        res[n] = [_unpack(r, whole)[k] for r in out]
    return (loss, grad_x, *[res[n][0] for n in _WEIGHTS], *[res[n][1] for n in _WEIGHTS],
            *[res[n][2] for n in _WEIGHTS], *[res[n][3] for n in _WEIGHTS])
```

```python
import functools
import math

import jax
import jax.numpy as jnp
from jax import lax
from jax.experimental import pallas as pl
from jax.experimental.pallas import tpu as pltpu

F32 = jnp.float32
BF16 = jnp.bfloat16
HIGHEST = lax.Precision.HIGHEST
MESH = pl.DeviceIdType.MESH

D_MODEL = 1024
DEPTH = 2
DN_ALPHA = (2 * DEPTH) ** 0.25
SSD_D_INNER = 2048
SSD_P = 64
SSD_H = 32
SSD_G = 8
SSD_K = 4
SSD_N = 128
SSD_L = 128
SSD_GN = SSD_G * SSD_N
MLA_H = 8
MLA_Q_RANK = 384
MLA_KV_RANK = 256
MLA_NOPE = 128
MLA_ROPE = 64
MLA_V = 128
ROPE_THETA = 10000.0
FFN_HIDDEN = 2816
LN_EPS = 1e-5
RMS_EPS = 1e-6
ADAM_LR = 0.001
ADAM_B1 = 0.9
ADAM_B2 = 0.999
ADAM_EPS = 1e-08
ADAM_WD = 0.01
ADAM_STEP = 10

N_CHIPS = 4
N_DEV = 8
LANE = 128
VMEM_LIMIT = 56 * 1024 * 1024


def _cparams(**kw):
    return pltpu.CompilerParams(vmem_limit_bytes=VMEM_LIMIT, **kw)


def _tile(dim, cap):
    best = None
    t = LANE
    while t <= min(dim, cap):
        if dim % t == 0:
            best = t
        t += LANE
    return dim if best is None else best


MM_TILE_CAP = 1408
MM_WHOLE_K = 2816
MM_VMEM_BUDGET = 40 * 1024 * 1024


def _mm_tiles(m, n, k, a_bytes, b_bytes, o_bytes, has_add):
    tm, tn = _tile(m, MM_TILE_CAP), _tile(n, MM_TILE_CAP)
    tk = k if k <= MM_WHOLE_K else _tile(k, 1024)

    def need(tm, tn):
        acc = tm * tn * 4 if tk < k else 0
        return 2 * (tm * tk * a_bytes + tk * tn * b_bytes + tm * tn * o_bytes + (tm * tn * 4 if has_add else 0)) + acc

    while need(tm, tn) > MM_VMEM_BUDGET:
        if tm >= tn and _tile(m, tm // 2) < tm:
            tm = _tile(m, tm // 2)
        elif _tile(n, tn // 2) < tn:
            tn = _tile(n, tn // 2)
        else:
            break
    return tm, tn, tk


def _mm(a, b, *, ta=False, tb=False, add=None, out_dtype=F32, name):
    m, k = (a.shape[1], a.shape[0]) if ta else a.shape
    n = b.shape[0] if tb else b.shape[1]
    assert (b.shape[1] if tb else b.shape[0]) == k
    tm, tn, tk = _mm_tiles(m, n, k, a.dtype.itemsize, b.dtype.itemsize, jnp.dtype(out_dtype).itemsize, add is not None)
    nk = k // tk
    dims = (((0 if ta else 1,), (1 if tb else 0,)), ((), ()))

    def partial_product(a_ref, b_ref):
        return lax.dot_general(a_ref[...].astype(BF16), b_ref[...].astype(BF16), dims, preferred_element_type=F32)

    def body_one(*refs):
        if add is None:
            a_ref, b_ref, o_ref = refs
            o_ref[...] = partial_product(a_ref, b_ref).astype(out_dtype)
        else:
            a_ref, b_ref, c_ref, o_ref = refs
            o_ref[...] = (partial_product(a_ref, b_ref) + c_ref[...]).astype(out_dtype)

    def body_acc(*refs):
        if add is None:
            a_ref, b_ref, o_ref, acc = refs
        else:
            a_ref, b_ref, c_ref, o_ref, acc = refs
        kk = pl.program_id(2)

        @pl.when(kk == 0)
        def _():
            acc[...] = jnp.zeros_like(acc) if add is None else c_ref[...]

        acc[...] += partial_product(a_ref, b_ref)

        @pl.when(kk == nk - 1)
        def _():
            o_ref[...] = acc[...].astype(out_dtype)

    a_spec = pl.BlockSpec((tk, tm), lambda i, j, kk: (kk, i)) if ta else pl.BlockSpec((tm, tk), lambda i, j, kk: (i, kk))
    b_spec = pl.BlockSpec((tn, tk), lambda i, j, kk: (j, kk)) if tb else pl.BlockSpec((tk, tn), lambda i, j, kk: (kk, j))
    o_spec = pl.BlockSpec((tm, tn), lambda i, j, kk: (i, j))
    ins, specs = [a, b], [a_spec, b_spec]
    if add is not None:
        ins.append(add)
        specs.append(o_spec)
    return pl.pallas_call(
        body_one if nk == 1 else body_acc, name=name, grid=(m // tm, n // tn, nk), in_specs=specs, out_specs=o_spec,
        out_shape=jax.ShapeDtypeStruct((m, n), out_dtype),
        scratch_shapes=[] if nk == 1 else [pltpu.VMEM((tm, tn), F32)],
        compiler_params=_cparams(dimension_semantics=("parallel", "parallel", "arbitrary")),
    )(*ins)


def _spec(op):
    return pl.BlockSpec(op[1], op[2])


def _bw_fwd(name, fn, grid, ins, outs, out_dtypes):
    n_in = len(ins)
    flat = [(o, dt) for o, dts in zip(outs, out_dtypes) for dt in dts]

    def body(*refs):
        res = fn(*[r[...].astype(F32) for r in refs[:n_in]])
        orefs = iter(refs[n_in:])
        for v, dts in zip(res, out_dtypes):
            for dt in dts:
                next(orefs)[...] = v.astype(dt)

    return pl.pallas_call(
        body, name=name, grid=grid, in_specs=[_spec(o) for o in ins],
        out_specs=[pl.BlockSpec(o[1], o[2]) for o, _ in flat],
        out_shape=[jax.ShapeDtypeStruct(o[0], dt) for o, dt in flat], compiler_params=_cparams(),
    )(*[o[0] for o in ins])


def _bw_bwd(name, fn, grid, data, params, consts, cts, red_axes, grad_dtypes):
    nd, npar, nc, nct = len(data), len(params), len(consts), len(cts)

    def body(*refs):
        first = None
        for ax in red_axes:
            z = pl.program_id(ax) == 0
            first = z if first is None else jnp.logical_and(first, z)
        vals = [r[...].astype(F32) for r in refs[:nd + npar + nc + nct]]
        d, p, c, g = vals[:nd], vals[nd:nd + npar], vals[nd + npar:nd + npar + nc], vals[nd + npar + nc:]
        _, vjp = jax.vjp(lambda dd, pp: tuple(fn(*dd, *pp, *c)), d, p)
        gd, gp = vjp(tuple(g))
        orefs = refs[nd + npar + nc + nct:]
        for r, v in zip(orefs[:nd], gd):
            r[...] = v.astype(r.dtype)
        if npar:
            @pl.when(first)
            def _():
                for r in orefs[nd:]:
                    r[...] = jnp.zeros_like(r)

            for r, v in zip(orefs[nd:], gp):
                r[...] += v

    ins = [d[:3] for d in data] + list(params) + list(consts) + list(cts)
    outs = [d[3] if len(d) > 3 else d for d in data] + list(params)
    dtypes = list(grad_dtypes) + [F32] * npar
    return pl.pallas_call(
        body, name=name, grid=grid, in_specs=[_spec(o) for o in ins], out_specs=[_spec(o) for o in outs],
        out_shape=[jax.ShapeDtypeStruct(o[0].shape, dt) for o, dt in zip(outs, dtypes)], compiler_params=_cparams(),
    )(*[o[0] for o in ins])


def _shift_down(x, s):
    row = lax.broadcasted_iota(jnp.int32, x.shape, 0)
    return jnp.where(row < s, 0.0, pltpu.roll(x, s, 0))


def _shift_up(x, s):
    n = x.shape[0]
    row = lax.broadcasted_iota(jnp.int32, x.shape, 0)
    return jnp.where(row >= n - s, 0.0, pltpu.roll(x, n - s, 0))


def _time_shift(s):
    if s == 0:
        return lambda x: x

    @jax.custom_vjp
    def shift(x):
        return _shift_down(x, s)

    shift.defvjp(lambda x: (_shift_down(x, s), None), lambda _, g: (_shift_up(g, s),))
    return shift


def _rot_half_raw(x):
    lane = lax.broadcasted_iota(jnp.int32, x.shape, 1)
    return jnp.where(lane % MLA_ROPE < MLA_ROPE // 2, -pltpu.roll(x, LANE - MLA_ROPE // 2, 1), pltpu.roll(x, MLA_ROPE // 2, 1))


@jax.custom_vjp
def _rot_half(x):
    return _rot_half_raw(x)


_rot_half.defvjp(lambda x: (_rot_half_raw(x), None), lambda _, g: (-_rot_half_raw(g),))


@jax.custom_vjp
def _roll_half_lanes(x):
    return pltpu.roll(x, LANE // 2, 1)


_roll_half_lanes.defvjp(lambda x: (pltpu.roll(x, LANE // 2, 1), None), lambda _, g: (pltpu.roll(g, LANE // 2, 1),))


def _causal_conv(u, w, b):
    width = w.shape[0]
    y = b
    for k in range(width):
        y = y + w[k:k + 1, :] * _time_shift(width - 1 - k)(u)
    return y


def _silu(x):
    return x * jax.nn.sigmoid(x)


def _f_ln(h, mix, g, b):
    x = DN_ALPHA * h + mix
    mu = jnp.mean(x, axis=-1, keepdims=True)
    xc = x - mu
    var = jnp.mean(xc * xc, axis=-1, keepdims=True)
    return (xc * lax.rsqrt(var + LN_EPS) * g + b,)


def _f_convsilu(u, w, b):
    return (_silu(_causal_conv(u, w, b)),)


def _f_convffn(ug, uv, wg, wv, bg, bv):
    return (_silu(_causal_conv(ug, wg, bg)) * _causal_conv(uv, wv, bv),)


def _f_dt(dt_raw, bias, a_log):
    x = dt_raw + bias
    dt = jnp.maximum(x, 0.0) + jnp.log(1.0 + jnp.exp(-jnp.abs(x)))
    a = dt * (-jnp.exp(a_log))
    n = a.shape[0]
    lower = (lax.broadcasted_iota(jnp.int32, (n, n), 0) >= lax.broadcasted_iota(jnp.int32, (n, n), 1)).astype(F32)
    cum = jnp.dot(lower, a, precision=HIGHEST, preferred_element_type=F32)
    cum_last = jnp.sum(a, axis=0, keepdims=True)
    return dt, cum, jnp.exp(cum), jnp.exp(cum_last - cum) * dt


def _f_ssd_post(y, xs, z, d_exp, ng):
    t = (y + d_exp * xs) * _silu(z)
    return (t * lax.rsqrt(jnp.mean(t * t, axis=-1, keepdims=True) + LN_EPS) * ng,)


def _f_rms(x, g):
    return (x * lax.rsqrt(jnp.mean(x * x, axis=-1, keepdims=True) + RMS_EPS) * g,)


def _f_rope(x, cos, sin):
    return (x * cos + _rot_half(x) * sin,)


def _f_rope_dup(x, cos, sin):
    r = x * cos + _rot_half(x) * sin
    return (r + _roll_half_lanes(r),)


def _rows(t, cap=512):
    for c in (cap, 256, 128, 64, 32, 16, 8):
        if c <= cap and t % c == 0:
            return c
    return t


class _Blockwise:
    def __init__(self, name, fn, grid, data, params, consts, outs, red_axes):
        self.name, self.fn, self.grid = name, fn, grid
        self.data, self.params, self.consts, self.outs, self.red_axes = data, params, consts, outs, red_axes

    def fwd(self, *out_dtypes):
        out_dtypes = out_dtypes or tuple((F32,) for _ in self.outs)
        return _bw_fwd(self.name + "_fwd", self.fn, self.grid, self.data + self.params + self.consts, self.outs, out_dtypes)

    def bwd(self, cts, grad_dtypes=None):
        cts = [(c, o[1], o[2]) for c, o in zip(cts, self.outs)]
        grad_dtypes = grad_dtypes or [F32] * len(self.data)
        res = _bw_bwd(self.name + "_bwd", self.fn, self.grid, self.data, self.params, self.consts, cts, self.red_axes,
                      grad_dtypes)
        return res[:len(self.data)], res[len(self.data):]


def _op_ln(name, h, mix, g, b):
    t, d = h.shape
    tr = _rows(t, 256)
    row = ((tr, d), lambda i: (i, 0))
    par = ((1, d), lambda i: (0, 0))
    return _Blockwise(name, _f_ln, (t // tr,), [(h, *row), (mix, *row)], [(g, *par), (b, *par)], [],
                      [((t, d), *row)], (0,))


def _op_conv(name, fn, us, ws, bs, nb, seq):
    c = us[0].shape[1]
    ct = _tile(c, 256)
    blk = ((seq, ct), lambda j, bb: (bb, j))
    data = [(u, *blk) for u in us]
    params = [(w, (w.shape[0], ct), lambda j, bb: (0, j)) for w in ws] + [(b, (1, ct), lambda j, bb: (0, j)) for b in bs]
    return _Blockwise(name, fn, (c // ct, nb), data, params, [], [((nb * seq, c), *blk)], (1,))


def _op_dt(name, dt_raw, bias, a_log):
    t = dt_raw.shape[0]
    row = ((SSD_L, LANE), lambda i: (i, 0))
    par = ((1, LANE), lambda i: (0, 0))
    return _Blockwise(name, _f_dt, (t // SSD_L,), [(dt_raw, *row)], [(bias, *par), (a_log, *par)], [],
                      [((t, LANE), *row)] * 4, (0,))


def _op_ssd_post(name, y, xs, z, d_exp, ng):
    t, c = y.shape
    gw = c // SSD_G
    tr = _rows(t, 512)
    blk = ((tr, gw), lambda g, i: (i, g))
    par = ((1, gw), lambda g, i: (0, g))
    return _Blockwise(name, _f_ssd_post, (SSD_G, t // tr), [(y, *blk), (xs, *blk), (z, *blk)],
                      [(d_exp, *par), (ng, *par)], [], [((t, c), *blk)], (1,))


def _op_rms(name, x, g, window=None):
    t = x.shape[0]
    j, c = window or (0, x.shape[1])
    tr = _rows(t, 512)
    own = ((tr, c), lambda i: (i, 0))
    return _Blockwise(name, _f_rms, (t // tr,), [(x, (tr, c), lambda i: (i, j), (jax.ShapeDtypeStruct((t, c), F32), *own))],
                      [(g, (1, c), lambda i: (0, 0))], [], [((t, c), *own)], (0,))


def _op_rope(name, fn, x, cos, sin, window=None):
    t = x.shape[0]
    j0, n = window or (0, x.shape[1] // LANE)
    tr = _rows(t, 512)
    own = ((tr, LANE), lambda i, j: (i, j))
    cs = ((tr, LANE), lambda i, j: (i, 0))
    return _Blockwise(name, fn, (t // tr, n), [(x, (tr, LANE), lambda i, j: (i, j0 + j), (jax.ShapeDtypeStruct((t, n * LANE), F32), *own))],
                      [], [(cos, *cs), (sin, *cs)], [((t, n * LANE), *own)], ())


_NT = (((1,), (1,)), ((), ()))
_TN = (((0,), (0,)), ((), ()))


def _ssd_head(x, g, bc, cc, s, cum_row, dt_row, cum_col, ecum_col, wend_col):
    n = g.shape[0]
    row = lax.broadcasted_iota(jnp.int32, (n, n), 0)
    col = lax.broadcasted_iota(jnp.int32, (n, n), 1)
    decay = jnp.exp(jnp.where(row >= col, cum_col - cum_row, -jnp.inf))
    w = g * decay * dt_row
    y = jnp.dot(w.astype(BF16), x.astype(BF16), preferred_element_type=F32)
    y = y + lax.dot_general((cc * ecum_col).astype(BF16), s.astype(BF16), _NT, preferred_element_type=F32)
    lane = lax.broadcasted_iota(jnp.int32, cum_row.shape, 1)
    cum_last = jnp.sum(jnp.where(lane == n - 1, cum_row, 0.0), axis=1, keepdims=True)
    s_new = s * jnp.exp(cum_last) + lax.dot_general(x.astype(BF16), (bc * wend_col).astype(BF16), _TN,
                                                    preferred_element_type=F32)
    return y, s_new


SSD_KPAD = 8
SSD_ROWS = 4 * SSD_KPAD
_CUM, _DT, _ECUM, _WEND = 0, SSD_KPAD, 2 * SSD_KPAD, 3 * SSD_KPAD


def _ssd_head_args(rows, cols, k):
    return (rows[_CUM + k:_CUM + k + 1], rows[_DT + k:_DT + k + 1], cols[:, _CUM + k:_CUM + k + 1],
            cols[:, _ECUM + k:_ECUM + k + 1], cols[:, _WEND + k:_WEND + k + 1])


def _carried_exchange(arrs, in_refs, out_refs, sems, scatter, n_steps):
    if not arrs:
        return lambda: None, lambda: None
    first = functools.reduce(jnp.logical_and, [pl.program_id(ax) == 0 for ax in range(len(n_steps))])
    last = functools.reduce(jnp.logical_and, [pl.program_id(ax) == n - 1 for ax, n in enumerate(n_steps)])
    start, wait = _chip_exchange(in_refs, out_refs, sems, scatter)
    return (lambda: pl.when(first)(start)), (lambda: pl.when(last)(wait))


def _ssd_scan_fwd(xs, bm, cm, rows, nb, seq, gather=()):
    nc = seq // SSD_L
    kp = SSD_K * SSD_P
    ng = len(gather)

    def body(*refs):
        xs_ref, b_ref, c_ref, row_ref = refs[:4]
        y_ref, st_ref = refs[4 + ng:6 + ng]
        start, wait = _carried_exchange(gather, refs[4:4 + ng], refs[6 + ng:6 + 2 * ng], refs[6 + 2 * ng:], False,
                                        (nb, SSD_G))
        start()

        def chunk(c, states):
            sl = pl.ds(pl.multiple_of(c * SSD_L, SSD_L), SSD_L)
            bc, cc = b_ref[sl, :], c_ref[sl, :]
            g = lax.dot_general(cc.astype(BF16), bc.astype(BF16), _NT, preferred_element_type=F32)
            rows_c = row_ref[0, :, sl]
            cols_c = rows_c.T
            new = []
            for k in range(SSD_K):
                hs = pl.ds(k * SSD_P, SSD_P)
                st_ref[0, c * SSD_K + k] = states[k]
                y, s_new = _ssd_head(xs_ref[sl, hs], g, bc, cc, states[k], *_ssd_head_args(rows_c, cols_c, k))
                y_ref[sl, hs] = y
                new.append(s_new)
            return tuple(new)

        lax.fori_loop(0, nc, chunk, tuple(jnp.zeros((SSD_P, SSD_N), F32) for _ in range(SSD_K)))
        wait()

    t = xs.shape[0]
    any_spec = pl.BlockSpec(memory_space=pl.ANY)
    return pl.pallas_call(
        body, name="ssd_scan_fwd", grid=(nb, SSD_G),
        in_specs=[pl.BlockSpec((seq, kp), lambda b, g: (b, g)),
                  pl.BlockSpec((seq, SSD_N), lambda b, g: (b, g)),
                  pl.BlockSpec((seq, SSD_N), lambda b, g: (b, g)),
                  pl.BlockSpec((1, SSD_ROWS, seq), lambda b, g: (b * SSD_G + g, 0, 0))] + [any_spec] * ng,
        out_specs=[pl.BlockSpec((seq, kp), lambda b, g: (b, g)),
                   pl.BlockSpec((1, nc * SSD_K, SSD_P, SSD_N), lambda b, g: (b * SSD_G + g, 0, 0, 0))] + [any_spec] * ng,
        out_shape=[jax.ShapeDtypeStruct((t, SSD_D_INNER), F32),
                   jax.ShapeDtypeStruct((nb * SSD_G, nc * SSD_K, SSD_P, SSD_N), F32)] + _exchange_shapes(gather, False),
        scratch_shapes=_exchange_sems(ng) if ng else [],
        compiler_params=_cparams(dimension_semantics=("arbitrary", "arbitrary")),
    )(xs, bm, cm, rows, *gather)


def _ssd_scan_bwd(xs, bm, cm, rows, states, dy, dxs_skip, nb, seq, scatter=()):
    nc = seq // SSD_L
    kp = SSD_K * SSD_P
    ns = len(scatter)

    def body(*refs):
        xs_ref, b_ref, c_ref, row_ref, st_ref, dy_ref, skip_ref = refs[:7]
        dxs_ref, db_ref, dc_ref, drow_ref = refs[7 + ns:11 + ns]
        dcol_ref = refs[11 + 2 * ns]
        start, wait = _carried_exchange(scatter, refs[7:7 + ns], refs[11 + ns:11 + 2 * ns], refs[12 + 2 * ns:], True,
                                        (nb, SSD_G))
        start()

        def chunk(i, dstates):
            c = nc - 1 - i
            sl = pl.ds(pl.multiple_of(c * SSD_L, SSD_L), SSD_L)
            bc, cc = b_ref[sl, :], c_ref[sl, :]
            bcb, ccb = bc.astype(BF16), cc.astype(BF16)
            g = lax.dot_general(ccb, bcb, _NT, preferred_element_type=F32)
            rows_c = row_ref[0, :, sl]
            cols_c = rows_c.T
            drow_ref[0, :, sl] = jnp.zeros((SSD_ROWS, SSD_L), F32)
            dcol_ref[...] = jnp.zeros((SSD_L, SSD_ROWS), F32)
            db = jnp.zeros((SSD_L, SSD_N), F32)
            dc = jnp.zeros((SSD_L, SSD_N), F32)
            dg = jnp.zeros((SSD_L, SSD_L), F32)
            new = []
            for k in range(SSD_K):
                hs = pl.ds(k * SSD_P, SSD_P)
                _, vjp = jax.vjp(_ssd_head, xs_ref[sl, hs], g, bc, cc, st_ref[0, c * SSD_K + k],
                                 *_ssd_head_args(rows_c, cols_c, k))
                dx, dgk, dbk, dck, ds, d_cum_row, d_dt_row, d_cum_col, d_ecum_col, d_wend_col = vjp(
                    (dy_ref[sl, hs], dstates[k]))
                dxs_ref[sl, hs] = dx + skip_ref[sl, hs]
                db, dc, dg = db + dbk, dc + dck, dg + dgk
                drow_ref[0, _CUM + k:_CUM + k + 1, sl] = d_cum_row
                drow_ref[0, _DT + k:_DT + k + 1, sl] = d_dt_row
                dcol_ref[:, _CUM + k:_CUM + k + 1] = d_cum_col
                dcol_ref[:, _ECUM + k:_ECUM + k + 1] = d_ecum_col
                dcol_ref[:, _WEND + k:_WEND + k + 1] = d_wend_col
                new.append(ds)
            drow_ref[0, :, sl] += dcol_ref[...].T
            dgb = dg.astype(BF16)
            dc_ref[sl, :] = dc + jnp.dot(dgb, bcb, preferred_element_type=F32)
            db_ref[sl, :] = db + lax.dot_general(dgb, ccb, _TN, preferred_element_type=F32)
            return tuple(new)

        lax.fori_loop(0, nc, chunk, tuple(jnp.zeros((SSD_P, SSD_N), F32) for _ in range(SSD_K)))
        wait()

    t = xs.shape[0]
    x_spec = pl.BlockSpec((seq, kp), lambda b, g: (b, g))
    n_spec = pl.BlockSpec((seq, SSD_N), lambda b, g: (b, g))
    r_spec = pl.BlockSpec((1, SSD_ROWS, seq), lambda b, g: (b * SSD_G + g, 0, 0))
    any_spec = pl.BlockSpec(memory_space=pl.ANY)
    return pl.pallas_call(
        body, name="ssd_scan_bwd", grid=(nb, SSD_G),
        in_specs=[x_spec, n_spec, n_spec, r_spec,
                  pl.BlockSpec((1, nc * SSD_K, SSD_P, SSD_N), lambda b, g: (b * SSD_G + g, 0, 0, 0)), x_spec, x_spec]
        + [any_spec] * ns,
        out_specs=[x_spec, n_spec, n_spec, r_spec] + [any_spec] * ns,
        out_shape=[jax.ShapeDtypeStruct((t, SSD_D_INNER), F32), jax.ShapeDtypeStruct((t, SSD_GN), F32),
                   jax.ShapeDtypeStruct((t, SSD_GN), F32), jax.ShapeDtypeStruct(rows.shape, F32)]
        + _exchange_shapes(scatter, True),
        scratch_shapes=[pltpu.VMEM((SSD_L, SSD_ROWS), F32)] + (_exchange_sems(ns) if ns else []),
        compiler_params=_cparams(dimension_semantics=("arbitrary", "arbitrary")),
    )(xs, bm, cm, rows, states, dy, dxs_skip, *scatter)


ATT_TQ = 256
ATT_TE = 256
ATT_SCALE = (MLA_NOPE + MLA_ROPE) ** -0.5


def _for_key_extent(qi, seq, fn):
    te = min(ATT_TE, seq)
    per = te // ATT_TQ
    for e in range(seq // te):
        pl.when(jnp.logical_and(qi >= e * per, qi < (e + 1) * per))(functools.partial(fn, (e + 1) * te))


def _pair_masks(shape):
    lane = lax.broadcasted_iota(jnp.int32, shape, 1)
    return lane < MLA_ROPE, lane >= MLA_ROPE


def _scores(qn, qr, kn, kr, q0):
    s = lax.dot_general(qn, kn, _NT, preferred_element_type=F32) + lax.dot_general(qr, kr, _NT, preferred_element_type=F32)
    row = lax.broadcasted_iota(jnp.int32, s.shape, 0)
    col = lax.broadcasted_iota(jnp.int32, s.shape, 1)
    return jnp.where(col <= q0 + row, s * ATT_SCALE, -jnp.inf)


def _attn_specs(nb, seq):
    nq = seq // ATT_TQ
    qn = pl.BlockSpec((ATT_TQ, 2 * MLA_NOPE), lambda b, hp, qi: (b * nq + qi, hp))
    qr = pl.BlockSpec((ATT_TQ, LANE), lambda b, hp, qi: (b * nq + qi, hp))
    kn = pl.BlockSpec((seq, 2 * MLA_NOPE), lambda b, hp, qi: (b, hp))
    kr = pl.BlockSpec((seq, LANE), lambda b, hp, qi: (b, 0))
    return (nb, MLA_H // 2, nq), qn, qr, kn, kr


def _attn_fwd(qn, qr, kn, kr, v, nb, seq, gather=()):
    grid, s_qn, s_qr, s_kn, s_kr = _attn_specs(nb, seq)
    ng = len(gather)

    def body(*refs):
        qn_ref, qr_ref, kn_ref, kr_ref, v_ref = refs[:5]
        o_ref, lse_ref = refs[5 + ng:7 + ng]
        start, wait = _carried_exchange(gather, refs[5:5 + ng], refs[7 + ng:7 + 2 * ng], refs[7 + 2 * ng:], False, grid)
        start()
        qi = pl.program_id(2)

        def attend(ext):
            qr = qr_ref[...]
            masks = _pair_masks(qr.shape)
            outs, lses = [], []
            for j in range(2):
                hs = pl.ds(j * MLA_NOPE, MLA_NOPE)
                qr_j = jnp.where(masks[j], qr, jnp.zeros_like(qr))
                s = _scores(qn_ref[:, hs], qr_j, kn_ref[0:ext, hs], kr_ref[0:ext, :], qi * ATT_TQ)
                m = jnp.max(s, axis=1, keepdims=True)
                p = jnp.exp(s - m)
                l = jnp.sum(p, axis=1, keepdims=True)
                outs.append(jnp.dot(p.astype(BF16), v_ref[0:ext, hs], preferred_element_type=F32) / l)
                lses.append(m + jnp.log(l))
            o_ref[...] = jnp.concatenate(outs, axis=1).astype(o_ref.dtype)
            lse_ref[...] = jnp.where(masks[0], lses[0], lses[1])

        _for_key_extent(qi, seq, attend)
        wait()

    any_spec = pl.BlockSpec(memory_space=pl.ANY)
    return pl.pallas_call(
        body, name="attn_fwd", grid=grid, in_specs=[s_qn, s_qr, s_kn, s_kr, s_kn] + [any_spec] * ng,
        out_specs=[s_qn, s_qr] + [any_spec] * ng,
        out_shape=[jax.ShapeDtypeStruct(qn.shape, BF16), jax.ShapeDtypeStruct(qr.shape, F32)] + _exchange_shapes(gather, False),
        scratch_shapes=_exchange_sems(ng) if ng else [],
        compiler_params=_cparams(dimension_semantics=("arbitrary", "arbitrary", "arbitrary")),
    )(qn, qr, kn, kr, v, *gather)


def _attn_bwd(qn, qr, kn, kr, v, o, lse, do, nb, seq):
    grid, s_qn, s_qr, s_kn, s_kr = _attn_specs(nb, seq)

    def body(qn_ref, qr_ref, kn_ref, kr_ref, v_ref, o_ref, lse_ref, do_ref, dqn_ref, dqr_ref, dkn_ref, dkr_ref, dv_ref):
        hp, qi = pl.program_id(1), pl.program_id(2)

        @pl.when(qi == 0)
        def _():
            dkn_ref[...] = jnp.zeros_like(dkn_ref)
            dv_ref[...] = jnp.zeros_like(dv_ref)

        @pl.when(jnp.logical_and(qi == 0, hp == 0))
        def _():
            dkr_ref[...] = jnp.zeros_like(dkr_ref)

        def attend(ext):
            qr, lse = qr_ref[...], lse_ref[...]
            masks = _pair_masks(qr.shape)
            dqr_heads = []
            for j in range(2):
                hs = pl.ds(j * MLA_NOPE, MLA_NOPE)
                qn_j, qr_j = qn_ref[:, hs], jnp.where(masks[j], qr, jnp.zeros_like(qr))
                kn_j, kr, v_j = kn_ref[0:ext, hs], kr_ref[0:ext, :], v_ref[0:ext, hs]
                do_j = do_ref[:, hs]
                dob = do_j.astype(BF16)
                delta = jnp.sum(do_j * o_ref[:, hs].astype(F32), axis=1, keepdims=True)
                p = jnp.exp(_scores(qn_j, qr_j, kn_j, kr, qi * ATT_TQ) - lse[:, j * MLA_ROPE:j * MLA_ROPE + 1])
                dp = lax.dot_general(dob, v_j, _NT, preferred_element_type=F32)
                ds = (p * (dp - delta) * ATT_SCALE).astype(BF16)
                dkn_ref[0:ext, hs] += lax.dot_general(ds, qn_j, _TN, preferred_element_type=F32)
                dkr_ref[0:ext, :] += lax.dot_general(ds, qr_j, _TN, preferred_element_type=F32)
                dv_ref[0:ext, hs] += lax.dot_general(p.astype(BF16), dob, _TN, preferred_element_type=F32)
                dqn_ref[:, hs] = jnp.dot(ds, kn_j, preferred_element_type=F32).astype(dqn_ref.dtype)
                dqr_heads.append(jnp.dot(ds, kr, preferred_element_type=F32))
            dqr_ref[...] = jnp.where(masks[0], dqr_heads[0], dqr_heads[1])

        _for_key_extent(qi, seq, attend)

    return pl.pallas_call(
        body, name="attn_bwd", grid=grid, in_specs=[s_qn, s_qr, s_kn, s_kr, s_kn, s_qn, s_qr, s_qn],
        out_specs=[s_qn, s_qr, s_kn, s_kr, s_kn],
        out_shape=[jax.ShapeDtypeStruct(qn.shape, BF16)] + [jax.ShapeDtypeStruct(a.shape, F32) for a in (qr, kn, kr, v)],
        compiler_params=_cparams(),
    )(qn, qr, kn, kr, v, o, lse, do)


def _loss_head(y, target):
    t, d = y.shape
    tr = _rows(t, 512)

    def body(y_ref, t_ref, l_ref, dy_ref):
        err = y_ref[...] - t_ref[...]
        dy_ref[...] = err * (1.0 / d)
        part = 0.5 * jnp.sum(jnp.sum(err * err, axis=1, keepdims=True), axis=0, keepdims=True) * (1.0 / d)
        l_ref[...] = jnp.broadcast_to(part, l_ref.shape)

    row = pl.BlockSpec((tr, d), lambda i: (i, 0))
    parts, dy = pl.pallas_call(
        body, name="loss_head", grid=(t // tr,), in_specs=[row, row],
        out_specs=[pl.BlockSpec((8, LANE), lambda i: (i, 0)), row],
        out_shape=[jax.ShapeDtypeStruct((8 * (t // tr), LANE), F32), jax.ShapeDtypeStruct((t, d), F32)],
        compiler_params=_cparams(),
    )(y, target)
    return parts, dy


def _sum_parts(stack, name):
    n, r, c = stack.shape
    tr = _rows(r, 512)

    def body(s_ref, o_ref):
        acc = s_ref[0].astype(F32)
        for i in range(1, n):
            acc = acc + s_ref[i].astype(F32)
        o_ref[...] = acc

    return pl.pallas_call(
        body, name=name, grid=(r // tr,), in_specs=[pl.BlockSpec((n, tr, c), lambda i: (0, i, 0))],
        out_specs=pl.BlockSpec((tr, c), lambda i: (i, 0)), out_shape=jax.ShapeDtypeStruct((r, c), F32),
        compiler_params=_cparams(),
    )(stack)


def _adamw(w, g_mine, g_other, m, v, name):
    r, c = w.shape
    tr = _rows(r, 256)
    bc1 = 1.0 / (1.0 - ADAM_B1 ** ADAM_STEP)
    bc2 = 1.0 / (1.0 - ADAM_B2 ** ADAM_STEP)
    two = g_other is not None

    def body(*refs):
        if two:
            w_ref, g_ref, g2_ref, m_ref, v_ref, go_ref, d_ref, mo_ref, vo_ref = refs
            g = g_ref[...] + g2_ref[...]
        else:
            w_ref, g_ref, m_ref, v_ref, go_ref, d_ref, mo_ref, vo_ref = refs
            g = g_ref[...]
        mn = ADAM_B1 * m_ref[...] + (1.0 - ADAM_B1) * g
        vn = ADAM_B2 * v_ref[...] + (1.0 - ADAM_B2) * (g * g)
        go_ref[...] = g
        mo_ref[...] = mn
        vo_ref[...] = vn
        d_ref[...] = -ADAM_LR * ((mn * bc1) / (jnp.sqrt(vn * bc2) + ADAM_EPS) + ADAM_WD * w_ref[...])

    blk = pl.BlockSpec((tr, c), lambda i: (i, 0))
    ins = [w, g_mine] + ([g_other] if two else []) + [m, v]
    return pl.pallas_call(
        body, name=name, grid=(r // tr,), in_specs=[blk] * len(ins), out_specs=[blk] * 4,
        out_shape=[jax.ShapeDtypeStruct((r, c), F32)] * 4, compiler_params=_cparams(),
    )(*ins)


def _chip_peers():
    x, y, c = lax.axis_index("x"), lax.axis_index("y"), lax.axis_index("c")
    return (x, y, c), [(1 - x, y), (x, 1 - y), (1 - x, 1 - y)]


def _chip_exchange(ins, outs, sems, scatter):
    send_sems, recv_sems, local_sems = sems
    (x, y, c), chips = _chip_peers()
    me = 2 * x + y
    n = len(ins)

    def src(i, chip):
        return ins[i].at[chip] if scatter else ins[i]

    def local(i):
        return pltpu.make_async_copy(src(i, me), outs[i].at[me], local_sems.at[i])

    def remote(i, j, piece, slot):
        px, py = chips[j]
        return pltpu.make_async_remote_copy(src_ref=src(i, piece), dst_ref=outs[i].at[slot], send_sem=send_sems.at[i, j],
                                            recv_sem=recv_sems.at[i, j], device_id=(px, py, c), device_id_type=MESH)

    def start():
        for i in range(n):
            local(i).start()
            for j, (px, py) in enumerate(chips):
                remote(i, j, 2 * px + py, me).start()

    def wait():
        for i in range(n):
            for j, (px, py) in enumerate(chips):
                remote(i, j, me, 2 * px + py).wait_recv()
        for i in range(n):
            for j, (px, py) in enumerate(chips):
                remote(i, j, 2 * px + py, me).wait_send()
            local(i).wait()

    return start, wait


def _exchange_sems(n):
    return [pltpu.SemaphoreType.DMA((n, 3)), pltpu.SemaphoreType.DMA((n, 3)), pltpu.SemaphoreType.DMA((n,))]


def _exchange_shapes(arrs, scatter):
    return [jax.ShapeDtypeStruct(s.shape if scatter else (N_CHIPS,) + s.shape, s.dtype) for s in arrs]


def _exchange_call(name, arrs, scatter):
    n = len(arrs)

    def body(*refs):
        start, wait = _chip_exchange(refs[:n], refs[n:2 * n], refs[2 * n:], scatter)
        start()
        wait()

    any_spec = pl.BlockSpec(memory_space=pl.ANY)
    return pl.pallas_call(body, name=name, in_specs=[any_spec] * n, out_specs=[any_spec] * n,
                          out_shape=_exchange_shapes(arrs, scatter), scratch_shapes=_exchange_sems(n))(*arrs)


def _gather_chips(shards, name="gather_chips"):
    return _exchange_call(name, shards, False)


def _scatter_chips(stacks, name="scatter_chips"):
    return _exchange_call(name, stacks, True)


def _swap_cores(arrs):
    n = len(arrs)

    def body(*refs):
        ins, outs = refs[:n], refs[n:2 * n]
        send_sems, recv_sems = refs[2 * n:]
        x, y, c = lax.axis_index("x"), lax.axis_index("y"), lax.axis_index("c")
        cps = []
        for i in range(n):
            cp = pltpu.make_async_remote_copy(src_ref=ins[i], dst_ref=outs[i], send_sem=send_sems.at[i],
                                              recv_sem=recv_sems.at[i], device_id=(x, y, 1 - c), device_id_type=MESH)
            cp.start()
            cps.append(cp)
        for cp in cps:
            cp.wait()

    any_spec = pl.BlockSpec(memory_space=pl.ANY)
    return pl.pallas_call(
        body, name="swap_cores", in_specs=[any_spec] * n, out_specs=[any_spec] * n,
        out_shape=[jax.ShapeDtypeStruct(s.shape, s.dtype) for s in arrs],
        scratch_shapes=[pltpu.SemaphoreType.DMA((n,)), pltpu.SemaphoreType.DMA((n,))],
    )(*arrs)


def _gather_all(block):
    m_per, n = block.shape

    def body(x_ref, out_ref, send_sems, recv_sems, local_sem):
        x, y, c = lax.axis_index("x"), lax.axis_index("y"), lax.axis_index("c")
        me, sibling = (x, y, c), (x, y, 1 - c)
        chips = [(1 - x, y), (x, 1 - y), (1 - x, 1 - y)]

        def rows(px, py, pc):
            return out_ref.at[pl.ds((4 * px + 2 * py + pc) * m_per, m_per), :]

        def copy(k, blk, to, src=None):
            return pltpu.make_async_remote_copy(src_ref=rows(*blk) if src is None else src, dst_ref=rows(*blk),
                                                send_sem=send_sems.at[k], recv_sem=recv_sems.at[k], device_id=to,
                                                device_id_type=MESH)

        mine = pltpu.make_async_copy(x_ref, rows(*me), local_sem)
        mine.start()
        first = [copy(0, me, sibling, src=x_ref)]
        first += [copy(1 + j, me, (*chip, c), src=x_ref) for j, chip in enumerate(chips)]
        for cp in first:
            cp.start()
        passed = [copy(4 + j, (*chip, c), sibling) for j, chip in enumerate(chips)]
        for j, chip in enumerate(chips):
            copy(1 + j, (*chip, c), me).wait_recv()
            passed[j].start()
        copy(0, sibling, me).wait_recv()
        for j, chip in enumerate(chips):
            copy(4 + j, (*chip, 1 - c), me).wait_recv()
        for cp in first + passed:
            cp.wait_send()
        mine.wait()

    return pl.pallas_call(
        body, name="gather_all", out_shape=jax.ShapeDtypeStruct((N_DEV * m_per, n), block.dtype),
        in_specs=[pl.BlockSpec(memory_space=pltpu.VMEM)], out_specs=pl.BlockSpec(memory_space=pltpu.VMEM),
        scratch_shapes=[pltpu.SemaphoreType.DMA((7,)), pltpu.SemaphoreType.DMA((7,)), pltpu.SemaphoreType.DMA],
    )(block)


def _rope_tables(positions):
    inv_freq = 1.0 / (ROPE_THETA ** (jnp.arange(0, MLA_ROPE, 2, dtype=F32) / MLA_ROPE))
    ang = positions.astype(F32).reshape(-1, 1) * inv_freq
    return jnp.tile(jnp.cos(ang), (1, 4)), jnp.tile(jnp.sin(ang), (1, 4))


def _pad_cols(a, n):
    return jnp.pad(a, ((0, 0), (0, n - a.shape[1])))


def _ffn_fwd(l, h, w, nb, seq):
    ug = _mm(h, w["ffn_up_g"][l], name=f"ffn{l}_up_g")
    uv = _mm(h, w["ffn_up_v"][l], name=f"ffn{l}_up_v")
    op = _op_conv(f"ffn{l}_conv", _f_convffn, [ug, uv], [w["ffn_cw_g"][l], w["ffn_cw_v"][l]],
                  [w["ffn_cb_g"][l], w["ffn_cb_v"][l]], nb, seq)
    act = op.fwd((BF16,))[0]
    ff = _mm(act, w["ffn_down"][l], name=f"ffn{l}_down")
    return ff, (h, op, act)


def _ffn_bwd(l, saved, dff, w, grads, dh_skip):
    h, op, act = saved
    grads[f"ffn_down{l}"] = _mm(act, dff, ta=True, name=f"ffn{l}_down_dw")
    dact = _mm(dff, w["ffn_down"][l], tb=True, name=f"ffn{l}_down_dx")
    (dug, duv), (dwg, dwv, dbg, dbv) = op.bwd([dact], [BF16, BF16])
    grads[f"ffn_up_g{l}"] = _mm(h, dug, ta=True, name=f"ffn{l}_up_g_dw")
    grads[f"ffn_up_v{l}"] = _mm(h, duv, ta=True, name=f"ffn{l}_up_v_dw")
    grads[f"ffn_cw_g{l}"], grads[f"ffn_cw_v{l}"], grads[f"ffn_cb_g{l}"], grads[f"ffn_cb_v{l}"] = dwg, dwv, dbg, dbv
    dh = _mm(dug, w["ffn_up_g"][l], tb=True, add=dh_skip, name=f"ffn{l}_up_g_dx")
    return _mm(duv, w["ffn_up_v"][l], tb=True, add=dh, name=f"ffn{l}_up_v_dx")


class _Carried:
    def __init__(self, shards, finish_weights, ffn_shards, set_ffn, grad_stacks):
        self.shards, self.finish_weights, self.ffn_shards, self.set_ffn = shards, finish_weights, ffn_shards, set_ffn
        self.grad_stacks, self.received = grad_stacks, None


def _local_step(x, positions, target, w, carried=None):
    nb, seq, d = x.shape
    t = nb * seq
    x2, tgt2 = x.reshape(t, d), target.reshape(t, d)
    cos, sin = _rope_tables(positions)
    grads = {}

    xb = x2.astype(BF16)

    z = _mm(xb, w["in_z"], name="ssd_in_z")
    raw = [_mm(xb, w[k], name="ssd_" + k) for k in ("in_x", "in_b", "in_c")]
    dt_raw = _mm(xb, w["in_dt"], name="ssd_in_dt")
    conv_ops = [_op_conv("ssd_conv_" + s, _f_convsilu, [r], [w["conv_w_" + s]], [w["conv_b_" + s]], nb, seq)
                for s, r in zip("xbc", raw)]
    xs, bm, cm = [op.fwd()[0] for op in conv_ops]
    dt_op = _op_dt("ssd_dt", dt_raw, w["dt_bias"], w["a_log"])
    dt, cum, ecum, wend = dt_op.fwd()

    def to_rows(v):
        v = jnp.swapaxes(v.reshape(nb, seq, LANE), 1, 2)[:, :SSD_H, :].reshape(nb * SSD_G, SSD_K, seq)
        return jnp.pad(v, ((0, 0), (0, SSD_KPAD - SSD_K), (0, 0)))

    def from_rows(v):
        v = v[:, :SSD_K, :].reshape(nb, SSD_H, seq)
        return jnp.swapaxes(jnp.pad(v, ((0, 0), (0, LANE - SSD_H), (0, 0))), 1, 2).reshape(t, LANE)

    ssd_rows = jnp.concatenate([to_rows(cum), to_rows(dt), to_rows(ecum), to_rows(wend)], axis=1)
    y, states, *gathered = _ssd_scan_fwd(xs, bm, cm, ssd_rows, nb, seq, carried.shards if carried else ())
    if carried:
        w = {**w, **carried.finish_weights(gathered)}
    post_op = _op_ssd_post("ssd_post", y, xs, z, w["d_exp"], w["norm_g"])
    yn = post_op.fwd((BF16,))[0]
    mix0 = _mm(yn, w["ssd_out"], name="ssd_out")
    ln0m = _op_ln("ln_mix0", x2, mix0, w["ln_mix_g"][0], w["ln_mix_b"][0])
    h0a, h0a_b = ln0m.fwd((F32, BF16))
    ff0, ffn0_saved = _ffn_fwd(0, h0a_b, w, nb, seq)
    ln0f = _op_ln("ln_ffn0", h0a, ff0, w["ln_ffn_g"][0], w["ln_ffn_b"][0])
    h1, h1_b = ln0f.fwd((F32, BF16))

    down = _mm(h1_b, w["kvq_down"], name="kvq_down")
    kvn_op = _op_rms("kv_norm", down, w["kv_norm_g"], window=(0, MLA_KV_RANK))
    ckvn = kvn_op.fwd((BF16,))[0]
    kr_op = _op_rope("k_rope", _f_rope_dup, down, cos, sin, window=(MLA_KV_RANK // LANE, 1))
    kr = kr_op.fwd((BF16,))[0]
    kn = _mm(ckvn, w["kv_up_k"], out_dtype=BF16, name="kv_up_k")
    v = _mm(ckvn, w["kv_up_v"], out_dtype=BF16, name="kv_up_v")
    qn_op = _op_rms("q_norm", down, w["q_norm_g"], window=(1, MLA_Q_RANK))
    cq = qn_op.fwd((BF16,))[0]
    qn = _mm(cq, w["q_up_n"], out_dtype=BF16, name="q_up_n")
    qr_raw = _mm(cq, w["q_up_r"], name="q_up_r")
    qr_op = _op_rope("q_rope", _f_rope, qr_raw, cos, sin)
    qr = qr_op.fwd((BF16,))[0]
    o, lse, *gathered = _attn_fwd(qn, qr, kn, kr, v, nb, seq, carried.ffn_shards if carried else ())
    if carried:
        carried.set_ffn(w, gathered)
    mix1 = _mm(o, w["attn_out"], name="attn_out")
    ln1m = _op_ln("ln_mix1", h1, mix1, w["ln_mix_g"][1], w["ln_mix_b"][1])
    h1a, h1a_b = ln1m.fwd((F32, BF16))
    ff1, ffn1_saved = _ffn_fwd(1, h1a_b, w, nb, seq)
    ln1f = _op_ln("ln_ffn1", h1a, ff1, w["ln_ffn_g"][1], w["ln_ffn_b"][1])
    h2 = ln1f.fwd()[0]

    loss_parts, dh2 = _loss_head(h2, tgt2)

    (dh1a, dff1), (grads["ln_ffn_g1"], grads["ln_ffn_b1"]) = ln1f.bwd([dh2], [F32, BF16])
    dh1a = _ffn_bwd(1, ffn1_saved, dff1, w, grads, dh1a)
    (dh1, dmix1), (grads["ln_mix_g1"], grads["ln_mix_b1"]) = ln1m.bwd([dh1a], [F32, BF16])
    grads["attn_out"] = _mm(o, dmix1, ta=True, name="attn_out_dw")
    do = _mm(dmix1, w["attn_out"], tb=True, name="attn_out_dx")
    dqn, dqr, dkn, dkr, dv = _attn_bwd(qn, qr, kn, kr, v, o, lse, do, nb, seq)
    (dqr_raw,), _ = qr_op.bwd([dqr], [BF16])
    grads["q_up_n"] = _mm(cq, dqn, ta=True, name="q_up_n_dw")
    grads["q_up_r"] = _mm(cq, dqr_raw, ta=True, name="q_up_r_dw")
    dcq = _mm(dqn, w["q_up_n"], tb=True, name="q_up_n_dx")
    dcq = _mm(dqr_raw, w["q_up_r"], tb=True, add=dcq, name="q_up_r_dx")
    (dcq_raw,), (grads["q_norm_g"],) = qn_op.bwd([dcq], [BF16])
    grads["kv_up_k"] = _mm(ckvn, dkn, ta=True, name="kv_up_k_dw")
    grads["kv_up_v"] = _mm(ckvn, dv, ta=True, name="kv_up_v_dw")
    dckvn = _mm(dkn, w["kv_up_k"], tb=True, name="kv_up_k_dx")
    dckvn = _mm(dv, w["kv_up_v"], tb=True, add=dckvn, name="kv_up_v_dx")
    (dckv,), (grads["kv_norm_g"],) = kvn_op.bwd([dckvn], [BF16])
    (dkr_in,), _ = kr_op.bwd([dkr], [BF16])
    ddown = jnp.concatenate([dckv, dkr_in, dcq_raw], axis=1)
    grads["kvq_down"] = _mm(h1_b, ddown, ta=True, name="kvq_down_dw")
    dh1 = _mm(ddown, w["kvq_down"], tb=True, add=dh1, name="kvq_down_dx")

    (dh0a, dff0), (grads["ln_ffn_g0"], grads["ln_ffn_b0"]) = ln0f.bwd([dh1], [F32, BF16])
    dh0a = _ffn_bwd(0, ffn0_saved, dff0, w, grads, dh0a)
    (dx, dmix0), (grads["ln_mix_g0"], grads["ln_mix_b0"]) = ln0m.bwd([dh0a], [F32, BF16])
    grads["ssd_out"] = _mm(yn, dmix0, ta=True, name="ssd_out_dw")
    dyn = _mm(dmix0, w["ssd_out"], tb=True, name="ssd_out_dx")
    (dy, dxs_post, dz), (grads["d_exp"], grads["norm_g"]) = post_op.bwd([dyn], [F32, F32, BF16])
    dxs, dbm, dcm, drows, *received = _ssd_scan_bwd(xs, bm, cm, ssd_rows, states, dy, dxs_post, nb, seq,
                                                    carried.grad_stacks(grads) if carried else ())
    if carried:
        carried.received = received
    d_dt, d_cum, d_ecum, d_wend = [from_rows(drows[:, o:o + SSD_KPAD]) for o in (_DT, _CUM, _ECUM, _WEND)]
    (ddt_raw,), (grads["dt_bias"], grads["a_log"]) = dt_op.bwd([d_dt, d_cum, d_ecum, d_wend], [BF16])
    draws = []
    for s, op, dout in zip("xbc", conv_ops, (dxs, dbm, dcm)):
        (dr,), (grads["conv_w_" + s], grads["conv_b_" + s]) = op.bwd([dout], [BF16])
        draws.append(dr)
    for k, dr in zip(("in_x", "in_b", "in_c"), draws):
        grads[k] = _mm(xb, dr, ta=True, name=f"ssd_{k}_dw")
        dx = _mm(dr, w[k], tb=True, add=dx, name=f"ssd_{k}_dx")
    grads["in_z"] = _mm(xb, dz, ta=True, name="ssd_in_z_dw")
    grads["in_dt"] = _mm(xb, ddt_raw, ta=True, name="ssd_in_dt_dw")
    dx = _mm(dz, w["in_z"], tb=True, add=dx, name="ssd_in_z_dx")
    dx = _mm(ddt_raw, w["in_dt"], tb=True, add=dx, name="ssd_in_dt_dx")
    return loss_parts, dx.reshape(nb, seq, d), grads


_XE, _BE, _CE = SSD_D_INNER, SSD_D_INNER + SSD_GN, SSD_D_INNER + 2 * SSD_GN
_KVR = MLA_KV_RANK + LANE


def _prep_weights(fw):
    w = {}
    f = FFN_HIDDEN
    if "ssd_in_proj" in fw:
        ip = fw["ssd_in_proj"][0]
        o = SSD_D_INNER
        w["in_z"], w["in_x"], w["in_b"], w["in_c"] = ip[:, :o], ip[:, o:o + _XE], ip[:, o + _XE:o + _BE], ip[:, o + _BE:o + _CE]
        w["in_dt"] = _pad_cols(ip[:, o + _CE:], LANE)
        cw, cb = fw["ssd_conv_w"][0], fw["ssd_conv_b"]
        for s, (lo, hi) in zip("xbc", ((0, _XE), (_XE, _BE), (_BE, _CE))):
            w["conv_w_" + s], w["conv_b_" + s] = cw[:, lo:hi], cb[:, lo:hi]
    if "ssd_dt_bias" in fw:
        w["dt_bias"], w["a_log"] = _pad_cols(fw["ssd_dt_bias"], LANE), _pad_cols(fw["ssd_A_log"], LANE)
        w["d_exp"] = jnp.repeat(fw["ssd_D"][0], SSD_P)[None, :]
        w["kv_norm_g"], w["q_norm_g"] = fw["kv_norm_g"][None, :], fw["q_norm_g"]
        w["ffn_cb_g"] = [fw["ffn_conv_b"][l:l + 1, :f] for l in range(DEPTH)]
        w["ffn_cb_v"] = [fw["ffn_conv_b"][l:l + 1, f:] for l in range(DEPTH)]
        for k in ("ln_mix_g", "ln_mix_b", "ln_ffn_g", "ln_ffn_b"):
            w[k] = [fw[k][l:l + 1] for l in range(DEPTH)]
    if "ssd_out_proj" in fw:
        w["norm_g"], w["ssd_out"] = fw["ssd_norm_g"], fw["ssd_out_proj"][0]
        kd = fw["kv_down_proj"]
        w["kvq_down"] = jnp.concatenate([_pad_cols(kd, _KVR), fw["q_down_proj"][0]], axis=1)
        w["kv_up_k"], w["kv_up_v"] = fw["kv_up_k"], fw["kv_up_v"]
        qu = fw["q_up_proj"][0].reshape(MLA_Q_RANK, MLA_H, MLA_NOPE + MLA_ROPE)
        w["q_up_n"] = qu[:, :, :MLA_NOPE].reshape(MLA_Q_RANK, MLA_H * MLA_NOPE)
        w["q_up_r"] = qu[:, :, MLA_NOPE:].reshape(MLA_Q_RANK, MLA_H * MLA_ROPE)
        w["attn_out"] = fw["attn_out_proj"][0]
        w["ffn_up_g"], w["ffn_up_v"], w["ffn_down"] = [None] * DEPTH, [None] * DEPTH, [None] * DEPTH
        for l in range(DEPTH):
            if fw["ffn_up"][l] is not None:
                _set_ffn_weights(w, l, fw["ffn_up"][l], fw["ffn_down"][l])
        w["ffn_cw_g"] = [fw["ffn_conv_w"][l][:, :f] for l in range(DEPTH)]
        w["ffn_cw_v"] = [fw["ffn_conv_w"][l][:, f:] for l in range(DEPTH)]
    return w


def _set_ffn_weights(w, l, up, down):
    w["ffn_up_g"][l], w["ffn_up_v"][l], w["ffn_down"][l] = up[:, :FFN_HIDDEN], up[:, FFN_HIDDEN:], down


def _assemble_grads(g, names):
    make = {
        "ssd_in_proj": lambda: jnp.concatenate([g["in_z"], g["in_x"], g["in_b"], g["in_c"], g["in_dt"][:, :SSD_H]], axis=1)[None],
        "ssd_conv_w": lambda: jnp.concatenate([g["conv_w_" + s] for s in "xbc"], axis=1)[None],
        "ssd_conv_b": lambda: jnp.concatenate([g["conv_b_" + s] for s in "xbc"], axis=1),
        "ssd_dt_bias": lambda: g["dt_bias"][:, :SSD_H],
        "ssd_A_log": lambda: g["a_log"][:, :SSD_H],
        "ssd_D": lambda: jnp.sum(g["d_exp"].reshape(SSD_H, SSD_P), axis=1)[None, :],
        "ssd_norm_g": lambda: g["norm_g"],
        "ssd_out_proj": lambda: g["ssd_out"][None],
        "kv_down_proj": lambda: g["kvq_down"][:, :MLA_KV_RANK + MLA_ROPE],
        "kv_norm_g": lambda: g["kv_norm_g"][0],
        "kv_up_k": lambda: g["kv_up_k"],
        "kv_up_v": lambda: g["kv_up_v"],
        "q_down_proj": lambda: g["kvq_down"][None, :, _KVR:],
        "q_norm_g": lambda: g["q_norm_g"],
        "q_up_proj": lambda: jnp.concatenate([g["q_up_n"].reshape(MLA_Q_RANK, MLA_H, MLA_NOPE),
                                              g["q_up_r"].reshape(MLA_Q_RANK, MLA_H, MLA_ROPE)], axis=2).reshape(1, MLA_Q_RANK, -1),
        "attn_out_proj": lambda: g["attn_out"][None],
        "ffn_up": lambda: jnp.stack([jnp.concatenate([g[f"ffn_up_g{l}"], g[f"ffn_up_v{l}"]], axis=1) for l in range(DEPTH)]),
        "ffn_conv_w": lambda: jnp.stack([jnp.concatenate([g[f"ffn_cw_g{l}"], g[f"ffn_cw_v{l}"]], axis=1) for l in range(DEPTH)]),
        "ffn_conv_b": lambda: jnp.concatenate([jnp.concatenate([g[f"ffn_cb_g{l}"], g[f"ffn_cb_v{l}"]], axis=1)
                                               for l in range(DEPTH)], axis=0),
        "ffn_down": lambda: jnp.stack([g[f"ffn_down{l}"] for l in range(DEPTH)]),
    }
    for k in ("ln_mix_g", "ln_mix_b", "ln_ffn_g", "ln_ffn_b"):
        make[k] = lambda k=k: jnp.concatenate([g[f"{k}{l}"] for l in range(DEPTH)], axis=0)
    return {n: make[n]() for n in names}


_WEIGHTS = ["ssd_in_proj", "ssd_conv_w", "ssd_conv_b", "ssd_dt_bias", "ssd_A_log", "ssd_D", "ssd_norm_g", "ssd_out_proj",
            "kv_down_proj", "kv_norm_g", "kv_up_k", "kv_up_v", "q_down_proj", "q_norm_g", "q_up_proj", "attn_out_proj",
            "ffn_up", "ffn_conv_w", "ffn_conv_b", "ffn_down", "ln_mix_g", "ln_mix_b", "ln_ffn_g", "ln_ffn_b"]
_COL_CUT = ["ssd_in_proj", "ssd_conv_w", "ssd_conv_b", "ssd_norm_g", "kv_up_k", "kv_up_v", "q_up_proj", "ffn_up", "ffn_conv_w"]
_ROW_CUT = ["ssd_out_proj", "kv_down_proj", "q_down_proj", "attn_out_proj", "ffn_down"]
_CUT = _COL_CUT + _ROW_CUT
_WHOLE = [n for n in _WEIGHTS if n not in _CUT]
_EARLY = ["ssd_in_proj", "ssd_conv_w", "ssd_conv_b"]
_LATE = [n for n in _CUT if n not in _EARLY]
_FFN_MATRICES = ["ffn_up", "ffn_down"]
_MXU_WEIGHTS = ["ssd_in_proj", "ssd_out_proj", "kv_down_proj", "kv_up_k", "kv_up_v", "q_down_proj", "q_up_proj",
                "attn_out_proj", "ffn_up", "ffn_down"]
_PACK_ROWS = 160


def _shard_2d(name, s):
    return s.reshape(-1, s.shape[-1])


def _unstack(name, g, shard_shape):
    if name in _COL_CUT:
        lead = shard_shape[:-1]
        return jnp.swapaxes(g, 0, 1).reshape(*lead, N_CHIPS * shard_shape[-1])
    lead, rs, c = shard_shape[:-2], shard_shape[-2], shard_shape[-1]
    n_lead = math.prod(lead)
    return jnp.swapaxes(g.reshape(N_CHIPS, n_lead, rs, c), 0, 1).reshape(*lead, N_CHIPS * rs, c)


def _stack(name, full, shard_shape):
    if name in _COL_CUT:
        cs = shard_shape[-1]
        return jnp.swapaxes(full.reshape(-1, N_CHIPS, cs), 0, 1)
    lead, rs, c = shard_shape[:-2], shard_shape[-2], shard_shape[-1]
    n_lead = math.prod(lead)
    return jnp.swapaxes(full.reshape(n_lead, N_CHIPS, rs, c), 0, 1).reshape(N_CHIPS, n_lead * rs, c)


def _pack(arrs):
    flat = jnp.concatenate([a.reshape(-1) for a in arrs])
    return jnp.pad(flat, (0, _PACK_ROWS * LANE - flat.shape[0])).reshape(_PACK_ROWS, LANE)


def _unpack(packed, like):
    flat, out, o = packed.reshape(-1), [], 0
    for a in like:
        out.append(flat[o:o + a.size].reshape(a.shape))
        o += a.size
    return out


_ARGS = ["x", "positions"] + _WEIGHTS + ["loss_target"] + ["m_" + n for n in _WEIGHTS] + ["v_" + n for n in _WEIGHTS]


def kernel(x, positions, ssd_in_proj, ssd_conv_w, ssd_conv_b, ssd_dt_bias, ssd_A_log, ssd_D, ssd_norm_g,
           ssd_out_proj, kv_down_proj, kv_norm_g, kv_up_k, kv_up_v, q_down_proj, q_norm_g, q_up_proj,
           attn_out_proj, ffn_up, ffn_conv_w, ffn_conv_b, ffn_down, ln_mix_g, ln_mix_b, ln_ffn_g, ln_ffn_b,
           loss_target, m_ssd_in_proj, m_ssd_conv_w, m_ssd_conv_b, m_ssd_dt_bias, m_ssd_A_log, m_ssd_D,
           m_ssd_norm_g, m_ssd_out_proj, m_kv_down_proj, m_kv_norm_g, m_kv_up_k, m_kv_up_v, m_q_down_proj,
           m_q_norm_g, m_q_up_proj, m_attn_out_proj, m_ffn_up, m_ffn_conv_w, m_ffn_conv_b, m_ffn_down,
           m_ln_mix_g, m_ln_mix_b, m_ln_ffn_g, m_ln_ffn_b, v_ssd_in_proj, v_ssd_conv_w, v_ssd_conv_b,
           v_ssd_dt_bias, v_ssd_A_log, v_ssd_D, v_ssd_norm_g, v_ssd_out_proj, v_kv_down_proj, v_kv_norm_g,
           v_kv_up_k, v_kv_up_v, v_q_down_proj, v_q_norm_g, v_q_up_proj, v_attn_out_proj, v_ffn_up,
           v_ffn_conv_w, v_ffn_conv_b, v_ffn_down, v_ln_mix_g, v_ln_mix_b, v_ln_ffn_g, v_ln_ffn_b):
    args = (x, positions, ssd_in_proj, ssd_conv_w, ssd_conv_b, ssd_dt_bias, ssd_A_log, ssd_D, ssd_norm_g,
            ssd_out_proj, kv_down_proj, kv_norm_g, kv_up_k, kv_up_v, q_down_proj, q_norm_g, q_up_proj,
            attn_out_proj, ffn_up, ffn_conv_w, ffn_conv_b, ffn_down, ln_mix_g, ln_mix_b, ln_ffn_g, ln_ffn_b,
            loss_target, m_ssd_in_proj, m_ssd_conv_w, m_ssd_conv_b, m_ssd_dt_bias, m_ssd_A_log, m_ssd_D,
            m_ssd_norm_g, m_ssd_out_proj, m_kv_down_proj, m_kv_norm_g, m_kv_up_k, m_kv_up_v, m_q_down_proj,
            m_q_norm_g, m_q_up_proj, m_attn_out_proj, m_ffn_up, m_ffn_conv_w, m_ffn_conv_b, m_ffn_down,
            m_ln_mix_g, m_ln_mix_b, m_ln_ffn_g, m_ln_ffn_b, v_ssd_in_proj, v_ssd_conv_w, v_ssd_conv_b,
            v_ssd_dt_bias, v_ssd_A_log, v_ssd_D, v_ssd_norm_g, v_ssd_out_proj, v_kv_down_proj, v_kv_norm_g,
            v_kv_up_k, v_kv_up_v, v_q_down_proj, v_q_norm_g, v_q_up_proj, v_attn_out_proj, v_ffn_up,
            v_ffn_conv_w, v_ffn_conv_b, v_ffn_down, v_ln_mix_g, v_ln_mix_b, v_ln_ffn_g, v_ln_ffn_b)
    a = dict(zip(_ARGS, args, strict=True))

    last = DEPTH - 1

    def layers(n, late):
        if n not in _FFN_MATRICES:
            return a[n]
        return a[n][last:] if late else a[n][:last]

    def shard(n, late=False):
        s = _shard_2d(n, layers(n, late))
        return s.astype(BF16) if n in _MXU_WEIGHTS else s

    def full_weights(names, gathered, late=False):
        fw = {n: _unstack(n, g, layers(n, late).shape) for n, g in zip(names, gathered)}
        for n in _FFN_MATRICES:
            if n in fw:
                fw[n] = [None] * last + list(fw[n]) if late else list(fw[n]) + [None]
        return fw

    def set_last_ffn(w, gathered):
        fw = full_weights(_FFN_MATRICES, gathered, late=True)
        _set_ffn_weights(w, last, fw["ffn_up"][last], fw["ffn_down"][last])

    def grad_stacks(names, pieces):
        full = _assemble_grads(pieces, names)
        return [_stack(n, full[n], a[n].shape).astype(BF16) for n in names]

    fw = full_weights(_EARLY, _gather_chips([shard(n) for n in _EARLY]))
    fw.update({n: a[n] for n in _WHOLE})
    carried = _Carried([shard(n) for n in _LATE], lambda got: _prep_weights(full_weights(_LATE, got)),
                       [shard(n, late=True) for n in _FFN_MATRICES], set_last_ffn,
                       lambda pieces: grad_stacks(_LATE, pieces))

    loss_parts, grad_x, pieces = _local_step(a["x"], a["positions"], a["loss_target"], _prep_weights(fw), carried)
    loss = lax.psum(jnp.sum(loss_parts[::8, 0]), ("x", "y", "c"))

    bufs = dict(zip(_LATE, carried.received))
    bufs.update(zip(_EARLY, _scatter_chips(grad_stacks(_EARLY, pieces))))
    sums = [_sum_parts(bufs[n], "sum_chips_" + n) for n in _CUT]
    others = _swap_cores(sums)
    full = _assemble_grads(pieces, _WHOLE)
    every = _gather_all(_pack([full[n] for n in _WHOLE]))
    g_whole = _sum_parts(every.reshape(N_DEV, _PACK_ROWS, LANE), "sum_devices")

    res = {}
    for n, s, o in zip(_CUT, sums, others):
        out = _adamw(_shard_2d(n, a[n]), s, o, _shard_2d(n, a["m_" + n]), _shard_2d(n, a["v_" + n]), "adamw_" + n)
        res[n] = [r.reshape(a[n].shape) for r in out]
    whole = [a[n] for n in _WHOLE]
    out = _adamw(_pack(whole), g_whole, None, _pack([a["m_" + n] for n in _WHOLE]), _pack([a["v_" + n] for n in _WHOLE]),
                 "adamw_whole")
    for k, n in enumerate(_WHOLE):
        res[n] = [_unpack(r, whole)[k] for r in out]
    return (loss, grad_x, *[res[n][0] for n in _WEIGHTS], *[res[n][1] for n in _WEIGHTS],
            *[res[n][2] for n in _WEIGHTS], *[res[n][3] for n in _WEIGHTS])
```

```python
import functools
import math

import jax
import jax.numpy as jnp
from jax import lax
from jax.experimental import pallas as pl
from jax.experimental.pallas import tpu as pltpu

F32 = jnp.float32
BF16 = jnp.bfloat16
HIGHEST = lax.Precision.HIGHEST
MESH = pl.DeviceIdType.MESH

D_MODEL = 1024
DEPTH = 2
DN_ALPHA = (2 * DEPTH) ** 0.25
SSD_D_INNER = 2048
SSD_P = 64
SSD_H = 32
SSD_G = 8
SSD_K = 4
SSD_N = 128
SSD_L = 128
SSD_GN = SSD_G * SSD_N
MLA_H = 8
MLA_Q_RANK = 384
MLA_KV_RANK = 256
MLA_NOPE = 128
MLA_ROPE = 64
MLA_V = 128
ROPE_THETA = 10000.0
FFN_HIDDEN = 2816
LN_EPS = 1e-5
RMS_EPS = 1e-6
ADAM_LR = 0.001
ADAM_B1 = 0.9
ADAM_B2 = 0.999
ADAM_EPS = 1e-08
ADAM_WD = 0.01
ADAM_STEP = 10

N_CHIPS = 4
N_DEV = 8
LANE = 128
VMEM_LIMIT = 56 * 1024 * 1024


def _cparams(**kw):
    return pltpu.CompilerParams(vmem_limit_bytes=VMEM_LIMIT, **kw)


def _tile(dim, cap):
    best = None
    t = LANE
    while t <= min(dim, cap):
        if dim % t == 0:
            best = t
        t += LANE
    return dim if best is None else best


MM_TILE_CAP = 1408
MM_WHOLE_K = 2816
MM_VMEM_BUDGET = 40 * 1024 * 1024


def _mm_tiles(m, n, k, a_bytes, b_bytes, o_bytes, has_add):
    tm, tn = _tile(m, MM_TILE_CAP), _tile(n, MM_TILE_CAP)
    tk = k if k <= MM_WHOLE_K else _tile(k, 1024)

    def need(tm, tn):
        acc = tm * tn * 4 if tk < k else 0
        return 2 * (tm * tk * a_bytes + tk * tn * b_bytes + tm * tn * o_bytes + (tm * tn * 4 if has_add else 0)) + acc

    while need(tm, tn) > MM_VMEM_BUDGET:
        if tm >= tn and _tile(m, tm // 2) < tm:
            tm = _tile(m, tm // 2)
        elif _tile(n, tn // 2) < tn:
            tn = _tile(n, tn // 2)
        else:
            break
    return tm, tn, tk


def _mm(a, b, *, ta=False, tb=False, add=None, out_dtype=F32, name, scatter=()):
    m, k = (a.shape[1], a.shape[0]) if ta else a.shape
    n = b.shape[0] if tb else b.shape[1]
    assert (b.shape[1] if tb else b.shape[0]) == k
    tm, tn, tk = _mm_tiles(m, n, k, a.dtype.itemsize, b.dtype.itemsize, jnp.dtype(out_dtype).itemsize, add is not None)
    nk = k // tk
    grid = (m // tm, n // tn, nk)
    dims = (((0 if ta else 1,), (1 if tb else 0,)), ((), ()))
    n_in = 2 + (add is not None)
    ns = len(scatter)

    def body(*refs):
        a_ref, b_ref = refs[:2]
        o_ref = refs[n_in + ns]
        start, wait = _carried_exchange(scatter, refs[n_in:n_in + ns], refs[n_in + ns + 1:n_in + 2 * ns + 1],
                                        refs[n_in + 2 * ns + 1 + (nk > 1):], True, grid)
        start()
        product = lax.dot_general(a_ref[...].astype(BF16), b_ref[...].astype(BF16), dims, preferred_element_type=F32)
        if nk == 1:
            o_ref[...] = (product if add is None else product + refs[2][...]).astype(out_dtype)
        else:
            acc = refs[n_in + 2 * ns + 1]
            kk = pl.program_id(2)

            @pl.when(kk == 0)
            def _():
                acc[...] = jnp.zeros_like(acc) if add is None else refs[2][...]

            acc[...] += product

            @pl.when(kk == nk - 1)
            def _():
                o_ref[...] = acc[...].astype(out_dtype)
        wait()

    a_spec = pl.BlockSpec((tk, tm), lambda i, j, kk: (kk, i)) if ta else pl.BlockSpec((tm, tk), lambda i, j, kk: (i, kk))
    b_spec = pl.BlockSpec((tn, tk), lambda i, j, kk: (j, kk)) if tb else pl.BlockSpec((tk, tn), lambda i, j, kk: (kk, j))
    o_spec = pl.BlockSpec((tm, tn), lambda i, j, kk: (i, j))
    any_spec = pl.BlockSpec(memory_space=pl.ANY)
    ins, specs = [a, b], [a_spec, b_spec]
    if add is not None:
        ins.append(add)
        specs.append(o_spec)
    res = pl.pallas_call(
        body, name=name, grid=grid, in_specs=specs + [any_spec] * ns, out_specs=[o_spec] + [any_spec] * ns,
        out_shape=[jax.ShapeDtypeStruct((m, n), out_dtype)] + _exchange_shapes(scatter, True),
        scratch_shapes=([] if nk == 1 else [pltpu.VMEM((tm, tn), F32)]) + (_exchange_sems(ns) if ns else []),
        compiler_params=_cparams(dimension_semantics=("arbitrary",) * 3 if ns else ("parallel", "parallel", "arbitrary")),
    )(*ins, *scatter)
    return res if ns else res[0]


def _spec(op):
    return pl.BlockSpec(op[1], op[2])


def _bw_fwd(name, fn, grid, ins, outs, out_dtypes):
    n_in = len(ins)
    flat = [(o, dt) for o, dts in zip(outs, out_dtypes) for dt in dts]

    def body(*refs):
        res = fn(*[r[...].astype(F32) for r in refs[:n_in]])
        orefs = iter(refs[n_in:])
        for v, dts in zip(res, out_dtypes):
            for dt in dts:
                next(orefs)[...] = v.astype(dt)

    return pl.pallas_call(
        body, name=name, grid=grid, in_specs=[_spec(o) for o in ins],
        out_specs=[pl.BlockSpec(o[1], o[2]) for o, _ in flat],
        out_shape=[jax.ShapeDtypeStruct(o[0], dt) for o, dt in flat], compiler_params=_cparams(),
    )(*[o[0] for o in ins])


def _bw_bwd(name, fn, grid, data, params, consts, cts, red_axes, grad_dtypes):
    nd, npar, nc, nct = len(data), len(params), len(consts), len(cts)

    def body(*refs):
        first = None
        for ax in red_axes:
            z = pl.program_id(ax) == 0
            first = z if first is None else jnp.logical_and(first, z)
        vals = [r[...].astype(F32) for r in refs[:nd + npar + nc + nct]]
        d, p, c, g = vals[:nd], vals[nd:nd + npar], vals[nd + npar:nd + npar + nc], vals[nd + npar + nc:]
        _, vjp = jax.vjp(lambda dd, pp: tuple(fn(*dd, *pp, *c)), d, p)
        gd, gp = vjp(tuple(g))
        orefs = refs[nd + npar + nc + nct:]
        for r, v in zip(orefs[:nd], gd):
            r[...] = v.astype(r.dtype)
        if npar:
            @pl.when(first)
            def _():
                for r in orefs[nd:]:
                    r[...] = jnp.zeros_like(r)

            for r, v in zip(orefs[nd:], gp):
                r[...] += v

    ins = [d[:3] for d in data] + list(params) + list(consts) + list(cts)
    outs = [d[3] if len(d) > 3 else d for d in data] + list(params)
    dtypes = list(grad_dtypes) + [F32] * npar
    return pl.pallas_call(
        body, name=name, grid=grid, in_specs=[_spec(o) for o in ins], out_specs=[_spec(o) for o in outs],
        out_shape=[jax.ShapeDtypeStruct(o[0].shape, dt) for o, dt in zip(outs, dtypes)], compiler_params=_cparams(),
    )(*[o[0] for o in ins])


def _shift_down(x, s):
    row = lax.broadcasted_iota(jnp.int32, x.shape, 0)
    return jnp.where(row < s, 0.0, pltpu.roll(x, s, 0))


def _shift_up(x, s):
    n = x.shape[0]
    row = lax.broadcasted_iota(jnp.int32, x.shape, 0)
    return jnp.where(row >= n - s, 0.0, pltpu.roll(x, n - s, 0))


def _time_shift(s):
    if s == 0:
        return lambda x: x

    @jax.custom_vjp
    def shift(x):
        return _shift_down(x, s)

    shift.defvjp(lambda x: (_shift_down(x, s), None), lambda _, g: (_shift_up(g, s),))
    return shift


def _rot_half_raw(x):
    lane = lax.broadcasted_iota(jnp.int32, x.shape, 1)
    return jnp.where(lane % MLA_ROPE < MLA_ROPE // 2, -pltpu.roll(x, LANE - MLA_ROPE // 2, 1), pltpu.roll(x, MLA_ROPE // 2, 1))


@jax.custom_vjp
def _rot_half(x):
    return _rot_half_raw(x)


_rot_half.defvjp(lambda x: (_rot_half_raw(x), None), lambda _, g: (-_rot_half_raw(g),))


@jax.custom_vjp
def _roll_half_lanes(x):
    return pltpu.roll(x, LANE // 2, 1)


_roll_half_lanes.defvjp(lambda x: (pltpu.roll(x, LANE // 2, 1), None), lambda _, g: (pltpu.roll(g, LANE // 2, 1),))


def _causal_conv(u, w, b):
    width = w.shape[0]
    y = b
    for k in range(width):
        y = y + w[k:k + 1, :] * _time_shift(width - 1 - k)(u)
    return y


def _silu(x):
    return x * jax.nn.sigmoid(x)


def _f_ln(h, mix, g, b):
    x = DN_ALPHA * h + mix
    mu = jnp.mean(x, axis=-1, keepdims=True)
    xc = x - mu
    var = jnp.mean(xc * xc, axis=-1, keepdims=True)
    return (xc * lax.rsqrt(var + LN_EPS) * g + b,)


def _f_convsilu(u, w, b):
    return (_silu(_causal_conv(u, w, b)),)


def _f_convffn(ug, uv, wg, wv, bg, bv):
    return (_silu(_causal_conv(ug, wg, bg)) * _causal_conv(uv, wv, bv),)


def _f_dt(dt_raw, bias, a_log):
    x = dt_raw + bias
    dt = jnp.maximum(x, 0.0) + jnp.log(1.0 + jnp.exp(-jnp.abs(x)))
    a = dt * (-jnp.exp(a_log))
    n = a.shape[0]
    lower = (lax.broadcasted_iota(jnp.int32, (n, n), 0) >= lax.broadcasted_iota(jnp.int32, (n, n), 1)).astype(F32)
    cum = jnp.dot(lower, a, precision=HIGHEST, preferred_element_type=F32)
    cum_last = jnp.sum(a, axis=0, keepdims=True)
    return dt, cum, jnp.exp(cum), jnp.exp(cum_last - cum) * dt


def _f_ssd_post(y, xs, z, d_exp, ng):
    t = (y + d_exp * xs) * _silu(z)
    return (t * lax.rsqrt(jnp.mean(t * t, axis=-1, keepdims=True) + LN_EPS) * ng,)


def _f_rms(x, g):
    return (x * lax.rsqrt(jnp.mean(x * x, axis=-1, keepdims=True) + RMS_EPS) * g,)


def _f_rope(x, cos, sin):
    return (x * cos + _rot_half(x) * sin,)


def _f_rope_dup(x, cos, sin):
    r = x * cos + _rot_half(x) * sin
    return (r + _roll_half_lanes(r),)


def _rows(t, cap=512):
    for c in (cap, 256, 128, 64, 32, 16, 8):
        if c <= cap and t % c == 0:
            return c
    return t


class _Blockwise:
    def __init__(self, name, fn, grid, data, params, consts, outs, red_axes):
        self.name, self.fn, self.grid = name, fn, grid
        self.data, self.params, self.consts, self.outs, self.red_axes = data, params, consts, outs, red_axes

    def fwd(self, *out_dtypes):
        out_dtypes = out_dtypes or tuple((F32,) for _ in self.outs)
        return _bw_fwd(self.name + "_fwd", self.fn, self.grid, self.data + self.params + self.consts, self.outs, out_dtypes)

    def bwd(self, cts, grad_dtypes=None):
        cts = [(c, o[1], o[2]) for c, o in zip(cts, self.outs)]
        grad_dtypes = grad_dtypes or [F32] * len(self.data)
        res = _bw_bwd(self.name + "_bwd", self.fn, self.grid, self.data, self.params, self.consts, cts, self.red_axes,
                      grad_dtypes)
        return res[:len(self.data)], res[len(self.data):]


def _op_ln(name, h, mix, g, b):
    t, d = h.shape
    tr = _rows(t, 256)
    row = ((tr, d), lambda i: (i, 0))
    par = ((1, d), lambda i: (0, 0))
    return _Blockwise(name, _f_ln, (t // tr,), [(h, *row), (mix, *row)], [(g, *par), (b, *par)], [],
                      [((t, d), *row)], (0,))


def _op_conv(name, fn, us, ws, bs, nb, seq):
    c = us[0].shape[1]
    ct = _tile(c, 256)
    blk = ((seq, ct), lambda j, bb: (bb, j))
    data = [(u, *blk) for u in us]
    params = [(w, (w.shape[0], ct), lambda j, bb: (0, j)) for w in ws] + [(b, (1, ct), lambda j, bb: (0, j)) for b in bs]
    return _Blockwise(name, fn, (c // ct, nb), data, params, [], [((nb * seq, c), *blk)], (1,))


def _op_dt(name, dt_raw, bias, a_log):
    t = dt_raw.shape[0]
    row = ((SSD_L, LANE), lambda i: (i, 0))
    par = ((1, LANE), lambda i: (0, 0))
    return _Blockwise(name, _f_dt, (t // SSD_L,), [(dt_raw, *row)], [(bias, *par), (a_log, *par)], [],
                      [((t, LANE), *row)] * 4, (0,))


def _op_ssd_post(name, y, xs, z, d_exp, ng):
    t, c = y.shape
    gw = c // SSD_G
    tr = _rows(t, 512)
    blk = ((tr, gw), lambda g, i: (i, g))
    par = ((1, gw), lambda g, i: (0, g))
    return _Blockwise(name, _f_ssd_post, (SSD_G, t // tr), [(y, *blk), (xs, *blk), (z, *blk)],
                      [(d_exp, *par), (ng, *par)], [], [((t, c), *blk)], (1,))


def _op_rms(name, x, g, window=None):
    t = x.shape[0]
    j, c = window or (0, x.shape[1])
    tr = _rows(t, 512)
    own = ((tr, c), lambda i: (i, 0))
    return _Blockwise(name, _f_rms, (t // tr,), [(x, (tr, c), lambda i: (i, j), (jax.ShapeDtypeStruct((t, c), F32), *own))],
                      [(g, (1, c), lambda i: (0, 0))], [], [((t, c), *own)], (0,))


def _op_rope(name, fn, x, cos, sin, window=None):
    t = x.shape[0]
    j0, n = window or (0, x.shape[1] // LANE)
    tr = _rows(t, 512)
    own = ((tr, LANE), lambda i, j: (i, j))
    cs = ((tr, LANE), lambda i, j: (i, 0))
    return _Blockwise(name, fn, (t // tr, n), [(x, (tr, LANE), lambda i, j: (i, j0 + j), (jax.ShapeDtypeStruct((t, n * LANE), F32), *own))],
                      [], [(cos, *cs), (sin, *cs)], [((t, n * LANE), *own)], ())


_NT = (((1,), (1,)), ((), ()))
_TN = (((0,), (0,)), ((), ()))


def _ssd_head(x, g, bc, cc, s, cum_row, dt_row, cum_col, ecum_col, wend_col):
    n = g.shape[0]
    row = lax.broadcasted_iota(jnp.int32, (n, n), 0)
    col = lax.broadcasted_iota(jnp.int32, (n, n), 1)
    decay = jnp.exp(jnp.where(row >= col, cum_col - cum_row, -jnp.inf))
    w = g * decay * dt_row
    y = jnp.dot(w.astype(BF16), x.astype(BF16), preferred_element_type=F32)
    y = y + lax.dot_general((cc * ecum_col).astype(BF16), s.astype(BF16), _NT, preferred_element_type=F32)
    lane = lax.broadcasted_iota(jnp.int32, cum_row.shape, 1)
    cum_last = jnp.sum(jnp.where(lane == n - 1, cum_row, 0.0), axis=1, keepdims=True)
    s_new = s * jnp.exp(cum_last) + lax.dot_general(x.astype(BF16), (bc * wend_col).astype(BF16), _TN,
                                                    preferred_element_type=F32)
    return y, s_new


SSD_KPAD = 8
SSD_ROWS = 4 * SSD_KPAD
_CUM, _DT, _ECUM, _WEND = 0, SSD_KPAD, 2 * SSD_KPAD, 3 * SSD_KPAD


def _ssd_head_args(rows, cols, k):
    return (rows[_CUM + k:_CUM + k + 1], rows[_DT + k:_DT + k + 1], cols[:, _CUM + k:_CUM + k + 1],
            cols[:, _ECUM + k:_ECUM + k + 1], cols[:, _WEND + k:_WEND + k + 1])


def _carried_exchange(arrs, in_refs, out_refs, sems, scatter, n_steps):
    if not arrs:
        return lambda: None, lambda: None
    first = functools.reduce(jnp.logical_and, [pl.program_id(ax) == 0 for ax in range(len(n_steps))])
    last = functools.reduce(jnp.logical_and, [pl.program_id(ax) == n - 1 for ax, n in enumerate(n_steps)])
    start, wait = _chip_exchange(in_refs, out_refs, sems, scatter)
    return (lambda: pl.when(first)(start)), (lambda: pl.when(last)(wait))


def _ssd_scan_fwd(xs, bm, cm, rows, nb, seq, gather=()):
    nc = seq // SSD_L
    kp = SSD_K * SSD_P
    ng = len(gather)

    def body(*refs):
        xs_ref, b_ref, c_ref, row_ref = refs[:4]
        y_ref, st_ref = refs[4 + ng:6 + ng]
        start, wait = _carried_exchange(gather, refs[4:4 + ng], refs[6 + ng:6 + 2 * ng], refs[6 + 2 * ng:], False,
                                        (nb, SSD_G))
        start()

        def chunk(c, states):
            sl = pl.ds(pl.multiple_of(c * SSD_L, SSD_L), SSD_L)
            bc, cc = b_ref[sl, :], c_ref[sl, :]
            g = lax.dot_general(cc.astype(BF16), bc.astype(BF16), _NT, preferred_element_type=F32)
            rows_c = row_ref[0, :, sl]
            cols_c = rows_c.T
            new = []
            for k in range(SSD_K):
                hs = pl.ds(k * SSD_P, SSD_P)
                st_ref[0, c * SSD_K + k] = states[k]
                y, s_new = _ssd_head(xs_ref[sl, hs], g, bc, cc, states[k], *_ssd_head_args(rows_c, cols_c, k))
                y_ref[sl, hs] = y
                new.append(s_new)
            return tuple(new)

        lax.fori_loop(0, nc, chunk, tuple(jnp.zeros((SSD_P, SSD_N), F32) for _ in range(SSD_K)))
        wait()

    t = xs.shape[0]
    any_spec = pl.BlockSpec(memory_space=pl.ANY)
    return pl.pallas_call(
        body, name="ssd_scan_fwd", grid=(nb, SSD_G),
        in_specs=[pl.BlockSpec((seq, kp), lambda b, g: (b, g)),
                  pl.BlockSpec((seq, SSD_N), lambda b, g: (b, g)),
                  pl.BlockSpec((seq, SSD_N), lambda b, g: (b, g)),
                  pl.BlockSpec((1, SSD_ROWS, seq), lambda b, g: (b * SSD_G + g, 0, 0))] + [any_spec] * ng,
        out_specs=[pl.BlockSpec((seq, kp), lambda b, g: (b, g)),
                   pl.BlockSpec((1, nc * SSD_K, SSD_P, SSD_N), lambda b, g: (b * SSD_G + g, 0, 0, 0))] + [any_spec] * ng,
        out_shape=[jax.ShapeDtypeStruct((t, SSD_D_INNER), F32),
                   jax.ShapeDtypeStruct((nb * SSD_G, nc * SSD_K, SSD_P, SSD_N), F32)] + _exchange_shapes(gather, False),
        scratch_shapes=_exchange_sems(ng) if ng else [],
        compiler_params=_cparams(dimension_semantics=("arbitrary", "arbitrary")),
    )(xs, bm, cm, rows, *gather)


def _ssd_scan_bwd(xs, bm, cm, rows, states, dy, dxs_skip, nb, seq, scatter=()):
    nc = seq // SSD_L
    kp = SSD_K * SSD_P
    ns = len(scatter)

    def body(*refs):
        xs_ref, b_ref, c_ref, row_ref, st_ref, dy_ref, skip_ref = refs[:7]
        dxs_ref, db_ref, dc_ref, drow_ref = refs[7 + ns:11 + ns]
        dcol_ref = refs[11 + 2 * ns]
        start, wait = _carried_exchange(scatter, refs[7:7 + ns], refs[11 + ns:11 + 2 * ns], refs[12 + 2 * ns:], True,
                                        (nb, SSD_G))
        start()

        def chunk(i, dstates):
            c = nc - 1 - i
            sl = pl.ds(pl.multiple_of(c * SSD_L, SSD_L), SSD_L)
            bc, cc = b_ref[sl, :], c_ref[sl, :]
            bcb, ccb = bc.astype(BF16), cc.astype(BF16)
            g = lax.dot_general(ccb, bcb, _NT, preferred_element_type=F32)
            rows_c = row_ref[0, :, sl]
            cols_c = rows_c.T
            drow_ref[0, :, sl] = jnp.zeros((SSD_ROWS, SSD_L), F32)
            dcol_ref[...] = jnp.zeros((SSD_L, SSD_ROWS), F32)
            db = jnp.zeros((SSD_L, SSD_N), F32)
            dc = jnp.zeros((SSD_L, SSD_N), F32)
            dg = jnp.zeros((SSD_L, SSD_L), F32)
            new = []
            for k in range(SSD_K):
                hs = pl.ds(k * SSD_P, SSD_P)
                _, vjp = jax.vjp(_ssd_head, xs_ref[sl, hs], g, bc, cc, st_ref[0, c * SSD_K + k],
                                 *_ssd_head_args(rows_c, cols_c, k))
                dx, dgk, dbk, dck, ds, d_cum_row, d_dt_row, d_cum_col, d_ecum_col, d_wend_col = vjp(
                    (dy_ref[sl, hs], dstates[k]))
                dxs_ref[sl, hs] = dx + skip_ref[sl, hs]
                db, dc, dg = db + dbk, dc + dck, dg + dgk
                drow_ref[0, _CUM + k:_CUM + k + 1, sl] = d_cum_row
                drow_ref[0, _DT + k:_DT + k + 1, sl] = d_dt_row
                dcol_ref[:, _CUM + k:_CUM + k + 1] = d_cum_col
                dcol_ref[:, _ECUM + k:_ECUM + k + 1] = d_ecum_col
                dcol_ref[:, _WEND + k:_WEND + k + 1] = d_wend_col
                new.append(ds)
            drow_ref[0, :, sl] += dcol_ref[...].T
            dgb = dg.astype(BF16)
            dc_ref[sl, :] = dc + jnp.dot(dgb, bcb, preferred_element_type=F32)
            db_ref[sl, :] = db + lax.dot_general(dgb, ccb, _TN, preferred_element_type=F32)
            return tuple(new)

        lax.fori_loop(0, nc, chunk, tuple(jnp.zeros((SSD_P, SSD_N), F32) for _ in range(SSD_K)))
        wait()

    t = xs.shape[0]
    x_spec = pl.BlockSpec((seq, kp), lambda b, g: (b, g))
    n_spec = pl.BlockSpec((seq, SSD_N), lambda b, g: (b, g))
    r_spec = pl.BlockSpec((1, SSD_ROWS, seq), lambda b, g: (b * SSD_G + g, 0, 0))
    any_spec = pl.BlockSpec(memory_space=pl.ANY)
    return pl.pallas_call(
        body, name="ssd_scan_bwd", grid=(nb, SSD_G),
        in_specs=[x_spec, n_spec, n_spec, r_spec,
                  pl.BlockSpec((1, nc * SSD_K, SSD_P, SSD_N), lambda b, g: (b * SSD_G + g, 0, 0, 0)), x_spec, x_spec]
        + [any_spec] * ns,
        out_specs=[x_spec, n_spec, n_spec, r_spec] + [any_spec] * ns,
        out_shape=[jax.ShapeDtypeStruct((t, SSD_D_INNER), F32), jax.ShapeDtypeStruct((t, SSD_GN), F32),
                   jax.ShapeDtypeStruct((t, SSD_GN), F32), jax.ShapeDtypeStruct(rows.shape, F32)]
        + _exchange_shapes(scatter, True),
        scratch_shapes=[pltpu.VMEM((SSD_L, SSD_ROWS), F32)] + (_exchange_sems(ns) if ns else []),
        compiler_params=_cparams(dimension_semantics=("arbitrary", "arbitrary")),
    )(xs, bm, cm, rows, states, dy, dxs_skip, *scatter)


ATT_TQ = 256
ATT_TE = 256
ATT_SCALE = (MLA_NOPE + MLA_ROPE) ** -0.5


def _for_key_extent(qi, seq, fn):
    te = min(ATT_TE, seq)
    per = te // ATT_TQ
    for e in range(seq // te):
        pl.when(jnp.logical_and(qi >= e * per, qi < (e + 1) * per))(functools.partial(fn, (e + 1) * te))


def _pair_masks(shape):
    lane = lax.broadcasted_iota(jnp.int32, shape, 1)
    return lane < MLA_ROPE, lane >= MLA_ROPE


def _scores(qn, qr, kn, kr, q0):
    s = lax.dot_general(qn, kn, _NT, preferred_element_type=F32) + lax.dot_general(qr, kr, _NT, preferred_element_type=F32)
    row = lax.broadcasted_iota(jnp.int32, s.shape, 0)
    col = lax.broadcasted_iota(jnp.int32, s.shape, 1)
    return jnp.where(col <= q0 + row, s * ATT_SCALE, -jnp.inf)


def _attn_specs(nb, seq):
    nq = seq // ATT_TQ
    qn = pl.BlockSpec((ATT_TQ, 2 * MLA_NOPE), lambda b, hp, qi: (b * nq + qi, hp))
    qr = pl.BlockSpec((ATT_TQ, LANE), lambda b, hp, qi: (b * nq + qi, hp))
    kn = pl.BlockSpec((seq, 2 * MLA_NOPE), lambda b, hp, qi: (b, hp))
    kr = pl.BlockSpec((seq, LANE), lambda b, hp, qi: (b, 0))
    return (nb, MLA_H // 2, nq), qn, qr, kn, kr


def _attn_fwd(qn, qr, kn, kr, v, nb, seq, gather=()):
    grid, s_qn, s_qr, s_kn, s_kr = _attn_specs(nb, seq)
    ng = len(gather)

    def body(*refs):
        qn_ref, qr_ref, kn_ref, kr_ref, v_ref = refs[:5]
        o_ref, lse_ref = refs[5 + ng:7 + ng]
        start, wait = _carried_exchange(gather, refs[5:5 + ng], refs[7 + ng:7 + 2 * ng], refs[7 + 2 * ng:], False, grid)
        start()
        qi = pl.program_id(2)

        def attend(ext):
            qr = qr_ref[...]
            masks = _pair_masks(qr.shape)
            outs, lses = [], []
            for j in range(2):
                hs = pl.ds(j * MLA_NOPE, MLA_NOPE)
                qr_j = jnp.where(masks[j], qr, jnp.zeros_like(qr))
                s = _scores(qn_ref[:, hs], qr_j, kn_ref[0:ext, hs], kr_ref[0:ext, :], qi * ATT_TQ)
                m = jnp.max(s, axis=1, keepdims=True)
                p = jnp.exp(s - m)
                l = jnp.sum(p, axis=1, keepdims=True)
                outs.append(jnp.dot(p.astype(BF16), v_ref[0:ext, hs], preferred_element_type=F32) / l)
                lses.append(m + jnp.log(l))
            o_ref[...] = jnp.concatenate(outs, axis=1).astype(o_ref.dtype)
            lse_ref[...] = jnp.where(masks[0], lses[0], lses[1])

        _for_key_extent(qi, seq, attend)
        wait()

    any_spec = pl.BlockSpec(memory_space=pl.ANY)
    return pl.pallas_call(
        body, name="attn_fwd", grid=grid, in_specs=[s_qn, s_qr, s_kn, s_kr, s_kn] + [any_spec] * ng,
        out_specs=[s_qn, s_qr] + [any_spec] * ng,
        out_shape=[jax.ShapeDtypeStruct(qn.shape, BF16), jax.ShapeDtypeStruct(qr.shape, F32)] + _exchange_shapes(gather, False),
        scratch_shapes=_exchange_sems(ng) if ng else [],
        compiler_params=_cparams(dimension_semantics=("arbitrary", "arbitrary", "arbitrary")),
    )(qn, qr, kn, kr, v, *gather)


def _attn_bwd(qn, qr, kn, kr, v, o, lse, do, nb, seq):
    grid, s_qn, s_qr, s_kn, s_kr = _attn_specs(nb, seq)

    def body(qn_ref, qr_ref, kn_ref, kr_ref, v_ref, o_ref, lse_ref, do_ref, dqn_ref, dqr_ref, dkn_ref, dkr_ref, dv_ref):
        hp, qi = pl.program_id(1), pl.program_id(2)

        @pl.when(qi == 0)
        def _():
            dkn_ref[...] = jnp.zeros_like(dkn_ref)
            dv_ref[...] = jnp.zeros_like(dv_ref)

        @pl.when(jnp.logical_and(qi == 0, hp == 0))
        def _():
            dkr_ref[...] = jnp.zeros_like(dkr_ref)

        def attend(ext):
            qr, lse = qr_ref[...], lse_ref[...]
            masks = _pair_masks(qr.shape)
            dqr_heads = []
            for j in range(2):
                hs = pl.ds(j * MLA_NOPE, MLA_NOPE)
                qn_j, qr_j = qn_ref[:, hs], jnp.where(masks[j], qr, jnp.zeros_like(qr))
                kn_j, kr, v_j = kn_ref[0:ext, hs], kr_ref[0:ext, :], v_ref[0:ext, hs]
                do_j = do_ref[:, hs]
                dob = do_j.astype(BF16)
                delta = jnp.sum(do_j * o_ref[:, hs].astype(F32), axis=1, keepdims=True)
                p = jnp.exp(_scores(qn_j, qr_j, kn_j, kr, qi * ATT_TQ) - lse[:, j * MLA_ROPE:j * MLA_ROPE + 1])
                dp = lax.dot_general(dob, v_j, _NT, preferred_element_type=F32)
                ds = (p * (dp - delta) * ATT_SCALE).astype(BF16)
                dkn_ref[0:ext, hs] += lax.dot_general(ds, qn_j, _TN, preferred_element_type=F32)
                dkr_ref[0:ext, :] += lax.dot_general(ds, qr_j, _TN, preferred_element_type=F32)
                dv_ref[0:ext, hs] += lax.dot_general(p.astype(BF16), dob, _TN, preferred_element_type=F32)
                dqn_ref[:, hs] = jnp.dot(ds, kn_j, preferred_element_type=F32).astype(dqn_ref.dtype)
                dqr_heads.append(jnp.dot(ds, kr, preferred_element_type=F32))
            dqr_ref[...] = jnp.where(masks[0], dqr_heads[0], dqr_heads[1])

        _for_key_extent(qi, seq, attend)

    return pl.pallas_call(
        body, name="attn_bwd", grid=grid, in_specs=[s_qn, s_qr, s_kn, s_kr, s_kn, s_qn, s_qr, s_qn],
        out_specs=[s_qn, s_qr, s_kn, s_kr, s_kn],
        out_shape=[jax.ShapeDtypeStruct(qn.shape, BF16)] + [jax.ShapeDtypeStruct(a.shape, F32) for a in (qr, kn, kr, v)],
        compiler_params=_cparams(),
    )(qn, qr, kn, kr, v, o, lse, do)


def _loss_head(y, target):
    t, d = y.shape
    tr = _rows(t, 512)

    def body(y_ref, t_ref, l_ref, dy_ref):
        err = y_ref[...] - t_ref[...]
        dy_ref[...] = err * (1.0 / d)
        part = 0.5 * jnp.sum(jnp.sum(err * err, axis=1, keepdims=True), axis=0, keepdims=True) * (1.0 / d)
        l_ref[...] = jnp.broadcast_to(part, l_ref.shape)

    row = pl.BlockSpec((tr, d), lambda i: (i, 0))
    parts, dy = pl.pallas_call(
        body, name="loss_head", grid=(t // tr,), in_specs=[row, row],
        out_specs=[pl.BlockSpec((8, LANE), lambda i: (i, 0)), row],
        out_shape=[jax.ShapeDtypeStruct((8 * (t // tr), LANE), F32), jax.ShapeDtypeStruct((t, d), F32)],
        compiler_params=_cparams(),
    )(y, target)
    return parts, dy


def _sum_parts(stack, name):
    n, r, c = stack.shape
    tr = _rows(r, 512)

    def body(s_ref, o_ref):
        acc = s_ref[0].astype(F32)
        for i in range(1, n):
            acc = acc + s_ref[i].astype(F32)
        o_ref[...] = acc

    return pl.pallas_call(
        body, name=name, grid=(r // tr,), in_specs=[pl.BlockSpec((n, tr, c), lambda i: (0, i, 0))],
        out_specs=pl.BlockSpec((tr, c), lambda i: (i, 0)), out_shape=jax.ShapeDtypeStruct((r, c), F32),
        compiler_params=_cparams(),
    )(stack)


def _adamw(w, g_mine, g_other, m, v, name):
    r, c = w.shape
    tr = _rows(r, 256)
    bc1 = 1.0 / (1.0 - ADAM_B1 ** ADAM_STEP)
    bc2 = 1.0 / (1.0 - ADAM_B2 ** ADAM_STEP)
    two = g_other is not None

    def body(*refs):
        if two:
            w_ref, g_ref, g2_ref, m_ref, v_ref, go_ref, d_ref, mo_ref, vo_ref = refs
            g = g_ref[...] + g2_ref[...]
        else:
            w_ref, g_ref, m_ref, v_ref, go_ref, d_ref, mo_ref, vo_ref = refs
            g = g_ref[...]
        mn = ADAM_B1 * m_ref[...] + (1.0 - ADAM_B1) * g
        vn = ADAM_B2 * v_ref[...] + (1.0 - ADAM_B2) * (g * g)
        go_ref[...] = g
        mo_ref[...] = mn
        vo_ref[...] = vn
        d_ref[...] = -ADAM_LR * ((mn * bc1) / (jnp.sqrt(vn * bc2) + ADAM_EPS) + ADAM_WD * w_ref[...])

    blk = pl.BlockSpec((tr, c), lambda i: (i, 0))
    ins = [w, g_mine] + ([g_other] if two else []) + [m, v]
    return pl.pallas_call(
        body, name=name, grid=(r // tr,), in_specs=[blk] * len(ins), out_specs=[blk] * 4,
        out_shape=[jax.ShapeDtypeStruct((r, c), F32)] * 4, compiler_params=_cparams(),
    )(*ins)


def _chip_peers():
    x, y, c = lax.axis_index("x"), lax.axis_index("y"), lax.axis_index("c")
    return (x, y, c), [(1 - x, y), (x, 1 - y), (1 - x, 1 - y)]


def _chip_exchange(ins, outs, sems, scatter):
    send_sems, recv_sems, local_sems = sems
    (x, y, c), chips = _chip_peers()
    me = 2 * x + y
    n = len(ins)

    def src(i, chip):
        return ins[i].at[chip] if scatter else ins[i]

    def local(i):
        return pltpu.make_async_copy(src(i, me), outs[i].at[me], local_sems.at[i])

    def remote(i, j, piece, slot):
        px, py = chips[j]
        return pltpu.make_async_remote_copy(src_ref=src(i, piece), dst_ref=outs[i].at[slot], send_sem=send_sems.at[i, j],
                                            recv_sem=recv_sems.at[i, j], device_id=(px, py, c), device_id_type=MESH)

    def start():
        for i in range(n):
            local(i).start()
            for j, (px, py) in enumerate(chips):
                remote(i, j, 2 * px + py, me).start()

    def wait():
        for i in range(n):
            for j, (px, py) in enumerate(chips):
                remote(i, j, me, 2 * px + py).wait_recv()
        for i in range(n):
            for j, (px, py) in enumerate(chips):
                remote(i, j, 2 * px + py, me).wait_send()
            local(i).wait()

    return start, wait


def _exchange_sems(n):
    return [pltpu.SemaphoreType.DMA((n, 3)), pltpu.SemaphoreType.DMA((n, 3)), pltpu.SemaphoreType.DMA((n,))]


def _exchange_shapes(arrs, scatter):
    return [jax.ShapeDtypeStruct(s.shape if scatter else (N_CHIPS,) + s.shape, s.dtype) for s in arrs]


def _exchange_call(name, arrs, scatter):
    n = len(arrs)

    def body(*refs):
        start, wait = _chip_exchange(refs[:n], refs[n:2 * n], refs[2 * n:], scatter)
        start()
        wait()

    any_spec = pl.BlockSpec(memory_space=pl.ANY)
    return pl.pallas_call(body, name=name, in_specs=[any_spec] * n, out_specs=[any_spec] * n,
                          out_shape=_exchange_shapes(arrs, scatter), scratch_shapes=_exchange_sems(n))(*arrs)


def _gather_chips(shards, name="gather_chips"):
    return _exchange_call(name, shards, False)


def _swap_cores(arrs):
    n = len(arrs)

    def body(*refs):
        ins, outs = refs[:n], refs[n:2 * n]
        send_sems, recv_sems = refs[2 * n:]
        x, y, c = lax.axis_index("x"), lax.axis_index("y"), lax.axis_index("c")
        cps = []
        for i in range(n):
            cp = pltpu.make_async_remote_copy(src_ref=ins[i], dst_ref=outs[i], send_sem=send_sems.at[i],
                                              recv_sem=recv_sems.at[i], device_id=(x, y, 1 - c), device_id_type=MESH)
            cp.start()
            cps.append(cp)
        for cp in cps:
            cp.wait()

    any_spec = pl.BlockSpec(memory_space=pl.ANY)
    return pl.pallas_call(
        body, name="swap_cores", in_specs=[any_spec] * n, out_specs=[any_spec] * n,
        out_shape=[jax.ShapeDtypeStruct(s.shape, s.dtype) for s in arrs],
        scratch_shapes=[pltpu.SemaphoreType.DMA((n,)), pltpu.SemaphoreType.DMA((n,))],
    )(*arrs)


def _gather_all(block):
    m_per, n = block.shape

    def body(x_ref, out_ref, send_sems, recv_sems, local_sem):
        x, y, c = lax.axis_index("x"), lax.axis_index("y"), lax.axis_index("c")
        me, sibling = (x, y, c), (x, y, 1 - c)
        chips = [(1 - x, y), (x, 1 - y), (1 - x, 1 - y)]

        def rows(px, py, pc):
            return out_ref.at[pl.ds((4 * px + 2 * py + pc) * m_per, m_per), :]

        def copy(k, blk, to, src=None):
            return pltpu.make_async_remote_copy(src_ref=rows(*blk) if src is None else src, dst_ref=rows(*blk),
                                                send_sem=send_sems.at[k], recv_sem=recv_sems.at[k], device_id=to,
                                                device_id_type=MESH)

        mine = pltpu.make_async_copy(x_ref, rows(*me), local_sem)
        mine.start()
        first = [copy(0, me, sibling, src=x_ref)]
        first += [copy(1 + j, me, (*chip, c), src=x_ref) for j, chip in enumerate(chips)]
        for cp in first:
            cp.start()
        passed = [copy(4 + j, (*chip, c), sibling) for j, chip in enumerate(chips)]
        for j, chip in enumerate(chips):
            copy(1 + j, (*chip, c), me).wait_recv()
            passed[j].start()
        copy(0, sibling, me).wait_recv()
        for j, chip in enumerate(chips):
            copy(4 + j, (*chip, 1 - c), me).wait_recv()
        for cp in first + passed:
            cp.wait_send()
        mine.wait()

    return pl.pallas_call(
        body, name="gather_all", out_shape=jax.ShapeDtypeStruct((N_DEV * m_per, n), block.dtype),
        in_specs=[pl.BlockSpec(memory_space=pltpu.VMEM)], out_specs=pl.BlockSpec(memory_space=pltpu.VMEM),
        scratch_shapes=[pltpu.SemaphoreType.DMA((7,)), pltpu.SemaphoreType.DMA((7,)), pltpu.SemaphoreType.DMA],
    )(block)


def _rope_tables(positions):
    inv_freq = 1.0 / (ROPE_THETA ** (jnp.arange(0, MLA_ROPE, 2, dtype=F32) / MLA_ROPE))
    ang = positions.astype(F32).reshape(-1, 1) * inv_freq
    return jnp.tile(jnp.cos(ang), (1, 4)), jnp.tile(jnp.sin(ang), (1, 4))


def _pad_cols(a, n):
    return jnp.pad(a, ((0, 0), (0, n - a.shape[1])))


def _ffn_fwd(l, h, w, nb, seq):
    ug = _mm(h, w["ffn_up_g"][l], name=f"ffn{l}_up_g")
    uv = _mm(h, w["ffn_up_v"][l], name=f"ffn{l}_up_v")
    op = _op_conv(f"ffn{l}_conv", _f_convffn, [ug, uv], [w["ffn_cw_g"][l], w["ffn_cw_v"][l]],
                  [w["ffn_cb_g"][l], w["ffn_cb_v"][l]], nb, seq)
    act = op.fwd((BF16,))[0]
    ff = _mm(act, w["ffn_down"][l], name=f"ffn{l}_down")
    return ff, (h, op, act)


def _ffn_bwd(l, saved, dff, w, grads, dh_skip):
    h, op, act = saved
    grads[f"ffn_down{l}"] = _mm(act, dff, ta=True, name=f"ffn{l}_down_dw")
    dact = _mm(dff, w["ffn_down"][l], tb=True, name=f"ffn{l}_down_dx")
    (dug, duv), (dwg, dwv, dbg, dbv) = op.bwd([dact], [BF16, BF16])
    grads[f"ffn_up_g{l}"] = _mm(h, dug, ta=True, name=f"ffn{l}_up_g_dw")
    grads[f"ffn_up_v{l}"] = _mm(h, duv, ta=True, name=f"ffn{l}_up_v_dw")
    grads[f"ffn_cw_g{l}"], grads[f"ffn_cw_v{l}"], grads[f"ffn_cb_g{l}"], grads[f"ffn_cb_v{l}"] = dwg, dwv, dbg, dbv
    dh = _mm(dug, w["ffn_up_g"][l], tb=True, add=dh_skip, name=f"ffn{l}_up_g_dx")
    return _mm(duv, w["ffn_up_v"][l], tb=True, add=dh, name=f"ffn{l}_up_v_dx")


class _Carried:
    def __init__(self, shards, finish_weights, ffn_shards, set_ffn, grad_stacks, last_stacks):
        self.shards, self.finish_weights, self.ffn_shards, self.set_ffn = shards, finish_weights, ffn_shards, set_ffn
        self.grad_stacks, self.last_stacks, self.received, self.last_received = grad_stacks, last_stacks, None, None


def _local_step(x, positions, target, w, carried=None):
    nb, seq, d = x.shape
    t = nb * seq
    x2, tgt2 = x.reshape(t, d), target.reshape(t, d)
    cos, sin = _rope_tables(positions)
    grads = {}

    xb = x2.astype(BF16)

    z = _mm(xb, w["in_z"], name="ssd_in_z")
    raw = [_mm(xb, w[k], name="ssd_" + k) for k in ("in_x", "in_b", "in_c")]
    dt_raw = _mm(xb, w["in_dt"], name="ssd_in_dt")
    conv_ops = [_op_conv("ssd_conv_" + s, _f_convsilu, [r], [w["conv_w_" + s]], [w["conv_b_" + s]], nb, seq)
                for s, r in zip("xbc", raw)]
    xs, bm, cm = [op.fwd()[0] for op in conv_ops]
    dt_op = _op_dt("ssd_dt", dt_raw, w["dt_bias"], w["a_log"])
    dt, cum, ecum, wend = dt_op.fwd()

    def to_rows(v):
        v = jnp.swapaxes(v.reshape(nb, seq, LANE), 1, 2)[:, :SSD_H, :].reshape(nb * SSD_G, SSD_K, seq)
        return jnp.pad(v, ((0, 0), (0, SSD_KPAD - SSD_K), (0, 0)))

    def from_rows(v):
        v = v[:, :SSD_K, :].reshape(nb, SSD_H, seq)
        return jnp.swapaxes(jnp.pad(v, ((0, 0), (0, LANE - SSD_H), (0, 0))), 1, 2).reshape(t, LANE)

    ssd_rows = jnp.concatenate([to_rows(cum), to_rows(dt), to_rows(ecum), to_rows(wend)], axis=1)
    y, states, *gathered = _ssd_scan_fwd(xs, bm, cm, ssd_rows, nb, seq, carried.shards if carried else ())
    if carried:
        w = {**w, **carried.finish_weights(gathered)}
    post_op = _op_ssd_post("ssd_post", y, xs, z, w["d_exp"], w["norm_g"])
    yn = post_op.fwd((BF16,))[0]
    mix0 = _mm(yn, w["ssd_out"], name="ssd_out")
    ln0m = _op_ln("ln_mix0", x2, mix0, w["ln_mix_g"][0], w["ln_mix_b"][0])
    h0a, h0a_b = ln0m.fwd((F32, BF16))
    ff0, ffn0_saved = _ffn_fwd(0, h0a_b, w, nb, seq)
    ln0f = _op_ln("ln_ffn0", h0a, ff0, w["ln_ffn_g"][0], w["ln_ffn_b"][0])
    h1, h1_b = ln0f.fwd((F32, BF16))

    down = _mm(h1_b, w["kvq_down"], name="kvq_down")
    kvn_op = _op_rms("kv_norm", down, w["kv_norm_g"], window=(0, MLA_KV_RANK))
    ckvn = kvn_op.fwd((BF16,))[0]
    kr_op = _op_rope("k_rope", _f_rope_dup, down, cos, sin, window=(MLA_KV_RANK // LANE, 1))
    kr = kr_op.fwd((BF16,))[0]
    kn = _mm(ckvn, w["kv_up_k"], out_dtype=BF16, name="kv_up_k")
    v = _mm(ckvn, w["kv_up_v"], out_dtype=BF16, name="kv_up_v")
    qn_op = _op_rms("q_norm", down, w["q_norm_g"], window=(1, MLA_Q_RANK))
    cq = qn_op.fwd((BF16,))[0]
    qn = _mm(cq, w["q_up_n"], out_dtype=BF16, name="q_up_n")
    qr_raw = _mm(cq, w["q_up_r"], name="q_up_r")
    qr_op = _op_rope("q_rope", _f_rope, qr_raw, cos, sin)
    qr = qr_op.fwd((BF16,))[0]
    o, lse, *gathered = _attn_fwd(qn, qr, kn, kr, v, nb, seq, carried.ffn_shards if carried else ())
    if carried:
        carried.set_ffn(w, gathered)
    mix1 = _mm(o, w["attn_out"], name="attn_out")
    ln1m = _op_ln("ln_mix1", h1, mix1, w["ln_mix_g"][1], w["ln_mix_b"][1])
    h1a, h1a_b = ln1m.fwd((F32, BF16))
    ff1, ffn1_saved = _ffn_fwd(1, h1a_b, w, nb, seq)
    ln1f = _op_ln("ln_ffn1", h1a, ff1, w["ln_ffn_g"][1], w["ln_ffn_b"][1])
    h2 = ln1f.fwd()[0]

    loss_parts, dh2 = _loss_head(h2, tgt2)

    (dh1a, dff1), (grads["ln_ffn_g1"], grads["ln_ffn_b1"]) = ln1f.bwd([dh2], [F32, BF16])
    dh1a = _ffn_bwd(1, ffn1_saved, dff1, w, grads, dh1a)
    (dh1, dmix1), (grads["ln_mix_g1"], grads["ln_mix_b1"]) = ln1m.bwd([dh1a], [F32, BF16])
    grads["attn_out"] = _mm(o, dmix1, ta=True, name="attn_out_dw")
    do = _mm(dmix1, w["attn_out"], tb=True, name="attn_out_dx")
    dqn, dqr, dkn, dkr, dv = _attn_bwd(qn, qr, kn, kr, v, o, lse, do, nb, seq)
    (dqr_raw,), _ = qr_op.bwd([dqr], [BF16])
    grads["q_up_n"] = _mm(cq, dqn, ta=True, name="q_up_n_dw")
    grads["q_up_r"] = _mm(cq, dqr_raw, ta=True, name="q_up_r_dw")
    dcq = _mm(dqn, w["q_up_n"], tb=True, name="q_up_n_dx")
    dcq = _mm(dqr_raw, w["q_up_r"], tb=True, add=dcq, name="q_up_r_dx")
    (dcq_raw,), (grads["q_norm_g"],) = qn_op.bwd([dcq], [BF16])
    grads["kv_up_k"] = _mm(ckvn, dkn, ta=True, name="kv_up_k_dw")
    grads["kv_up_v"] = _mm(ckvn, dv, ta=True, name="kv_up_v_dw")
    dckvn = _mm(dkn, w["kv_up_k"], tb=True, name="kv_up_k_dx")
    dckvn = _mm(dv, w["kv_up_v"], tb=True, add=dckvn, name="kv_up_v_dx")
    (dckv,), (grads["kv_norm_g"],) = kvn_op.bwd([dckvn], [BF16])
    (dkr_in,), _ = kr_op.bwd([dkr], [BF16])
    ddown = jnp.concatenate([dckv, dkr_in, dcq_raw], axis=1)
    grads["kvq_down"] = _mm(h1_b, ddown, ta=True, name="kvq_down_dw")
    dh1 = _mm(ddown, w["kvq_down"], tb=True, add=dh1, name="kvq_down_dx")

    (dh0a, dff0), (grads["ln_ffn_g0"], grads["ln_ffn_b0"]) = ln0f.bwd([dh1], [F32, BF16])
    dh0a = _ffn_bwd(0, ffn0_saved, dff0, w, grads, dh0a)
    (dx, dmix0), (grads["ln_mix_g0"], grads["ln_mix_b0"]) = ln0m.bwd([dh0a], [F32, BF16])
    grads["ssd_out"] = _mm(yn, dmix0, ta=True, name="ssd_out_dw")
    dyn = _mm(dmix0, w["ssd_out"], tb=True, name="ssd_out_dx")
    (dy, dxs_post, dz), (grads["d_exp"], grads["norm_g"]) = post_op.bwd([dyn], [F32, F32, BF16])
    dxs, dbm, dcm, drows, *received = _ssd_scan_bwd(xs, bm, cm, ssd_rows, states, dy, dxs_post, nb, seq,
                                                    carried.grad_stacks(grads) if carried else ())
    if carried:
        carried.received = received
    d_dt, d_cum, d_ecum, d_wend = [from_rows(drows[:, o:o + SSD_KPAD]) for o in (_DT, _CUM, _ECUM, _WEND)]
    (ddt_raw,), (grads["dt_bias"], grads["a_log"]) = dt_op.bwd([d_dt, d_cum, d_ecum, d_wend], [BF16])
    draws = []
    for s, op, dout in zip("xbc", conv_ops, (dxs, dbm, dcm)):
        (dr,), (grads["conv_w_" + s], grads["conv_b_" + s]) = op.bwd([dout], [BF16])
        draws.append(dr)
    in_parts = list(zip(("in_x", "in_z", "in_b", "in_c", "in_dt"), (draws[0], dz, draws[1], draws[2], ddt_raw)))
    for k, dr in in_parts:
        grads[k] = _mm(xb, dr, ta=True, name=f"ssd_{k}_dw")
    stacks = carried.last_stacks(grads) if carried else ((), ())
    received = []
    for i, (k, dr) in enumerate(in_parts):
        if i < len(stacks) and stacks[i]:
            dx, *got = _mm(dr, w[k], tb=True, add=dx, name=f"ssd_{k}_dx", scatter=stacks[i])
            received.append(got)
        else:
            dx = _mm(dr, w[k], tb=True, add=dx, name=f"ssd_{k}_dx")
    if carried:
        carried.last_received = received
    return loss_parts, dx.reshape(nb, seq, d), grads


_XE, _BE, _CE = SSD_D_INNER, SSD_D_INNER + SSD_GN, SSD_D_INNER + 2 * SSD_GN
_KVR = MLA_KV_RANK + LANE


def _prep_weights(fw):
    w = {}
    f = FFN_HIDDEN
    if "ssd_in_proj" in fw:
        ip = fw["ssd_in_proj"][0]
        o = SSD_D_INNER
        w["in_z"], w["in_x"], w["in_b"], w["in_c"] = ip[:, :o], ip[:, o:o + _XE], ip[:, o + _XE:o + _BE], ip[:, o + _BE:o + _CE]
        w["in_dt"] = _pad_cols(ip[:, o + _CE:], LANE)
        cw, cb = fw["ssd_conv_w"][0], fw["ssd_conv_b"]
        for s, (lo, hi) in zip("xbc", ((0, _XE), (_XE, _BE), (_BE, _CE))):
            w["conv_w_" + s], w["conv_b_" + s] = cw[:, lo:hi], cb[:, lo:hi]
    if "ssd_dt_bias" in fw:
        w["dt_bias"], w["a_log"] = _pad_cols(fw["ssd_dt_bias"], LANE), _pad_cols(fw["ssd_A_log"], LANE)
        w["d_exp"] = jnp.repeat(fw["ssd_D"][0], SSD_P)[None, :]
        w["kv_norm_g"], w["q_norm_g"] = fw["kv_norm_g"][None, :], fw["q_norm_g"]
        w["ffn_cb_g"] = [fw["ffn_conv_b"][l:l + 1, :f] for l in range(DEPTH)]
        w["ffn_cb_v"] = [fw["ffn_conv_b"][l:l + 1, f:] for l in range(DEPTH)]
        for k in ("ln_mix_g", "ln_mix_b", "ln_ffn_g", "ln_ffn_b"):
            w[k] = [fw[k][l:l + 1] for l in range(DEPTH)]
    if "ssd_out_proj" in fw:
        w["norm_g"], w["ssd_out"] = fw["ssd_norm_g"], fw["ssd_out_proj"][0]
        kd = fw["kv_down_proj"]
        w["kvq_down"] = jnp.concatenate([_pad_cols(kd, _KVR), fw["q_down_proj"][0]], axis=1)
        w["kv_up_k"], w["kv_up_v"] = fw["kv_up_k"], fw["kv_up_v"]
        qu = fw["q_up_proj"][0].reshape(MLA_Q_RANK, MLA_H, MLA_NOPE + MLA_ROPE)
        w["q_up_n"] = qu[:, :, :MLA_NOPE].reshape(MLA_Q_RANK, MLA_H * MLA_NOPE)
        w["q_up_r"] = qu[:, :, MLA_NOPE:].reshape(MLA_Q_RANK, MLA_H * MLA_ROPE)
        w["attn_out"] = fw["attn_out_proj"][0]
        w["ffn_up_g"], w["ffn_up_v"], w["ffn_down"] = [None] * DEPTH, [None] * DEPTH, [None] * DEPTH
        for l in range(DEPTH):
            if fw["ffn_up"][l] is not None:
                _set_ffn_weights(w, l, fw["ffn_up"][l], fw["ffn_down"][l])
        w["ffn_cw_g"] = [fw["ffn_conv_w"][l][:, :f] for l in range(DEPTH)]
        w["ffn_cw_v"] = [fw["ffn_conv_w"][l][:, f:] for l in range(DEPTH)]
    return w


def _set_ffn_weights(w, l, up, down):
    w["ffn_up_g"][l], w["ffn_up_v"][l], w["ffn_down"][l] = up[:, :FFN_HIDDEN], up[:, FFN_HIDDEN:], down


def _assemble_grads(g, names):
    make = {
        "ssd_in_proj": lambda: jnp.concatenate([g["in_z"], g["in_x"], g["in_b"], g["in_c"], g["in_dt"][:, :SSD_H]], axis=1)[None],
        "ssd_conv_w": lambda: jnp.concatenate([g["conv_w_" + s] for s in "xbc"], axis=1)[None],
        "ssd_conv_b": lambda: jnp.concatenate([g["conv_b_" + s] for s in "xbc"], axis=1),
        "ssd_dt_bias": lambda: g["dt_bias"][:, :SSD_H],
        "ssd_A_log": lambda: g["a_log"][:, :SSD_H],
        "ssd_D": lambda: jnp.sum(g["d_exp"].reshape(SSD_H, SSD_P), axis=1)[None, :],
        "ssd_norm_g": lambda: g["norm_g"],
        "ssd_out_proj": lambda: g["ssd_out"][None],
        "kv_down_proj": lambda: g["kvq_down"][:, :MLA_KV_RANK + MLA_ROPE],
        "kv_norm_g": lambda: g["kv_norm_g"][0],
        "kv_up_k": lambda: g["kv_up_k"],
        "kv_up_v": lambda: g["kv_up_v"],
        "q_down_proj": lambda: g["kvq_down"][None, :, _KVR:],
        "q_norm_g": lambda: g["q_norm_g"],
        "q_up_proj": lambda: jnp.concatenate([g["q_up_n"].reshape(MLA_Q_RANK, MLA_H, MLA_NOPE),
                                              g["q_up_r"].reshape(MLA_Q_RANK, MLA_H, MLA_ROPE)], axis=2).reshape(1, MLA_Q_RANK, -1),
        "attn_out_proj": lambda: g["attn_out"][None],
        "ffn_up": lambda: jnp.stack([jnp.concatenate([g[f"ffn_up_g{l}"], g[f"ffn_up_v{l}"]], axis=1) for l in range(DEPTH)]),
        "ffn_conv_w": lambda: jnp.stack([jnp.concatenate([g[f"ffn_cw_g{l}"], g[f"ffn_cw_v{l}"]], axis=1) for l in range(DEPTH)]),
        "ffn_conv_b": lambda: jnp.concatenate([jnp.concatenate([g[f"ffn_cb_g{l}"], g[f"ffn_cb_v{l}"]], axis=1)
                                               for l in range(DEPTH)], axis=0),
        "ffn_down": lambda: jnp.stack([g[f"ffn_down{l}"] for l in range(DEPTH)]),
    }
    for k in ("ln_mix_g", "ln_mix_b", "ln_ffn_g", "ln_ffn_b"):
        make[k] = lambda k=k: jnp.concatenate([g[f"{k}{l}"] for l in range(DEPTH)], axis=0)
    return {n: make[n]() for n in names}


_WEIGHTS = ["ssd_in_proj", "ssd_conv_w", "ssd_conv_b", "ssd_dt_bias", "ssd_A_log", "ssd_D", "ssd_norm_g", "ssd_out_proj",
            "kv_down_proj", "kv_norm_g", "kv_up_k", "kv_up_v", "q_down_proj", "q_norm_g", "q_up_proj", "attn_out_proj",
            "ffn_up", "ffn_conv_w", "ffn_conv_b", "ffn_down", "ln_mix_g", "ln_mix_b", "ln_ffn_g", "ln_ffn_b"]
_COL_CUT = ["ssd_in_proj", "ssd_conv_w", "ssd_conv_b", "ssd_norm_g", "kv_up_k", "kv_up_v", "q_up_proj", "ffn_up", "ffn_conv_w"]
_ROW_CUT = ["ssd_out_proj", "kv_down_proj", "q_down_proj", "attn_out_proj", "ffn_down"]
_CUT = _COL_CUT + _ROW_CUT
_WHOLE = [n for n in _WEIGHTS if n not in _CUT]
_EARLY = ["ssd_in_proj", "ssd_conv_w", "ssd_conv_b"]
_LATE = [n for n in _CUT if n not in _EARLY]
_FFN_MATRICES = ["ffn_up", "ffn_down"]
_MXU_WEIGHTS = ["ssd_in_proj", "ssd_out_proj", "kv_down_proj", "kv_up_k", "kv_up_v", "q_down_proj", "q_up_proj",
                "attn_out_proj", "ffn_up", "ffn_down"]
_PACK_ROWS = 160


def _shard_2d(name, s):
    return s.reshape(-1, s.shape[-1])


def _unstack(name, g, shard_shape):
    if name in _COL_CUT:
        lead = shard_shape[:-1]
        return jnp.swapaxes(g, 0, 1).reshape(*lead, N_CHIPS * shard_shape[-1])
    lead, rs, c = shard_shape[:-2], shard_shape[-2], shard_shape[-1]
    n_lead = math.prod(lead)
    return jnp.swapaxes(g.reshape(N_CHIPS, n_lead, rs, c), 0, 1).reshape(*lead, N_CHIPS * rs, c)


def _stack(name, full, shard_shape):
    if name in _COL_CUT:
        cs = shard_shape[-1]
        return jnp.swapaxes(full.reshape(-1, N_CHIPS, cs), 0, 1)
    lead, rs, c = shard_shape[:-2], shard_shape[-2], shard_shape[-1]
    n_lead = math.prod(lead)
    return jnp.swapaxes(full.reshape(n_lead, N_CHIPS, rs, c), 0, 1).reshape(N_CHIPS, n_lead * rs, c)


def _pack(arrs):
    flat = jnp.concatenate([a.reshape(-1) for a in arrs])
    return jnp.pad(flat, (0, _PACK_ROWS * LANE - flat.shape[0])).reshape(_PACK_ROWS, LANE)


def _unpack(packed, like):
    flat, out, o = packed.reshape(-1), [], 0
    for a in like:
        out.append(flat[o:o + a.size].reshape(a.shape))
        o += a.size
    return out


_ARGS = ["x", "positions"] + _WEIGHTS + ["loss_target"] + ["m_" + n for n in _WEIGHTS] + ["v_" + n for n in _WEIGHTS]


def kernel(x, positions, ssd_in_proj, ssd_conv_w, ssd_conv_b, ssd_dt_bias, ssd_A_log, ssd_D, ssd_norm_g,
           ssd_out_proj, kv_down_proj, kv_norm_g, kv_up_k, kv_up_v, q_down_proj, q_norm_g, q_up_proj,
           attn_out_proj, ffn_up, ffn_conv_w, ffn_conv_b, ffn_down, ln_mix_g, ln_mix_b, ln_ffn_g, ln_ffn_b,
           loss_target, m_ssd_in_proj, m_ssd_conv_w, m_ssd_conv_b, m_ssd_dt_bias, m_ssd_A_log, m_ssd_D,
           m_ssd_norm_g, m_ssd_out_proj, m_kv_down_proj, m_kv_norm_g, m_kv_up_k, m_kv_up_v, m_q_down_proj,
           m_q_norm_g, m_q_up_proj, m_attn_out_proj, m_ffn_up, m_ffn_conv_w, m_ffn_conv_b, m_ffn_down,
           m_ln_mix_g, m_ln_mix_b, m_ln_ffn_g, m_ln_ffn_b, v_ssd_in_proj, v_ssd_conv_w, v_ssd_conv_b,
           v_ssd_dt_bias, v_ssd_A_log, v_ssd_D, v_ssd_norm_g, v_ssd_out_proj, v_kv_down_proj, v_kv_norm_g,
           v_kv_up_k, v_kv_up_v, v_q_down_proj, v_q_norm_g, v_q_up_proj, v_attn_out_proj, v_ffn_up,
           v_ffn_conv_w, v_ffn_conv_b, v_ffn_down, v_ln_mix_g, v_ln_mix_b, v_ln_ffn_g, v_ln_ffn_b):
    args = (x, positions, ssd_in_proj, ssd_conv_w, ssd_conv_b, ssd_dt_bias, ssd_A_log, ssd_D, ssd_norm_g,
            ssd_out_proj, kv_down_proj, kv_norm_g, kv_up_k, kv_up_v, q_down_proj, q_norm_g, q_up_proj,
            attn_out_proj, ffn_up, ffn_conv_w, ffn_conv_b, ffn_down, ln_mix_g, ln_mix_b, ln_ffn_g, ln_ffn_b,
            loss_target, m_ssd_in_proj, m_ssd_conv_w, m_ssd_conv_b, m_ssd_dt_bias, m_ssd_A_log, m_ssd_D,
            m_ssd_norm_g, m_ssd_out_proj, m_kv_down_proj, m_kv_norm_g, m_kv_up_k, m_kv_up_v, m_q_down_proj,
            m_q_norm_g, m_q_up_proj, m_attn_out_proj, m_ffn_up, m_ffn_conv_w, m_ffn_conv_b, m_ffn_down,
            m_ln_mix_g, m_ln_mix_b, m_ln_ffn_g, m_ln_ffn_b, v_ssd_in_proj, v_ssd_conv_w, v_ssd_conv_b,
            v_ssd_dt_bias, v_ssd_A_log, v_ssd_D, v_ssd_norm_g, v_ssd_out_proj, v_kv_down_proj, v_kv_norm_g,
            v_kv_up_k, v_kv_up_v, v_q_down_proj, v_q_norm_g, v_q_up_proj, v_attn_out_proj, v_ffn_up,
            v_ffn_conv_w, v_ffn_conv_b, v_ffn_down, v_ln_mix_g, v_ln_mix_b, v_ln_ffn_g, v_ln_ffn_b)
    a = dict(zip(_ARGS, args, strict=True))

    last = DEPTH - 1

    def layers(n, late):
        if n not in _FFN_MATRICES:
            return a[n]
        return a[n][last:] if late else a[n][:last]

    def shard(n, late=False):
        s = _shard_2d(n, layers(n, late))
        return s.astype(BF16) if n in _MXU_WEIGHTS else s

    def full_weights(names, gathered, late=False):
        fw = {n: _unstack(n, g, layers(n, late).shape) for n, g in zip(names, gathered)}
        for n in _FFN_MATRICES:
            if n in fw:
                fw[n] = [None] * last + list(fw[n]) if late else list(fw[n]) + [None]
        return fw

    def set_last_ffn(w, gathered):
        fw = full_weights(_FFN_MATRICES, gathered, late=True)
        _set_ffn_weights(w, last, fw["ffn_up"][last], fw["ffn_down"][last])

    def grad_stacks(names, pieces):
        full = _assemble_grads(pieces, names)
        return [_stack(n, full[n], a[n].shape).astype(BF16) for n in names]

    fw = full_weights(_EARLY, _gather_chips([shard(n) for n in _EARLY]))
    fw.update({n: a[n] for n in _WHOLE})
    def last_stacks(pieces):
        in_proj, conv_w, conv_b = grad_stacks(_EARLY, pieces)
        half = in_proj.shape[1] // 2
        return [in_proj[:, :half], conv_w, conv_b], [in_proj[:, half:]]

    carried = _Carried([shard(n) for n in _LATE], lambda got: _prep_weights(full_weights(_LATE, got)),
                       [shard(n, late=True) for n in _FFN_MATRICES], set_last_ffn,
                       lambda pieces: grad_stacks(_LATE, pieces), last_stacks)

    loss_parts, grad_x, pieces = _local_step(a["x"], a["positions"], a["loss_target"], _prep_weights(fw), carried)
    loss = lax.psum(jnp.sum(loss_parts[::8, 0]), ("x", "y", "c"))

    bufs = dict(zip(_LATE, carried.received))
    (in_proj_a, conv_w, conv_b), (in_proj_b,) = carried.last_received
    bufs.update(zip(_EARLY, (jnp.concatenate([in_proj_a, in_proj_b], axis=1), conv_w, conv_b)))
    sums = [_sum_parts(bufs[n], "sum_chips_" + n) for n in _CUT]
    others = _swap_cores(sums)
    full = _assemble_grads(pieces, _WHOLE)
    every = _gather_all(_pack([full[n] for n in _WHOLE]))
    g_whole = _sum_parts(every.reshape(N_DEV, _PACK_ROWS, LANE), "sum_devices")

    res = {}
    for n, s, o in zip(_CUT, sums, others):
        out = _adamw(_shard_2d(n, a[n]), s, o, _shard_2d(n, a["m_" + n]), _shard_2d(n, a["v_" + n]), "adamw_" + n)
        res[n] = [r.reshape(a[n].shape) for r in out]
    whole = [a[n] for n in _WHOLE]
    out = _adamw(_pack(whole), g_whole, None, _pack([a["m_" + n] for n in _WHOLE]), _pack([a["v_" + n] for n in _WHOLE]),
                 "adamw_whole")
    for k, n in enumerate(_WHOLE):
        res[n] = [_unpack(r, whole)[k] for r in out]
    return (loss, grad_x, *[res[n][0] for n in _WEIGHTS], *[res[n][1] for n in _WEIGHTS],
            *[res[n][2] for n in _WEIGHTS], *[res[n][3] for n in _WEIGHTS])
```

```python
import functools
import math

import jax
import jax.numpy as jnp
from jax import lax
from jax.experimental import pallas as pl
from jax.experimental.pallas import tpu as pltpu

F32 = jnp.float32
BF16 = jnp.bfloat16
HIGHEST = lax.Precision.HIGHEST
MESH = pl.DeviceIdType.MESH

D_MODEL = 1024
DEPTH = 2
DN_ALPHA = (2 * DEPTH) ** 0.25
SSD_D_INNER = 2048
SSD_P = 64
SSD_H = 32
SSD_G = 8
SSD_K = 4
SSD_N = 128
SSD_L = 128
SSD_GN = SSD_G * SSD_N
MLA_H = 8
MLA_Q_RANK = 384
MLA_KV_RANK = 256
MLA_NOPE = 128
MLA_ROPE = 64
MLA_V = 128
ROPE_THETA = 10000.0
FFN_HIDDEN = 2816
LN_EPS = 1e-5
RMS_EPS = 1e-6
ADAM_LR = 0.001
ADAM_B1 = 0.9
ADAM_B2 = 0.999
ADAM_EPS = 1e-08
ADAM_WD = 0.01
ADAM_STEP = 10

N_CHIPS = 4
N_DEV = 8
LANE = 128
VMEM_LIMIT = 56 * 1024 * 1024


def _cparams(**kw):
    return pltpu.CompilerParams(vmem_limit_bytes=VMEM_LIMIT, **kw)


def _tile(dim, cap):
    best = None
    t = LANE
    while t <= min(dim, cap):
        if dim % t == 0:
            best = t
        t += LANE
    return dim if best is None else best


MM_TILE_CAP = 1408
MM_WHOLE_K = 2816
MM_VMEM_BUDGET = 40 * 1024 * 1024


def _mm_tiles(m, n, k, a_bytes, b_bytes, o_bytes, has_add):
    tm, tn = _tile(m, MM_TILE_CAP), _tile(n, MM_TILE_CAP)
    tk = k if k <= MM_WHOLE_K else _tile(k, 1024)

    def need(tm, tn):
        acc = tm * tn * 4 if tk < k else 0
        return 2 * (tm * tk * a_bytes + tk * tn * b_bytes + tm * tn * o_bytes + (tm * tn * 4 if has_add else 0)) + acc

    while need(tm, tn) > MM_VMEM_BUDGET:
        if tm >= tn and _tile(m, tm // 2) < tm:
            tm = _tile(m, tm // 2)
        elif _tile(n, tn // 2) < tn:
            tn = _tile(n, tn // 2)
        else:
            break
    return tm, tn, tk


def _mm(a, b, *, ta=False, tb=False, add=None, out_dtype=F32, name):
    m, k = (a.shape[1], a.shape[0]) if ta else a.shape
    n = b.shape[0] if tb else b.shape[1]
    assert (b.shape[1] if tb else b.shape[0]) == k
    tm, tn, tk = _mm_tiles(m, n, k, a.dtype.itemsize, b.dtype.itemsize, jnp.dtype(out_dtype).itemsize, add is not None)
    nk = k // tk
    dims = (((0 if ta else 1,), (1 if tb else 0,)), ((), ()))

    def partial_product(a_ref, b_ref):
        return lax.dot_general(a_ref[...].astype(BF16), b_ref[...].astype(BF16), dims, preferred_element_type=F32)

    def body_one(*refs):
        if add is None:
            a_ref, b_ref, o_ref = refs
            o_ref[...] = partial_product(a_ref, b_ref).astype(out_dtype)
        else:
            a_ref, b_ref, c_ref, o_ref = refs
            o_ref[...] = (partial_product(a_ref, b_ref) + c_ref[...]).astype(out_dtype)

    def body_acc(*refs):
        if add is None:
            a_ref, b_ref, o_ref, acc = refs
        else:
            a_ref, b_ref, c_ref, o_ref, acc = refs
        kk = pl.program_id(2)

        @pl.when(kk == 0)
        def _():
            acc[...] = jnp.zeros_like(acc) if add is None else c_ref[...]

        acc[...] += partial_product(a_ref, b_ref)

        @pl.when(kk == nk - 1)
        def _():
            o_ref[...] = acc[...].astype(out_dtype)

    a_spec = pl.BlockSpec((tk, tm), lambda i, j, kk: (kk, i)) if ta else pl.BlockSpec((tm, tk), lambda i, j, kk: (i, kk))
    b_spec = pl.BlockSpec((tn, tk), lambda i, j, kk: (j, kk)) if tb else pl.BlockSpec((tk, tn), lambda i, j, kk: (kk, j))
    o_spec = pl.BlockSpec((tm, tn), lambda i, j, kk: (i, j))
    ins, specs = [a, b], [a_spec, b_spec]
    if add is not None:
        ins.append(add)
        specs.append(o_spec)
    return pl.pallas_call(
        body_one if nk == 1 else body_acc, name=name, grid=(m // tm, n // tn, nk), in_specs=specs, out_specs=o_spec,
        out_shape=jax.ShapeDtypeStruct((m, n), out_dtype),
        scratch_shapes=[] if nk == 1 else [pltpu.VMEM((tm, tn), F32)],
        compiler_params=_cparams(dimension_semantics=("parallel", "parallel", "arbitrary")),
    )(*ins)


def _spec(op):
    return pl.BlockSpec(op[1], op[2])


def _bw_fwd(name, fn, grid, ins, outs, out_dtypes):
    n_in = len(ins)
    flat = [(o, dt) for o, dts in zip(outs, out_dtypes) for dt in dts]

    def body(*refs):
        res = fn(*[r[...].astype(F32) for r in refs[:n_in]])
        orefs = iter(refs[n_in:])
        for v, dts in zip(res, out_dtypes):
            for dt in dts:
                next(orefs)[...] = v.astype(dt)

    return pl.pallas_call(
        body, name=name, grid=grid, in_specs=[_spec(o) for o in ins],
        out_specs=[pl.BlockSpec(o[1], o[2]) for o, _ in flat],
        out_shape=[jax.ShapeDtypeStruct(o[0], dt) for o, dt in flat], compiler_params=_cparams(),
    )(*[o[0] for o in ins])


def _bw_bwd(name, fn, grid, data, params, consts, cts, red_axes, grad_dtypes):
    nd, npar, nc, nct = len(data), len(params), len(consts), len(cts)

    def body(*refs):
        first = None
        for ax in red_axes:
            z = pl.program_id(ax) == 0
            first = z if first is None else jnp.logical_and(first, z)
        vals = [r[...].astype(F32) for r in refs[:nd + npar + nc + nct]]
        d, p, c, g = vals[:nd], vals[nd:nd + npar], vals[nd + npar:nd + npar + nc], vals[nd + npar + nc:]
        _, vjp = jax.vjp(lambda dd, pp: tuple(fn(*dd, *pp, *c)), d, p)
        gd, gp = vjp(tuple(g))
        orefs = refs[nd + npar + nc + nct:]
        for r, v in zip(orefs[:nd], gd):
            r[...] = v.astype(r.dtype)
        if npar:
            @pl.when(first)
            def _():
                for r in orefs[nd:]:
                    r[...] = jnp.zeros_like(r)

            for r, v in zip(orefs[nd:], gp):
                r[...] += v

    ins = [d[:3] for d in data] + list(params) + list(consts) + list(cts)
    outs = [d[3] if len(d) > 3 else d for d in data] + list(params)
    dtypes = list(grad_dtypes) + [F32] * npar
    return pl.pallas_call(
        body, name=name, grid=grid, in_specs=[_spec(o) for o in ins], out_specs=[_spec(o) for o in outs],
        out_shape=[jax.ShapeDtypeStruct(o[0].shape, dt) for o, dt in zip(outs, dtypes)], compiler_params=_cparams(),
    )(*[o[0] for o in ins])


def _shift_down(x, s):
    row = lax.broadcasted_iota(jnp.int32, x.shape, 0)
    return jnp.where(row < s, 0.0, pltpu.roll(x, s, 0))


def _shift_up(x, s):
    n = x.shape[0]
    row = lax.broadcasted_iota(jnp.int32, x.shape, 0)
    return jnp.where(row >= n - s, 0.0, pltpu.roll(x, n - s, 0))


def _time_shift(s):
    if s == 0:
        return lambda x: x

    @jax.custom_vjp
    def shift(x):
        return _shift_down(x, s)

    shift.defvjp(lambda x: (_shift_down(x, s), None), lambda _, g: (_shift_up(g, s),))
    return shift


def _rot_half_raw(x):
    lane = lax.broadcasted_iota(jnp.int32, x.shape, 1)
    return jnp.where(lane % MLA_ROPE < MLA_ROPE // 2, -pltpu.roll(x, LANE - MLA_ROPE // 2, 1), pltpu.roll(x, MLA_ROPE // 2, 1))


@jax.custom_vjp
def _rot_half(x):
    return _rot_half_raw(x)


_rot_half.defvjp(lambda x: (_rot_half_raw(x), None), lambda _, g: (-_rot_half_raw(g),))


@jax.custom_vjp
def _roll_half_lanes(x):
    return pltpu.roll(x, LANE // 2, 1)


_roll_half_lanes.defvjp(lambda x: (pltpu.roll(x, LANE // 2, 1), None), lambda _, g: (pltpu.roll(g, LANE // 2, 1),))


def _causal_conv(u, w, b):
    width = w.shape[0]
    y = b
    for k in range(width):
        y = y + w[k:k + 1, :] * _time_shift(width - 1 - k)(u)
    return y


def _silu(x):
    return x * jax.nn.sigmoid(x)


def _f_ln(h, mix, g, b):
    x = DN_ALPHA * h + mix
    mu = jnp.mean(x, axis=-1, keepdims=True)
    xc = x - mu
    var = jnp.mean(xc * xc, axis=-1, keepdims=True)
    return (xc * lax.rsqrt(var + LN_EPS) * g + b,)


def _f_convsilu(u, w, b):
    return (_silu(_causal_conv(u, w, b)),)


def _f_convffn(ug, uv, wg, wv, bg, bv):
    return (_silu(_causal_conv(ug, wg, bg)) * _causal_conv(uv, wv, bv),)


def _f_dt(dt_raw, bias, a_log):
    x = dt_raw + bias
    dt = jnp.maximum(x, 0.0) + jnp.log(1.0 + jnp.exp(-jnp.abs(x)))
    a = dt * (-jnp.exp(a_log))
    n = a.shape[0]
    lower = (lax.broadcasted_iota(jnp.int32, (n, n), 0) >= lax.broadcasted_iota(jnp.int32, (n, n), 1)).astype(F32)
    cum = jnp.dot(lower, a, precision=HIGHEST, preferred_element_type=F32)
    cum_last = jnp.sum(a, axis=0, keepdims=True)
    return dt, cum, jnp.exp(cum), jnp.exp(cum_last - cum) * dt


def _f_ssd_post(y, xs, z, d_exp, ng):
    t = (y + d_exp * xs) * _silu(z)
    return (t * lax.rsqrt(jnp.mean(t * t, axis=-1, keepdims=True) + LN_EPS) * ng,)


def _f_rms(x, g):
    return (x * lax.rsqrt(jnp.mean(x * x, axis=-1, keepdims=True) + RMS_EPS) * g,)


def _f_rope(x, cos, sin):
    return (x * cos + _rot_half(x) * sin,)


def _f_rope_dup(x, cos, sin):
    r = x * cos + _rot_half(x) * sin
    return (r + _roll_half_lanes(r),)


def _rows(t, cap=512):
    for c in (cap, 256, 128, 64, 32, 16, 8):
        if c <= cap and t % c == 0:
            return c
    return t


class _Blockwise:
    def __init__(self, name, fn, grid, data, params, consts, outs, red_axes):
        self.name, self.fn, self.grid = name, fn, grid
        self.data, self.params, self.consts, self.outs, self.red_axes = data, params, consts, outs, red_axes

    def fwd(self, *out_dtypes):
        out_dtypes = out_dtypes or tuple((F32,) for _ in self.outs)
        return _bw_fwd(self.name + "_fwd", self.fn, self.grid, self.data + self.params + self.consts, self.outs, out_dtypes)

    def bwd(self, cts, grad_dtypes=None):
        cts = [(c, o[1], o[2]) for c, o in zip(cts, self.outs)]
        grad_dtypes = grad_dtypes or [F32] * len(self.data)
        res = _bw_bwd(self.name + "_bwd", self.fn, self.grid, self.data, self.params, self.consts, cts, self.red_axes,
                      grad_dtypes)
        return res[:len(self.data)], res[len(self.data):]


def _op_ln(name, h, mix, g, b):
    t, d = h.shape
    tr = _rows(t, 256)
    row = ((tr, d), lambda i: (i, 0))
    par = ((1, d), lambda i: (0, 0))
    return _Blockwise(name, _f_ln, (t // tr,), [(h, *row), (mix, *row)], [(g, *par), (b, *par)], [],
                      [((t, d), *row)], (0,))


def _op_conv(name, fn, us, ws, bs, nb, seq):
    c = us[0].shape[1]
    ct = _tile(c, 256)
    blk = ((seq, ct), lambda j, bb: (bb, j))
    data = [(u, *blk) for u in us]
    params = [(w, (w.shape[0], ct), lambda j, bb: (0, j)) for w in ws] + [(b, (1, ct), lambda j, bb: (0, j)) for b in bs]
    return _Blockwise(name, fn, (c // ct, nb), data, params, [], [((nb * seq, c), *blk)], (1,))


def _op_dt(name, dt_raw, bias, a_log):
    t = dt_raw.shape[0]
    row = ((SSD_L, LANE), lambda i: (i, 0))
    par = ((1, LANE), lambda i: (0, 0))
    return _Blockwise(name, _f_dt, (t // SSD_L,), [(dt_raw, *row)], [(bias, *par), (a_log, *par)], [],
                      [((t, LANE), *row)] * 4, (0,))


def _op_ssd_post(name, y, xs, z, d_exp, ng):
    t, c = y.shape
    gw = c // SSD_G
    tr = _rows(t, 512)
    blk = ((tr, gw), lambda g, i: (i, g))
    par = ((1, gw), lambda g, i: (0, g))
    return _Blockwise(name, _f_ssd_post, (SSD_G, t // tr), [(y, *blk), (xs, *blk), (z, *blk)],
                      [(d_exp, *par), (ng, *par)], [], [((t, c), *blk)], (1,))


def _op_rms(name, x, g, window=None):
    t = x.shape[0]
    j, c = window or (0, x.shape[1])
    tr = _rows(t, 512)
    own = ((tr, c), lambda i: (i, 0))
    return _Blockwise(name, _f_rms, (t // tr,), [(x, (tr, c), lambda i: (i, j), (jax.ShapeDtypeStruct((t, c), F32), *own))],
                      [(g, (1, c), lambda i: (0, 0))], [], [((t, c), *own)], (0,))


def _op_rope(name, fn, x, cos, sin, window=None):
    t = x.shape[0]
    j0, n = window or (0, x.shape[1] // LANE)
    tr = _rows(t, 512)
    own = ((tr, LANE), lambda i, j: (i, j))
    cs = ((tr, LANE), lambda i, j: (i, 0))
    return _Blockwise(name, fn, (t // tr, n), [(x, (tr, LANE), lambda i, j: (i, j0 + j), (jax.ShapeDtypeStruct((t, n * LANE), F32), *own))],
                      [], [(cos, *cs), (sin, *cs)], [((t, n * LANE), *own)], ())


_NT = (((1,), (1,)), ((), ()))
_TN = (((0,), (0,)), ((), ()))


def _ssd_head(x, g, bc, cc, s, cum_row, dt_row, cum_col, ecum_col, wend_col):
    n = g.shape[0]
    row = lax.broadcasted_iota(jnp.int32, (n, n), 0)
    col = lax.broadcasted_iota(jnp.int32, (n, n), 1)
    decay = jnp.exp(jnp.where(row >= col, cum_col - cum_row, -jnp.inf))
    w = g * decay * dt_row
    y = jnp.dot(w.astype(BF16), x.astype(BF16), preferred_element_type=F32)
    y = y + lax.dot_general((cc * ecum_col).astype(BF16), s.astype(BF16), _NT, preferred_element_type=F32)
    lane = lax.broadcasted_iota(jnp.int32, cum_row.shape, 1)
    cum_last = jnp.sum(jnp.where(lane == n - 1, cum_row, 0.0), axis=1, keepdims=True)
    s_new = s * jnp.exp(cum_last) + lax.dot_general(x.astype(BF16), (bc * wend_col).astype(BF16), _TN,
                                                    preferred_element_type=F32)
    return y, s_new


SSD_KPAD = 8
SSD_ROWS = 4 * SSD_KPAD
_CUM, _DT, _ECUM, _WEND = 0, SSD_KPAD, 2 * SSD_KPAD, 3 * SSD_KPAD


def _ssd_head_args(rows, cols, k):
    return (rows[_CUM + k:_CUM + k + 1], rows[_DT + k:_DT + k + 1], cols[:, _CUM + k:_CUM + k + 1],
            cols[:, _ECUM + k:_ECUM + k + 1], cols[:, _WEND + k:_WEND + k + 1])


def _carried_exchange(arrs, in_refs, out_refs, sems, scatter, n_steps):
    if not arrs:
        return lambda: None, lambda: None
    first = functools.reduce(jnp.logical_and, [pl.program_id(ax) == 0 for ax in range(len(n_steps))])
    last = functools.reduce(jnp.logical_and, [pl.program_id(ax) == n - 1 for ax, n in enumerate(n_steps)])
    start, wait = _chip_exchange(in_refs, out_refs, sems, scatter)
    return (lambda: pl.when(first)(start)), (lambda: pl.when(last)(wait))


def _ssd_scan_fwd(xs, bm, cm, rows, nb, seq, gather=()):
    nc = seq // SSD_L
    kp = SSD_K * SSD_P
    ng = len(gather)

    def body(*refs):
        xs_ref, b_ref, c_ref, row_ref = refs[:4]
        y_ref, st_ref = refs[4 + ng:6 + ng]
        start, wait = _carried_exchange(gather, refs[4:4 + ng], refs[6 + ng:6 + 2 * ng], refs[6 + 2 * ng:], False,
                                        (nb, SSD_G))
        start()

        def chunk(c, states):
            sl = pl.ds(pl.multiple_of(c * SSD_L, SSD_L), SSD_L)
            bc, cc = b_ref[sl, :], c_ref[sl, :]
            g = lax.dot_general(cc.astype(BF16), bc.astype(BF16), _NT, preferred_element_type=F32)
            rows_c = row_ref[0, :, sl]
            cols_c = rows_c.T
            new = []
            for k in range(SSD_K):
                hs = pl.ds(k * SSD_P, SSD_P)
                st_ref[0, c * SSD_K + k] = states[k]
                y, s_new = _ssd_head(xs_ref[sl, hs], g, bc, cc, states[k], *_ssd_head_args(rows_c, cols_c, k))
                y_ref[sl, hs] = y
                new.append(s_new)
            return tuple(new)

        lax.fori_loop(0, nc, chunk, tuple(jnp.zeros((SSD_P, SSD_N), F32) for _ in range(SSD_K)))
        wait()

    t = xs.shape[0]
    any_spec = pl.BlockSpec(memory_space=pl.ANY)
    return pl.pallas_call(
        body, name="ssd_scan_fwd", grid=(nb, SSD_G),
        in_specs=[pl.BlockSpec((seq, kp), lambda b, g: (b, g)),
                  pl.BlockSpec((seq, SSD_N), lambda b, g: (b, g)),
                  pl.BlockSpec((seq, SSD_N), lambda b, g: (b, g)),
                  pl.BlockSpec((1, SSD_ROWS, seq), lambda b, g: (b * SSD_G + g, 0, 0))] + [any_spec] * ng,
        out_specs=[pl.BlockSpec((seq, kp), lambda b, g: (b, g)),
                   pl.BlockSpec((1, nc * SSD_K, SSD_P, SSD_N), lambda b, g: (b * SSD_G + g, 0, 0, 0))] + [any_spec] * ng,
        out_shape=[jax.ShapeDtypeStruct((t, SSD_D_INNER), F32),
                   jax.ShapeDtypeStruct((nb * SSD_G, nc * SSD_K, SSD_P, SSD_N), F32)] + _exchange_shapes(gather, False),
        scratch_shapes=_exchange_sems(ng) if ng else [],
        compiler_params=_cparams(dimension_semantics=("arbitrary", "arbitrary")),
    )(xs, bm, cm, rows, *gather)


def _ssd_scan_bwd(xs, bm, cm, rows, states, dy, dxs_skip, nb, seq, scatter=()):
    nc = seq // SSD_L
    kp = SSD_K * SSD_P
    ns = len(scatter)

    def body(*refs):
        xs_ref, b_ref, c_ref, row_ref, st_ref, dy_ref, skip_ref = refs[:7]
        dxs_ref, db_ref, dc_ref, drow_ref = refs[7 + ns:11 + ns]
        dcol_ref = refs[11 + 2 * ns]
        start, wait = _carried_exchange(scatter, refs[7:7 + ns], refs[11 + ns:11 + 2 * ns], refs[12 + 2 * ns:], True,
                                        (nb, SSD_G))
        start()

        def chunk(i, dstates):
            c = nc - 1 - i
            sl = pl.ds(pl.multiple_of(c * SSD_L, SSD_L), SSD_L)
            bc, cc = b_ref[sl, :], c_ref[sl, :]
            bcb, ccb = bc.astype(BF16), cc.astype(BF16)
            g = lax.dot_general(ccb, bcb, _NT, preferred_element_type=F32)
            rows_c = row_ref[0, :, sl]
            cols_c = rows_c.T
            drow_ref[0, :, sl] = jnp.zeros((SSD_ROWS, SSD_L), F32)
            dcol_ref[...] = jnp.zeros((SSD_L, SSD_ROWS), F32)
            db = jnp.zeros((SSD_L, SSD_N), F32)
            dc = jnp.zeros((SSD_L, SSD_N), F32)
            dg = jnp.zeros((SSD_L, SSD_L), F32)
            new = []
            for k in range(SSD_K):
                hs = pl.ds(k * SSD_P, SSD_P)
                _, vjp = jax.vjp(_ssd_head, xs_ref[sl, hs], g, bc, cc, st_ref[0, c * SSD_K + k],
                                 *_ssd_head_args(rows_c, cols_c, k))
                dx, dgk, dbk, dck, ds, d_cum_row, d_dt_row, d_cum_col, d_ecum_col, d_wend_col = vjp(
                    (dy_ref[sl, hs], dstates[k]))
                dxs_ref[sl, hs] = dx + skip_ref[sl, hs]
                db, dc, dg = db + dbk, dc + dck, dg + dgk
                drow_ref[0, _CUM + k:_CUM + k + 1, sl] = d_cum_row
                drow_ref[0, _DT + k:_DT + k + 1, sl] = d_dt_row
                dcol_ref[:, _CUM + k:_CUM + k + 1] = d_cum_col
                dcol_ref[:, _ECUM + k:_ECUM + k + 1] = d_ecum_col
                dcol_ref[:, _WEND + k:_WEND + k + 1] = d_wend_col
                new.append(ds)
            drow_ref[0, :, sl] += dcol_ref[...].T
            dgb = dg.astype(BF16)
            dc_ref[sl, :] = dc + jnp.dot(dgb, bcb, preferred_element_type=F32)
            db_ref[sl, :] = db + lax.dot_general(dgb, ccb, _TN, preferred_element_type=F32)
            return tuple(new)

        lax.fori_loop(0, nc, chunk, tuple(jnp.zeros((SSD_P, SSD_N), F32) for _ in range(SSD_K)))
        wait()

    t = xs.shape[0]
    x_spec = pl.BlockSpec((seq, kp), lambda b, g: (b, g))
    n_spec = pl.BlockSpec((seq, SSD_N), lambda b, g: (b, g))
    r_spec = pl.BlockSpec((1, SSD_ROWS, seq), lambda b, g: (b * SSD_G + g, 0, 0))
    any_spec = pl.BlockSpec(memory_space=pl.ANY)
    return pl.pallas_call(
        body, name="ssd_scan_bwd", grid=(nb, SSD_G),
        in_specs=[x_spec, n_spec, n_spec, r_spec,
                  pl.BlockSpec((1, nc * SSD_K, SSD_P, SSD_N), lambda b, g: (b * SSD_G + g, 0, 0, 0)), x_spec, x_spec]
        + [any_spec] * ns,
        out_specs=[x_spec, n_spec, n_spec, r_spec] + [any_spec] * ns,
        out_shape=[jax.ShapeDtypeStruct((t, SSD_D_INNER), F32), jax.ShapeDtypeStruct((t, SSD_GN), F32),
                   jax.ShapeDtypeStruct((t, SSD_GN), F32), jax.ShapeDtypeStruct(rows.shape, F32)]
        + _exchange_shapes(scatter, True),
        scratch_shapes=[pltpu.VMEM((SSD_L, SSD_ROWS), F32)] + (_exchange_sems(ns) if ns else []),
        compiler_params=_cparams(dimension_semantics=("arbitrary", "arbitrary")),
    )(xs, bm, cm, rows, states, dy, dxs_skip, *scatter)


ATT_TQ = 256
ATT_TE = 256
ATT_SCALE = (MLA_NOPE + MLA_ROPE) ** -0.5


def _for_key_extent(qi, seq, fn):
    te = min(ATT_TE, seq)
    per = te // ATT_TQ
    for e in range(seq // te):
        pl.when(jnp.logical_and(qi >= e * per, qi < (e + 1) * per))(functools.partial(fn, (e + 1) * te))


def _pair_masks(shape):
    lane = lax.broadcasted_iota(jnp.int32, shape, 1)
    return lane < MLA_ROPE, lane >= MLA_ROPE


def _scores(qn, qr, kn, kr, q0):
    s = lax.dot_general(qn, kn, _NT, preferred_element_type=F32) + lax.dot_general(qr, kr, _NT, preferred_element_type=F32)
    row = lax.broadcasted_iota(jnp.int32, s.shape, 0)
    col = lax.broadcasted_iota(jnp.int32, s.shape, 1)
    return jnp.where(col <= q0 + row, s * ATT_SCALE, -jnp.inf)


def _attn_specs(nb, seq):
    nq = seq // ATT_TQ
    qn = pl.BlockSpec((ATT_TQ, 2 * MLA_NOPE), lambda b, hp, qi: (b * nq + qi, hp))
    qr = pl.BlockSpec((ATT_TQ, LANE), lambda b, hp, qi: (b * nq + qi, hp))
    kn = pl.BlockSpec((seq, 2 * MLA_NOPE), lambda b, hp, qi: (b, hp))
    kr = pl.BlockSpec((seq, LANE), lambda b, hp, qi: (b, 0))
    return (nb, MLA_H // 2, nq), qn, qr, kn, kr


def _attn_fwd(qn, qr, kn, kr, v, nb, seq, gather=()):
    grid, s_qn, s_qr, s_kn, s_kr = _attn_specs(nb, seq)
    ng = len(gather)

    def body(*refs):
        qn_ref, qr_ref, kn_ref, kr_ref, v_ref = refs[:5]
        o_ref, lse_ref = refs[5 + ng:7 + ng]
        start, wait = _carried_exchange(gather, refs[5:5 + ng], refs[7 + ng:7 + 2 * ng], refs[7 + 2 * ng:], False, grid)
        start()
        qi = pl.program_id(2)

        def attend(ext):
            qr = qr_ref[...]
            masks = _pair_masks(qr.shape)
            outs, lses = [], []
            for j in range(2):
                hs = pl.ds(j * MLA_NOPE, MLA_NOPE)
                qr_j = jnp.where(masks[j], qr, jnp.zeros_like(qr))
                s = _scores(qn_ref[:, hs], qr_j, kn_ref[0:ext, hs], kr_ref[0:ext, :], qi * ATT_TQ)
                m = jnp.max(s, axis=1, keepdims=True)
                p = jnp.exp(s - m)
                l = jnp.sum(p, axis=1, keepdims=True)
                outs.append(jnp.dot(p.astype(BF16), v_ref[0:ext, hs], preferred_element_type=F32) / l)
                lses.append(m + jnp.log(l))
            o_ref[...] = jnp.concatenate(outs, axis=1).astype(o_ref.dtype)
            lse_ref[...] = jnp.where(masks[0], lses[0], lses[1])

        _for_key_extent(qi, seq, attend)
        wait()

    any_spec = pl.BlockSpec(memory_space=pl.ANY)
    return pl.pallas_call(
        body, name="attn_fwd", grid=grid, in_specs=[s_qn, s_qr, s_kn, s_kr, s_kn] + [any_spec] * ng,
        out_specs=[s_qn, s_qr] + [any_spec] * ng,
        out_shape=[jax.ShapeDtypeStruct(qn.shape, BF16), jax.ShapeDtypeStruct(qr.shape, F32)] + _exchange_shapes(gather, False),
        scratch_shapes=_exchange_sems(ng) if ng else [],
        compiler_params=_cparams(dimension_semantics=("arbitrary", "arbitrary", "arbitrary")),
    )(qn, qr, kn, kr, v, *gather)


def _attn_bwd(qn, qr, kn, kr, v, o, lse, do, nb, seq):
    grid, s_qn, s_qr, s_kn, s_kr = _attn_specs(nb, seq)

    def body(qn_ref, qr_ref, kn_ref, kr_ref, v_ref, o_ref, lse_ref, do_ref, dqn_ref, dqr_ref, dkn_ref, dkr_ref, dv_ref):
        hp, qi = pl.program_id(1), pl.program_id(2)

        @pl.when(qi == 0)
        def _():
            dkn_ref[...] = jnp.zeros_like(dkn_ref)
            dv_ref[...] = jnp.zeros_like(dv_ref)

        @pl.when(jnp.logical_and(qi == 0, hp == 0))
        def _():
            dkr_ref[...] = jnp.zeros_like(dkr_ref)

        def attend(ext):
            qr, lse = qr_ref[...], lse_ref[...]
            masks = _pair_masks(qr.shape)
            dqr_heads = []
            for j in range(2):
                hs = pl.ds(j * MLA_NOPE, MLA_NOPE)
                qn_j, qr_j = qn_ref[:, hs], jnp.where(masks[j], qr, jnp.zeros_like(qr))
                kn_j, kr, v_j = kn_ref[0:ext, hs], kr_ref[0:ext, :], v_ref[0:ext, hs]
                do_j = do_ref[:, hs]
                dob = do_j.astype(BF16)
                delta = jnp.sum(do_j * o_ref[:, hs].astype(F32), axis=1, keepdims=True)
                p = jnp.exp(_scores(qn_j, qr_j, kn_j, kr, qi * ATT_TQ) - lse[:, j * MLA_ROPE:j * MLA_ROPE + 1])
                dp = lax.dot_general(dob, v_j, _NT, preferred_element_type=F32)
                ds = (p * (dp - delta) * ATT_SCALE).astype(BF16)
                dkn_ref[0:ext, hs] += lax.dot_general(ds, qn_j, _TN, preferred_element_type=F32)
                dkr_ref[0:ext, :] += lax.dot_general(ds, qr_j, _TN, preferred_element_type=F32)
                dv_ref[0:ext, hs] += lax.dot_general(p.astype(BF16), dob, _TN, preferred_element_type=F32)
                dqn_ref[:, hs] = jnp.dot(ds, kn_j, preferred_element_type=F32).astype(dqn_ref.dtype)
                dqr_heads.append(jnp.dot(ds, kr, preferred_element_type=F32))
            dqr_ref[...] = jnp.where(masks[0], dqr_heads[0], dqr_heads[1])

        _for_key_extent(qi, seq, attend)

    return pl.pallas_call(
        body, name="attn_bwd", grid=grid, in_specs=[s_qn, s_qr, s_kn, s_kr, s_kn, s_qn, s_qr, s_qn],
        out_specs=[s_qn, s_qr, s_kn, s_kr, s_kn],
        out_shape=[jax.ShapeDtypeStruct(qn.shape, BF16)] + [jax.ShapeDtypeStruct(a.shape, F32) for a in (qr, kn, kr, v)],
        compiler_params=_cparams(),
    )(qn, qr, kn, kr, v, o, lse, do)


def _loss_head(y, target):
    t, d = y.shape
    tr = _rows(t, 512)

    def body(y_ref, t_ref, l_ref, dy_ref):
        err = y_ref[...] - t_ref[...]
        dy_ref[...] = err * (1.0 / d)
        part = 0.5 * jnp.sum(jnp.sum(err * err, axis=1, keepdims=True), axis=0, keepdims=True) * (1.0 / d)
        l_ref[...] = jnp.broadcast_to(part, l_ref.shape)

    row = pl.BlockSpec((tr, d), lambda i: (i, 0))
    parts, dy = pl.pallas_call(
        body, name="loss_head", grid=(t // tr,), in_specs=[row, row],
        out_specs=[pl.BlockSpec((8, LANE), lambda i: (i, 0)), row],
        out_shape=[jax.ShapeDtypeStruct((8 * (t // tr), LANE), F32), jax.ShapeDtypeStruct((t, d), F32)],
        compiler_params=_cparams(),
    )(y, target)
    return parts, dy


def _sum_parts(stack, name):
    n, r, c = stack.shape
    tr = _rows(r, 512)

    def body(s_ref, o_ref):
        acc = s_ref[0].astype(F32)
        for i in range(1, n):
            acc = acc + s_ref[i].astype(F32)
        o_ref[...] = acc

    return pl.pallas_call(
        body, name=name, grid=(r // tr,), in_specs=[pl.BlockSpec((n, tr, c), lambda i: (0, i, 0))],
        out_specs=pl.BlockSpec((tr, c), lambda i: (i, 0)), out_shape=jax.ShapeDtypeStruct((r, c), F32),
        compiler_params=_cparams(),
    )(stack)


def _adamw(w, g_mine, g_other, m, v, name):
    r, c = w.shape
    tr = _rows(r, 256)
    bc1 = 1.0 / (1.0 - ADAM_B1 ** ADAM_STEP)
    bc2 = 1.0 / (1.0 - ADAM_B2 ** ADAM_STEP)
    two = g_other is not None

    def body(*refs):
        if two:
            w_ref, g_ref, g2_ref, m_ref, v_ref, go_ref, d_ref, mo_ref, vo_ref = refs
            g = g_ref[...] + g2_ref[...]
        else:
            w_ref, g_ref, m_ref, v_ref, go_ref, d_ref, mo_ref, vo_ref = refs
            g = g_ref[...]
        mn = ADAM_B1 * m_ref[...] + (1.0 - ADAM_B1) * g
        vn = ADAM_B2 * v_ref[...] + (1.0 - ADAM_B2) * (g * g)
        go_ref[...] = g
        mo_ref[...] = mn
        vo_ref[...] = vn
        d_ref[...] = -ADAM_LR * ((mn * bc1) / (jnp.sqrt(vn * bc2) + ADAM_EPS) + ADAM_WD * w_ref[...])

    blk = pl.BlockSpec((tr, c), lambda i: (i, 0))
    ins = [w, g_mine] + ([g_other] if two else []) + [m, v]
    return pl.pallas_call(
        body, name=name, grid=(r // tr,), in_specs=[blk] * len(ins), out_specs=[blk] * 4,
        out_shape=[jax.ShapeDtypeStruct((r, c), F32)] * 4, compiler_params=_cparams(),
    )(*ins)


def _chip_peers():
    x, y, c = lax.axis_index("x"), lax.axis_index("y"), lax.axis_index("c")
    return (x, y, c), [(1 - x, y), (x, 1 - y), (1 - x, 1 - y)]


def _chip_exchange(ins, outs, sems, scatter):
    send_sems, recv_sems, local_sems = sems
    (x, y, c), chips = _chip_peers()
    me = 2 * x + y
    n = len(ins)

    def halved(i):
        tile = 32 // ins[i].dtype.itemsize
        return not scatter and ins[i].shape[0] % (2 * tile) == 0

    def half(i, core):
        h = ins[i].shape[0] // 2
        return pl.ds(pl.multiple_of(core * h, 8), h)

    def src(i, chip):
        if scatter:
            return ins[i].at[chip]
        return ins[i].at[half(i, c)] if halved(i) else ins[i]

    def slot(i, chip, core):
        return outs[i].at[chip, half(i, core)] if halved(i) else outs[i].at[chip]

    def local(i):
        return pltpu.make_async_copy(ins[i].at[me] if scatter else ins[i], outs[i].at[me], local_sems.at[i])

    def remote(i, j, piece, chip):
        px, py = chips[j]
        return pltpu.make_async_remote_copy(src_ref=src(i, piece), dst_ref=slot(i, chip, c), send_sem=send_sems.at[i, j],
                                            recv_sem=recv_sems.at[i, j], device_id=(px, py, c), device_id_type=MESH)

    def passed_on(i, j, core):
        px, py = chips[j]
        ref = slot(i, 2 * px + py, core)
        return pltpu.make_async_remote_copy(src_ref=ref, dst_ref=ref, send_sem=send_sems.at[i, 3 + j],
                                            recv_sem=recv_sems.at[i, 3 + j], device_id=(x, y, 1 - c), device_id_type=MESH)

    def start():
        for i in range(n):
            local(i).start()
            for j, (px, py) in enumerate(chips):
                remote(i, j, 2 * px + py, me).start()

    def wait():
        for i in range(n):
            for j, (px, py) in enumerate(chips):
                remote(i, j, me, 2 * px + py).wait_recv()
                if halved(i):
                    passed_on(i, j, c).start()
        for i in range(n):
            for j, (px, py) in enumerate(chips):
                if halved(i):
                    passed_on(i, j, 1 - c).wait_recv()
                    passed_on(i, j, c).wait_send()
                remote(i, j, 2 * px + py, me).wait_send()
            local(i).wait()

    return start, wait


def _exchange_sems(n):
    return [pltpu.SemaphoreType.DMA((n, 6)), pltpu.SemaphoreType.DMA((n, 6)), pltpu.SemaphoreType.DMA((n,))]


def _exchange_shapes(arrs, scatter):
    return [jax.ShapeDtypeStruct(s.shape if scatter else (N_CHIPS,) + s.shape, s.dtype) for s in arrs]


def _exchange_call(name, arrs, scatter):
    n = len(arrs)

    def body(*refs):
        start, wait = _chip_exchange(refs[:n], refs[n:2 * n], refs[2 * n:], scatter)
        start()
        wait()

    any_spec = pl.BlockSpec(memory_space=pl.ANY)
    return pl.pallas_call(body, name=name, in_specs=[any_spec] * n, out_specs=[any_spec] * n,
                          out_shape=_exchange_shapes(arrs, scatter), scratch_shapes=_exchange_sems(n))(*arrs)


def _gather_chips(shards, name="gather_chips"):
    return _exchange_call(name, shards, False)


def _scatter_chips(stacks, name="scatter_chips"):
    return _exchange_call(name, stacks, True)


def _swap_cores(arrs):
    n = len(arrs)

    def body(*refs):
        ins, outs = refs[:n], refs[n:2 * n]
        send_sems, recv_sems = refs[2 * n:]
        x, y, c = lax.axis_index("x"), lax.axis_index("y"), lax.axis_index("c")
        cps = []
        for i in range(n):
            cp = pltpu.make_async_remote_copy(src_ref=ins[i], dst_ref=outs[i], send_sem=send_sems.at[i],
                                              recv_sem=recv_sems.at[i], device_id=(x, y, 1 - c), device_id_type=MESH)
            cp.start()
            cps.append(cp)
        for cp in cps:
            cp.wait()

    any_spec = pl.BlockSpec(memory_space=pl.ANY)
    return pl.pallas_call(
        body, name="swap_cores", in_specs=[any_spec] * n, out_specs=[any_spec] * n,
        out_shape=[jax.ShapeDtypeStruct(s.shape, s.dtype) for s in arrs],
        scratch_shapes=[pltpu.SemaphoreType.DMA((n,)), pltpu.SemaphoreType.DMA((n,))],
    )(*arrs)


def _gather_all(block):
    m_per, n = block.shape

    def body(x_ref, out_ref, send_sems, recv_sems, local_sem):
        x, y, c = lax.axis_index("x"), lax.axis_index("y"), lax.axis_index("c")
        me, sibling = (x, y, c), (x, y, 1 - c)
        chips = [(1 - x, y), (x, 1 - y), (1 - x, 1 - y)]

        def rows(px, py, pc):
            return out_ref.at[pl.ds((4 * px + 2 * py + pc) * m_per, m_per), :]

        def copy(k, blk, to, src=None):
            return pltpu.make_async_remote_copy(src_ref=rows(*blk) if src is None else src, dst_ref=rows(*blk),
                                                send_sem=send_sems.at[k], recv_sem=recv_sems.at[k], device_id=to,
                                                device_id_type=MESH)

        mine = pltpu.make_async_copy(x_ref, rows(*me), local_sem)
        mine.start()
        first = [copy(0, me, sibling, src=x_ref)]
        first += [copy(1 + j, me, (*chip, c), src=x_ref) for j, chip in enumerate(chips)]
        for cp in first:
            cp.start()
        passed = [copy(4 + j, (*chip, c), sibling) for j, chip in enumerate(chips)]
        for j, chip in enumerate(chips):
            copy(1 + j, (*chip, c), me).wait_recv()
            passed[j].start()
        copy(0, sibling, me).wait_recv()
        for j, chip in enumerate(chips):
            copy(4 + j, (*chip, 1 - c), me).wait_recv()
        for cp in first + passed:
            cp.wait_send()
        mine.wait()

    return pl.pallas_call(
        body, name="gather_all", out_shape=jax.ShapeDtypeStruct((N_DEV * m_per, n), block.dtype),
        in_specs=[pl.BlockSpec(memory_space=pltpu.VMEM)], out_specs=pl.BlockSpec(memory_space=pltpu.VMEM),
        scratch_shapes=[pltpu.SemaphoreType.DMA((7,)), pltpu.SemaphoreType.DMA((7,)), pltpu.SemaphoreType.DMA],
    )(block)


def _rope_tables(positions):
    inv_freq = 1.0 / (ROPE_THETA ** (jnp.arange(0, MLA_ROPE, 2, dtype=F32) / MLA_ROPE))
    ang = positions.astype(F32).reshape(-1, 1) * inv_freq
    return jnp.tile(jnp.cos(ang), (1, 4)), jnp.tile(jnp.sin(ang), (1, 4))


def _pad_cols(a, n):
    return jnp.pad(a, ((0, 0), (0, n - a.shape[1])))


def _ffn_fwd(l, h, w, nb, seq):
    ug = _mm(h, w["ffn_up_g"][l], name=f"ffn{l}_up_g")
    uv = _mm(h, w["ffn_up_v"][l], name=f"ffn{l}_up_v")
    op = _op_conv(f"ffn{l}_conv", _f_convffn, [ug, uv], [w["ffn_cw_g"][l], w["ffn_cw_v"][l]],
                  [w["ffn_cb_g"][l], w["ffn_cb_v"][l]], nb, seq)
    act = op.fwd((BF16,))[0]
    ff = _mm(act, w["ffn_down"][l], name=f"ffn{l}_down")
    return ff, (h, op, act)


def _ffn_bwd(l, saved, dff, w, grads, dh_skip):
    h, op, act = saved
    grads[f"ffn_down{l}"] = _mm(act, dff, ta=True, name=f"ffn{l}_down_dw")
    dact = _mm(dff, w["ffn_down"][l], tb=True, name=f"ffn{l}_down_dx")
    (dug, duv), (dwg, dwv, dbg, dbv) = op.bwd([dact], [BF16, BF16])
    grads[f"ffn_up_g{l}"] = _mm(h, dug, ta=True, name=f"ffn{l}_up_g_dw")
    grads[f"ffn_up_v{l}"] = _mm(h, duv, ta=True, name=f"ffn{l}_up_v_dw")
    grads[f"ffn_cw_g{l}"], grads[f"ffn_cw_v{l}"], grads[f"ffn_cb_g{l}"], grads[f"ffn_cb_v{l}"] = dwg, dwv, dbg, dbv
    dh = _mm(dug, w["ffn_up_g"][l], tb=True, add=dh_skip, name=f"ffn{l}_up_g_dx")
    return _mm(duv, w["ffn_up_v"][l], tb=True, add=dh, name=f"ffn{l}_up_v_dx")


class _Carried:
    def __init__(self, shards, finish_weights, ffn_shards, set_ffn, grad_stacks):
        self.shards, self.finish_weights, self.ffn_shards, self.set_ffn = shards, finish_weights, ffn_shards, set_ffn
        self.grad_stacks, self.received = grad_stacks, None


def _local_step(x, positions, target, w, carried=None):
    nb, seq, d = x.shape
    t = nb * seq
    x2, tgt2 = x.reshape(t, d), target.reshape(t, d)
    cos, sin = _rope_tables(positions)
    grads = {}

    xb = x2.astype(BF16)

    z = _mm(xb, w["in_z"], name="ssd_in_z")
    raw = [_mm(xb, w[k], name="ssd_" + k) for k in ("in_x", "in_b", "in_c")]
    dt_raw = _mm(xb, w["in_dt"], name="ssd_in_dt")
    conv_ops = [_op_conv("ssd_conv_" + s, _f_convsilu, [r], [w["conv_w_" + s]], [w["conv_b_" + s]], nb, seq)
                for s, r in zip("xbc", raw)]
    xs, bm, cm = [op.fwd()[0] for op in conv_ops]
    dt_op = _op_dt("ssd_dt", dt_raw, w["dt_bias"], w["a_log"])
    dt, cum, ecum, wend = dt_op.fwd()

    def to_rows(v):
        v = jnp.swapaxes(v.reshape(nb, seq, LANE), 1, 2)[:, :SSD_H, :].reshape(nb * SSD_G, SSD_K, seq)
        return jnp.pad(v, ((0, 0), (0, SSD_KPAD - SSD_K), (0, 0)))

    def from_rows(v):
        v = v[:, :SSD_K, :].reshape(nb, SSD_H, seq)
        return jnp.swapaxes(jnp.pad(v, ((0, 0), (0, LANE - SSD_H), (0, 0))), 1, 2).reshape(t, LANE)

    ssd_rows = jnp.concatenate([to_rows(cum), to_rows(dt), to_rows(ecum), to_rows(wend)], axis=1)
    y, states, *gathered = _ssd_scan_fwd(xs, bm, cm, ssd_rows, nb, seq, carried.shards if carried else ())
    if carried:
        w = {**w, **carried.finish_weights(gathered)}
    post_op = _op_ssd_post("ssd_post", y, xs, z, w["d_exp"], w["norm_g"])
    yn = post_op.fwd((BF16,))[0]
    mix0 = _mm(yn, w["ssd_out"], name="ssd_out")
    ln0m = _op_ln("ln_mix0", x2, mix0, w["ln_mix_g"][0], w["ln_mix_b"][0])
    h0a, h0a_b = ln0m.fwd((F32, BF16))
    ff0, ffn0_saved = _ffn_fwd(0, h0a_b, w, nb, seq)
    ln0f = _op_ln("ln_ffn0", h0a, ff0, w["ln_ffn_g"][0], w["ln_ffn_b"][0])
    h1, h1_b = ln0f.fwd((F32, BF16))

    down = _mm(h1_b, w["kvq_down"], name="kvq_down")
    kvn_op = _op_rms("kv_norm", down, w["kv_norm_g"], window=(0, MLA_KV_RANK))
    ckvn = kvn_op.fwd((BF16,))[0]
    kr_op = _op_rope("k_rope", _f_rope_dup, down, cos, sin, window=(MLA_KV_RANK // LANE, 1))
    kr = kr_op.fwd((BF16,))[0]
    kn = _mm(ckvn, w["kv_up_k"], out_dtype=BF16, name="kv_up_k")
    v = _mm(ckvn, w["kv_up_v"], out_dtype=BF16, name="kv_up_v")
    qn_op = _op_rms("q_norm", down, w["q_norm_g"], window=(1, MLA_Q_RANK))
    cq = qn_op.fwd((BF16,))[0]
    qn = _mm(cq, w["q_up_n"], out_dtype=BF16, name="q_up_n")
    qr_raw = _mm(cq, w["q_up_r"], name="q_up_r")
    qr_op = _op_rope("q_rope", _f_rope, qr_raw, cos, sin)
    qr = qr_op.fwd((BF16,))[0]
    o, lse, *gathered = _attn_fwd(qn, qr, kn, kr, v, nb, seq, carried.ffn_shards if carried else ())
    if carried:
        carried.set_ffn(w, gathered)
    mix1 = _mm(o, w["attn_out"], name="attn_out")
    ln1m = _op_ln("ln_mix1", h1, mix1, w["ln_mix_g"][1], w["ln_mix_b"][1])
    h1a, h1a_b = ln1m.fwd((F32, BF16))
    ff1, ffn1_saved = _ffn_fwd(1, h1a_b, w, nb, seq)
    ln1f = _op_ln("ln_ffn1", h1a, ff1, w["ln_ffn_g"][1], w["ln_ffn_b"][1])
    h2 = ln1f.fwd()[0]

    loss_parts, dh2 = _loss_head(h2, tgt2)

    (dh1a, dff1), (grads["ln_ffn_g1"], grads["ln_ffn_b1"]) = ln1f.bwd([dh2], [F32, BF16])
    dh1a = _ffn_bwd(1, ffn1_saved, dff1, w, grads, dh1a)
    (dh1, dmix1), (grads["ln_mix_g1"], grads["ln_mix_b1"]) = ln1m.bwd([dh1a], [F32, BF16])
    grads["attn_out"] = _mm(o, dmix1, ta=True, name="attn_out_dw")
    do = _mm(dmix1, w["attn_out"], tb=True, name="attn_out_dx")
    dqn, dqr, dkn, dkr, dv = _attn_bwd(qn, qr, kn, kr, v, o, lse, do, nb, seq)
    (dqr_raw,), _ = qr_op.bwd([dqr], [BF16])
    grads["q_up_n"] = _mm(cq, dqn, ta=True, name="q_up_n_dw")
    grads["q_up_r"] = _mm(cq, dqr_raw, ta=True, name="q_up_r_dw")
    dcq = _mm(dqn, w["q_up_n"], tb=True, name="q_up_n_dx")
    dcq = _mm(dqr_raw, w["q_up_r"], tb=True, add=dcq, name="q_up_r_dx")
    (dcq_raw,), (grads["q_norm_g"],) = qn_op.bwd([dcq], [BF16])
    grads["kv_up_k"] = _mm(ckvn, dkn, ta=True, name="kv_up_k_dw")
    grads["kv_up_v"] = _mm(ckvn, dv, ta=True, name="kv_up_v_dw")
    dckvn = _mm(dkn, w["kv_up_k"], tb=True, name="kv_up_k_dx")
    dckvn = _mm(dv, w["kv_up_v"], tb=True, add=dckvn, name="kv_up_v_dx")
    (dckv,), (grads["kv_norm_g"],) = kvn_op.bwd([dckvn], [BF16])
    (dkr_in,), _ = kr_op.bwd([dkr], [BF16])
    ddown = jnp.concatenate([dckv, dkr_in, dcq_raw], axis=1)
    grads["kvq_down"] = _mm(h1_b, ddown, ta=True, name="kvq_down_dw")
    dh1 = _mm(ddown, w["kvq_down"], tb=True, add=dh1, name="kvq_down_dx")

    (dh0a, dff0), (grads["ln_ffn_g0"], grads["ln_ffn_b0"]) = ln0f.bwd([dh1], [F32, BF16])
    dh0a = _ffn_bwd(0, ffn0_saved, dff0, w, grads, dh0a)
    (dx, dmix0), (grads["ln_mix_g0"], grads["ln_mix_b0"]) = ln0m.bwd([dh0a], [F32, BF16])
    grads["ssd_out"] = _mm(yn, dmix0, ta=True, name="ssd_out_dw")
    dyn = _mm(dmix0, w["ssd_out"], tb=True, name="ssd_out_dx")
    (dy, dxs_post, dz), (grads["d_exp"], grads["norm_g"]) = post_op.bwd([dyn], [F32, F32, BF16])
    dxs, dbm, dcm, drows, *received = _ssd_scan_bwd(xs, bm, cm, ssd_rows, states, dy, dxs_post, nb, seq,
                                                    carried.grad_stacks(grads) if carried else ())
    if carried:
        carried.received = received
    d_dt, d_cum, d_ecum, d_wend = [from_rows(drows[:, o:o + SSD_KPAD]) for o in (_DT, _CUM, _ECUM, _WEND)]
    (ddt_raw,), (grads["dt_bias"], grads["a_log"]) = dt_op.bwd([d_dt, d_cum, d_ecum, d_wend], [BF16])
    draws = []
    for s, op, dout in zip("xbc", conv_ops, (dxs, dbm, dcm)):
        (dr,), (grads["conv_w_" + s], grads["conv_b_" + s]) = op.bwd([dout], [BF16])
        draws.append(dr)
    for k, dr in zip(("in_x", "in_b", "in_c"), draws):
        grads[k] = _mm(xb, dr, ta=True, name=f"ssd_{k}_dw")
        dx = _mm(dr, w[k], tb=True, add=dx, name=f"ssd_{k}_dx")
    grads["in_z"] = _mm(xb, dz, ta=True, name="ssd_in_z_dw")
    grads["in_dt"] = _mm(xb, ddt_raw, ta=True, name="ssd_in_dt_dw")
    dx = _mm(dz, w["in_z"], tb=True, add=dx, name="ssd_in_z_dx")
    dx = _mm(ddt_raw, w["in_dt"], tb=True, add=dx, name="ssd_in_dt_dx")
    return loss_parts, dx.reshape(nb, seq, d), grads


_XE, _BE, _CE = SSD_D_INNER, SSD_D_INNER + SSD_GN, SSD_D_INNER + 2 * SSD_GN
_KVR = MLA_KV_RANK + LANE


def _prep_weights(fw):
    w = {}
    f = FFN_HIDDEN
    if "ssd_in_proj" in fw:
        ip = fw["ssd_in_proj"][0]
        o = SSD_D_INNER
        w["in_z"], w["in_x"], w["in_b"], w["in_c"] = ip[:, :o], ip[:, o:o + _XE], ip[:, o + _XE:o + _BE], ip[:, o + _BE:o + _CE]
        w["in_dt"] = _pad_cols(ip[:, o + _CE:], LANE)
        cw, cb = fw["ssd_conv_w"][0], fw["ssd_conv_b"]
        for s, (lo, hi) in zip("xbc", ((0, _XE), (_XE, _BE), (_BE, _CE))):
            w["conv_w_" + s], w["conv_b_" + s] = cw[:, lo:hi], cb[:, lo:hi]
    if "ssd_dt_bias" in fw:
        w["dt_bias"], w["a_log"] = _pad_cols(fw["ssd_dt_bias"], LANE), _pad_cols(fw["ssd_A_log"], LANE)
        w["d_exp"] = jnp.repeat(fw["ssd_D"][0], SSD_P)[None, :]
        w["kv_norm_g"], w["q_norm_g"] = fw["kv_norm_g"][None, :], fw["q_norm_g"]
        w["ffn_cb_g"] = [fw["ffn_conv_b"][l:l + 1, :f] for l in range(DEPTH)]
        w["ffn_cb_v"] = [fw["ffn_conv_b"][l:l + 1, f:] for l in range(DEPTH)]
        for k in ("ln_mix_g", "ln_mix_b", "ln_ffn_g", "ln_ffn_b"):
            w[k] = [fw[k][l:l + 1] for l in range(DEPTH)]
    if "ssd_out_proj" in fw:
        w["norm_g"], w["ssd_out"] = fw["ssd_norm_g"], fw["ssd_out_proj"][0]
        kd = fw["kv_down_proj"]
        w["kvq_down"] = jnp.concatenate([_pad_cols(kd, _KVR), fw["q_down_proj"][0]], axis=1)
        w["kv_up_k"], w["kv_up_v"] = fw["kv_up_k"], fw["kv_up_v"]
        qu = fw["q_up_proj"][0].reshape(MLA_Q_RANK, MLA_H, MLA_NOPE + MLA_ROPE)
        w["q_up_n"] = qu[:, :, :MLA_NOPE].reshape(MLA_Q_RANK, MLA_H * MLA_NOPE)
        w["q_up_r"] = qu[:, :, MLA_NOPE:].reshape(MLA_Q_RANK, MLA_H * MLA_ROPE)
        w["attn_out"] = fw["attn_out_proj"][0]
        w["ffn_up_g"], w["ffn_up_v"], w["ffn_down"] = [None] * DEPTH, [None] * DEPTH, [None] * DEPTH
        for l in range(DEPTH):
            if fw["ffn_up"][l] is not None:
                _set_ffn_weights(w, l, fw["ffn_up"][l], fw["ffn_down"][l])
        w["ffn_cw_g"] = [fw["ffn_conv_w"][l][:, :f] for l in range(DEPTH)]
        w["ffn_cw_v"] = [fw["ffn_conv_w"][l][:, f:] for l in range(DEPTH)]
    return w


def _set_ffn_weights(w, l, up, down):
    w["ffn_up_g"][l], w["ffn_up_v"][l], w["ffn_down"][l] = up[:, :FFN_HIDDEN], up[:, FFN_HIDDEN:], down


def _assemble_grads(g, names):
    make = {
        "ssd_in_proj": lambda: jnp.concatenate([g["in_z"], g["in_x"], g["in_b"], g["in_c"], g["in_dt"][:, :SSD_H]], axis=1)[None],
        "ssd_conv_w": lambda: jnp.concatenate([g["conv_w_" + s] for s in "xbc"], axis=1)[None],
        "ssd_conv_b": lambda: jnp.concatenate([g["conv_b_" + s] for s in "xbc"], axis=1),
        "ssd_dt_bias": lambda: g["dt_bias"][:, :SSD_H],
        "ssd_A_log": lambda: g["a_log"][:, :SSD_H],
        "ssd_D": lambda: jnp.sum(g["d_exp"].reshape(SSD_H, SSD_P), axis=1)[None, :],
        "ssd_norm_g": lambda: g["norm_g"],
        "ssd_out_proj": lambda: g["ssd_out"][None],
        "kv_down_proj": lambda: g["kvq_down"][:, :MLA_KV_RANK + MLA_ROPE],
        "kv_norm_g": lambda: g["kv_norm_g"][0],
        "kv_up_k": lambda: g["kv_up_k"],
        "kv_up_v": lambda: g["kv_up_v"],
        "q_down_proj": lambda: g["kvq_down"][None, :, _KVR:],
        "q_norm_g": lambda: g["q_norm_g"],
        "q_up_proj": lambda: jnp.concatenate([g["q_up_n"].reshape(MLA_Q_RANK, MLA_H, MLA_NOPE),
                                              g["q_up_r"].reshape(MLA_Q_RANK, MLA_H, MLA_ROPE)], axis=2).reshape(1, MLA_Q_RANK, -1),
        "attn_out_proj": lambda: g["attn_out"][None],
        "ffn_up": lambda: jnp.stack([jnp.concatenate([g[f"ffn_up_g{l}"], g[f"ffn_up_v{l}"]], axis=1) for l in range(DEPTH)]),
        "ffn_conv_w": lambda: jnp.stack([jnp.concatenate([g[f"ffn_cw_g{l}"], g[f"ffn_cw_v{l}"]], axis=1) for l in range(DEPTH)]),
        "ffn_conv_b": lambda: jnp.concatenate([jnp.concatenate([g[f"ffn_cb_g{l}"], g[f"ffn_cb_v{l}"]], axis=1)
                                               for l in range(DEPTH)], axis=0),
        "ffn_down": lambda: jnp.stack([g[f"ffn_down{l}"] for l in range(DEPTH)]),
    }
    for k in ("ln_mix_g", "ln_mix_b", "ln_ffn_g", "ln_ffn_b"):
        make[k] = lambda k=k: jnp.concatenate([g[f"{k}{l}"] for l in range(DEPTH)], axis=0)
    return {n: make[n]() for n in names}


_WEIGHTS = ["ssd_in_proj", "ssd_conv_w", "ssd_conv_b", "ssd_dt_bias", "ssd_A_log", "ssd_D", "ssd_norm_g", "ssd_out_proj",
            "kv_down_proj", "kv_norm_g", "kv_up_k", "kv_up_v", "q_down_proj", "q_norm_g", "q_up_proj", "attn_out_proj",
            "ffn_up", "ffn_conv_w", "ffn_conv_b", "ffn_down", "ln_mix_g", "ln_mix_b", "ln_ffn_g", "ln_ffn_b"]
_COL_CUT = ["ssd_in_proj", "ssd_conv_w", "ssd_conv_b", "ssd_norm_g", "kv_up_k", "kv_up_v", "q_up_proj", "ffn_up", "ffn_conv_w"]
_ROW_CUT = ["ssd_out_proj", "kv_down_proj", "q_down_proj", "attn_out_proj", "ffn_down"]
_CUT = _COL_CUT + _ROW_CUT
_WHOLE = [n for n in _WEIGHTS if n not in _CUT]
_EARLY = ["ssd_in_proj", "ssd_conv_w", "ssd_conv_b"]
_LATE = [n for n in _CUT if n not in _EARLY]
_FFN_MATRICES = ["ffn_up", "ffn_down"]
_MXU_WEIGHTS = ["ssd_in_proj", "ssd_out_proj", "kv_down_proj", "kv_up_k", "kv_up_v", "q_down_proj", "q_up_proj",
                "attn_out_proj", "ffn_up", "ffn_down"]
_PACK_ROWS = 160


def _shard_2d(name, s):
    return s.reshape(-1, s.shape[-1])


def _unstack(name, g, shard_shape):
    if name in _COL_CUT:
        lead = shard_shape[:-1]
        return jnp.swapaxes(g, 0, 1).reshape(*lead, N_CHIPS * shard_shape[-1])
    lead, rs, c = shard_shape[:-2], shard_shape[-2], shard_shape[-1]
    n_lead = math.prod(lead)
    return jnp.swapaxes(g.reshape(N_CHIPS, n_lead, rs, c), 0, 1).reshape(*lead, N_CHIPS * rs, c)


def _stack(name, full, shard_shape):
    if name in _COL_CUT:
        cs = shard_shape[-1]
        return jnp.swapaxes(full.reshape(-1, N_CHIPS, cs), 0, 1)
    lead, rs, c = shard_shape[:-2], shard_shape[-2], shard_shape[-1]
    n_lead = math.prod(lead)
    return jnp.swapaxes(full.reshape(n_lead, N_CHIPS, rs, c), 0, 1).reshape(N_CHIPS, n_lead * rs, c)


def _pack(arrs):
    flat = jnp.concatenate([a.reshape(-1) for a in arrs])
    return jnp.pad(flat, (0, _PACK_ROWS * LANE - flat.shape[0])).reshape(_PACK_ROWS, LANE)


def _unpack(packed, like):
    flat, out, o = packed.reshape(-1), [], 0
    for a in like:
        out.append(flat[o:o + a.size].reshape(a.shape))
        o += a.size
    return out


_ARGS = ["x", "positions"] + _WEIGHTS + ["loss_target"] + ["m_" + n for n in _WEIGHTS] + ["v_" + n for n in _WEIGHTS]


def kernel(x, positions, ssd_in_proj, ssd_conv_w, ssd_conv_b, ssd_dt_bias, ssd_A_log, ssd_D, ssd_norm_g,
           ssd_out_proj, kv_down_proj, kv_norm_g, kv_up_k, kv_up_v, q_down_proj, q_norm_g, q_up_proj,
           attn_out_proj, ffn_up, ffn_conv_w, ffn_conv_b, ffn_down, ln_mix_g, ln_mix_b, ln_ffn_g, ln_ffn_b,
           loss_target, m_ssd_in_proj, m_ssd_conv_w, m_ssd_conv_b, m_ssd_dt_bias, m_ssd_A_log, m_ssd_D,
           m_ssd_norm_g, m_ssd_out_proj, m_kv_down_proj, m_kv_norm_g, m_kv_up_k, m_kv_up_v, m_q_down_proj,
           m_q_norm_g, m_q_up_proj, m_attn_out_proj, m_ffn_up, m_ffn_conv_w, m_ffn_conv_b, m_ffn_down,
           m_ln_mix_g, m_ln_mix_b, m_ln_ffn_g, m_ln_ffn_b, v_ssd_in_proj, v_ssd_conv_w, v_ssd_conv_b,
           v_ssd_dt_bias, v_ssd_A_log, v_ssd_D, v_ssd_norm_g, v_ssd_out_proj, v_kv_down_proj, v_kv_norm_g,
           v_kv_up_k, v_kv_up_v, v_q_down_proj, v_q_norm_g, v_q_up_proj, v_attn_out_proj, v_ffn_up,
           v_ffn_conv_w, v_ffn_conv_b, v_ffn_down, v_ln_mix_g, v_ln_mix_b, v_ln_ffn_g, v_ln_ffn_b):
    args = (x, positions, ssd_in_proj, ssd_conv_w, ssd_conv_b, ssd_dt_bias, ssd_A_log, ssd_D, ssd_norm_g,
            ssd_out_proj, kv_down_proj, kv_norm_g, kv_up_k, kv_up_v, q_down_proj, q_norm_g, q_up_proj,
            attn_out_proj, ffn_up, ffn_conv_w, ffn_conv_b, ffn_down, ln_mix_g, ln_mix_b, ln_ffn_g, ln_ffn_b,
            loss_target, m_ssd_in_proj, m_ssd_conv_w, m_ssd_conv_b, m_ssd_dt_bias, m_ssd_A_log, m_ssd_D,
            m_ssd_norm_g, m_ssd_out_proj, m_kv_down_proj, m_kv_norm_g, m_kv_up_k, m_kv_up_v, m_q_down_proj,
            m_q_norm_g, m_q_up_proj, m_attn_out_proj, m_ffn_up, m_ffn_conv_w, m_ffn_conv_b, m_ffn_down,
            m_ln_mix_g, m_ln_mix_b, m_ln_ffn_g, m_ln_ffn_b, v_ssd_in_proj, v_ssd_conv_w, v_ssd_conv_b,
            v_ssd_dt_bias, v_ssd_A_log, v_ssd_D, v_ssd_norm_g, v_ssd_out_proj, v_kv_down_proj, v_kv_norm_g,
            v_kv_up_k, v_kv_up_v, v_q_down_proj, v_q_norm_g, v_q_up_proj, v_attn_out_proj, v_ffn_up,
            v_ffn_conv_w, v_ffn_conv_b, v_ffn_down, v_ln_mix_g, v_ln_mix_b, v_ln_ffn_g, v_ln_ffn_b)
    a = dict(zip(_ARGS, args, strict=True))

    last = DEPTH - 1

    def layers(n, late):
        if n not in _FFN_MATRICES:
            return a[n]
        return a[n][last:] if late else a[n][:last]

    def shard(n, late=False):
        s = _shard_2d(n, layers(n, late))
        return s.astype(BF16) if n in _MXU_WEIGHTS else s

    def full_weights(names, gathered, late=False):
        fw = {n: _unstack(n, g, layers(n, late).shape) for n, g in zip(names, gathered)}
        for n in _FFN_MATRICES:
            if n in fw:
                fw[n] = [None] * last + list(fw[n]) if late else list(fw[n]) + [None]
        return fw

    def set_last_ffn(w, gathered):
        fw = full_weights(_FFN_MATRICES, gathered, late=True)
        _set_ffn_weights(w, last, fw["ffn_up"][last], fw["ffn_down"][last])

    def grad_stacks(names, pieces):
        full = _assemble_grads(pieces, names)
        return [_stack(n, full[n], a[n].shape).astype(BF16) for n in names]

    fw = full_weights(_EARLY, _gather_chips([shard(n) for n in _EARLY]))
    fw.update({n: a[n] for n in _WHOLE})
    carried = _Carried([shard(n) for n in _LATE], lambda got: _prep_weights(full_weights(_LATE, got)),
                       [shard(n, late=True) for n in _FFN_MATRICES], set_last_ffn,
                       lambda pieces: grad_stacks(_LATE, pieces))

    loss_parts, grad_x, pieces = _local_step(a["x"], a["positions"], a["loss_target"], _prep_weights(fw), carried)
    loss = lax.psum(jnp.sum(loss_parts[::8, 0]), ("x", "y", "c"))

    bufs = dict(zip(_LATE, carried.received))
    bufs.update(zip(_EARLY, _scatter_chips(grad_stacks(_EARLY, pieces))))
    sums = [_sum_parts(bufs[n], "sum_chips_" + n) for n in _CUT]
    others = _swap_cores(sums)
    full = _assemble_grads(pieces, _WHOLE)
    every = _gather_all(_pack([full[n] for n in _WHOLE]))
    g_whole = _sum_parts(every.reshape(N_DEV, _PACK_ROWS, LANE), "sum_devices")

    res = {}
    for n, s, o in zip(_CUT, sums, others):
        out = _adamw(_shard_2d(n, a[n]), s, o, _shard_2d(n, a["m_" + n]), _shard_2d(n, a["v_" + n]), "adamw_" + n)
        res[n] = [r.reshape(a[n].shape) for r in out]
    whole = [a[n] for n in _WHOLE]
    out = _adamw(_pack(whole), g_whole, None, _pack([a["m_" + n] for n in _WHOLE]), _pack([a["v_" + n] for n in _WHOLE]),
                 "adamw_whole")
    for k, n in enumerate(_WHOLE):
        res[n] = [_unpack(r, whole)[k] for r in out]
    return (loss, grad_x, *[res[n][0] for n in _WEIGHTS], *[res[n][1] for n in _WEIGHTS],
            *[res[n][2] for n in _WEIGHTS], *[res[n][3] for n in _WEIGHTS])
```

```python
import functools
import math

import jax
import jax.numpy as jnp
from jax import lax
from jax.experimental import pallas as pl
from jax.experimental.pallas import tpu as pltpu

F32 = jnp.float32
BF16 = jnp.bfloat16
HIGHEST = lax.Precision.HIGHEST
MESH = pl.DeviceIdType.MESH

D_MODEL = 1024
DEPTH = 2
DN_ALPHA = (2 * DEPTH) ** 0.25
SSD_D_INNER = 2048
SSD_P = 64
SSD_H = 32
SSD_G = 8
SSD_K = 4
SSD_N = 128
SSD_L = 128
SSD_GN = SSD_G * SSD_N
MLA_H = 8
MLA_Q_RANK = 384
MLA_KV_RANK = 256
MLA_NOPE = 128
MLA_ROPE = 64
MLA_V = 128
ROPE_THETA = 10000.0
FFN_HIDDEN = 2816
LN_EPS = 1e-5
RMS_EPS = 1e-6
ADAM_LR = 0.001
ADAM_B1 = 0.9
ADAM_B2 = 0.999
ADAM_EPS = 1e-08
ADAM_WD = 0.01
ADAM_STEP = 10

N_CHIPS = 4
N_DEV = 8
LANE = 128
VMEM_LIMIT = 56 * 1024 * 1024


def _cparams(**kw):
    return pltpu.CompilerParams(vmem_limit_bytes=VMEM_LIMIT, **kw)


def _tile(dim, cap):
    best = None
    t = LANE
    while t <= min(dim, cap):
        if dim % t == 0:
            best = t
        t += LANE
    return dim if best is None else best


MM_TILE_CAP = 1408
MM_WHOLE_K = 2816
MM_VMEM_BUDGET = 40 * 1024 * 1024


def _mm_tiles(m, n, k, a_bytes, b_bytes, o_bytes, has_add):
    tm, tn = _tile(m, MM_TILE_CAP), _tile(n, MM_TILE_CAP)
    tk = k if k <= MM_WHOLE_K else _tile(k, 1024)

    def need(tm, tn):
        acc = tm * tn * 4 if tk < k else 0
        return 2 * (tm * tk * a_bytes + tk * tn * b_bytes + tm * tn * o_bytes + (tm * tn * 4 if has_add else 0)) + acc

    while need(tm, tn) > MM_VMEM_BUDGET:
        if tm >= tn and _tile(m, tm // 2) < tm:
            tm = _tile(m, tm // 2)
        elif _tile(n, tn // 2) < tn:
            tn = _tile(n, tn // 2)
        else:
            break
    return tm, tn, tk


def _mm(a, b, *, ta=False, tb=False, add=None, out_dtype=F32, name):
    m, k = (a.shape[1], a.shape[0]) if ta else a.shape
    n = b.shape[0] if tb else b.shape[1]
    assert (b.shape[1] if tb else b.shape[0]) == k
    tm, tn, tk = _mm_tiles(m, n, k, a.dtype.itemsize, b.dtype.itemsize, jnp.dtype(out_dtype).itemsize, add is not None)
    nk = k // tk
    dims = (((0 if ta else 1,), (1 if tb else 0,)), ((), ()))

    def partial_product(a_ref, b_ref):
        return lax.dot_general(a_ref[...].astype(BF16), b_ref[...].astype(BF16), dims, preferred_element_type=F32)

    def body_one(*refs):
        if add is None:
            a_ref, b_ref, o_ref = refs
            o_ref[...] = partial_product(a_ref, b_ref).astype(out_dtype)
        else:
            a_ref, b_ref, c_ref, o_ref = refs
            o_ref[...] = (partial_product(a_ref, b_ref) + c_ref[...]).astype(out_dtype)

    def body_acc(*refs):
        if add is None:
            a_ref, b_ref, o_ref, acc = refs
        else:
            a_ref, b_ref, c_ref, o_ref, acc = refs
        kk = pl.program_id(2)

        @pl.when(kk == 0)
        def _():
            acc[...] = jnp.zeros_like(acc) if add is None else c_ref[...]

        acc[...] += partial_product(a_ref, b_ref)

        @pl.when(kk == nk - 1)
        def _():
            o_ref[...] = acc[...].astype(out_dtype)

    a_spec = pl.BlockSpec((tk, tm), lambda i, j, kk: (kk, i)) if ta else pl.BlockSpec((tm, tk), lambda i, j, kk: (i, kk))
    b_spec = pl.BlockSpec((tn, tk), lambda i, j, kk: (j, kk)) if tb else pl.BlockSpec((tk, tn), lambda i, j, kk: (kk, j))
    o_spec = pl.BlockSpec((tm, tn), lambda i, j, kk: (i, j))
    ins, specs = [a, b], [a_spec, b_spec]
    if add is not None:
        ins.append(add)
        specs.append(o_spec)
    return pl.pallas_call(
        body_one if nk == 1 else body_acc, name=name, grid=(m // tm, n // tn, nk), in_specs=specs, out_specs=o_spec,
        out_shape=jax.ShapeDtypeStruct((m, n), out_dtype),
        scratch_shapes=[] if nk == 1 else [pltpu.VMEM((tm, tn), F32)],
        compiler_params=_cparams(dimension_semantics=("parallel", "parallel", "arbitrary")),
    )(*ins)


def _spec(op):
    return pl.BlockSpec(op[1], op[2])


def _bw_fwd(name, fn, grid, ins, outs, out_dtypes):
    n_in = len(ins)
    flat = [(o, dt) for o, dts in zip(outs, out_dtypes) for dt in dts]

    def body(*refs):
        res = fn(*[r[...].astype(F32) for r in refs[:n_in]])
        orefs = iter(refs[n_in:])
        for v, dts in zip(res, out_dtypes):
            for dt in dts:
                next(orefs)[...] = v.astype(dt)

    return pl.pallas_call(
        body, name=name, grid=grid, in_specs=[_spec(o) for o in ins],
        out_specs=[pl.BlockSpec(o[1], o[2]) for o, _ in flat],
        out_shape=[jax.ShapeDtypeStruct(o[0], dt) for o, dt in flat], compiler_params=_cparams(),
    )(*[o[0] for o in ins])


def _bw_bwd(name, fn, grid, data, params, consts, cts, red_axes, grad_dtypes):
    nd, npar, nc, nct = len(data), len(params), len(consts), len(cts)

    def body(*refs):
        first = None
        for ax in red_axes:
            z = pl.program_id(ax) == 0
            first = z if first is None else jnp.logical_and(first, z)
        vals = [r[...].astype(F32) for r in refs[:nd + npar + nc + nct]]
        d, p, c, g = vals[:nd], vals[nd:nd + npar], vals[nd + npar:nd + npar + nc], vals[nd + npar + nc:]
        _, vjp = jax.vjp(lambda dd, pp: tuple(fn(*dd, *pp, *c)), d, p)
        gd, gp = vjp(tuple(g))
        orefs = refs[nd + npar + nc + nct:]
        for r, v in zip(orefs[:nd], gd):
            r[...] = v.astype(r.dtype)
        if npar:
            @pl.when(first)
            def _():
                for r in orefs[nd:]:
                    r[...] = jnp.zeros_like(r)

            for r, v in zip(orefs[nd:], gp):
                r[...] += v

    ins = [d[:3] for d in data] + list(params) + list(consts) + list(cts)
    outs = [d[3] if len(d) > 3 else d for d in data] + list(params)
    dtypes = list(grad_dtypes) + [F32] * npar
    return pl.pallas_call(
        body, name=name, grid=grid, in_specs=[_spec(o) for o in ins], out_specs=[_spec(o) for o in outs],
        out_shape=[jax.ShapeDtypeStruct(o[0].shape, dt) for o, dt in zip(outs, dtypes)], compiler_params=_cparams(),
    )(*[o[0] for o in ins])


def _shift_down(x, s):
    row = lax.broadcasted_iota(jnp.int32, x.shape, 0)
    return jnp.where(row < s, 0.0, pltpu.roll(x, s, 0))


def _shift_up(x, s):
    n = x.shape[0]
    row = lax.broadcasted_iota(jnp.int32, x.shape, 0)
    return jnp.where(row >= n - s, 0.0, pltpu.roll(x, n - s, 0))


def _time_shift(s):
    if s == 0:
        return lambda x: x

    @jax.custom_vjp
    def shift(x):
        return _shift_down(x, s)

    shift.defvjp(lambda x: (_shift_down(x, s), None), lambda _, g: (_shift_up(g, s),))
    return shift


def _rot_half_raw(x):
    lane = lax.broadcasted_iota(jnp.int32, x.shape, 1)
    return jnp.where(lane % MLA_ROPE < MLA_ROPE // 2, -pltpu.roll(x, LANE - MLA_ROPE // 2, 1), pltpu.roll(x, MLA_ROPE // 2, 1))


@jax.custom_vjp
def _rot_half(x):
    return _rot_half_raw(x)


_rot_half.defvjp(lambda x: (_rot_half_raw(x), None), lambda _, g: (-_rot_half_raw(g),))


@jax.custom_vjp
def _roll_half_lanes(x):
    return pltpu.roll(x, LANE // 2, 1)


_roll_half_lanes.defvjp(lambda x: (pltpu.roll(x, LANE // 2, 1), None), lambda _, g: (pltpu.roll(g, LANE // 2, 1),))


def _causal_conv(u, w, b):
    width = w.shape[0]
    y = b
    for k in range(width):
        y = y + w[k:k + 1, :] * _time_shift(width - 1 - k)(u)
    return y


def _silu(x):
    return x * jax.nn.sigmoid(x)


def _f_ln(h, mix, g, b):
    x = DN_ALPHA * h + mix
    mu = jnp.mean(x, axis=-1, keepdims=True)
    xc = x - mu
    var = jnp.mean(xc * xc, axis=-1, keepdims=True)
    return (xc * lax.rsqrt(var + LN_EPS) * g + b,)


def _f_convsilu(u, w, b):
    return (_silu(_causal_conv(u, w, b)),)


def _f_convffn(ug, uv, wg, wv, bg, bv):
    return (_silu(_causal_conv(ug, wg, bg)) * _causal_conv(uv, wv, bv),)


def _f_dt(dt_raw, bias, a_log):
    x = dt_raw + bias
    dt = jnp.maximum(x, 0.0) + jnp.log(1.0 + jnp.exp(-jnp.abs(x)))
    a = dt * (-jnp.exp(a_log))
    n = a.shape[0]
    lower = (lax.broadcasted_iota(jnp.int32, (n, n), 0) >= lax.broadcasted_iota(jnp.int32, (n, n), 1)).astype(F32)
    cum = jnp.dot(lower, a, precision=HIGHEST, preferred_element_type=F32)
    cum_last = jnp.sum(a, axis=0, keepdims=True)
    return dt, cum, jnp.exp(cum), jnp.exp(cum_last - cum) * dt


def _f_ssd_post(y, xs, z, d_exp, ng):
    t = (y + d_exp * xs) * _silu(z)
    return (t * lax.rsqrt(jnp.mean(t * t, axis=-1, keepdims=True) + LN_EPS) * ng,)


def _f_rms(x, g):
    return (x * lax.rsqrt(jnp.mean(x * x, axis=-1, keepdims=True) + RMS_EPS) * g,)


def _f_rope(x, cos, sin):
    return (x * cos + _rot_half(x) * sin,)


def _f_rope_dup(x, cos, sin):
    r = x * cos + _rot_half(x) * sin
    return (r + _roll_half_lanes(r),)


def _rows(t, cap=512):
    for c in (cap, 256, 128, 64, 32, 16, 8):
        if c <= cap and t % c == 0:
            return c
    return t


class _Blockwise:
    def __init__(self, name, fn, grid, data, params, consts, outs, red_axes):
        self.name, self.fn, self.grid = name, fn, grid
        self.data, self.params, self.consts, self.outs, self.red_axes = data, params, consts, outs, red_axes

    def fwd(self, *out_dtypes):
        out_dtypes = out_dtypes or tuple((F32,) for _ in self.outs)
        return _bw_fwd(self.name + "_fwd", self.fn, self.grid, self.data + self.params + self.consts, self.outs, out_dtypes)

    def bwd(self, cts, grad_dtypes=None):
        cts = [(c, o[1], o[2]) for c, o in zip(cts, self.outs)]
        grad_dtypes = grad_dtypes or [F32] * len(self.data)
        res = _bw_bwd(self.name + "_bwd", self.fn, self.grid, self.data, self.params, self.consts, cts, self.red_axes,
                      grad_dtypes)
        return res[:len(self.data)], res[len(self.data):]


def _op_ln(name, h, mix, g, b):
    t, d = h.shape
    tr = _rows(t, 256)
    row = ((tr, d), lambda i: (i, 0))
    par = ((1, d), lambda i: (0, 0))
    return _Blockwise(name, _f_ln, (t // tr,), [(h, *row), (mix, *row)], [(g, *par), (b, *par)], [],
                      [((t, d), *row)], (0,))


def _op_conv(name, fn, us, ws, bs, nb, seq):
    c = us[0].shape[1]
    ct = _tile(c, 256)
    blk = ((seq, ct), lambda j, bb: (bb, j))
    data = [(u, *blk) for u in us]
    params = [(w, (w.shape[0], ct), lambda j, bb: (0, j)) for w in ws] + [(b, (1, ct), lambda j, bb: (0, j)) for b in bs]
    return _Blockwise(name, fn, (c // ct, nb), data, params, [], [((nb * seq, c), *blk)], (1,))


def _op_dt(name, dt_raw, bias, a_log):
    t = dt_raw.shape[0]
    row = ((SSD_L, LANE), lambda i: (i, 0))
    par = ((1, LANE), lambda i: (0, 0))
    return _Blockwise(name, _f_dt, (t // SSD_L,), [(dt_raw, *row)], [(bias, *par), (a_log, *par)], [],
                      [((t, LANE), *row)] * 4, (0,))


def _op_ssd_post(name, y, xs, z, d_exp, ng):
    t, c = y.shape
    gw = c // SSD_G
    tr = _rows(t, 512)
    blk = ((tr, gw), lambda g, i: (i, g))
    par = ((1, gw), lambda g, i: (0, g))
    return _Blockwise(name, _f_ssd_post, (SSD_G, t // tr), [(y, *blk), (xs, *blk), (z, *blk)],
                      [(d_exp, *par), (ng, *par)], [], [((t, c), *blk)], (1,))


def _op_rms(name, x, g, window=None):
    t = x.shape[0]
    j, c = window or (0, x.shape[1])
    tr = _rows(t, 512)
    own = ((tr, c), lambda i: (i, 0))
    return _Blockwise(name, _f_rms, (t // tr,), [(x, (tr, c), lambda i: (i, j), (jax.ShapeDtypeStruct((t, c), F32), *own))],
                      [(g, (1, c), lambda i: (0, 0))], [], [((t, c), *own)], (0,))


def _op_rope(name, fn, x, cos, sin, window=None):
    t = x.shape[0]
    j0, n = window or (0, x.shape[1] // LANE)
    tr = _rows(t, 512)
    own = ((tr, LANE), lambda i, j: (i, j))
    cs = ((tr, LANE), lambda i, j: (i, 0))
    return _Blockwise(name, fn, (t // tr, n), [(x, (tr, LANE), lambda i, j: (i, j0 + j), (jax.ShapeDtypeStruct((t, n * LANE), F32), *own))],
                      [], [(cos, *cs), (sin, *cs)], [((t, n * LANE), *own)], ())


_NT = (((1,), (1,)), ((), ()))
_TN = (((0,), (0,)), ((), ()))


def _ssd_head(x, g, bc, cc, s, cum_row, dt_row, cum_col, ecum_col, wend_col):
    n = g.shape[0]
    row = lax.broadcasted_iota(jnp.int32, (n, n), 0)
    col = lax.broadcasted_iota(jnp.int32, (n, n), 1)
    decay = jnp.exp(jnp.where(row >= col, cum_col - cum_row, -jnp.inf))
    w = g * decay * dt_row
    y = jnp.dot(w.astype(BF16), x.astype(BF16), preferred_element_type=F32)
    y = y + lax.dot_general((cc * ecum_col).astype(BF16), s.astype(BF16), _NT, preferred_element_type=F32)
    lane = lax.broadcasted_iota(jnp.int32, cum_row.shape, 1)
    cum_last = jnp.sum(jnp.where(lane == n - 1, cum_row, 0.0), axis=1, keepdims=True)
    s_new = s * jnp.exp(cum_last) + lax.dot_general(x.astype(BF16), (bc * wend_col).astype(BF16), _TN,
                                                    preferred_element_type=F32)
    return y, s_new


SSD_KPAD = 8
SSD_ROWS = 4 * SSD_KPAD
_CUM, _DT, _ECUM, _WEND = 0, SSD_KPAD, 2 * SSD_KPAD, 3 * SSD_KPAD


def _ssd_head_args(rows, cols, k):
    return (rows[_CUM + k:_CUM + k + 1], rows[_DT + k:_DT + k + 1], cols[:, _CUM + k:_CUM + k + 1],
            cols[:, _ECUM + k:_ECUM + k + 1], cols[:, _WEND + k:_WEND + k + 1])


def _carried_exchange(arrs, in_refs, out_refs, sems, scatter, n_steps):
    if not arrs:
        return lambda: None, lambda: None
    first = functools.reduce(jnp.logical_and, [pl.program_id(ax) == 0 for ax in range(len(n_steps))])
    last = functools.reduce(jnp.logical_and, [pl.program_id(ax) == n - 1 for ax, n in enumerate(n_steps)])
    start, wait = _chip_exchange(in_refs, out_refs, sems, scatter)
    return (lambda: pl.when(first)(start)), (lambda: pl.when(last)(wait))


def _ssd_scan_fwd(xs, bm, cm, rows, nb, seq, gather=()):
    nc = seq // SSD_L
    kp = SSD_K * SSD_P
    ng = len(gather)

    def body(*refs):
        xs_ref, b_ref, c_ref, row_ref = refs[:4]
        y_ref, st_ref = refs[4 + ng:6 + ng]
        start, wait = _carried_exchange(gather, refs[4:4 + ng], refs[6 + ng:6 + 2 * ng], refs[6 + 2 * ng:], False,
                                        (nb, SSD_G))
        start()

        def chunk(c, states):
            sl = pl.ds(pl.multiple_of(c * SSD_L, SSD_L), SSD_L)
            bc, cc = b_ref[sl, :], c_ref[sl, :]
            g = lax.dot_general(cc.astype(BF16), bc.astype(BF16), _NT, preferred_element_type=F32)
            rows_c = row_ref[0, :, sl]
            cols_c = rows_c.T
            new = []
            for k in range(SSD_K):
                hs = pl.ds(k * SSD_P, SSD_P)
                st_ref[0, c * SSD_K + k] = states[k]
                y, s_new = _ssd_head(xs_ref[sl, hs], g, bc, cc, states[k], *_ssd_head_args(rows_c, cols_c, k))
                y_ref[sl, hs] = y
                new.append(s_new)
            return tuple(new)

        lax.fori_loop(0, nc, chunk, tuple(jnp.zeros((SSD_P, SSD_N), F32) for _ in range(SSD_K)))
        wait()

    t = xs.shape[0]
    any_spec = pl.BlockSpec(memory_space=pl.ANY)
    return pl.pallas_call(
        body, name="ssd_scan_fwd", grid=(nb, SSD_G),
        in_specs=[pl.BlockSpec((seq, kp), lambda b, g: (b, g)),
                  pl.BlockSpec((seq, SSD_N), lambda b, g: (b, g)),
                  pl.BlockSpec((seq, SSD_N), lambda b, g: (b, g)),
                  pl.BlockSpec((1, SSD_ROWS, seq), lambda b, g: (b * SSD_G + g, 0, 0))] + [any_spec] * ng,
        out_specs=[pl.BlockSpec((seq, kp), lambda b, g: (b, g)),
                   pl.BlockSpec((1, nc * SSD_K, SSD_P, SSD_N), lambda b, g: (b * SSD_G + g, 0, 0, 0))] + [any_spec] * ng,
        out_shape=[jax.ShapeDtypeStruct((t, SSD_D_INNER), F32),
                   jax.ShapeDtypeStruct((nb * SSD_G, nc * SSD_K, SSD_P, SSD_N), F32)] + _exchange_shapes(gather, False),
        scratch_shapes=_exchange_sems(ng) if ng else [],
        compiler_params=_cparams(dimension_semantics=("arbitrary", "arbitrary")),
    )(xs, bm, cm, rows, *gather)


def _ssd_scan_bwd(xs, bm, cm, rows, states, dy, dxs_skip, nb, seq, scatter=()):
    nc = seq // SSD_L
    kp = SSD_K * SSD_P
    ns = len(scatter)

    def body(*refs):
        xs_ref, b_ref, c_ref, row_ref, st_ref, dy_ref, skip_ref = refs[:7]
        dxs_ref, db_ref, dc_ref, drow_ref = refs[7 + ns:11 + ns]
        dcol_ref = refs[11 + 2 * ns]
        start, wait = _carried_exchange(scatter, refs[7:7 + ns], refs[11 + ns:11 + 2 * ns], refs[12 + 2 * ns:], True,
                                        (nb, SSD_G))
        start()

        def chunk(i, dstates):
            c = nc - 1 - i
            sl = pl.ds(pl.multiple_of(c * SSD_L, SSD_L), SSD_L)
            bc, cc = b_ref[sl, :], c_ref[sl, :]
            bcb, ccb = bc.astype(BF16), cc.astype(BF16)
            g = lax.dot_general(ccb, bcb, _NT, preferred_element_type=F32)
            rows_c = row_ref[0, :, sl]
            cols_c = rows_c.T
            drow_ref[0, :, sl] = jnp.zeros((SSD_ROWS, SSD_L), F32)
            dcol_ref[...] = jnp.zeros((SSD_L, SSD_ROWS), F32)
            db = jnp.zeros((SSD_L, SSD_N), F32)
            dc = jnp.zeros((SSD_L, SSD_N), F32)
            dg = jnp.zeros((SSD_L, SSD_L), F32)
            new = []
            for k in range(SSD_K):
                hs = pl.ds(k * SSD_P, SSD_P)
                _, vjp = jax.vjp(_ssd_head, xs_ref[sl, hs], g, bc, cc, st_ref[0, c * SSD_K + k],
                                 *_ssd_head_args(rows_c, cols_c, k))
                dx, dgk, dbk, dck, ds, d_cum_row, d_dt_row, d_cum_col, d_ecum_col, d_wend_col = vjp(
                    (dy_ref[sl, hs], dstates[k]))
                dxs_ref[sl, hs] = dx + skip_ref[sl, hs]
                db, dc, dg = db + dbk, dc + dck, dg + dgk
                drow_ref[0, _CUM + k:_CUM + k + 1, sl] = d_cum_row
                drow_ref[0, _DT + k:_DT + k + 1, sl] = d_dt_row
                dcol_ref[:, _CUM + k:_CUM + k + 1] = d_cum_col
                dcol_ref[:, _ECUM + k:_ECUM + k + 1] = d_ecum_col
                dcol_ref[:, _WEND + k:_WEND + k + 1] = d_wend_col
                new.append(ds)
            drow_ref[0, :, sl] += dcol_ref[...].T
            dgb = dg.astype(BF16)
            dc_ref[sl, :] = dc + jnp.dot(dgb, bcb, preferred_element_type=F32)
            db_ref[sl, :] = db + lax.dot_general(dgb, ccb, _TN, preferred_element_type=F32)
            return tuple(new)

        lax.fori_loop(0, nc, chunk, tuple(jnp.zeros((SSD_P, SSD_N), F32) for _ in range(SSD_K)))
        wait()

    t = xs.shape[0]
    x_spec = pl.BlockSpec((seq, kp), lambda b, g: (b, g))
    n_spec = pl.BlockSpec((seq, SSD_N), lambda b, g: (b, g))
    r_spec = pl.BlockSpec((1, SSD_ROWS, seq), lambda b, g: (b * SSD_G + g, 0, 0))
    any_spec = pl.BlockSpec(memory_space=pl.ANY)
    return pl.pallas_call(
        body, name="ssd_scan_bwd", grid=(nb, SSD_G),
        in_specs=[x_spec, n_spec, n_spec, r_spec,
                  pl.BlockSpec((1, nc * SSD_K, SSD_P, SSD_N), lambda b, g: (b * SSD_G + g, 0, 0, 0)), x_spec, x_spec]
        + [any_spec] * ns,
        out_specs=[x_spec, n_spec, n_spec, r_spec] + [any_spec] * ns,
        out_shape=[jax.ShapeDtypeStruct((t, SSD_D_INNER), F32), jax.ShapeDtypeStruct((t, SSD_GN), F32),
                   jax.ShapeDtypeStruct((t, SSD_GN), F32), jax.ShapeDtypeStruct(rows.shape, F32)]
        + _exchange_shapes(scatter, True),
        scratch_shapes=[pltpu.VMEM((SSD_L, SSD_ROWS), F32)] + (_exchange_sems(ns) if ns else []),
        compiler_params=_cparams(dimension_semantics=("arbitrary", "arbitrary")),
    )(xs, bm, cm, rows, states, dy, dxs_skip, *scatter)


ATT_TQ = 256
ATT_TE = 256
ATT_SCALE = (MLA_NOPE + MLA_ROPE) ** -0.5


def _for_key_extent(qi, seq, fn):
    te = min(ATT_TE, seq)
    per = te // ATT_TQ
    for e in range(seq // te):
        pl.when(jnp.logical_and(qi >= e * per, qi < (e + 1) * per))(functools.partial(fn, (e + 1) * te))


def _pair_masks(shape):
    lane = lax.broadcasted_iota(jnp.int32, shape, 1)
    return lane < MLA_ROPE, lane >= MLA_ROPE


def _scores(qn, qr, kn, kr, q0):
    s = lax.dot_general(qn, kn, _NT, preferred_element_type=F32) + lax.dot_general(qr, kr, _NT, preferred_element_type=F32)
    row = lax.broadcasted_iota(jnp.int32, s.shape, 0)
    col = lax.broadcasted_iota(jnp.int32, s.shape, 1)
    return jnp.where(col <= q0 + row, s * ATT_SCALE, -jnp.inf)


def _attn_specs(nb, seq):
    nq = seq // ATT_TQ
    qn = pl.BlockSpec((ATT_TQ, 2 * MLA_NOPE), lambda b, hp, qi: (b * nq + qi, hp))
    qr = pl.BlockSpec((ATT_TQ, LANE), lambda b, hp, qi: (b * nq + qi, hp))
    kn = pl.BlockSpec((seq, 2 * MLA_NOPE), lambda b, hp, qi: (b, hp))
    kr = pl.BlockSpec((seq, LANE), lambda b, hp, qi: (b, 0))
    return (nb, MLA_H // 2, nq), qn, qr, kn, kr


def _attn_fwd(qn, qr, kn, kr, v, nb, seq, gather=()):
    grid, s_qn, s_qr, s_kn, s_kr = _attn_specs(nb, seq)
    ng = len(gather)

    def body(*refs):
        qn_ref, qr_ref, kn_ref, kr_ref, v_ref = refs[:5]
        o_ref, lse_ref = refs[5 + ng:7 + ng]
        start, wait = _carried_exchange(gather, refs[5:5 + ng], refs[7 + ng:7 + 2 * ng], refs[7 + 2 * ng:], False, grid)
        start()
        qi = pl.program_id(2)

        def attend(ext):
            qr = qr_ref[...]
            masks = _pair_masks(qr.shape)
            outs, lses = [], []
            for j in range(2):
                hs = pl.ds(j * MLA_NOPE, MLA_NOPE)
                qr_j = jnp.where(masks[j], qr, jnp.zeros_like(qr))
                s = _scores(qn_ref[:, hs], qr_j, kn_ref[0:ext, hs], kr_ref[0:ext, :], qi * ATT_TQ)
                m = jnp.max(s, axis=1, keepdims=True)
                p = jnp.exp(s - m)
                l = jnp.sum(p, axis=1, keepdims=True)
                outs.append(jnp.dot(p.astype(BF16), v_ref[0:ext, hs], preferred_element_type=F32) / l)
                lses.append(m + jnp.log(l))
            o_ref[...] = jnp.concatenate(outs, axis=1).astype(o_ref.dtype)
            lse_ref[...] = jnp.where(masks[0], lses[0], lses[1])

        _for_key_extent(qi, seq, attend)
        wait()

    any_spec = pl.BlockSpec(memory_space=pl.ANY)
    return pl.pallas_call(
        body, name="attn_fwd", grid=grid, in_specs=[s_qn, s_qr, s_kn, s_kr, s_kn] + [any_spec] * ng,
        out_specs=[s_qn, s_qr] + [any_spec] * ng,
        out_shape=[jax.ShapeDtypeStruct(qn.shape, BF16), jax.ShapeDtypeStruct(qr.shape, F32)] + _exchange_shapes(gather, False),
        scratch_shapes=_exchange_sems(ng) if ng else [],
        compiler_params=_cparams(dimension_semantics=("arbitrary", "arbitrary", "arbitrary")),
    )(qn, qr, kn, kr, v, *gather)


def _attn_bwd(qn, qr, kn, kr, v, o, lse, do, nb, seq):
    grid, s_qn, s_qr, s_kn, s_kr = _attn_specs(nb, seq)

    def body(qn_ref, qr_ref, kn_ref, kr_ref, v_ref, o_ref, lse_ref, do_ref, dqn_ref, dqr_ref, dkn_ref, dkr_ref, dv_ref):
        hp, qi = pl.program_id(1), pl.program_id(2)

        @pl.when(qi == 0)
        def _():
            dkn_ref[...] = jnp.zeros_like(dkn_ref)
            dv_ref[...] = jnp.zeros_like(dv_ref)

        @pl.when(jnp.logical_and(qi == 0, hp == 0))
        def _():
            dkr_ref[...] = jnp.zeros_like(dkr_ref)

        def attend(ext):
            qr, lse = qr_ref[...], lse_ref[...]
            masks = _pair_masks(qr.shape)
            dqr_heads = []
            for j in range(2):
                hs = pl.ds(j * MLA_NOPE, MLA_NOPE)
                qn_j, qr_j = qn_ref[:, hs], jnp.where(masks[j], qr, jnp.zeros_like(qr))
                kn_j, kr, v_j = kn_ref[0:ext, hs], kr_ref[0:ext, :], v_ref[0:ext, hs]
                do_j = do_ref[:, hs]
                dob = do_j.astype(BF16)
                delta = jnp.sum(do_j * o_ref[:, hs].astype(F32), axis=1, keepdims=True)
                p = jnp.exp(_scores(qn_j, qr_j, kn_j, kr, qi * ATT_TQ) - lse[:, j * MLA_ROPE:j * MLA_ROPE + 1])
                dp = lax.dot_general(dob, v_j, _NT, preferred_element_type=F32)
                ds = (p * (dp - delta) * ATT_SCALE).astype(BF16)
                dkn_ref[0:ext, hs] += lax.dot_general(ds, qn_j, _TN, preferred_element_type=F32)
                dkr_ref[0:ext, :] += lax.dot_general(ds, qr_j, _TN, preferred_element_type=F32)
                dv_ref[0:ext, hs] += lax.dot_general(p.astype(BF16), dob, _TN, preferred_element_type=F32)
                dqn_ref[:, hs] = jnp.dot(ds, kn_j, preferred_element_type=F32).astype(dqn_ref.dtype)
                dqr_heads.append(jnp.dot(ds, kr, preferred_element_type=F32))
            dqr_ref[...] = jnp.where(masks[0], dqr_heads[0], dqr_heads[1])

        _for_key_extent(qi, seq, attend)

    return pl.pallas_call(
        body, name="attn_bwd", grid=grid, in_specs=[s_qn, s_qr, s_kn, s_kr, s_kn, s_qn, s_qr, s_qn],
        out_specs=[s_qn, s_qr, s_kn, s_kr, s_kn],
        out_shape=[jax.ShapeDtypeStruct(qn.shape, BF16)] + [jax.ShapeDtypeStruct(a.shape, F32) for a in (qr, kn, kr, v)],
        compiler_params=_cparams(),
    )(qn, qr, kn, kr, v, o, lse, do)


def _loss_head(y, target):
    t, d = y.shape
    tr = _rows(t, 512)

    def body(y_ref, t_ref, l_ref, dy_ref):
        err = y_ref[...] - t_ref[...]
        dy_ref[...] = err * (1.0 / d)
        part = 0.5 * jnp.sum(jnp.sum(err * err, axis=1, keepdims=True), axis=0, keepdims=True) * (1.0 / d)
        l_ref[...] = jnp.broadcast_to(part, l_ref.shape)

    row = pl.BlockSpec((tr, d), lambda i: (i, 0))
    parts, dy = pl.pallas_call(
        body, name="loss_head", grid=(t // tr,), in_specs=[row, row],
        out_specs=[pl.BlockSpec((8, LANE), lambda i: (i, 0)), row],
        out_shape=[jax.ShapeDtypeStruct((8 * (t // tr), LANE), F32), jax.ShapeDtypeStruct((t, d), F32)],
        compiler_params=_cparams(),
    )(y, target)
    return parts, dy


def _sum_parts(stack, name):
    n, r, c = stack.shape
    tr = _rows(r, 512)

    def body(s_ref, o_ref):
        acc = s_ref[0].astype(F32)
        for i in range(1, n):
            acc = acc + s_ref[i].astype(F32)
        o_ref[...] = acc

    return pl.pallas_call(
        body, name=name, grid=(r // tr,), in_specs=[pl.BlockSpec((n, tr, c), lambda i: (0, i, 0))],
        out_specs=pl.BlockSpec((tr, c), lambda i: (i, 0)), out_shape=jax.ShapeDtypeStruct((r, c), F32),
        compiler_params=_cparams(),
    )(stack)


def _adamw(w, g_mine, g_other, m, v, name):
    r, c = w.shape
    tr = _rows(r, 256)
    bc1 = 1.0 / (1.0 - ADAM_B1 ** ADAM_STEP)
    bc2 = 1.0 / (1.0 - ADAM_B2 ** ADAM_STEP)
    two = g_other is not None

    def body(*refs):
        if two:
            w_ref, g_ref, g2_ref, m_ref, v_ref, go_ref, d_ref, mo_ref, vo_ref = refs
            g = g_ref[...] + g2_ref[...]
        else:
            w_ref, g_ref, m_ref, v_ref, go_ref, d_ref, mo_ref, vo_ref = refs
            g = g_ref[...]
        mn = ADAM_B1 * m_ref[...] + (1.0 - ADAM_B1) * g
        vn = ADAM_B2 * v_ref[...] + (1.0 - ADAM_B2) * (g * g)
        go_ref[...] = g
        mo_ref[...] = mn
        vo_ref[...] = vn
        d_ref[...] = -ADAM_LR * ((mn * bc1) / (jnp.sqrt(vn * bc2) + ADAM_EPS) + ADAM_WD * w_ref[...])

    blk = pl.BlockSpec((tr, c), lambda i: (i, 0))
    ins = [w, g_mine] + ([g_other] if two else []) + [m, v]
    return pl.pallas_call(
        body, name=name, grid=(r // tr,), in_specs=[blk] * len(ins), out_specs=[blk] * 4,
        out_shape=[jax.ShapeDtypeStruct((r, c), F32)] * 4, compiler_params=_cparams(),
    )(*ins)


def _chip_peers():
    x, y, c = lax.axis_index("x"), lax.axis_index("y"), lax.axis_index("c")
    return (x, y, c), [(1 - x, y), (x, 1 - y), (1 - x, 1 - y)]


def _chip_exchange(ins, outs, sems, scatter, once=False):
    send_sems, recv_sems, local_sems = sems
    (x, y, c), chips = _chip_peers()
    me = 2 * x + y
    n = len(ins)

    def halved(i):
        tile = 32 // ins[i].dtype.itemsize
        return once and not scatter and ins[i].shape[0] % (2 * tile) == 0

    def half(i, core):
        h = ins[i].shape[0] // 2
        return pl.ds(pl.multiple_of(core * h, 8), h)

    def src(i, chip):
        if scatter:
            return ins[i].at[chip]
        return ins[i].at[half(i, c)] if halved(i) else ins[i]

    def slot(i, chip, core):
        return outs[i].at[chip, half(i, core)] if halved(i) else outs[i].at[chip]

    def local(i):
        return pltpu.make_async_copy(ins[i].at[me] if scatter else ins[i], outs[i].at[me], local_sems.at[i])

    def remote(i, j, piece, chip):
        px, py = chips[j]
        return pltpu.make_async_remote_copy(src_ref=src(i, piece), dst_ref=slot(i, chip, c), send_sem=send_sems.at[i, j],
                                            recv_sem=recv_sems.at[i, j], device_id=(px, py, c), device_id_type=MESH)

    def passed_on(i, j, core):
        px, py = chips[j]
        ref = slot(i, 2 * px + py, core)
        return pltpu.make_async_remote_copy(src_ref=ref, dst_ref=ref, send_sem=send_sems.at[i, 3 + j],
                                            recv_sem=recv_sems.at[i, 3 + j], device_id=(x, y, 1 - c), device_id_type=MESH)

    def start():
        for i in range(n):
            local(i).start()
            for j, (px, py) in enumerate(chips):
                remote(i, j, 2 * px + py, me).start()

    def wait():
        for i in range(n):
            for j, (px, py) in enumerate(chips):
                remote(i, j, me, 2 * px + py).wait_recv()
                if halved(i):
                    passed_on(i, j, c).start()
        for i in range(n):
            for j, (px, py) in enumerate(chips):
                if halved(i):
                    passed_on(i, j, 1 - c).wait_recv()
                    passed_on(i, j, c).wait_send()
                remote(i, j, 2 * px + py, me).wait_send()
            local(i).wait()

    return start, wait


def _exchange_sems(n):
    return [pltpu.SemaphoreType.DMA((n, 6)), pltpu.SemaphoreType.DMA((n, 6)), pltpu.SemaphoreType.DMA((n,))]


def _exchange_shapes(arrs, scatter):
    return [jax.ShapeDtypeStruct(s.shape if scatter else (N_CHIPS,) + s.shape, s.dtype) for s in arrs]


def _exchange_call(name, arrs, scatter):
    n = len(arrs)

    def body(*refs):
        start, wait = _chip_exchange(refs[:n], refs[n:2 * n], refs[2 * n:], scatter, once=True)
        start()
        wait()

    any_spec = pl.BlockSpec(memory_space=pl.ANY)
    return pl.pallas_call(body, name=name, in_specs=[any_spec] * n, out_specs=[any_spec] * n,
                          out_shape=_exchange_shapes(arrs, scatter), scratch_shapes=_exchange_sems(n))(*arrs)


def _gather_chips(shards, name="gather_chips"):
    return _exchange_call(name, shards, False)


def _scatter_chips(stacks, name="scatter_chips"):
    return _exchange_call(name, stacks, True)


def _swap_cores(arrs):
    n = len(arrs)

    def body(*refs):
        ins, outs = refs[:n], refs[n:2 * n]
        send_sems, recv_sems = refs[2 * n:]
        x, y, c = lax.axis_index("x"), lax.axis_index("y"), lax.axis_index("c")
        cps = []
        for i in range(n):
            cp = pltpu.make_async_remote_copy(src_ref=ins[i], dst_ref=outs[i], send_sem=send_sems.at[i],
                                              recv_sem=recv_sems.at[i], device_id=(x, y, 1 - c), device_id_type=MESH)
            cp.start()
            cps.append(cp)
        for cp in cps:
            cp.wait()

    any_spec = pl.BlockSpec(memory_space=pl.ANY)
    return pl.pallas_call(
        body, name="swap_cores", in_specs=[any_spec] * n, out_specs=[any_spec] * n,
        out_shape=[jax.ShapeDtypeStruct(s.shape, s.dtype) for s in arrs],
        scratch_shapes=[pltpu.SemaphoreType.DMA((n,)), pltpu.SemaphoreType.DMA((n,))],
    )(*arrs)


def _gather_all(block):
    m_per, n = block.shape

    def body(x_ref, out_ref, send_sems, recv_sems, local_sem):
        x, y, c = lax.axis_index("x"), lax.axis_index("y"), lax.axis_index("c")
        me, sibling = (x, y, c), (x, y, 1 - c)
        chips = [(1 - x, y), (x, 1 - y), (1 - x, 1 - y)]

        def rows(px, py, pc):
            return out_ref.at[pl.ds((4 * px + 2 * py + pc) * m_per, m_per), :]

        def copy(k, blk, to, src=None):
            return pltpu.make_async_remote_copy(src_ref=rows(*blk) if src is None else src, dst_ref=rows(*blk),
                                                send_sem=send_sems.at[k], recv_sem=recv_sems.at[k], device_id=to,
                                                device_id_type=MESH)

        mine = pltpu.make_async_copy(x_ref, rows(*me), local_sem)
        mine.start()
        first = [copy(0, me, sibling, src=x_ref)]
        first += [copy(1 + j, me, (*chip, c), src=x_ref) for j, chip in enumerate(chips)]
        for cp in first:
            cp.start()
        passed = [copy(4 + j, (*chip, c), sibling) for j, chip in enumerate(chips)]
        for j, chip in enumerate(chips):
            copy(1 + j, (*chip, c), me).wait_recv()
            passed[j].start()
        copy(0, sibling, me).wait_recv()
        for j, chip in enumerate(chips):
            copy(4 + j, (*chip, 1 - c), me).wait_recv()
        for cp in first + passed:
            cp.wait_send()
        mine.wait()

    return pl.pallas_call(
        body, name="gather_all", out_shape=jax.ShapeDtypeStruct((N_DEV * m_per, n), block.dtype),
        in_specs=[pl.BlockSpec(memory_space=pltpu.VMEM)], out_specs=pl.BlockSpec(memory_space=pltpu.VMEM),
        scratch_shapes=[pltpu.SemaphoreType.DMA((7,)), pltpu.SemaphoreType.DMA((7,)), pltpu.SemaphoreType.DMA],
    )(block)


def _rope_tables(positions):
    inv_freq = 1.0 / (ROPE_THETA ** (jnp.arange(0, MLA_ROPE, 2, dtype=F32) / MLA_ROPE))
    ang = positions.astype(F32).reshape(-1, 1) * inv_freq
    return jnp.tile(jnp.cos(ang), (1, 4)), jnp.tile(jnp.sin(ang), (1, 4))


def _pad_cols(a, n):
    return jnp.pad(a, ((0, 0), (0, n - a.shape[1])))


def _ffn_fwd(l, h, w, nb, seq):
    ug = _mm(h, w["ffn_up_g"][l], name=f"ffn{l}_up_g")
    uv = _mm(h, w["ffn_up_v"][l], name=f"ffn{l}_up_v")
    op = _op_conv(f"ffn{l}_conv", _f_convffn, [ug, uv], [w["ffn_cw_g"][l], w["ffn_cw_v"][l]],
                  [w["ffn_cb_g"][l], w["ffn_cb_v"][l]], nb, seq)
    act = op.fwd((BF16,))[0]
    ff = _mm(act, w["ffn_down"][l], name=f"ffn{l}_down")
    return ff, (h, op, act)


def _ffn_bwd(l, saved, dff, w, grads, dh_skip):
    h, op, act = saved
    grads[f"ffn_down{l}"] = _mm(act, dff, ta=True, name=f"ffn{l}_down_dw")
    dact = _mm(dff, w["ffn_down"][l], tb=True, name=f"ffn{l}_down_dx")
    (dug, duv), (dwg, dwv, dbg, dbv) = op.bwd([dact], [BF16, BF16])
    grads[f"ffn_up_g{l}"] = _mm(h, dug, ta=True, name=f"ffn{l}_up_g_dw")
    grads[f"ffn_up_v{l}"] = _mm(h, duv, ta=True, name=f"ffn{l}_up_v_dw")
    grads[f"ffn_cw_g{l}"], grads[f"ffn_cw_v{l}"], grads[f"ffn_cb_g{l}"], grads[f"ffn_cb_v{l}"] = dwg, dwv, dbg, dbv
    dh = _mm(dug, w["ffn_up_g"][l], tb=True, add=dh_skip, name=f"ffn{l}_up_g_dx")
    return _mm(duv, w["ffn_up_v"][l], tb=True, add=dh, name=f"ffn{l}_up_v_dx")


class _Carried:
    def __init__(self, shards, finish_weights, ffn_shards, set_ffn, grad_stacks):
        self.shards, self.finish_weights, self.ffn_shards, self.set_ffn = shards, finish_weights, ffn_shards, set_ffn
        self.grad_stacks, self.received = grad_stacks, None


def _local_step(x, positions, target, w, carried=None):
    nb, seq, d = x.shape
    t = nb * seq
    x2, tgt2 = x.reshape(t, d), target.reshape(t, d)
    cos, sin = _rope_tables(positions)
    grads = {}

    xb = x2.astype(BF16)

    z = _mm(xb, w["in_z"], name="ssd_in_z")
    raw = [_mm(xb, w[k], name="ssd_" + k) for k in ("in_x", "in_b", "in_c")]
    dt_raw = _mm(xb, w["in_dt"], name="ssd_in_dt")
    conv_ops = [_op_conv("ssd_conv_" + s, _f_convsilu, [r], [w["conv_w_" + s]], [w["conv_b_" + s]], nb, seq)
                for s, r in zip("xbc", raw)]
    xs, bm, cm = [op.fwd()[0] for op in conv_ops]
    dt_op = _op_dt("ssd_dt", dt_raw, w["dt_bias"], w["a_log"])
    dt, cum, ecum, wend = dt_op.fwd()

    def to_rows(v):
        v = jnp.swapaxes(v.reshape(nb, seq, LANE), 1, 2)[:, :SSD_H, :].reshape(nb * SSD_G, SSD_K, seq)
        return jnp.pad(v, ((0, 0), (0, SSD_KPAD - SSD_K), (0, 0)))

    def from_rows(v):
        v = v[:, :SSD_K, :].reshape(nb, SSD_H, seq)
        return jnp.swapaxes(jnp.pad(v, ((0, 0), (0, LANE - SSD_H), (0, 0))), 1, 2).reshape(t, LANE)

    ssd_rows = jnp.concatenate([to_rows(cum), to_rows(dt), to_rows(ecum), to_rows(wend)], axis=1)
    y, states, *gathered = _ssd_scan_fwd(xs, bm, cm, ssd_rows, nb, seq, carried.shards if carried else ())
    if carried:
        w = {**w, **carried.finish_weights(gathered)}
    post_op = _op_ssd_post("ssd_post", y, xs, z, w["d_exp"], w["norm_g"])
    yn = post_op.fwd((BF16,))[0]
    mix0 = _mm(yn, w["ssd_out"], name="ssd_out")
    ln0m = _op_ln("ln_mix0", x2, mix0, w["ln_mix_g"][0], w["ln_mix_b"][0])
    h0a, h0a_b = ln0m.fwd((F32, BF16))
    ff0, ffn0_saved = _ffn_fwd(0, h0a_b, w, nb, seq)
    ln0f = _op_ln("ln_ffn0", h0a, ff0, w["ln_ffn_g"][0], w["ln_ffn_b"][0])
    h1, h1_b = ln0f.fwd((F32, BF16))

    down = _mm(h1_b, w["kvq_down"], name="kvq_down")
    kvn_op = _op_rms("kv_norm", down, w["kv_norm_g"], window=(0, MLA_KV_RANK))
    ckvn = kvn_op.fwd((BF16,))[0]
    kr_op = _op_rope("k_rope", _f_rope_dup, down, cos, sin, window=(MLA_KV_RANK // LANE, 1))
    kr = kr_op.fwd((BF16,))[0]
    kn = _mm(ckvn, w["kv_up_k"], out_dtype=BF16, name="kv_up_k")
    v = _mm(ckvn, w["kv_up_v"], out_dtype=BF16, name="kv_up_v")
    qn_op = _op_rms("q_norm", down, w["q_norm_g"], window=(1, MLA_Q_RANK))
    cq = qn_op.fwd((BF16,))[0]
    qn = _mm(cq, w["q_up_n"], out_dtype=BF16, name="q_up_n")
    qr_raw = _mm(cq, w["q_up_r"], name="q_up_r")
    qr_op = _op_rope("q_rope", _f_rope, qr_raw, cos, sin)
    qr = qr_op.fwd((BF16,))[0]
    o, lse, *gathered = _attn_fwd(qn, qr, kn, kr, v, nb, seq, carried.ffn_shards if carried else ())
    if carried:
        carried.set_ffn(w, gathered)
    mix1 = _mm(o, w["attn_out"], name="attn_out")
    ln1m = _op_ln("ln_mix1", h1, mix1, w["ln_mix_g"][1], w["ln_mix_b"][1])
    h1a, h1a_b = ln1m.fwd((F32, BF16))
    ff1, ffn1_saved = _ffn_fwd(1, h1a_b, w, nb, seq)
    ln1f = _op_ln("ln_ffn1", h1a, ff1, w["ln_ffn_g"][1], w["ln_ffn_b"][1])
    h2 = ln1f.fwd()[0]

    loss_parts, dh2 = _loss_head(h2, tgt2)

    (dh1a, dff1), (grads["ln_ffn_g1"], grads["ln_ffn_b1"]) = ln1f.bwd([dh2], [F32, BF16])
    dh1a = _ffn_bwd(1, ffn1_saved, dff1, w, grads, dh1a)
    (dh1, dmix1), (grads["ln_mix_g1"], grads["ln_mix_b1"]) = ln1m.bwd([dh1a], [F32, BF16])
    grads["attn_out"] = _mm(o, dmix1, ta=True, name="attn_out_dw")
    do = _mm(dmix1, w["attn_out"], tb=True, name="attn_out_dx")
    dqn, dqr, dkn, dkr, dv = _attn_bwd(qn, qr, kn, kr, v, o, lse, do, nb, seq)
    (dqr_raw,), _ = qr_op.bwd([dqr], [BF16])
    grads["q_up_n"] = _mm(cq, dqn, ta=True, name="q_up_n_dw")
    grads["q_up_r"] = _mm(cq, dqr_raw, ta=True, name="q_up_r_dw")
    dcq = _mm(dqn, w["q_up_n"], tb=True, name="q_up_n_dx")
    dcq = _mm(dqr_raw, w["q_up_r"], tb=True, add=dcq, name="q_up_r_dx")
    (dcq_raw,), (grads["q_norm_g"],) = qn_op.bwd([dcq], [BF16])
    grads["kv_up_k"] = _mm(ckvn, dkn, ta=True, name="kv_up_k_dw")
    grads["kv_up_v"] = _mm(ckvn, dv, ta=True, name="kv_up_v_dw")
    dckvn = _mm(dkn, w["kv_up_k"], tb=True, name="kv_up_k_dx")
    dckvn = _mm(dv, w["kv_up_v"], tb=True, add=dckvn, name="kv_up_v_dx")
    (dckv,), (grads["kv_norm_g"],) = kvn_op.bwd([dckvn], [BF16])
    (dkr_in,), _ = kr_op.bwd([dkr], [BF16])
    ddown = jnp.concatenate([dckv, dkr_in, dcq_raw], axis=1)
    grads["kvq_down"] = _mm(h1_b, ddown, ta=True, name="kvq_down_dw")
    dh1 = _mm(ddown, w["kvq_down"], tb=True, add=dh1, name="kvq_down_dx")

    (dh0a, dff0), (grads["ln_ffn_g0"], grads["ln_ffn_b0"]) = ln0f.bwd([dh1], [F32, BF16])
    dh0a = _ffn_bwd(0, ffn0_saved, dff0, w, grads, dh0a)
    (dx, dmix0), (grads["ln_mix_g0"], grads["ln_mix_b0"]) = ln0m.bwd([dh0a], [F32, BF16])
    grads["ssd_out"] = _mm(yn, dmix0, ta=True, name="ssd_out_dw")
    dyn = _mm(dmix0, w["ssd_out"], tb=True, name="ssd_out_dx")
    (dy, dxs_post, dz), (grads["d_exp"], grads["norm_g"]) = post_op.bwd([dyn], [F32, F32, BF16])
    dxs, dbm, dcm, drows, *received = _ssd_scan_bwd(xs, bm, cm, ssd_rows, states, dy, dxs_post, nb, seq,
                                                    carried.grad_stacks(grads) if carried else ())
    if carried:
        carried.received = received
    d_dt, d_cum, d_ecum, d_wend = [from_rows(drows[:, o:o + SSD_KPAD]) for o in (_DT, _CUM, _ECUM, _WEND)]
    (ddt_raw,), (grads["dt_bias"], grads["a_log"]) = dt_op.bwd([d_dt, d_cum, d_ecum, d_wend], [BF16])
    draws = []
    for s, op, dout in zip("xbc", conv_ops, (dxs, dbm, dcm)):
        (dr,), (grads["conv_w_" + s], grads["conv_b_" + s]) = op.bwd([dout], [BF16])
        draws.append(dr)
    for k, dr in zip(("in_x", "in_b", "in_c"), draws):
        grads[k] = _mm(xb, dr, ta=True, name=f"ssd_{k}_dw")
        dx = _mm(dr, w[k], tb=True, add=dx, name=f"ssd_{k}_dx")
    grads["in_z"] = _mm(xb, dz, ta=True, name="ssd_in_z_dw")
    grads["in_dt"] = _mm(xb, ddt_raw, ta=True, name="ssd_in_dt_dw")
    dx = _mm(dz, w["in_z"], tb=True, add=dx, name="ssd_in_z_dx")
    dx = _mm(ddt_raw, w["in_dt"], tb=True, add=dx, name="ssd_in_dt_dx")
    return loss_parts, dx.reshape(nb, seq, d), grads


_XE, _BE, _CE = SSD_D_INNER, SSD_D_INNER + SSD_GN, SSD_D_INNER + 2 * SSD_GN
_KVR = MLA_KV_RANK + LANE


def _prep_weights(fw):
    w = {}
    f = FFN_HIDDEN
    if "ssd_in_proj" in fw:
        ip = fw["ssd_in_proj"][0]
        o = SSD_D_INNER
        w["in_z"], w["in_x"], w["in_b"], w["in_c"] = ip[:, :o], ip[:, o:o + _XE], ip[:, o + _XE:o + _BE], ip[:, o + _BE:o + _CE]
        w["in_dt"] = _pad_cols(ip[:, o + _CE:], LANE)
        cw, cb = fw["ssd_conv_w"][0], fw["ssd_conv_b"]
        for s, (lo, hi) in zip("xbc", ((0, _XE), (_XE, _BE), (_BE, _CE))):
            w["conv_w_" + s], w["conv_b_" + s] = cw[:, lo:hi], cb[:, lo:hi]
    if "ssd_dt_bias" in fw:
        w["dt_bias"], w["a_log"] = _pad_cols(fw["ssd_dt_bias"], LANE), _pad_cols(fw["ssd_A_log"], LANE)
        w["d_exp"] = jnp.repeat(fw["ssd_D"][0], SSD_P)[None, :]
        w["kv_norm_g"], w["q_norm_g"] = fw["kv_norm_g"][None, :], fw["q_norm_g"]
        w["ffn_cb_g"] = [fw["ffn_conv_b"][l:l + 1, :f] for l in range(DEPTH)]
        w["ffn_cb_v"] = [fw["ffn_conv_b"][l:l + 1, f:] for l in range(DEPTH)]
        for k in ("ln_mix_g", "ln_mix_b", "ln_ffn_g", "ln_ffn_b"):
            w[k] = [fw[k][l:l + 1] for l in range(DEPTH)]
    if "ssd_out_proj" in fw:
        w["norm_g"], w["ssd_out"] = fw["ssd_norm_g"], fw["ssd_out_proj"][0]
        kd = fw["kv_down_proj"]
        w["kvq_down"] = jnp.concatenate([_pad_cols(kd, _KVR), fw["q_down_proj"][0]], axis=1)
        w["kv_up_k"], w["kv_up_v"] = fw["kv_up_k"], fw["kv_up_v"]
        qu = fw["q_up_proj"][0].reshape(MLA_Q_RANK, MLA_H, MLA_NOPE + MLA_ROPE)
        w["q_up_n"] = qu[:, :, :MLA_NOPE].reshape(MLA_Q_RANK, MLA_H * MLA_NOPE)
        w["q_up_r"] = qu[:, :, MLA_NOPE:].reshape(MLA_Q_RANK, MLA_H * MLA_ROPE)
        w["attn_out"] = fw["attn_out_proj"][0]
        w["ffn_up_g"], w["ffn_up_v"], w["ffn_down"] = [None] * DEPTH, [None] * DEPTH, [None] * DEPTH
        for l in range(DEPTH):
            if fw["ffn_up"][l] is not None:
                _set_ffn_weights(w, l, fw["ffn_up"][l], fw["ffn_down"][l])
        w["ffn_cw_g"] = [fw["ffn_conv_w"][l][:, :f] for l in range(DEPTH)]
        w["ffn_cw_v"] = [fw["ffn_conv_w"][l][:, f:] for l in range(DEPTH)]
    return w


def _set_ffn_weights(w, l, up, down):
    w["ffn_up_g"][l], w["ffn_up_v"][l], w["ffn_down"][l] = up[:, :FFN_HIDDEN], up[:, FFN_HIDDEN:], down


def _assemble_grads(g, names):
    make = {
        "ssd_in_proj": lambda: jnp.concatenate([g["in_z"], g["in_x"], g["in_b"], g["in_c"], g["in_dt"][:, :SSD_H]], axis=1)[None],
        "ssd_conv_w": lambda: jnp.concatenate([g["conv_w_" + s] for s in "xbc"], axis=1)[None],
        "ssd_conv_b": lambda: jnp.concatenate([g["conv_b_" + s] for s in "xbc"], axis=1),
        "ssd_dt_bias": lambda: g["dt_bias"][:, :SSD_H],
        "ssd_A_log": lambda: g["a_log"][:, :SSD_H],
        "ssd_D": lambda: jnp.sum(g["d_exp"].reshape(SSD_H, SSD_P), axis=1)[None, :],
        "ssd_norm_g": lambda: g["norm_g"],
        "ssd_out_proj": lambda: g["ssd_out"][None],
        "kv_down_proj": lambda: g["kvq_down"][:, :MLA_KV_RANK + MLA_ROPE],
        "kv_norm_g": lambda: g["kv_norm_g"][0],
        "kv_up_k": lambda: g["kv_up_k"],
        "kv_up_v": lambda: g["kv_up_v"],
        "q_down_proj": lambda: g["kvq_down"][None, :, _KVR:],
        "q_norm_g": lambda: g["q_norm_g"],
        "q_up_proj": lambda: jnp.concatenate([g["q_up_n"].reshape(MLA_Q_RANK, MLA_H, MLA_NOPE),
                                              g["q_up_r"].reshape(MLA_Q_RANK, MLA_H, MLA_ROPE)], axis=2).reshape(1, MLA_Q_RANK, -1),
        "attn_out_proj": lambda: g["attn_out"][None],
        "ffn_up": lambda: jnp.stack([jnp.concatenate([g[f"ffn_up_g{l}"], g[f"ffn_up_v{l}"]], axis=1) for l in range(DEPTH)]),
        "ffn_conv_w": lambda: jnp.stack([jnp.concatenate([g[f"ffn_cw_g{l}"], g[f"ffn_cw_v{l}"]], axis=1) for l in range(DEPTH)]),
        "ffn_conv_b": lambda: jnp.concatenate([jnp.concatenate([g[f"ffn_cb_g{l}"], g[f"ffn_cb_v{l}"]], axis=1)
                                               for l in range(DEPTH)], axis=0),
        "ffn_down": lambda: jnp.stack([g[f"ffn_down{l}"] for l in range(DEPTH)]),
    }
    for k in ("ln_mix_g", "ln_mix_b", "ln_ffn_g", "ln_ffn_b"):
        make[k] = lambda k=k: jnp.concatenate([g[f"{k}{l}"] for l in range(DEPTH)], axis=0)
    return {n: make[n]() for n in names}


_WEIGHTS = ["ssd_in_proj", "ssd_conv_w", "ssd_conv_b", "ssd_dt_bias", "ssd_A_log", "ssd_D", "ssd_norm_g", "ssd_out_proj",
            "kv_down_proj", "kv_norm_g", "kv_up_k", "kv_up_v", "q_down_proj", "q_norm_g", "q_up_proj", "attn_out_proj",
            "ffn_up", "ffn_conv_w", "ffn_conv_b", "ffn_down", "ln_mix_g", "ln_mix_b", "ln_ffn_g", "ln_ffn_b"]
_COL_CUT = ["ssd_in_proj", "ssd_conv_w", "ssd_conv_b", "ssd_norm_g", "kv_up_k", "kv_up_v", "q_up_proj", "ffn_up", "ffn_conv_w"]
_ROW_CUT = ["ssd_out_proj", "kv_down_proj", "q_down_proj", "attn_out_proj", "ffn_down"]
_CUT = _COL_CUT + _ROW_CUT
_WHOLE = [n for n in _WEIGHTS if n not in _CUT]
_EARLY = ["ssd_in_proj", "ssd_conv_w", "ssd_conv_b"]
_LATE = [n for n in _CUT if n not in _EARLY]
_FFN_MATRICES = ["ffn_up", "ffn_down"]
_MXU_WEIGHTS = ["ssd_in_proj", "ssd_out_proj", "kv_down_proj", "kv_up_k", "kv_up_v", "q_down_proj", "q_up_proj",
                "attn_out_proj", "ffn_up", "ffn_down"]
_PACK_ROWS = 160


def _shard_2d(name, s):
    return s.reshape(-1, s.shape[-1])


def _unstack(name, g, shard_shape):
    if name in _COL_CUT:
        lead = shard_shape[:-1]
        return jnp.swapaxes(g, 0, 1).reshape(*lead, N_CHIPS * shard_shape[-1])
    lead, rs, c = shard_shape[:-2], shard_shape[-2], shard_shape[-1]
    n_lead = math.prod(lead)
    return jnp.swapaxes(g.reshape(N_CHIPS, n_lead, rs, c), 0, 1).reshape(*lead, N_CHIPS * rs, c)


def _stack(name, full, shard_shape):
    if name in _COL_CUT:
        cs = shard_shape[-1]
        return jnp.swapaxes(full.reshape(-1, N_CHIPS, cs), 0, 1)
    lead, rs, c = shard_shape[:-2], shard_shape[-2], shard_shape[-1]
    n_lead = math.prod(lead)
    return jnp.swapaxes(full.reshape(n_lead, N_CHIPS, rs, c), 0, 1).reshape(N_CHIPS, n_lead * rs, c)


def _pack(arrs):
    flat = jnp.concatenate([a.reshape(-1) for a in arrs])
    return jnp.pad(flat, (0, _PACK_ROWS * LANE - flat.shape[0])).reshape(_PACK_ROWS, LANE)


def _unpack(packed, like):
    flat, out, o = packed.reshape(-1), [], 0
    for a in like:
        out.append(flat[o:o + a.size].reshape(a.shape))
        o += a.size
    return out


_ARGS = ["x", "positions"] + _WEIGHTS + ["loss_target"] + ["m_" + n for n in _WEIGHTS] + ["v_" + n for n in _WEIGHTS]


def kernel(x, positions, ssd_in_proj, ssd_conv_w, ssd_conv_b, ssd_dt_bias, ssd_A_log, ssd_D, ssd_norm_g,
           ssd_out_proj, kv_down_proj, kv_norm_g, kv_up_k, kv_up_v, q_down_proj, q_norm_g, q_up_proj,
           attn_out_proj, ffn_up, ffn_conv_w, ffn_conv_b, ffn_down, ln_mix_g, ln_mix_b, ln_ffn_g, ln_ffn_b,
           loss_target, m_ssd_in_proj, m_ssd_conv_w, m_ssd_conv_b, m_ssd_dt_bias, m_ssd_A_log, m_ssd_D,
           m_ssd_norm_g, m_ssd_out_proj, m_kv_down_proj, m_kv_norm_g, m_kv_up_k, m_kv_up_v, m_q_down_proj,
           m_q_norm_g, m_q_up_proj, m_attn_out_proj, m_ffn_up, m_ffn_conv_w, m_ffn_conv_b, m_ffn_down,
           m_ln_mix_g, m_ln_mix_b, m_ln_ffn_g, m_ln_ffn_b, v_ssd_in_proj, v_ssd_conv_w, v_ssd_conv_b,
           v_ssd_dt_bias, v_ssd_A_log, v_ssd_D, v_ssd_norm_g, v_ssd_out_proj, v_kv_down_proj, v_kv_norm_g,
           v_kv_up_k, v_kv_up_v, v_q_down_proj, v_q_norm_g, v_q_up_proj, v_attn_out_proj, v_ffn_up,
           v_ffn_conv_w, v_ffn_conv_b, v_ffn_down, v_ln_mix_g, v_ln_mix_b, v_ln_ffn_g, v_ln_ffn_b):
    args = (x, positions, ssd_in_proj, ssd_conv_w, ssd_conv_b, ssd_dt_bias, ssd_A_log, ssd_D, ssd_norm_g,
            ssd_out_proj, kv_down_proj, kv_norm_g, kv_up_k, kv_up_v, q_down_proj, q_norm_g, q_up_proj,
            attn_out_proj, ffn_up, ffn_conv_w, ffn_conv_b, ffn_down, ln_mix_g, ln_mix_b, ln_ffn_g, ln_ffn_b,
            loss_target, m_ssd_in_proj, m_ssd_conv_w, m_ssd_conv_b, m_ssd_dt_bias, m_ssd_A_log, m_ssd_D,
            m_ssd_norm_g, m_ssd_out_proj, m_kv_down_proj, m_kv_norm_g, m_kv_up_k, m_kv_up_v, m_q_down_proj,
            m_q_norm_g, m_q_up_proj, m_attn_out_proj, m_ffn_up, m_ffn_conv_w, m_ffn_conv_b, m_ffn_down,
            m_ln_mix_g, m_ln_mix_b, m_ln_ffn_g, m_ln_ffn_b, v_ssd_in_proj, v_ssd_conv_w, v_ssd_conv_b,
            v_ssd_dt_bias, v_ssd_A_log, v_ssd_D, v_ssd_norm_g, v_ssd_out_proj, v_kv_down_proj, v_kv_norm_g,
            v_kv_up_k, v_kv_up_v, v_q_down_proj, v_q_norm_g, v_q_up_proj, v_attn_out_proj, v_ffn_up,
            v_ffn_conv_w, v_ffn_conv_b, v_ffn_down, v_ln_mix_g, v_ln_mix_b, v_ln_ffn_g, v_ln_ffn_b)
    a = dict(zip(_ARGS, args, strict=True))

    last = DEPTH - 1

    def layers(n, late):
        if n not in _FFN_MATRICES:
            return a[n]
        return a[n][last:] if late else a[n][:last]

    def shard(n, late=False):
        s = _shard_2d(n, layers(n, late))
        return s.astype(BF16) if n in _MXU_WEIGHTS else s

    def full_weights(names, gathered, late=False):
        fw = {n: _unstack(n, g, layers(n, late).shape) for n, g in zip(names, gathered)}
        for n in _FFN_MATRICES:
            if n in fw:
                fw[n] = [None] * last + list(fw[n]) if late else list(fw[n]) + [None]
        return fw

    def set_last_ffn(w, gathered):
        fw = full_weights(_FFN_MATRICES, gathered, late=True)
        _set_ffn_weights(w, last, fw["ffn_up"][last], fw["ffn_down"][last])

    def grad_stacks(names, pieces):
        full = _assemble_grads(pieces, names)
        return [_stack(n, full[n], a[n].shape).astype(BF16) for n in names]

    fw = full_weights(_EARLY, _gather_chips([shard(n) for n in _EARLY]))
    fw.update({n: a[n] for n in _WHOLE})
    carried = _Carried([shard(n) for n in _LATE], lambda got: _prep_weights(full_weights(_LATE, got)),
                       [shard(n, late=True) for n in _FFN_MATRICES], set_last_ffn,
                       lambda pieces: grad_stacks(_LATE, pieces))

    loss_parts, grad_x, pieces = _local_step(a["x"], a["positions"], a["loss_target"], _prep_weights(fw), carried)
    loss = lax.psum(jnp.sum(loss_parts[::8, 0]), ("x", "y", "c"))

    bufs = dict(zip(_LATE, carried.received))
    bufs.update(zip(_EARLY, _scatter_chips(grad_stacks(_EARLY, pieces))))
    sums = [_sum_parts(bufs[n], "sum_chips_" + n) for n in _CUT]
    others = _swap_cores(sums)
    full = _assemble_grads(pieces, _WHOLE)
    every = _gather_all(_pack([full[n] for n in _WHOLE]))
    g_whole = _sum_parts(every.reshape(N_DEV, _PACK_ROWS, LANE), "sum_devices")

    res = {}
    for n, s, o in zip(_CUT, sums, others):
        out = _adamw(_shard_2d(n, a[n]), s, o, _shard_2d(n, a["m_" + n]), _shard_2d(n, a["v_" + n]), "adamw_" + n)
        res[n] = [r.reshape(a[n].shape) for r in out]
    whole = [a[n] for n in _WHOLE]
    out = _adamw(_pack(whole), g_whole, None, _pack([a["m_" + n] for n in _WHOLE]), _pack([a["v_" + n] for n in _WHOLE]),
                 "adamw_whole")
    for k, n in enumerate(_WHOLE):
        res[n] = [_unpack(r, whole)[k] for r in out]
    return (loss, grad_x, *[res[n][0] for n in _WEIGHTS], *[res[n][1] for n in _WEIGHTS],
            *[res[n][2] for n in _WEIGHTS], *[res[n][3] for n in _WEIGHTS])
```

```python
import functools
import math

import jax
import jax.numpy as jnp
from jax import lax
from jax.experimental import pallas as pl
from jax.experimental.pallas import tpu as pltpu

F32 = jnp.float32
BF16 = jnp.bfloat16
HIGHEST = lax.Precision.HIGHEST
MESH = pl.DeviceIdType.MESH

D_MODEL = 1024
DEPTH = 2
DN_ALPHA = (2 * DEPTH) ** 0.25
SSD_D_INNER = 2048
SSD_P = 64
SSD_H = 32
SSD_G = 8
SSD_K = 4
SSD_N = 128
SSD_L = 128
SSD_GN = SSD_G * SSD_N
MLA_H = 8
MLA_Q_RANK = 384
MLA_KV_RANK = 256
MLA_NOPE = 128
MLA_ROPE = 64
MLA_V = 128
ROPE_THETA = 10000.0
FFN_HIDDEN = 2816
LN_EPS = 1e-5
RMS_EPS = 1e-6
ADAM_LR = 0.001
ADAM_B1 = 0.9
ADAM_B2 = 0.999
ADAM_EPS = 1e-08
ADAM_WD = 0.01
ADAM_STEP = 10

N_CHIPS = 4
N_DEV = 8
LANE = 128
VMEM_LIMIT = 56 * 1024 * 1024


def _cparams(**kw):
    return pltpu.CompilerParams(vmem_limit_bytes=VMEM_LIMIT, **kw)


def _tile(dim, cap):
    best = None
    t = LANE
    while t <= min(dim, cap):
        if dim % t == 0:
            best = t
        t += LANE
    return dim if best is None else best


MM_TILE_CAP = 1408
MM_WHOLE_K = 2816
MM_VMEM_BUDGET = 40 * 1024 * 1024


def _mm_tiles(m, n, k, a_bytes, b_bytes, o_bytes, has_add):
    tm, tn = _tile(m, MM_TILE_CAP), _tile(n, MM_TILE_CAP)
    tk = k if k <= MM_WHOLE_K else _tile(k, 1024)

    def need(tm, tn):
        acc = tm * tn * 4 if tk < k else 0
        return 2 * (tm * tk * a_bytes + tk * tn * b_bytes + tm * tn * o_bytes + (tm * tn * 4 if has_add else 0)) + acc

    while need(tm, tn) > MM_VMEM_BUDGET:
        if tm >= tn and _tile(m, tm // 2) < tm:
            tm = _tile(m, tm // 2)
        elif _tile(n, tn // 2) < tn:
            tn = _tile(n, tn // 2)
        else:
            break
    return tm, tn, tk


def _mm(a, b, *, ta=False, tb=False, add=None, out_dtype=F32, name):
    m, k = (a.shape[1], a.shape[0]) if ta else a.shape
    n = b.shape[0] if tb else b.shape[1]
    assert (b.shape[1] if tb else b.shape[0]) == k
    tm, tn, tk = _mm_tiles(m, n, k, a.dtype.itemsize, b.dtype.itemsize, jnp.dtype(out_dtype).itemsize, add is not None)
    nk = k // tk
    dims = (((0 if ta else 1,), (1 if tb else 0,)), ((), ()))

    def partial_product(a_ref, b_ref):
        return lax.dot_general(a_ref[...].astype(BF16), b_ref[...].astype(BF16), dims, preferred_element_type=F32)

    def body_one(*refs):
        if add is None:
            a_ref, b_ref, o_ref = refs
            o_ref[...] = partial_product(a_ref, b_ref).astype(out_dtype)
        else:
            a_ref, b_ref, c_ref, o_ref = refs
            o_ref[...] = (partial_product(a_ref, b_ref) + c_ref[...]).astype(out_dtype)

    def body_acc(*refs):
        if add is None:
            a_ref, b_ref, o_ref, acc = refs
        else:
            a_ref, b_ref, c_ref, o_ref, acc = refs
        kk = pl.program_id(2)

        @pl.when(kk == 0)
        def _():
            acc[...] = jnp.zeros_like(acc) if add is None else c_ref[...]

        acc[...] += partial_product(a_ref, b_ref)

        @pl.when(kk == nk - 1)
        def _():
            o_ref[...] = acc[...].astype(out_dtype)

    a_spec = pl.BlockSpec((tk, tm), lambda i, j, kk: (kk, i)) if ta else pl.BlockSpec((tm, tk), lambda i, j, kk: (i, kk))
    b_spec = pl.BlockSpec((tn, tk), lambda i, j, kk: (j, kk)) if tb else pl.BlockSpec((tk, tn), lambda i, j, kk: (kk, j))
    o_spec = pl.BlockSpec((tm, tn), lambda i, j, kk: (i, j))
    ins, specs = [a, b], [a_spec, b_spec]
    if add is not None:
        ins.append(add)
        specs.append(o_spec)
    return pl.pallas_call(
        body_one if nk == 1 else body_acc, name=name, grid=(m // tm, n // tn, nk), in_specs=specs, out_specs=o_spec,
        out_shape=jax.ShapeDtypeStruct((m, n), out_dtype),
        scratch_shapes=[] if nk == 1 else [pltpu.VMEM((tm, tn), F32)],
        compiler_params=_cparams(dimension_semantics=("parallel", "parallel", "arbitrary")),
    )(*ins)


def _spec(op):
    return pl.BlockSpec(op[1], op[2])


def _bw_fwd(name, fn, grid, ins, outs, out_dtypes):
    n_in = len(ins)
    flat = [(o, dt) for o, dts in zip(outs, out_dtypes) for dt in dts]

    def body(*refs):
        res = fn(*[r[...].astype(F32) for r in refs[:n_in]])
        orefs = iter(refs[n_in:])
        for v, dts in zip(res, out_dtypes):
            for dt in dts:
                next(orefs)[...] = v.astype(dt)

    return pl.pallas_call(
        body, name=name, grid=grid, in_specs=[_spec(o) for o in ins],
        out_specs=[pl.BlockSpec(o[1], o[2]) for o, _ in flat],
        out_shape=[jax.ShapeDtypeStruct(o[0], dt) for o, dt in flat], compiler_params=_cparams(),
    )(*[o[0] for o in ins])


def _bw_bwd(name, fn, grid, data, params, consts, cts, red_axes, grad_dtypes):
    nd, npar, nc, nct = len(data), len(params), len(consts), len(cts)

    def body(*refs):
        first = None
        for ax in red_axes:
            z = pl.program_id(ax) == 0
            first = z if first is None else jnp.logical_and(first, z)
        vals = [r[...].astype(F32) for r in refs[:nd + npar + nc + nct]]
        d, p, c, g = vals[:nd], vals[nd:nd + npar], vals[nd + npar:nd + npar + nc], vals[nd + npar + nc:]
        _, vjp = jax.vjp(lambda dd, pp: tuple(fn(*dd, *pp, *c)), d, p)
        gd, gp = vjp(tuple(g))
        orefs = refs[nd + npar + nc + nct:]
        for r, v in zip(orefs[:nd], gd):
            r[...] = v.astype(r.dtype)
        if npar:
            @pl.when(first)
            def _():
                for r in orefs[nd:]:
                    r[...] = jnp.zeros_like(r)

            for r, v in zip(orefs[nd:], gp):
                r[...] += v

    ins = [d[:3] for d in data] + list(params) + list(consts) + list(cts)
    outs = [d[3] if len(d) > 3 else d for d in data] + list(params)
    dtypes = list(grad_dtypes) + [F32] * npar
    return pl.pallas_call(
        body, name=name, grid=grid, in_specs=[_spec(o) for o in ins], out_specs=[_spec(o) for o in outs],
        out_shape=[jax.ShapeDtypeStruct(o[0].shape, dt) for o, dt in zip(outs, dtypes)], compiler_params=_cparams(),
    )(*[o[0] for o in ins])


def _shift_down(x, s):
    row = lax.broadcasted_iota(jnp.int32, x.shape, 0)
    return jnp.where(row < s, 0.0, pltpu.roll(x, s, 0))


def _shift_up(x, s):
    n = x.shape[0]
    row = lax.broadcasted_iota(jnp.int32, x.shape, 0)
    return jnp.where(row >= n - s, 0.0, pltpu.roll(x, n - s, 0))


def _time_shift(s):
    if s == 0:
        return lambda x: x

    @jax.custom_vjp
    def shift(x):
        return _shift_down(x, s)

    shift.defvjp(lambda x: (_shift_down(x, s), None), lambda _, g: (_shift_up(g, s),))
    return shift


def _rot_half_raw(x):
    lane = lax.broadcasted_iota(jnp.int32, x.shape, 1)
    return jnp.where(lane % MLA_ROPE < MLA_ROPE // 2, -pltpu.roll(x, LANE - MLA_ROPE // 2, 1), pltpu.roll(x, MLA_ROPE // 2, 1))


@jax.custom_vjp
def _rot_half(x):
    return _rot_half_raw(x)


_rot_half.defvjp(lambda x: (_rot_half_raw(x), None), lambda _, g: (-_rot_half_raw(g),))


@jax.custom_vjp
def _roll_half_lanes(x):
    return pltpu.roll(x, LANE // 2, 1)


_roll_half_lanes.defvjp(lambda x: (pltpu.roll(x, LANE // 2, 1), None), lambda _, g: (pltpu.roll(g, LANE // 2, 1),))


def _causal_conv(u, w, b):
    width = w.shape[0]
    y = b
    for k in range(width):
        y = y + w[k:k + 1, :] * _time_shift(width - 1 - k)(u)
    return y


def _silu(x):
    return x * jax.nn.sigmoid(x)


def _f_ln(h, mix, g, b):
    x = DN_ALPHA * h + mix
    mu = jnp.mean(x, axis=-1, keepdims=True)
    xc = x - mu
    var = jnp.mean(xc * xc, axis=-1, keepdims=True)
    return (xc * lax.rsqrt(var + LN_EPS) * g + b,)


def _f_convsilu(u, w, b):
    return (_silu(_causal_conv(u, w, b)),)


def _f_convffn(ug, uv, wg, wv, bg, bv):
    return (_silu(_causal_conv(ug, wg, bg)) * _causal_conv(uv, wv, bv),)


def _f_dt(dt_raw, bias, a_log):
    x = dt_raw + bias
    dt = jnp.maximum(x, 0.0) + jnp.log(1.0 + jnp.exp(-jnp.abs(x)))
    a = dt * (-jnp.exp(a_log))
    n = a.shape[0]
    lower = (lax.broadcasted_iota(jnp.int32, (n, n), 0) >= lax.broadcasted_iota(jnp.int32, (n, n), 1)).astype(F32)
    cum = jnp.dot(lower, a, precision=HIGHEST, preferred_element_type=F32)
    cum_last = jnp.sum(a, axis=0, keepdims=True)
    return dt, cum, jnp.exp(cum), jnp.exp(cum_last - cum) * dt


def _f_ssd_post(y, xs, z, d_exp, ng):
    t = (y + d_exp * xs) * _silu(z)
    return (t * lax.rsqrt(jnp.mean(t * t, axis=-1, keepdims=True) + LN_EPS) * ng,)


def _f_rms(x, g):
    return (x * lax.rsqrt(jnp.mean(x * x, axis=-1, keepdims=True) + RMS_EPS) * g,)


def _f_rope(x, cos, sin):
    return (x * cos + _rot_half(x) * sin,)


def _f_rope_dup(x, cos, sin):
    r = x * cos + _rot_half(x) * sin
    return (r + _roll_half_lanes(r),)


def _rows(t, cap=512):
    for c in (cap, 256, 128, 64, 32, 16, 8):
        if c <= cap and t % c == 0:
            return c
    return t


class _Blockwise:
    def __init__(self, name, fn, grid, data, params, consts, outs, red_axes):
        self.name, self.fn, self.grid = name, fn, grid
        self.data, self.params, self.consts, self.outs, self.red_axes = data, params, consts, outs, red_axes

    def fwd(self, *out_dtypes):
        out_dtypes = out_dtypes or tuple((F32,) for _ in self.outs)
        return _bw_fwd(self.name + "_fwd", self.fn, self.grid, self.data + self.params + self.consts, self.outs, out_dtypes)

    def bwd(self, cts, grad_dtypes=None):
        cts = [(c, o[1], o[2]) for c, o in zip(cts, self.outs)]
        grad_dtypes = grad_dtypes or [F32] * len(self.data)
        res = _bw_bwd(self.name + "_bwd", self.fn, self.grid, self.data, self.params, self.consts, cts, self.red_axes,
                      grad_dtypes)
        return res[:len(self.data)], res[len(self.data):]


def _op_ln(name, h, mix, g, b):
    t, d = h.shape
    tr = _rows(t, 256)
    row = ((tr, d), lambda i: (i, 0))
    par = ((1, d), lambda i: (0, 0))
    return _Blockwise(name, _f_ln, (t // tr,), [(h, *row), (mix, *row)], [(g, *par), (b, *par)], [],
                      [((t, d), *row)], (0,))


def _op_conv(name, fn, us, ws, bs, nb, seq):
    c = us[0].shape[1]
    ct = _tile(c, 256)
    blk = ((seq, ct), lambda j, bb: (bb, j))
    data = [(u, *blk) for u in us]
    params = [(w, (w.shape[0], ct), lambda j, bb: (0, j)) for w in ws] + [(b, (1, ct), lambda j, bb: (0, j)) for b in bs]
    return _Blockwise(name, fn, (c // ct, nb), data, params, [], [((nb * seq, c), *blk)], (1,))


def _op_dt(name, dt_raw, bias, a_log):
    t = dt_raw.shape[0]
    row = ((SSD_L, LANE), lambda i: (i, 0))
    par = ((1, LANE), lambda i: (0, 0))
    return _Blockwise(name, _f_dt, (t // SSD_L,), [(dt_raw, *row)], [(bias, *par), (a_log, *par)], [],
                      [((t, LANE), *row)] * 4, (0,))


def _op_ssd_post(name, y, xs, z, d_exp, ng):
    t, c = y.shape
    gw = c // SSD_G
    tr = _rows(t, 512)
    blk = ((tr, gw), lambda g, i: (i, g))
    par = ((1, gw), lambda g, i: (0, g))
    return _Blockwise(name, _f_ssd_post, (SSD_G, t // tr), [(y, *blk), (xs, *blk), (z, *blk)],
                      [(d_exp, *par), (ng, *par)], [], [((t, c), *blk)], (1,))


def _op_rms(name, x, g, window=None):
    t = x.shape[0]
    j, c = window or (0, x.shape[1])
    tr = _rows(t, 512)
    own = ((tr, c), lambda i: (i, 0))
    return _Blockwise(name, _f_rms, (t // tr,), [(x, (tr, c), lambda i: (i, j), (jax.ShapeDtypeStruct((t, c), F32), *own))],
                      [(g, (1, c), lambda i: (0, 0))], [], [((t, c), *own)], (0,))


def _op_rope(name, fn, x, cos, sin, window=None):
    t = x.shape[0]
    j0, n = window or (0, x.shape[1] // LANE)
    tr = _rows(t, 512)
    own = ((tr, LANE), lambda i, j: (i, j))
    cs = ((tr, LANE), lambda i, j: (i, 0))
    return _Blockwise(name, fn, (t // tr, n), [(x, (tr, LANE), lambda i, j: (i, j0 + j), (jax.ShapeDtypeStruct((t, n * LANE), F32), *own))],
                      [], [(cos, *cs), (sin, *cs)], [((t, n * LANE), *own)], ())


_NT = (((1,), (1,)), ((), ()))
_TN = (((0,), (0,)), ((), ()))


def _ssd_head(x, g, bc, cc, s, cum_row, dt_row, cum_col, ecum_col, wend_col):
    n = g.shape[0]
    row = lax.broadcasted_iota(jnp.int32, (n, n), 0)
    col = lax.broadcasted_iota(jnp.int32, (n, n), 1)
    decay = jnp.exp(jnp.where(row >= col, cum_col - cum_row, -jnp.inf))
    w = g * decay * dt_row
    y = jnp.dot(w.astype(BF16), x.astype(BF16), preferred_element_type=F32)
    y = y + lax.dot_general((cc * ecum_col).astype(BF16), s.astype(BF16), _NT, preferred_element_type=F32)
    lane = lax.broadcasted_iota(jnp.int32, cum_row.shape, 1)
    cum_last = jnp.sum(jnp.where(lane == n - 1, cum_row, 0.0), axis=1, keepdims=True)
    s_new = s * jnp.exp(cum_last) + lax.dot_general(x.astype(BF16), (bc * wend_col).astype(BF16), _TN,
                                                    preferred_element_type=F32)
    return y, s_new


SSD_KPAD = 8
SSD_ROWS = 4 * SSD_KPAD
_CUM, _DT, _ECUM, _WEND = 0, SSD_KPAD, 2 * SSD_KPAD, 3 * SSD_KPAD


def _ssd_head_args(rows, cols, k):
    return (rows[_CUM + k:_CUM + k + 1], rows[_DT + k:_DT + k + 1], cols[:, _CUM + k:_CUM + k + 1],
            cols[:, _ECUM + k:_ECUM + k + 1], cols[:, _WEND + k:_WEND + k + 1])


def _carried_exchange(arrs, in_refs, out_refs, sems, scatter, n_steps):
    if not arrs:
        return lambda: None, lambda: None
    first = functools.reduce(jnp.logical_and, [pl.program_id(ax) == 0 for ax in range(len(n_steps))])
    last = functools.reduce(jnp.logical_and, [pl.program_id(ax) == n - 1 for ax, n in enumerate(n_steps)])
    start, wait = _chip_exchange(in_refs, out_refs, sems, scatter)
    return (lambda: pl.when(first)(start)), (lambda: pl.when(last)(wait))


def _ssd_scan_fwd(xs, bm, cm, rows, nb, seq, gather=()):
    nc = seq // SSD_L
    kp = SSD_K * SSD_P
    ng = len(gather)

    def body(*refs):
        xs_ref, b_ref, c_ref, row_ref = refs[:4]
        y_ref, st_ref = refs[4 + ng:6 + ng]
        start, wait = _carried_exchange(gather, refs[4:4 + ng], refs[6 + ng:6 + 2 * ng], refs[6 + 2 * ng:], False,
                                        (nb, SSD_G))
        start()

        def chunk(c, states):
            sl = pl.ds(pl.multiple_of(c * SSD_L, SSD_L), SSD_L)
            bc, cc = b_ref[sl, :], c_ref[sl, :]
            g = lax.dot_general(cc.astype(BF16), bc.astype(BF16), _NT, preferred_element_type=F32)
            rows_c = row_ref[0, :, sl]
            cols_c = rows_c.T
            new = []
            for k in range(SSD_K):
                hs = pl.ds(k * SSD_P, SSD_P)
                st_ref[0, c * SSD_K + k] = states[k]
                y, s_new = _ssd_head(xs_ref[sl, hs], g, bc, cc, states[k], *_ssd_head_args(rows_c, cols_c, k))
                y_ref[sl, hs] = y
                new.append(s_new)
            return tuple(new)

        lax.fori_loop(0, nc, chunk, tuple(jnp.zeros((SSD_P, SSD_N), F32) for _ in range(SSD_K)))
        wait()

    t = xs.shape[0]
    any_spec = pl.BlockSpec(memory_space=pl.ANY)
    return pl.pallas_call(
        body, name="ssd_scan_fwd", grid=(nb, SSD_G),
        in_specs=[pl.BlockSpec((seq, kp), lambda b, g: (b, g)),
                  pl.BlockSpec((seq, SSD_N), lambda b, g: (b, g)),
                  pl.BlockSpec((seq, SSD_N), lambda b, g: (b, g)),
                  pl.BlockSpec((1, SSD_ROWS, seq), lambda b, g: (b * SSD_G + g, 0, 0))] + [any_spec] * ng,
        out_specs=[pl.BlockSpec((seq, kp), lambda b, g: (b, g)),
                   pl.BlockSpec((1, nc * SSD_K, SSD_P, SSD_N), lambda b, g: (b * SSD_G + g, 0, 0, 0))] + [any_spec] * ng,
        out_shape=[jax.ShapeDtypeStruct((t, SSD_D_INNER), F32),
                   jax.ShapeDtypeStruct((nb * SSD_G, nc * SSD_K, SSD_P, SSD_N), F32)] + _exchange_shapes(gather, False),
        scratch_shapes=_exchange_sems(ng) if ng else [],
        compiler_params=_cparams(dimension_semantics=("arbitrary", "arbitrary")),
    )(xs, bm, cm, rows, *gather)


def _ssd_scan_bwd(xs, bm, cm, rows, states, dy, dxs_skip, nb, seq, scatter=()):
    nc = seq // SSD_L
    kp = SSD_K * SSD_P
    ns = len(scatter)

    def body(*refs):
        xs_ref, b_ref, c_ref, row_ref, st_ref, dy_ref, skip_ref = refs[:7]
        dxs_ref, db_ref, dc_ref, drow_ref = refs[7 + ns:11 + ns]
        dcol_ref = refs[11 + 2 * ns]
        start, wait = _carried_exchange(scatter, refs[7:7 + ns], refs[11 + ns:11 + 2 * ns], refs[12 + 2 * ns:], True,
                                        (nb, SSD_G))
        start()

        def chunk(i, dstates):
            c = nc - 1 - i
            sl = pl.ds(pl.multiple_of(c * SSD_L, SSD_L), SSD_L)
            bc, cc = b_ref[sl, :], c_ref[sl, :]
            bcb, ccb = bc.astype(BF16), cc.astype(BF16)
            g = lax.dot_general(ccb, bcb, _NT, preferred_element_type=F32)
            rows_c = row_ref[0, :, sl]
            cols_c = rows_c.T
            drow_ref[0, :, sl] = jnp.zeros((SSD_ROWS, SSD_L), F32)
            dcol_ref[...] = jnp.zeros((SSD_L, SSD_ROWS), F32)
            db = jnp.zeros((SSD_L, SSD_N), F32)
            dc = jnp.zeros((SSD_L, SSD_N), F32)
            dg = jnp.zeros((SSD_L, SSD_L), F32)
            new = []
            for k in range(SSD_K):
                hs = pl.ds(k * SSD_P, SSD_P)
                _, vjp = jax.vjp(_ssd_head, xs_ref[sl, hs], g, bc, cc, st_ref[0, c * SSD_K + k],
                                 *_ssd_head_args(rows_c, cols_c, k))
                dx, dgk, dbk, dck, ds, d_cum_row, d_dt_row, d_cum_col, d_ecum_col, d_wend_col = vjp(
                    (dy_ref[sl, hs], dstates[k]))
                dxs_ref[sl, hs] = dx + skip_ref[sl, hs]
                db, dc, dg = db + dbk, dc + dck, dg + dgk
                drow_ref[0, _CUM + k:_CUM + k + 1, sl] = d_cum_row
                drow_ref[0, _DT + k:_DT + k + 1, sl] = d_dt_row
                dcol_ref[:, _CUM + k:_CUM + k + 1] = d_cum_col
                dcol_ref[:, _ECUM + k:_ECUM + k + 1] = d_ecum_col
                dcol_ref[:, _WEND + k:_WEND + k + 1] = d_wend_col
                new.append(ds)
            drow_ref[0, :, sl] += dcol_ref[...].T
            dgb = dg.astype(BF16)
            dc_ref[sl, :] = dc + jnp.dot(dgb, bcb, preferred_element_type=F32)
            db_ref[sl, :] = db + lax.dot_general(dgb, ccb, _TN, preferred_element_type=F32)
            return tuple(new)

        lax.fori_loop(0, nc, chunk, tuple(jnp.zeros((SSD_P, SSD_N), F32) for _ in range(SSD_K)))
        wait()

    t = xs.shape[0]
    x_spec = pl.BlockSpec((seq, kp), lambda b, g: (b, g))
    n_spec = pl.BlockSpec((seq, SSD_N), lambda b, g: (b, g))
    r_spec = pl.BlockSpec((1, SSD_ROWS, seq), lambda b, g: (b * SSD_G + g, 0, 0))
    any_spec = pl.BlockSpec(memory_space=pl.ANY)
    return pl.pallas_call(
        body, name="ssd_scan_bwd", grid=(nb, SSD_G),
        in_specs=[x_spec, n_spec, n_spec, r_spec,
                  pl.BlockSpec((1, nc * SSD_K, SSD_P, SSD_N), lambda b, g: (b * SSD_G + g, 0, 0, 0)), x_spec, x_spec]
        + [any_spec] * ns,
        out_specs=[x_spec, n_spec, n_spec, r_spec] + [any_spec] * ns,
        out_shape=[jax.ShapeDtypeStruct((t, SSD_D_INNER), F32), jax.ShapeDtypeStruct((t, SSD_GN), F32),
                   jax.ShapeDtypeStruct((t, SSD_GN), F32), jax.ShapeDtypeStruct(rows.shape, F32)]
        + _exchange_shapes(scatter, True),
        scratch_shapes=[pltpu.VMEM((SSD_L, SSD_ROWS), F32)] + (_exchange_sems(ns) if ns else []),
        compiler_params=_cparams(dimension_semantics=("arbitrary", "arbitrary")),
    )(xs, bm, cm, rows, states, dy, dxs_skip, *scatter)


ATT_TQ = 512
ATT_TE = 512
ATT_SCALE = (MLA_NOPE + MLA_ROPE) ** -0.5


def _for_key_extent(qi, seq, fn):
    te = min(ATT_TE, seq)
    per = te // ATT_TQ
    for e in range(seq // te):
        pl.when(jnp.logical_and(qi >= e * per, qi < (e + 1) * per))(functools.partial(fn, (e + 1) * te))


def _pair_masks(shape):
    lane = lax.broadcasted_iota(jnp.int32, shape, 1)
    return lane < MLA_ROPE, lane >= MLA_ROPE


def _scores(qn, qr, kn, kr, q0):
    s = lax.dot_general(qn, kn, _NT, preferred_element_type=F32) + lax.dot_general(qr, kr, _NT, preferred_element_type=F32)
    row = lax.broadcasted_iota(jnp.int32, s.shape, 0)
    col = lax.broadcasted_iota(jnp.int32, s.shape, 1)
    return jnp.where(col <= q0 + row, s * ATT_SCALE, -jnp.inf)


def _attn_specs(nb, seq):
    nq = seq // ATT_TQ
    qn = pl.BlockSpec((ATT_TQ, 2 * MLA_NOPE), lambda b, hp, qi: (b * nq + qi, hp))
    qr = pl.BlockSpec((ATT_TQ, LANE), lambda b, hp, qi: (b * nq + qi, hp))
    kn = pl.BlockSpec((seq, 2 * MLA_NOPE), lambda b, hp, qi: (b, hp))
    kr = pl.BlockSpec((seq, LANE), lambda b, hp, qi: (b, 0))
    return (nb, MLA_H // 2, nq), qn, qr, kn, kr


def _attn_fwd(qn, qr, kn, kr, v, nb, seq, gather=()):
    grid, s_qn, s_qr, s_kn, s_kr = _attn_specs(nb, seq)
    ng = len(gather)

    def body(*refs):
        qn_ref, qr_ref, kn_ref, kr_ref, v_ref = refs[:5]
        o_ref, lse_ref = refs[5 + ng:7 + ng]
        start, wait = _carried_exchange(gather, refs[5:5 + ng], refs[7 + ng:7 + 2 * ng], refs[7 + 2 * ng:], False, grid)
        start()
        qi = pl.program_id(2)

        def attend(ext):
            qr = qr_ref[...]
            masks = _pair_masks(qr.shape)
            outs, lses = [], []
            for j in range(2):
                hs = pl.ds(j * MLA_NOPE, MLA_NOPE)
                qr_j = jnp.where(masks[j], qr, jnp.zeros_like(qr))
                s = _scores(qn_ref[:, hs], qr_j, kn_ref[0:ext, hs], kr_ref[0:ext, :], qi * ATT_TQ)
                m = jnp.max(s, axis=1, keepdims=True)
                p = jnp.exp(s - m)
                l = jnp.sum(p, axis=1, keepdims=True)
                outs.append(jnp.dot(p.astype(BF16), v_ref[0:ext, hs], preferred_element_type=F32) / l)
                lses.append(m + jnp.log(l))
            o_ref[...] = jnp.concatenate(outs, axis=1).astype(o_ref.dtype)
            lse_ref[...] = jnp.where(masks[0], lses[0], lses[1])

        _for_key_extent(qi, seq, attend)
        wait()

    any_spec = pl.BlockSpec(memory_space=pl.ANY)
    return pl.pallas_call(
        body, name="attn_fwd", grid=grid, in_specs=[s_qn, s_qr, s_kn, s_kr, s_kn] + [any_spec] * ng,
        out_specs=[s_qn, s_qr] + [any_spec] * ng,
        out_shape=[jax.ShapeDtypeStruct(qn.shape, BF16), jax.ShapeDtypeStruct(qr.shape, F32)] + _exchange_shapes(gather, False),
        scratch_shapes=_exchange_sems(ng) if ng else [],
        compiler_params=_cparams(dimension_semantics=("arbitrary", "arbitrary", "arbitrary")),
    )(qn, qr, kn, kr, v, *gather)


def _attn_bwd(qn, qr, kn, kr, v, o, lse, do, nb, seq):
    grid, s_qn, s_qr, s_kn, s_kr = _attn_specs(nb, seq)

    def body(qn_ref, qr_ref, kn_ref, kr_ref, v_ref, o_ref, lse_ref, do_ref, dqn_ref, dqr_ref, dkn_ref, dkr_ref, dv_ref):
        hp, qi = pl.program_id(1), pl.program_id(2)

        @pl.when(qi == 0)
        def _():
            dkn_ref[...] = jnp.zeros_like(dkn_ref)
            dv_ref[...] = jnp.zeros_like(dv_ref)

        @pl.when(jnp.logical_and(qi == 0, hp == 0))
        def _():
            dkr_ref[...] = jnp.zeros_like(dkr_ref)

        def attend(ext):
            qr, lse = qr_ref[...], lse_ref[...]
            masks = _pair_masks(qr.shape)
            dqr_heads = []
            for j in range(2):
                hs = pl.ds(j * MLA_NOPE, MLA_NOPE)
                qn_j, qr_j = qn_ref[:, hs], jnp.where(masks[j], qr, jnp.zeros_like(qr))
                kn_j, kr, v_j = kn_ref[0:ext, hs], kr_ref[0:ext, :], v_ref[0:ext, hs]
                do_j = do_ref[:, hs]
                dob = do_j.astype(BF16)
                delta = jnp.sum(do_j * o_ref[:, hs].astype(F32), axis=1, keepdims=True)
                p = jnp.exp(_scores(qn_j, qr_j, kn_j, kr, qi * ATT_TQ) - lse[:, j * MLA_ROPE:j * MLA_ROPE + 1])
                dp = lax.dot_general(dob, v_j, _NT, preferred_element_type=F32)
                ds = (p * (dp - delta) * ATT_SCALE).astype(BF16)
                dkn_ref[0:ext, hs] += lax.dot_general(ds, qn_j, _TN, preferred_element_type=F32)
                dkr_ref[0:ext, :] += lax.dot_general(ds, qr_j, _TN, preferred_element_type=F32)
                dv_ref[0:ext, hs] += lax.dot_general(p.astype(BF16), dob, _TN, preferred_element_type=F32)
                dqn_ref[:, hs] = jnp.dot(ds, kn_j, preferred_element_type=F32).astype(dqn_ref.dtype)
                dqr_heads.append(jnp.dot(ds, kr, preferred_element_type=F32))
            dqr_ref[...] = jnp.where(masks[0], dqr_heads[0], dqr_heads[1])

        _for_key_extent(qi, seq, attend)

    return pl.pallas_call(
        body, name="attn_bwd", grid=grid, in_specs=[s_qn, s_qr, s_kn, s_kr, s_kn, s_qn, s_qr, s_qn],
        out_specs=[s_qn, s_qr, s_kn, s_kr, s_kn],
        out_shape=[jax.ShapeDtypeStruct(qn.shape, BF16)] + [jax.ShapeDtypeStruct(a.shape, F32) for a in (qr, kn, kr, v)],
        compiler_params=_cparams(),
    )(qn, qr, kn, kr, v, o, lse, do)


def _loss_head(y, target):
    t, d = y.shape
    tr = _rows(t, 512)

    def body(y_ref, t_ref, l_ref, dy_ref):
        err = y_ref[...] - t_ref[...]
        dy_ref[...] = err * (1.0 / d)
        part = 0.5 * jnp.sum(jnp.sum(err * err, axis=1, keepdims=True), axis=0, keepdims=True) * (1.0 / d)
        l_ref[...] = jnp.broadcast_to(part, l_ref.shape)

    row = pl.BlockSpec((tr, d), lambda i: (i, 0))
    parts, dy = pl.pallas_call(
        body, name="loss_head", grid=(t // tr,), in_specs=[row, row],
        out_specs=[pl.BlockSpec((8, LANE), lambda i: (i, 0)), row],
        out_shape=[jax.ShapeDtypeStruct((8 * (t // tr), LANE), F32), jax.ShapeDtypeStruct((t, d), F32)],
        compiler_params=_cparams(),
    )(y, target)
    return parts, dy


def _sum_parts(stack, name):
    n, r, c = stack.shape
    tr = _rows(r, 512)

    def body(s_ref, o_ref):
        acc = s_ref[0].astype(F32)
        for i in range(1, n):
            acc = acc + s_ref[i].astype(F32)
        o_ref[...] = acc

    return pl.pallas_call(
        body, name=name, grid=(r // tr,), in_specs=[pl.BlockSpec((n, tr, c), lambda i: (0, i, 0))],
        out_specs=pl.BlockSpec((tr, c), lambda i: (i, 0)), out_shape=jax.ShapeDtypeStruct((r, c), F32),
        compiler_params=_cparams(),
    )(stack)


def _adamw(w, g_mine, g_other, m, v, name):
    r, c = w.shape
    tr = _rows(r, 256)
    bc1 = 1.0 / (1.0 - ADAM_B1 ** ADAM_STEP)
    bc2 = 1.0 / (1.0 - ADAM_B2 ** ADAM_STEP)
    two = g_other is not None

    def body(*refs):
        if two:
            w_ref, g_ref, g2_ref, m_ref, v_ref, go_ref, d_ref, mo_ref, vo_ref = refs
            g = g_ref[...] + g2_ref[...]
        else:
            w_ref, g_ref, m_ref, v_ref, go_ref, d_ref, mo_ref, vo_ref = refs
            g = g_ref[...]
        mn = ADAM_B1 * m_ref[...] + (1.0 - ADAM_B1) * g
        vn = ADAM_B2 * v_ref[...] + (1.0 - ADAM_B2) * (g * g)
        go_ref[...] = g
        mo_ref[...] = mn
        vo_ref[...] = vn
        d_ref[...] = -ADAM_LR * ((mn * bc1) / (jnp.sqrt(vn * bc2) + ADAM_EPS) + ADAM_WD * w_ref[...])

    blk = pl.BlockSpec((tr, c), lambda i: (i, 0))
    ins = [w, g_mine] + ([g_other] if two else []) + [m, v]
    return pl.pallas_call(
        body, name=name, grid=(r // tr,), in_specs=[blk] * len(ins), out_specs=[blk] * 4,
        out_shape=[jax.ShapeDtypeStruct((r, c), F32)] * 4, compiler_params=_cparams(),
    )(*ins)


def _chip_peers():
    x, y, c = lax.axis_index("x"), lax.axis_index("y"), lax.axis_index("c")
    return (x, y, c), [(1 - x, y), (x, 1 - y), (1 - x, 1 - y)]


def _chip_exchange(ins, outs, sems, scatter, once=False):
    send_sems, recv_sems, local_sems = sems
    (x, y, c), chips = _chip_peers()
    me = 2 * x + y
    n = len(ins)

    def halved(i):
        tile = 32 // ins[i].dtype.itemsize
        return once and not scatter and ins[i].shape[0] % (2 * tile) == 0

    def half(i, core):
        h = ins[i].shape[0] // 2
        return pl.ds(pl.multiple_of(core * h, 8), h)

    def src(i, chip):
        if scatter:
            return ins[i].at[chip]
        return ins[i].at[half(i, c)] if halved(i) else ins[i]

    def slot(i, chip, core):
        return outs[i].at[chip, half(i, core)] if halved(i) else outs[i].at[chip]

    def local(i):
        return pltpu.make_async_copy(ins[i].at[me] if scatter else ins[i], outs[i].at[me], local_sems.at[i])

    def remote(i, j, piece, chip):
        px, py = chips[j]
        return pltpu.make_async_remote_copy(src_ref=src(i, piece), dst_ref=slot(i, chip, c), send_sem=send_sems.at[i, j],
                                            recv_sem=recv_sems.at[i, j], device_id=(px, py, c), device_id_type=MESH)

    def passed_on(i, j, core):
        px, py = chips[j]
        ref = slot(i, 2 * px + py, core)
        return pltpu.make_async_remote_copy(src_ref=ref, dst_ref=ref, send_sem=send_sems.at[i, 3 + j],
                                            recv_sem=recv_sems.at[i, 3 + j], device_id=(x, y, 1 - c), device_id_type=MESH)

    def start():
        for i in range(n):
            local(i).start()
            for j, (px, py) in enumerate(chips):
                remote(i, j, 2 * px + py, me).start()

    def wait():
        for i in range(n):
            for j, (px, py) in enumerate(chips):
                remote(i, j, me, 2 * px + py).wait_recv()
                if halved(i):
                    passed_on(i, j, c).start()
        for i in range(n):
            for j, (px, py) in enumerate(chips):
                if halved(i):
                    passed_on(i, j, 1 - c).wait_recv()
                    passed_on(i, j, c).wait_send()
                remote(i, j, 2 * px + py, me).wait_send()
            local(i).wait()

    return start, wait


def _exchange_sems(n):
    return [pltpu.SemaphoreType.DMA((n, 6)), pltpu.SemaphoreType.DMA((n, 6)), pltpu.SemaphoreType.DMA((n,))]


def _exchange_shapes(arrs, scatter):
    return [jax.ShapeDtypeStruct(s.shape if scatter else (N_CHIPS,) + s.shape, s.dtype) for s in arrs]


def _exchange_call(name, arrs, scatter):
    n = len(arrs)

    def body(*refs):
        start, wait = _chip_exchange(refs[:n], refs[n:2 * n], refs[2 * n:], scatter, once=True)
        start()
        wait()

    any_spec = pl.BlockSpec(memory_space=pl.ANY)
    return pl.pallas_call(body, name=name, in_specs=[any_spec] * n, out_specs=[any_spec] * n,
                          out_shape=_exchange_shapes(arrs, scatter), scratch_shapes=_exchange_sems(n))(*arrs)


def _gather_chips(shards, name="gather_chips"):
    return _exchange_call(name, shards, False)


def _scatter_chips(stacks, name="scatter_chips"):
    return _exchange_call(name, stacks, True)


def _swap_cores(arrs):
    n = len(arrs)

    def body(*refs):
        ins, outs = refs[:n], refs[n:2 * n]
        send_sems, recv_sems = refs[2 * n:]
        x, y, c = lax.axis_index("x"), lax.axis_index("y"), lax.axis_index("c")
        cps = []
        for i in range(n):
            cp = pltpu.make_async_remote_copy(src_ref=ins[i], dst_ref=outs[i], send_sem=send_sems.at[i],
                                              recv_sem=recv_sems.at[i], device_id=(x, y, 1 - c), device_id_type=MESH)
            cp.start()
            cps.append(cp)
        for cp in cps:
            cp.wait()

    any_spec = pl.BlockSpec(memory_space=pl.ANY)
    return pl.pallas_call(
        body, name="swap_cores", in_specs=[any_spec] * n, out_specs=[any_spec] * n,
        out_shape=[jax.ShapeDtypeStruct(s.shape, s.dtype) for s in arrs],
        scratch_shapes=[pltpu.SemaphoreType.DMA((n,)), pltpu.SemaphoreType.DMA((n,))],
    )(*arrs)


def _gather_all(block):
    m_per, n = block.shape

    def body(x_ref, out_ref, send_sems, recv_sems, local_sem):
        x, y, c = lax.axis_index("x"), lax.axis_index("y"), lax.axis_index("c")
        me, sibling = (x, y, c), (x, y, 1 - c)
        chips = [(1 - x, y), (x, 1 - y), (1 - x, 1 - y)]

        def rows(px, py, pc):
            return out_ref.at[pl.ds((4 * px + 2 * py + pc) * m_per, m_per), :]

        def copy(k, blk, to, src=None):
            return pltpu.make_async_remote_copy(src_ref=rows(*blk) if src is None else src, dst_ref=rows(*blk),
                                                send_sem=send_sems.at[k], recv_sem=recv_sems.at[k], device_id=to,
                                                device_id_type=MESH)

        mine = pltpu.make_async_copy(x_ref, rows(*me), local_sem)
        mine.start()
        first = [copy(0, me, sibling, src=x_ref)]
        first += [copy(1 + j, me, (*chip, c), src=x_ref) for j, chip in enumerate(chips)]
        for cp in first:
            cp.start()
        passed = [copy(4 + j, (*chip, c), sibling) for j, chip in enumerate(chips)]
        for j, chip in enumerate(chips):
            copy(1 + j, (*chip, c), me).wait_recv()
            passed[j].start()
        copy(0, sibling, me).wait_recv()
        for j, chip in enumerate(chips):
            copy(4 + j, (*chip, 1 - c), me).wait_recv()
        for cp in first + passed:
            cp.wait_send()
        mine.wait()

    return pl.pallas_call(
        body, name="gather_all", out_shape=jax.ShapeDtypeStruct((N_DEV * m_per, n), block.dtype),
        in_specs=[pl.BlockSpec(memory_space=pltpu.VMEM)], out_specs=pl.BlockSpec(memory_space=pltpu.VMEM),
        scratch_shapes=[pltpu.SemaphoreType.DMA((7,)), pltpu.SemaphoreType.DMA((7,)), pltpu.SemaphoreType.DMA],
    )(block)


def _rope_tables(positions):
    inv_freq = 1.0 / (ROPE_THETA ** (jnp.arange(0, MLA_ROPE, 2, dtype=F32) / MLA_ROPE))
    ang = positions.astype(F32).reshape(-1, 1) * inv_freq
    return jnp.tile(jnp.cos(ang), (1, 4)), jnp.tile(jnp.sin(ang), (1, 4))


def _pad_cols(a, n):
    return jnp.pad(a, ((0, 0), (0, n - a.shape[1])))


def _ffn_fwd(l, h, w, nb, seq):
    ug = _mm(h, w["ffn_up_g"][l], name=f"ffn{l}_up_g")
    uv = _mm(h, w["ffn_up_v"][l], name=f"ffn{l}_up_v")
    op = _op_conv(f"ffn{l}_conv", _f_convffn, [ug, uv], [w["ffn_cw_g"][l], w["ffn_cw_v"][l]],
                  [w["ffn_cb_g"][l], w["ffn_cb_v"][l]], nb, seq)
    act = op.fwd((BF16,))[0]
    ff = _mm(act, w["ffn_down"][l], name=f"ffn{l}_down")
    return ff, (h, op, act)


def _ffn_bwd(l, saved, dff, w, grads, dh_skip):
    h, op, act = saved
    grads[f"ffn_down{l}"] = _mm(act, dff, ta=True, name=f"ffn{l}_down_dw")
    dact = _mm(dff, w["ffn_down"][l], tb=True, name=f"ffn{l}_down_dx")
    (dug, duv), (dwg, dwv, dbg, dbv) = op.bwd([dact], [BF16, BF16])
    grads[f"ffn_up_g{l}"] = _mm(h, dug, ta=True, name=f"ffn{l}_up_g_dw")
    grads[f"ffn_up_v{l}"] = _mm(h, duv, ta=True, name=f"ffn{l}_up_v_dw")
    grads[f"ffn_cw_g{l}"], grads[f"ffn_cw_v{l}"], grads[f"ffn_cb_g{l}"], grads[f"ffn_cb_v{l}"] = dwg, dwv, dbg, dbv
    dh = _mm(dug, w["ffn_up_g"][l], tb=True, add=dh_skip, name=f"ffn{l}_up_g_dx")
    return _mm(duv, w["ffn_up_v"][l], tb=True, add=dh, name=f"ffn{l}_up_v_dx")


class _Carried:
    def __init__(self, shards, finish_weights, ffn_shards, set_ffn, grad_stacks):
        self.shards, self.finish_weights, self.ffn_shards, self.set_ffn = shards, finish_weights, ffn_shards, set_ffn
        self.grad_stacks, self.received = grad_stacks, None


def _local_step(x, positions, target, w, carried=None):
    nb, seq, d = x.shape
    t = nb * seq
    x2, tgt2 = x.reshape(t, d), target.reshape(t, d)
    cos, sin = _rope_tables(positions)
    grads = {}

    xb = x2.astype(BF16)

    z = _mm(xb, w["in_z"], name="ssd_in_z")
    raw = [_mm(xb, w[k], name="ssd_" + k) for k in ("in_x", "in_b", "in_c")]
    dt_raw = _mm(xb, w["in_dt"], name="ssd_in_dt")
    conv_ops = [_op_conv("ssd_conv_" + s, _f_convsilu, [r], [w["conv_w_" + s]], [w["conv_b_" + s]], nb, seq)
                for s, r in zip("xbc", raw)]
    xs, bm, cm = [op.fwd()[0] for op in conv_ops]
    dt_op = _op_dt("ssd_dt", dt_raw, w["dt_bias"], w["a_log"])
    dt, cum, ecum, wend = dt_op.fwd()

    def to_rows(v):
        v = jnp.swapaxes(v.reshape(nb, seq, LANE), 1, 2)[:, :SSD_H, :].reshape(nb * SSD_G, SSD_K, seq)
        return jnp.pad(v, ((0, 0), (0, SSD_KPAD - SSD_K), (0, 0)))

    def from_rows(v):
        v = v[:, :SSD_K, :].reshape(nb, SSD_H, seq)
        return jnp.swapaxes(jnp.pad(v, ((0, 0), (0, LANE - SSD_H), (0, 0))), 1, 2).reshape(t, LANE)

    ssd_rows = jnp.concatenate([to_rows(cum), to_rows(dt), to_rows(ecum), to_rows(wend)], axis=1)
    y, states, *gathered = _ssd_scan_fwd(xs, bm, cm, ssd_rows, nb, seq, carried.shards if carried else ())
    if carried:
        w = {**w, **carried.finish_weights(gathered)}
    post_op = _op_ssd_post("ssd_post", y, xs, z, w["d_exp"], w["norm_g"])
    yn = post_op.fwd((BF16,))[0]
    mix0 = _mm(yn, w["ssd_out"], name="ssd_out")
    ln0m = _op_ln("ln_mix0", x2, mix0, w["ln_mix_g"][0], w["ln_mix_b"][0])
    h0a, h0a_b = ln0m.fwd((F32, BF16))
    ff0, ffn0_saved = _ffn_fwd(0, h0a_b, w, nb, seq)
    ln0f = _op_ln("ln_ffn0", h0a, ff0, w["ln_ffn_g"][0], w["ln_ffn_b"][0])
    h1, h1_b = ln0f.fwd((F32, BF16))

    down = _mm(h1_b, w["kvq_down"], name="kvq_down")
    kvn_op = _op_rms("kv_norm", down, w["kv_norm_g"], window=(0, MLA_KV_RANK))
    ckvn = kvn_op.fwd((BF16,))[0]
    kr_op = _op_rope("k_rope", _f_rope_dup, down, cos, sin, window=(MLA_KV_RANK // LANE, 1))
    kr = kr_op.fwd((BF16,))[0]
    kn = _mm(ckvn, w["kv_up_k"], out_dtype=BF16, name="kv_up_k")
    v = _mm(ckvn, w["kv_up_v"], out_dtype=BF16, name="kv_up_v")
    qn_op = _op_rms("q_norm", down, w["q_norm_g"], window=(1, MLA_Q_RANK))
    cq = qn_op.fwd((BF16,))[0]
    qn = _mm(cq, w["q_up_n"], out_dtype=BF16, name="q_up_n")
    qr_raw = _mm(cq, w["q_up_r"], name="q_up_r")
    qr_op = _op_rope("q_rope", _f_rope, qr_raw, cos, sin)
    qr = qr_op.fwd((BF16,))[0]
    o, lse, *gathered = _attn_fwd(qn, qr, kn, kr, v, nb, seq, carried.ffn_shards if carried else ())
    if carried:
        carried.set_ffn(w, gathered)
    mix1 = _mm(o, w["attn_out"], name="attn_out")
    ln1m = _op_ln("ln_mix1", h1, mix1, w["ln_mix_g"][1], w["ln_mix_b"][1])
    h1a, h1a_b = ln1m.fwd((F32, BF16))
    ff1, ffn1_saved = _ffn_fwd(1, h1a_b, w, nb, seq)
    ln1f = _op_ln("ln_ffn1", h1a, ff1, w["ln_ffn_g"][1], w["ln_ffn_b"][1])
    h2 = ln1f.fwd()[0]

    loss_parts, dh2 = _loss_head(h2, tgt2)

    (dh1a, dff1), (grads["ln_ffn_g1"], grads["ln_ffn_b1"]) = ln1f.bwd([dh2], [F32, BF16])
    dh1a = _ffn_bwd(1, ffn1_saved, dff1, w, grads, dh1a)
    (dh1, dmix1), (grads["ln_mix_g1"], grads["ln_mix_b1"]) = ln1m.bwd([dh1a], [F32, BF16])
    grads["attn_out"] = _mm(o, dmix1, ta=True, name="attn_out_dw")
    do = _mm(dmix1, w["attn_out"], tb=True, name="attn_out_dx")
    dqn, dqr, dkn, dkr, dv = _attn_bwd(qn, qr, kn, kr, v, o, lse, do, nb, seq)
    (dqr_raw,), _ = qr_op.bwd([dqr], [BF16])
    grads["q_up_n"] = _mm(cq, dqn, ta=True, name="q_up_n_dw")
    grads["q_up_r"] = _mm(cq, dqr_raw, ta=True, name="q_up_r_dw")
    dcq = _mm(dqn, w["q_up_n"], tb=True, name="q_up_n_dx")
    dcq = _mm(dqr_raw, w["q_up_r"], tb=True, add=dcq, name="q_up_r_dx")
    (dcq_raw,), (grads["q_norm_g"],) = qn_op.bwd([dcq], [BF16])
    grads["kv_up_k"] = _mm(ckvn, dkn, ta=True, name="kv_up_k_dw")
    grads["kv_up_v"] = _mm(ckvn, dv, ta=True, name="kv_up_v_dw")
    dckvn = _mm(dkn, w["kv_up_k"], tb=True, name="kv_up_k_dx")
    dckvn = _mm(dv, w["kv_up_v"], tb=True, add=dckvn, name="kv_up_v_dx")
    (dckv,), (grads["kv_norm_g"],) = kvn_op.bwd([dckvn], [BF16])
    (dkr_in,), _ = kr_op.bwd([dkr], [BF16])
    ddown = jnp.concatenate([dckv, dkr_in, dcq_raw], axis=1)
    grads["kvq_down"] = _mm(h1_b, ddown, ta=True, name="kvq_down_dw")
    dh1 = _mm(ddown, w["kvq_down"], tb=True, add=dh1, name="kvq_down_dx")

    (dh0a, dff0), (grads["ln_ffn_g0"], grads["ln_ffn_b0"]) = ln0f.bwd([dh1], [F32, BF16])
    dh0a = _ffn_bwd(0, ffn0_saved, dff0, w, grads, dh0a)
    (dx, dmix0), (grads["ln_mix_g0"], grads["ln_mix_b0"]) = ln0m.bwd([dh0a], [F32, BF16])
    grads["ssd_out"] = _mm(yn, dmix0, ta=True, name="ssd_out_dw")
    dyn = _mm(dmix0, w["ssd_out"], tb=True, name="ssd_out_dx")
    (dy, dxs_post, dz), (grads["d_exp"], grads["norm_g"]) = post_op.bwd([dyn], [F32, F32, BF16])
    dxs, dbm, dcm, drows, *received = _ssd_scan_bwd(xs, bm, cm, ssd_rows, states, dy, dxs_post, nb, seq,
                                                    carried.grad_stacks(grads) if carried else ())
    if carried:
        carried.received = received
    d_dt, d_cum, d_ecum, d_wend = [from_rows(drows[:, o:o + SSD_KPAD]) for o in (_DT, _CUM, _ECUM, _WEND)]
    (ddt_raw,), (grads["dt_bias"], grads["a_log"]) = dt_op.bwd([d_dt, d_cum, d_ecum, d_wend], [BF16])
    draws = []
    for s, op, dout in zip("xbc", conv_ops, (dxs, dbm, dcm)):
        (dr,), (grads["conv_w_" + s], grads["conv_b_" + s]) = op.bwd([dout], [BF16])
        draws.append(dr)
    for k, dr in zip(("in_x", "in_b", "in_c"), draws):
        grads[k] = _mm(xb, dr, ta=True, name=f"ssd_{k}_dw")
        dx = _mm(dr, w[k], tb=True, add=dx, name=f"ssd_{k}_dx")
    grads["in_z"] = _mm(xb, dz, ta=True, name="ssd_in_z_dw")
    grads["in_dt"] = _mm(xb, ddt_raw, ta=True, name="ssd_in_dt_dw")
    dx = _mm(dz, w["in_z"], tb=True, add=dx, name="ssd_in_z_dx")
    dx = _mm(ddt_raw, w["in_dt"], tb=True, add=dx, name="ssd_in_dt_dx")
    return loss_parts, dx.reshape(nb, seq, d), grads


_XE, _BE, _CE = SSD_D_INNER, SSD_D_INNER + SSD_GN, SSD_D_INNER + 2 * SSD_GN
_KVR = MLA_KV_RANK + LANE


def _prep_weights(fw):
    w = {}
    f = FFN_HIDDEN
    if "ssd_in_proj" in fw:
        ip = fw["ssd_in_proj"][0]
        o = SSD_D_INNER
        w["in_z"], w["in_x"], w["in_b"], w["in_c"] = ip[:, :o], ip[:, o:o + _XE], ip[:, o + _XE:o + _BE], ip[:, o + _BE:o + _CE]
        w["in_dt"] = _pad_cols(ip[:, o + _CE:], LANE)
        cw, cb = fw["ssd_conv_w"][0], fw["ssd_conv_b"]
        for s, (lo, hi) in zip("xbc", ((0, _XE), (_XE, _BE), (_BE, _CE))):
            w["conv_w_" + s], w["conv_b_" + s] = cw[:, lo:hi], cb[:, lo:hi]
    if "ssd_dt_bias" in fw:
        w["dt_bias"], w["a_log"] = _pad_cols(fw["ssd_dt_bias"], LANE), _pad_cols(fw["ssd_A_log"], LANE)
        w["d_exp"] = jnp.repeat(fw["ssd_D"][0], SSD_P)[None, :]
        w["kv_norm_g"], w["q_norm_g"] = fw["kv_norm_g"][None, :], fw["q_norm_g"]
        w["ffn_cb_g"] = [fw["ffn_conv_b"][l:l + 1, :f] for l in range(DEPTH)]
        w["ffn_cb_v"] = [fw["ffn_conv_b"][l:l + 1, f:] for l in range(DEPTH)]
        for k in ("ln_mix_g", "ln_mix_b", "ln_ffn_g", "ln_ffn_b"):
            w[k] = [fw[k][l:l + 1] for l in range(DEPTH)]
    if "ssd_out_proj" in fw:
        w["norm_g"], w["ssd_out"] = fw["ssd_norm_g"], fw["ssd_out_proj"][0]
        kd = fw["kv_down_proj"]
        w["kvq_down"] = jnp.concatenate([_pad_cols(kd, _KVR), fw["q_down_proj"][0]], axis=1)
        w["kv_up_k"], w["kv_up_v"] = fw["kv_up_k"], fw["kv_up_v"]
        qu = fw["q_up_proj"][0].reshape(MLA_Q_RANK, MLA_H, MLA_NOPE + MLA_ROPE)
        w["q_up_n"] = qu[:, :, :MLA_NOPE].reshape(MLA_Q_RANK, MLA_H * MLA_NOPE)
        w["q_up_r"] = qu[:, :, MLA_NOPE:].reshape(MLA_Q_RANK, MLA_H * MLA_ROPE)
        w["attn_out"] = fw["attn_out_proj"][0]
        w["ffn_up_g"], w["ffn_up_v"], w["ffn_down"] = [None] * DEPTH, [None] * DEPTH, [None] * DEPTH
        for l in range(DEPTH):
            if fw["ffn_up"][l] is not None:
                _set_ffn_weights(w, l, fw["ffn_up"][l], fw["ffn_down"][l])
        w["ffn_cw_g"] = [fw["ffn_conv_w"][l][:, :f] for l in range(DEPTH)]
        w["ffn_cw_v"] = [fw["ffn_conv_w"][l][:, f:] for l in range(DEPTH)]
    return w


def _set_ffn_weights(w, l, up, down):
    w["ffn_up_g"][l], w["ffn_up_v"][l], w["ffn_down"][l] = up[:, :FFN_HIDDEN], up[:, FFN_HIDDEN:], down


def _assemble_grads(g, names):
    make = {
        "ssd_in_proj": lambda: jnp.concatenate([g["in_z"], g["in_x"], g["in_b"], g["in_c"], g["in_dt"][:, :SSD_H]], axis=1)[None],
        "ssd_conv_w": lambda: jnp.concatenate([g["conv_w_" + s] for s in "xbc"], axis=1)[None],
        "ssd_conv_b": lambda: jnp.concatenate([g["conv_b_" + s] for s in "xbc"], axis=1),
        "ssd_dt_bias": lambda: g["dt_bias"][:, :SSD_H],
        "ssd_A_log": lambda: g["a_log"][:, :SSD_H],
        "ssd_D": lambda: jnp.sum(g["d_exp"].reshape(SSD_H, SSD_P), axis=1)[None, :],
        "ssd_norm_g": lambda: g["norm_g"],
        "ssd_out_proj": lambda: g["ssd_out"][None],
        "kv_down_proj": lambda: g["kvq_down"][:, :MLA_KV_RANK + MLA_ROPE],
        "kv_norm_g": lambda: g["kv_norm_g"][0],
        "kv_up_k": lambda: g["kv_up_k"],
        "kv_up_v": lambda: g["kv_up_v"],
        "q_down_proj": lambda: g["kvq_down"][None, :, _KVR:],
        "q_norm_g": lambda: g["q_norm_g"],
        "q_up_proj": lambda: jnp.concatenate([g["q_up_n"].reshape(MLA_Q_RANK, MLA_H, MLA_NOPE),
                                              g["q_up_r"].reshape(MLA_Q_RANK, MLA_H, MLA_ROPE)], axis=2).reshape(1, MLA_Q_RANK, -1),
        "attn_out_proj": lambda: g["attn_out"][None],
        "ffn_up": lambda: jnp.stack([jnp.concatenate([g[f"ffn_up_g{l}"], g[f"ffn_up_v{l}"]], axis=1) for l in range(DEPTH)]),
        "ffn_conv_w": lambda: jnp.stack([jnp.concatenate([g[f"ffn_cw_g{l}"], g[f"ffn_cw_v{l}"]], axis=1) for l in range(DEPTH)]),
        "ffn_conv_b": lambda: jnp.concatenate([jnp.concatenate([g[f"ffn_cb_g{l}"], g[f"ffn_cb_v{l}"]], axis=1)
                                               for l in range(DEPTH)], axis=0),
        "ffn_down": lambda: jnp.stack([g[f"ffn_down{l}"] for l in range(DEPTH)]),
    }
    for k in ("ln_mix_g", "ln_mix_b", "ln_ffn_g", "ln_ffn_b"):
        make[k] = lambda k=k: jnp.concatenate([g[f"{k}{l}"] for l in range(DEPTH)], axis=0)
    return {n: make[n]() for n in names}


_WEIGHTS = ["ssd_in_proj", "ssd_conv_w", "ssd_conv_b", "ssd_dt_bias", "ssd_A_log", "ssd_D", "ssd_norm_g", "ssd_out_proj",
            "kv_down_proj", "kv_norm_g", "kv_up_k", "kv_up_v", "q_down_proj", "q_norm_g", "q_up_proj", "attn_out_proj",
            "ffn_up", "ffn_conv_w", "ffn_conv_b", "ffn_down", "ln_mix_g", "ln_mix_b", "ln_ffn_g", "ln_ffn_b"]
_COL_CUT = ["ssd_in_proj", "ssd_conv_w", "ssd_conv_b", "ssd_norm_g", "kv_up_k", "kv_up_v", "q_up_proj", "ffn_up", "ffn_conv_w"]
_ROW_CUT = ["ssd_out_proj", "kv_down_proj", "q_down_proj", "attn_out_proj", "ffn_down"]
_CUT = _COL_CUT + _ROW_CUT
_WHOLE = [n for n in _WEIGHTS if n not in _CUT]
_EARLY = ["ssd_in_proj", "ssd_conv_w", "ssd_conv_b"]
_LATE = [n for n in _CUT if n not in _EARLY]
_FFN_MATRICES = ["ffn_up", "ffn_down"]
_MXU_WEIGHTS = ["ssd_in_proj", "ssd_out_proj", "kv_down_proj", "kv_up_k", "kv_up_v", "q_down_proj", "q_up_proj",
                "attn_out_proj", "ffn_up", "ffn_down"]
_PACK_ROWS = 160


def _shard_2d(name, s):
    return s.reshape(-1, s.shape[-1])


def _unstack(name, g, shard_shape):
    if name in _COL_CUT:
        lead = shard_shape[:-1]
        return jnp.swapaxes(g, 0, 1).reshape(*lead, N_CHIPS * shard_shape[-1])
    lead, rs, c = shard_shape[:-2], shard_shape[-2], shard_shape[-1]
    n_lead = math.prod(lead)
    return jnp.swapaxes(g.reshape(N_CHIPS, n_lead, rs, c), 0, 1).reshape(*lead, N_CHIPS * rs, c)


def _stack(name, full, shard_shape):
    if name in _COL_CUT:
        cs = shard_shape[-1]
        return jnp.swapaxes(full.reshape(-1, N_CHIPS, cs), 0, 1)
    lead, rs, c = shard_shape[:-2], shard_shape[-2], shard_shape[-1]
    n_lead = math.prod(lead)
    return jnp.swapaxes(full.reshape(n_lead, N_CHIPS, rs, c), 0, 1).reshape(N_CHIPS, n_lead * rs, c)


def _pack(arrs):
    flat = jnp.concatenate([a.reshape(-1) for a in arrs])
    return jnp.pad(flat, (0, _PACK_ROWS * LANE - flat.shape[0])).reshape(_PACK_ROWS, LANE)


def _unpack(packed, like):
    flat, out, o = packed.reshape(-1), [], 0
    for a in like:
        out.append(flat[o:o + a.size].reshape(a.shape))
        o += a.size
    return out


_ARGS = ["x", "positions"] + _WEIGHTS + ["loss_target"] + ["m_" + n for n in _WEIGHTS] + ["v_" + n for n in _WEIGHTS]


def kernel(x, positions, ssd_in_proj, ssd_conv_w, ssd_conv_b, ssd_dt_bias, ssd_A_log, ssd_D, ssd_norm_g,
           ssd_out_proj, kv_down_proj, kv_norm_g, kv_up_k, kv_up_v, q_down_proj, q_norm_g, q_up_proj,
           attn_out_proj, ffn_up, ffn_conv_w, ffn_conv_b, ffn_down, ln_mix_g, ln_mix_b, ln_ffn_g, ln_ffn_b,
           loss_target, m_ssd_in_proj, m_ssd_conv_w, m_ssd_conv_b, m_ssd_dt_bias, m_ssd_A_log, m_ssd_D,
           m_ssd_norm_g, m_ssd_out_proj, m_kv_down_proj, m_kv_norm_g, m_kv_up_k, m_kv_up_v, m_q_down_proj,
           m_q_norm_g, m_q_up_proj, m_attn_out_proj, m_ffn_up, m_ffn_conv_w, m_ffn_conv_b, m_ffn_down,
           m_ln_mix_g, m_ln_mix_b, m_ln_ffn_g, m_ln_ffn_b, v_ssd_in_proj, v_ssd_conv_w, v_ssd_conv_b,
           v_ssd_dt_bias, v_ssd_A_log, v_ssd_D, v_ssd_norm_g, v_ssd_out_proj, v_kv_down_proj, v_kv_norm_g,
           v_kv_up_k, v_kv_up_v, v_q_down_proj, v_q_norm_g, v_q_up_proj, v_attn_out_proj, v_ffn_up,
           v_ffn_conv_w, v_ffn_conv_b, v_ffn_down, v_ln_mix_g, v_ln_mix_b, v_ln_ffn_g, v_ln_ffn_b):
    args = (x, positions, ssd_in_proj, ssd_conv_w, ssd_conv_b, ssd_dt_bias, ssd_A_log, ssd_D, ssd_norm_g,
            ssd_out_proj, kv_down_proj, kv_norm_g, kv_up_k, kv_up_v, q_down_proj, q_norm_g, q_up_proj,
            attn_out_proj, ffn_up, ffn_conv_w, ffn_conv_b, ffn_down, ln_mix_g, ln_mix_b, ln_ffn_g, ln_ffn_b,
            loss_target, m_ssd_in_proj, m_ssd_conv_w, m_ssd_conv_b, m_ssd_dt_bias, m_ssd_A_log, m_ssd_D,
            m_ssd_norm_g, m_ssd_out_proj, m_kv_down_proj, m_kv_norm_g, m_kv_up_k, m_kv_up_v, m_q_down_proj,
            m_q_norm_g, m_q_up_proj, m_attn_out_proj, m_ffn_up, m_ffn_conv_w, m_ffn_conv_b, m_ffn_down,
            m_ln_mix_g, m_ln_mix_b, m_ln_ffn_g, m_ln_ffn_b, v_ssd_in_proj, v_ssd_conv_w, v_ssd_conv_b,
            v_ssd_dt_bias, v_ssd_A_log, v_ssd_D, v_ssd_norm_g, v_ssd_out_proj, v_kv_down_proj, v_kv_norm_g,
            v_kv_up_k, v_kv_up_v, v_q_down_proj, v_q_norm_g, v_q_up_proj, v_attn_out_proj, v_ffn_up,
            v_ffn_conv_w, v_ffn_conv_b, v_ffn_down, v_ln_mix_g, v_ln_mix_b, v_ln_ffn_g, v_ln_ffn_b)
    a = dict(zip(_ARGS, args, strict=True))

    last = DEPTH - 1

    def layers(n, late):
        if n not in _FFN_MATRICES:
            return a[n]
        return a[n][last:] if late else a[n][:last]

    def shard(n, late=False):
        s = _shard_2d(n, layers(n, late))
        return s.astype(BF16) if n in _MXU_WEIGHTS else s

    def full_weights(names, gathered, late=False):
        fw = {n: _unstack(n, g, layers(n, late).shape) for n, g in zip(names, gathered)}
        for n in _FFN_MATRICES:
            if n in fw:
                fw[n] = [None] * last + list(fw[n]) if late else list(fw[n]) + [None]
        return fw

    def set_last_ffn(w, gathered):
        fw = full_weights(_FFN_MATRICES, gathered, late=True)
        _set_ffn_weights(w, last, fw["ffn_up"][last], fw["ffn_down"][last])

    def grad_stacks(names, pieces):
        full = _assemble_grads(pieces, names)
        return [_stack(n, full[n], a[n].shape).astype(BF16) for n in names]

    fw = full_weights(_EARLY, _gather_chips([shard(n) for n in _EARLY]))
    fw.update({n: a[n] for n in _WHOLE})
    carried = _Carried([shard(n) for n in _LATE], lambda got: _prep_weights(full_weights(_LATE, got)),
                       [shard(n, late=True) for n in _FFN_MATRICES], set_last_ffn,
                       lambda pieces: grad_stacks(_LATE, pieces))

    loss_parts, grad_x, pieces = _local_step(a["x"], a["positions"], a["loss_target"], _prep_weights(fw), carried)
    loss = lax.psum(jnp.sum(loss_parts[::8, 0]), ("x", "y", "c"))

    bufs = dict(zip(_LATE, carried.received))
    bufs.update(zip(_EARLY, _scatter_chips(grad_stacks(_EARLY, pieces))))
    sums = [_sum_parts(bufs[n], "sum_chips_" + n) for n in _CUT]
    others = _swap_cores(sums)
    full = _assemble_grads(pieces, _WHOLE)
    every = _gather_all(_pack([full[n] for n in _WHOLE]))
    g_whole = _sum_parts(every.reshape(N_DEV, _PACK_ROWS, LANE), "sum_devices")

    res = {}
    for n, s, o in zip(_CUT, sums, others):
        out = _adamw(_shard_2d(n, a[n]), s, o, _shard_2d(n, a["m_" + n]), _shard_2d(n, a["v_" + n]), "adamw_" + n)
        res[n] = [r.reshape(a[n].shape) for r in out]
    whole = [a[n] for n in _WHOLE]
    out = _adamw(_pack(whole), g_whole, None, _pack([a["m_" + n] for n in _WHOLE]), _pack([a["v_" + n] for n in _WHOLE]),
                 "adamw_whole")
    for k, n in enumerate(_WHOLE):
        res[n] = [_unpack(r, whole)[k] for r in out]
    return (loss, grad_x, *[res[n][0] for n in _WEIGHTS], *[res[n][1] for n in _WEIGHTS],
            *[res[n][2] for n in _WEIGHTS], *[res[n][3] for n in _WEIGHTS])
```

```python
import functools
import math

import jax
import jax.numpy as jnp
from jax import lax
from jax.experimental import pallas as pl
from jax.experimental.pallas import tpu as pltpu

F32 = jnp.float32
BF16 = jnp.bfloat16
HIGHEST = lax.Precision.HIGHEST
MESH = pl.DeviceIdType.MESH

DEPTH = 2
DN_ALPHA = (2 * DEPTH) ** 0.25
SSD_D_INNER = 2048
SSD_P = 64
SSD_H = 32
SSD_G = 8
SSD_K = 4
SSD_N = 128
SSD_L = 128
SSD_GN = SSD_G * SSD_N
MLA_H = 8
MLA_Q_RANK = 384
MLA_KV_RANK = 256
MLA_NOPE = 128
MLA_ROPE = 64
ROPE_THETA = 10000.0
FFN_HIDDEN = 2816
LN_EPS = 1e-5
RMS_EPS = 1e-6
ADAM_LR = 0.001
ADAM_B1 = 0.9
ADAM_B2 = 0.999
ADAM_EPS = 1e-08
ADAM_WD = 0.01
ADAM_STEP = 10

N_CHIPS = 4
N_DEV = 8
LANE = 128
VMEM_LIMIT = 56 * 1024 * 1024


def _cparams(**kw):
    return pltpu.CompilerParams(vmem_limit_bytes=VMEM_LIMIT, **kw)


def _tile(dim, cap):
    best = None
    t = LANE
    while t <= min(dim, cap):
        if dim % t == 0:
            best = t
        t += LANE
    return dim if best is None else best


MM_TILE_CAP = 1408
MM_WHOLE_K = 2816
MM_VMEM_BUDGET = 40 * 1024 * 1024


def _mm_tiles(m, n, k, a_bytes, b_bytes, o_bytes, has_add):
    tm, tn = _tile(m, MM_TILE_CAP), _tile(n, MM_TILE_CAP)
    tk = k if k <= MM_WHOLE_K else _tile(k, 1024)

    def need(tm, tn):
        acc = tm * tn * 4 if tk < k else 0
        return 2 * (tm * tk * a_bytes + tk * tn * b_bytes + tm * tn * o_bytes + (tm * tn * 4 if has_add else 0)) + acc

    while need(tm, tn) > MM_VMEM_BUDGET:
        if tm >= tn and _tile(m, tm // 2) < tm:
            tm = _tile(m, tm // 2)
        elif _tile(n, tn // 2) < tn:
            tn = _tile(n, tn // 2)
        else:
            break
    return tm, tn, tk


def _mm(a, b, *, ta=False, tb=False, add=None, out_dtype=F32, name):
    m, k = (a.shape[1], a.shape[0]) if ta else a.shape
    n = b.shape[0] if tb else b.shape[1]
    assert (b.shape[1] if tb else b.shape[0]) == k
    tm, tn, tk = _mm_tiles(m, n, k, a.dtype.itemsize, b.dtype.itemsize, jnp.dtype(out_dtype).itemsize, add is not None)
    nk = k // tk
    dims = (((0 if ta else 1,), (1 if tb else 0,)), ((), ()))

    def partial_product(a_ref, b_ref):
        return lax.dot_general(a_ref[...].astype(BF16), b_ref[...].astype(BF16), dims, preferred_element_type=F32)

    def body_one(*refs):
        if add is None:
            a_ref, b_ref, o_ref = refs
            o_ref[...] = partial_product(a_ref, b_ref).astype(out_dtype)
        else:
            a_ref, b_ref, c_ref, o_ref = refs
            o_ref[...] = (partial_product(a_ref, b_ref) + c_ref[...]).astype(out_dtype)

    def body_acc(*refs):
        if add is None:
            a_ref, b_ref, o_ref, acc = refs
        else:
            a_ref, b_ref, c_ref, o_ref, acc = refs
        kk = pl.program_id(2)

        @pl.when(kk == 0)
        def _():
            acc[...] = jnp.zeros_like(acc) if add is None else c_ref[...]

        acc[...] += partial_product(a_ref, b_ref)

        @pl.when(kk == nk - 1)
        def _():
            o_ref[...] = acc[...].astype(out_dtype)

    a_spec = pl.BlockSpec((tk, tm), lambda i, j, kk: (kk, i)) if ta else pl.BlockSpec((tm, tk), lambda i, j, kk: (i, kk))
    b_spec = pl.BlockSpec((tn, tk), lambda i, j, kk: (j, kk)) if tb else pl.BlockSpec((tk, tn), lambda i, j, kk: (kk, j))
    o_spec = pl.BlockSpec((tm, tn), lambda i, j, kk: (i, j))
    ins, specs = [a, b], [a_spec, b_spec]
    if add is not None:
        ins.append(add)
        specs.append(o_spec)
    return pl.pallas_call(
        body_one if nk == 1 else body_acc, name=name, grid=(m // tm, n // tn, nk), in_specs=specs, out_specs=o_spec,
        out_shape=jax.ShapeDtypeStruct((m, n), out_dtype),
        scratch_shapes=[] if nk == 1 else [pltpu.VMEM((tm, tn), F32)],
        compiler_params=_cparams(dimension_semantics=("parallel", "parallel", "arbitrary")),
    )(*ins)


def _spec(op):
    return pl.BlockSpec(op[1], op[2])


def _bw_fwd(name, fn, grid, ins, outs, out_dtypes):
    n_in = len(ins)
    flat = [(o, dt) for o, dts in zip(outs, out_dtypes) for dt in dts]

    def body(*refs):
        res = fn(*[r[...].astype(F32) for r in refs[:n_in]])
        orefs = iter(refs[n_in:])
        for v, dts in zip(res, out_dtypes):
            for dt in dts:
                next(orefs)[...] = v.astype(dt)

    return pl.pallas_call(
        body, name=name, grid=grid, in_specs=[_spec(o) for o in ins],
        out_specs=[pl.BlockSpec(o[1], o[2]) for o, _ in flat],
        out_shape=[jax.ShapeDtypeStruct(o[0], dt) for o, dt in flat], compiler_params=_cparams(),
    )(*[o[0] for o in ins])


def _bw_bwd(name, fn, grid, data, params, consts, cts, red_axes, grad_dtypes):
    nd, npar, nc, nct = len(data), len(params), len(consts), len(cts)

    def body(*refs):
        first = None
        for ax in red_axes:
            z = pl.program_id(ax) == 0
            first = z if first is None else jnp.logical_and(first, z)
        vals = [r[...].astype(F32) for r in refs[:nd + npar + nc + nct]]
        d, p, c, g = vals[:nd], vals[nd:nd + npar], vals[nd + npar:nd + npar + nc], vals[nd + npar + nc:]
        _, vjp = jax.vjp(lambda dd, pp: tuple(fn(*dd, *pp, *c)), d, p)
        gd, gp = vjp(tuple(g))
        orefs = refs[nd + npar + nc + nct:]
        for r, v in zip(orefs[:nd], gd):
            r[...] = v.astype(r.dtype)
        if npar:
            @pl.when(first)
            def _():
                for r in orefs[nd:]:
                    r[...] = jnp.zeros_like(r)

            for r, v in zip(orefs[nd:], gp):
                r[...] += v

    ins = [d[:3] for d in data] + list(params) + list(consts) + list(cts)
    outs = [d[3] if len(d) > 3 else d for d in data] + list(params)
    dtypes = list(grad_dtypes) + [F32] * npar
    return pl.pallas_call(
        body, name=name, grid=grid, in_specs=[_spec(o) for o in ins], out_specs=[_spec(o) for o in outs],
        out_shape=[jax.ShapeDtypeStruct(o[0].shape, dt) for o, dt in zip(outs, dtypes)], compiler_params=_cparams(),
    )(*[o[0] for o in ins])


def _shift_down(x, s):
    row = lax.broadcasted_iota(jnp.int32, x.shape, 0)
    return jnp.where(row < s, 0.0, pltpu.roll(x, s, 0))


def _shift_up(x, s):
    n = x.shape[0]
    row = lax.broadcasted_iota(jnp.int32, x.shape, 0)
    return jnp.where(row >= n - s, 0.0, pltpu.roll(x, n - s, 0))


def _time_shift(s):
    if s == 0:
        return lambda x: x

    @jax.custom_vjp
    def shift(x):
        return _shift_down(x, s)

    shift.defvjp(lambda x: (_shift_down(x, s), None), lambda _, g: (_shift_up(g, s),))
    return shift


def _rot_half_raw(x):
    lane = lax.broadcasted_iota(jnp.int32, x.shape, 1)
    return jnp.where(lane % MLA_ROPE < MLA_ROPE // 2, -pltpu.roll(x, LANE - MLA_ROPE // 2, 1), pltpu.roll(x, MLA_ROPE // 2, 1))


@jax.custom_vjp
def _rot_half(x):
    return _rot_half_raw(x)


_rot_half.defvjp(lambda x: (_rot_half_raw(x), None), lambda _, g: (-_rot_half_raw(g),))


@jax.custom_vjp
def _roll_half_lanes(x):
    return pltpu.roll(x, LANE // 2, 1)


_roll_half_lanes.defvjp(lambda x: (pltpu.roll(x, LANE // 2, 1), None), lambda _, g: (pltpu.roll(g, LANE // 2, 1),))


def _causal_conv(u, w, b):
    width = w.shape[0]
    y = b
    for k in range(width):
        y = y + w[k:k + 1, :] * _time_shift(width - 1 - k)(u)
    return y


def _silu(x):
    return x * jax.nn.sigmoid(x)


def _f_ln(h, mix, g, b):
    x = DN_ALPHA * h + mix
    mu = jnp.mean(x, axis=-1, keepdims=True)
    xc = x - mu
    var = jnp.mean(xc * xc, axis=-1, keepdims=True)
    return (xc * lax.rsqrt(var + LN_EPS) * g + b,)


def _f_convsilu(u, w, b):
    return (_silu(_causal_conv(u, w, b)),)


def _f_convffn(ug, uv, wg, wv, bg, bv):
    return (_silu(_causal_conv(ug, wg, bg)) * _causal_conv(uv, wv, bv),)


def _f_dt(dt_raw, bias, a_log):
    x = dt_raw + bias
    dt = jnp.maximum(x, 0.0) + jnp.log(1.0 + jnp.exp(-jnp.abs(x)))
    a = dt * (-jnp.exp(a_log))
    n = a.shape[0]
    lower = (lax.broadcasted_iota(jnp.int32, (n, n), 0) >= lax.broadcasted_iota(jnp.int32, (n, n), 1)).astype(F32)
    cum = jnp.dot(lower, a, precision=HIGHEST, preferred_element_type=F32)
    cum_last = jnp.sum(a, axis=0, keepdims=True)
    return dt, cum, jnp.exp(cum), jnp.exp(cum_last - cum) * dt


def _f_ssd_post(y, xs, z, d_exp, ng):
    t = (y + d_exp * xs) * _silu(z)
    return (t * lax.rsqrt(jnp.mean(t * t, axis=-1, keepdims=True) + LN_EPS) * ng,)


def _f_rms(x, g):
    return (x * lax.rsqrt(jnp.mean(x * x, axis=-1, keepdims=True) + RMS_EPS) * g,)


def _f_rope(x, cos, sin):
    return (x * cos + _rot_half(x) * sin,)


def _f_rope_dup(x, cos, sin):
    r = x * cos + _rot_half(x) * sin
    return (r + _roll_half_lanes(r),)


def _rows(t, cap=512):
    for c in (cap, 256, 128, 64, 32, 16, 8):
        if c <= cap and t % c == 0:
            return c
    return t


class _Blockwise:
    def __init__(self, name, fn, grid, data, params, consts, outs, red_axes):
        self.name, self.fn, self.grid = name, fn, grid
        self.data, self.params, self.consts, self.outs, self.red_axes = data, params, consts, outs, red_axes

    def fwd(self, *out_dtypes):
        out_dtypes = out_dtypes or tuple((F32,) for _ in self.outs)
        return _bw_fwd(self.name + "_fwd", self.fn, self.grid, self.data + self.params + self.consts, self.outs, out_dtypes)

    def bwd(self, cts, grad_dtypes=None):
        cts = [(c, o[1], o[2]) for c, o in zip(cts, self.outs)]
        grad_dtypes = grad_dtypes or [F32] * len(self.data)
        res = _bw_bwd(self.name + "_bwd", self.fn, self.grid, self.data, self.params, self.consts, cts, self.red_axes,
                      grad_dtypes)
        return res[:len(self.data)], res[len(self.data):]


def _op_ln(name, h, mix, g, b):
    t, d = h.shape
    tr = _rows(t, 512)
    row = ((tr, d), lambda i: (i, 0))
    par = ((1, d), lambda i: (0, 0))
    return _Blockwise(name, _f_ln, (t // tr,), [(h, *row), (mix, *row)], [(g, *par), (b, *par)], [],
                      [((t, d), *row)], (0,))


def _op_conv(name, fn, us, ws, bs, nb, seq):
    c = us[0].shape[1]
    ct = _tile(c, 256)
    blk = ((seq, ct), lambda j, bb: (bb, j))
    data = [(u, *blk) for u in us]
    params = [(w, (w.shape[0], ct), lambda j, bb: (0, j)) for w in ws] + [(b, (1, ct), lambda j, bb: (0, j)) for b in bs]
    return _Blockwise(name, fn, (c // ct, nb), data, params, [], [((nb * seq, c), *blk)], (1,))


def _op_dt(name, dt_raw, bias, a_log):
    t = dt_raw.shape[0]
    row = ((SSD_L, LANE), lambda i: (i, 0))
    par = ((1, LANE), lambda i: (0, 0))
    return _Blockwise(name, _f_dt, (t // SSD_L,), [(dt_raw, *row)], [(bias, *par), (a_log, *par)], [],
                      [((t, LANE), *row)] * 4, (0,))


def _op_ssd_post(name, y, xs, z, d_exp, ng):
    t, c = y.shape
    gw = c // SSD_G
    tr = _rows(t, 512)
    blk = ((tr, gw), lambda g, i: (i, g))
    par = ((1, gw), lambda g, i: (0, g))
    return _Blockwise(name, _f_ssd_post, (SSD_G, t // tr), [(y, *blk), (xs, *blk), (z, *blk)],
                      [(d_exp, *par), (ng, *par)], [], [((t, c), *blk)], (1,))


def _op_rms(name, x, g, window=None):
    t = x.shape[0]
    j, c = window or (0, x.shape[1])
    tr = _rows(t, 512)
    own = ((tr, c), lambda i: (i, 0))
    return _Blockwise(name, _f_rms, (t // tr,), [(x, (tr, c), lambda i: (i, j), (jax.ShapeDtypeStruct((t, c), F32), *own))],
                      [(g, (1, c), lambda i: (0, 0))], [], [((t, c), *own)], (0,))


def _op_rope(name, fn, x, cos, sin, window=None):
    t = x.shape[0]
    j0, n = window or (0, x.shape[1] // LANE)
    tr = _rows(t, 512)
    own = ((tr, LANE), lambda i, j: (i, j))
    cs = ((tr, LANE), lambda i, j: (i, 0))
    return _Blockwise(name, fn, (t // tr, n), [(x, (tr, LANE), lambda i, j: (i, j0 + j), (jax.ShapeDtypeStruct((t, n * LANE), F32), *own))],
                      [], [(cos, *cs), (sin, *cs)], [((t, n * LANE), *own)], ())


_NT = (((1,), (1,)), ((), ()))
_TN = (((0,), (0,)), ((), ()))


def _ssd_head(x, g, bc, cc, s, cum_row, dt_row, cum_col, ecum_col, wend_col):
    n = g.shape[0]
    row = lax.broadcasted_iota(jnp.int32, (n, n), 0)
    col = lax.broadcasted_iota(jnp.int32, (n, n), 1)
    decay = jnp.exp(jnp.where(row >= col, cum_col - cum_row, -jnp.inf))
    w = g * decay * dt_row
    y = jnp.dot(w.astype(BF16), x.astype(BF16), preferred_element_type=F32)
    y = y + lax.dot_general((cc * ecum_col).astype(BF16), s.astype(BF16), _NT, preferred_element_type=F32)
    lane = lax.broadcasted_iota(jnp.int32, cum_row.shape, 1)
    cum_last = jnp.sum(jnp.where(lane == n - 1, cum_row, 0.0), axis=1, keepdims=True)
    s_new = s * jnp.exp(cum_last) + lax.dot_general(x.astype(BF16), (bc * wend_col).astype(BF16), _TN,
                                                    preferred_element_type=F32)
    return y, s_new


SSD_KPAD = 8
SSD_ROWS = 4 * SSD_KPAD
_CUM, _DT, _ECUM, _WEND = 0, SSD_KPAD, 2 * SSD_KPAD, 3 * SSD_KPAD


def _ssd_head_args(rows, cols, k):
    return (rows[_CUM + k:_CUM + k + 1], rows[_DT + k:_DT + k + 1], cols[:, _CUM + k:_CUM + k + 1],
            cols[:, _ECUM + k:_ECUM + k + 1], cols[:, _WEND + k:_WEND + k + 1])


def _carried_exchange(arrs, in_refs, out_refs, sems, scatter, n_steps):
    if not arrs:
        return lambda: None, lambda: None
    first = functools.reduce(jnp.logical_and, [pl.program_id(ax) == 0 for ax in range(len(n_steps))])
    last = functools.reduce(jnp.logical_and, [pl.program_id(ax) == n - 1 for ax, n in enumerate(n_steps)])
    start, wait = _chip_exchange(in_refs, out_refs, sems, scatter)
    return (lambda: pl.when(first)(start)), (lambda: pl.when(last)(wait))


def _ssd_scan_fwd(xs, bm, cm, rows, nb, seq, gather=()):
    nc = seq // SSD_L
    kp = SSD_K * SSD_P
    ng = len(gather)

    def body(*refs):
        xs_ref, b_ref, c_ref, row_ref = refs[:4]
        y_ref, st_ref = refs[4 + ng:6 + ng]
        start, wait = _carried_exchange(gather, refs[4:4 + ng], refs[6 + ng:6 + 2 * ng], refs[6 + 2 * ng:], False,
                                        (nb, SSD_G))
        start()

        def chunk(c, states):
            sl = pl.ds(pl.multiple_of(c * SSD_L, SSD_L), SSD_L)
            bc, cc = b_ref[sl, :], c_ref[sl, :]
            g = lax.dot_general(cc.astype(BF16), bc.astype(BF16), _NT, preferred_element_type=F32)
            rows_c = row_ref[0, :, sl]
            cols_c = rows_c.T
            new = []
            for k in range(SSD_K):
                hs = pl.ds(k * SSD_P, SSD_P)
                st_ref[0, c * SSD_K + k] = states[k]
                y, s_new = _ssd_head(xs_ref[sl, hs], g, bc, cc, states[k], *_ssd_head_args(rows_c, cols_c, k))
                y_ref[sl, hs] = y
                new.append(s_new)
            return tuple(new)

        lax.fori_loop(0, nc, chunk, tuple(jnp.zeros((SSD_P, SSD_N), F32) for _ in range(SSD_K)))
        wait()

    t = xs.shape[0]
    any_spec = pl.BlockSpec(memory_space=pl.ANY)
    return pl.pallas_call(
        body, name="ssd_scan_fwd", grid=(nb, SSD_G),
        in_specs=[pl.BlockSpec((seq, kp), lambda b, g: (b, g)),
                  pl.BlockSpec((seq, SSD_N), lambda b, g: (b, g)),
                  pl.BlockSpec((seq, SSD_N), lambda b, g: (b, g)),
                  pl.BlockSpec((1, SSD_ROWS, seq), lambda b, g: (b * SSD_G + g, 0, 0))] + [any_spec] * ng,
        out_specs=[pl.BlockSpec((seq, kp), lambda b, g: (b, g)),
                   pl.BlockSpec((1, nc * SSD_K, SSD_P, SSD_N), lambda b, g: (b * SSD_G + g, 0, 0, 0))] + [any_spec] * ng,
        out_shape=[jax.ShapeDtypeStruct((t, SSD_D_INNER), F32),
                   jax.ShapeDtypeStruct((nb * SSD_G, nc * SSD_K, SSD_P, SSD_N), F32)] + _exchange_shapes(gather, False),
        scratch_shapes=_exchange_sems(ng) if ng else [],
        compiler_params=_cparams(dimension_semantics=("arbitrary", "arbitrary")),
    )(xs, bm, cm, rows, *gather)


def _ssd_scan_bwd(xs, bm, cm, rows, states, dy, dxs_skip, nb, seq, scatter=()):
    nc = seq // SSD_L
    kp = SSD_K * SSD_P
    ns = len(scatter)

    def body(*refs):
        xs_ref, b_ref, c_ref, row_ref, st_ref, dy_ref, skip_ref = refs[:7]
        dxs_ref, db_ref, dc_ref, drow_ref = refs[7 + ns:11 + ns]
        dcol_ref = refs[11 + 2 * ns]
        start, wait = _carried_exchange(scatter, refs[7:7 + ns], refs[11 + ns:11 + 2 * ns], refs[12 + 2 * ns:], True,
                                        (nb, SSD_G))
        start()

        def chunk(i, dstates):
            c = nc - 1 - i
            sl = pl.ds(pl.multiple_of(c * SSD_L, SSD_L), SSD_L)
            bc, cc = b_ref[sl, :], c_ref[sl, :]
            bcb, ccb = bc.astype(BF16), cc.astype(BF16)
            g = lax.dot_general(ccb, bcb, _NT, preferred_element_type=F32)
            rows_c = row_ref[0, :, sl]
            cols_c = rows_c.T
            drow_ref[0, :, sl] = jnp.zeros((SSD_ROWS, SSD_L), F32)
            dcol_ref[...] = jnp.zeros((SSD_L, SSD_ROWS), F32)
            db = jnp.zeros((SSD_L, SSD_N), F32)
            dc = jnp.zeros((SSD_L, SSD_N), F32)
            dg = jnp.zeros((SSD_L, SSD_L), F32)
            new = []
            for k in range(SSD_K):
                hs = pl.ds(k * SSD_P, SSD_P)
                _, vjp = jax.vjp(_ssd_head, xs_ref[sl, hs], g, bc, cc, st_ref[0, c * SSD_K + k],
                                 *_ssd_head_args(rows_c, cols_c, k))
                dx, dgk, dbk, dck, ds, d_cum_row, d_dt_row, d_cum_col, d_ecum_col, d_wend_col = vjp(
                    (dy_ref[sl, hs], dstates[k]))
                dxs_ref[sl, hs] = dx + skip_ref[sl, hs]
                db, dc, dg = db + dbk, dc + dck, dg + dgk
                drow_ref[0, _CUM + k:_CUM + k + 1, sl] = d_cum_row
                drow_ref[0, _DT + k:_DT + k + 1, sl] = d_dt_row
                dcol_ref[:, _CUM + k:_CUM + k + 1] = d_cum_col
                dcol_ref[:, _ECUM + k:_ECUM + k + 1] = d_ecum_col
                dcol_ref[:, _WEND + k:_WEND + k + 1] = d_wend_col
                new.append(ds)
            drow_ref[0, :, sl] += dcol_ref[...].T
            dgb = dg.astype(BF16)
            dc_ref[sl, :] = dc + jnp.dot(dgb, bcb, preferred_element_type=F32)
            db_ref[sl, :] = db + lax.dot_general(dgb, ccb, _TN, preferred_element_type=F32)
            return tuple(new)

        lax.fori_loop(0, nc, chunk, tuple(jnp.zeros((SSD_P, SSD_N), F32) for _ in range(SSD_K)))
        wait()

    t = xs.shape[0]
    x_spec = pl.BlockSpec((seq, kp), lambda b, g: (b, g))
    n_spec = pl.BlockSpec((seq, SSD_N), lambda b, g: (b, g))
    r_spec = pl.BlockSpec((1, SSD_ROWS, seq), lambda b, g: (b * SSD_G + g, 0, 0))
    any_spec = pl.BlockSpec(memory_space=pl.ANY)
    return pl.pallas_call(
        body, name="ssd_scan_bwd", grid=(nb, SSD_G),
        in_specs=[x_spec, n_spec, n_spec, r_spec,
                  pl.BlockSpec((1, nc * SSD_K, SSD_P, SSD_N), lambda b, g: (b * SSD_G + g, 0, 0, 0)), x_spec, x_spec]
        + [any_spec] * ns,
        out_specs=[x_spec, n_spec, n_spec, r_spec] + [any_spec] * ns,
        out_shape=[jax.ShapeDtypeStruct((t, SSD_D_INNER), F32), jax.ShapeDtypeStruct((t, SSD_GN), F32),
                   jax.ShapeDtypeStruct((t, SSD_GN), F32), jax.ShapeDtypeStruct(rows.shape, F32)]
        + _exchange_shapes(scatter, True),
        scratch_shapes=[pltpu.VMEM((SSD_L, SSD_ROWS), F32)] + (_exchange_sems(ns) if ns else []),
        compiler_params=_cparams(dimension_semantics=("arbitrary", "arbitrary")),
    )(xs, bm, cm, rows, states, dy, dxs_skip, *scatter)


ATT_TQ = 256
ATT_TE = 256
ATT_SCALE = (MLA_NOPE + MLA_ROPE) ** -0.5


def _for_key_extent(qi, seq, fn):
    te = min(ATT_TE, seq)
    per = te // ATT_TQ
    for e in range(seq // te):
        pl.when(jnp.logical_and(qi >= e * per, qi < (e + 1) * per))(functools.partial(fn, (e + 1) * te))


def _pair_masks(shape):
    lane = lax.broadcasted_iota(jnp.int32, shape, 1)
    return lane < MLA_ROPE, lane >= MLA_ROPE


def _scores(qn, qr, kn, kr, q0):
    s = lax.dot_general(qn, kn, _NT, preferred_element_type=F32) + lax.dot_general(qr, kr, _NT, preferred_element_type=F32)
    row = lax.broadcasted_iota(jnp.int32, s.shape, 0)
    col = lax.broadcasted_iota(jnp.int32, s.shape, 1)
    return jnp.where(col <= q0 + row, s * ATT_SCALE, -jnp.inf)


def _attn_specs(nb, seq):
    nq = seq // ATT_TQ
    qn = pl.BlockSpec((ATT_TQ, 2 * MLA_NOPE), lambda b, hp, qi: (b * nq + qi, hp))
    qr = pl.BlockSpec((ATT_TQ, LANE), lambda b, hp, qi: (b * nq + qi, hp))
    kn = pl.BlockSpec((seq, 2 * MLA_NOPE), lambda b, hp, qi: (b, hp))
    kr = pl.BlockSpec((seq, LANE), lambda b, hp, qi: (b, 0))
    return (nb, MLA_H // 2, nq), qn, qr, kn, kr


def _attn_fwd(qn, qr, kn, kr, v, nb, seq, gather=()):
    grid, s_qn, s_qr, s_kn, s_kr = _attn_specs(nb, seq)
    ng = len(gather)

    def body(*refs):
        qn_ref, qr_ref, kn_ref, kr_ref, v_ref = refs[:5]
        o_ref, lse_ref = refs[5 + ng:7 + ng]
        start, wait = _carried_exchange(gather, refs[5:5 + ng], refs[7 + ng:7 + 2 * ng], refs[7 + 2 * ng:], False, grid)
        start()
        qi = pl.program_id(2)

        def attend(ext):
            qr = qr_ref[...]
            masks = _pair_masks(qr.shape)
            outs, lses = [], []
            for j in range(2):
                hs = pl.ds(j * MLA_NOPE, MLA_NOPE)
                qr_j = jnp.where(masks[j], qr, jnp.zeros_like(qr))
                s = _scores(qn_ref[:, hs], qr_j, kn_ref[0:ext, hs], kr_ref[0:ext, :], qi * ATT_TQ)
                m = jnp.max(s, axis=1, keepdims=True)
                p = jnp.exp(s - m)
                l = jnp.sum(p, axis=1, keepdims=True)
                outs.append(jnp.dot(p.astype(BF16), v_ref[0:ext, hs], preferred_element_type=F32) / l)
                lses.append(m + jnp.log(l))
            o_ref[...] = jnp.concatenate(outs, axis=1).astype(o_ref.dtype)
            lse_ref[...] = jnp.where(masks[0], lses[0], lses[1])

        _for_key_extent(qi, seq, attend)
        wait()

    any_spec = pl.BlockSpec(memory_space=pl.ANY)
    return pl.pallas_call(
        body, name="attn_fwd", grid=grid, in_specs=[s_qn, s_qr, s_kn, s_kr, s_kn] + [any_spec] * ng,
        out_specs=[s_qn, s_qr] + [any_spec] * ng,
        out_shape=[jax.ShapeDtypeStruct(qn.shape, BF16), jax.ShapeDtypeStruct(qr.shape, F32)] + _exchange_shapes(gather, False),
        scratch_shapes=_exchange_sems(ng) if ng else [],
        compiler_params=_cparams(dimension_semantics=("arbitrary", "arbitrary", "arbitrary")),
    )(qn, qr, kn, kr, v, *gather)


def _attn_bwd(qn, qr, kn, kr, v, o, lse, do, nb, seq):
    grid, s_qn, s_qr, s_kn, s_kr = _attn_specs(nb, seq)

    def body(qn_ref, qr_ref, kn_ref, kr_ref, v_ref, o_ref, lse_ref, do_ref, dqn_ref, dqr_ref, dkn_ref, dkr_ref, dv_ref):
        hp, qi = pl.program_id(1), pl.program_id(2)

        @pl.when(qi == 0)
        def _():
            dkn_ref[...] = jnp.zeros_like(dkn_ref)
            dv_ref[...] = jnp.zeros_like(dv_ref)

        @pl.when(jnp.logical_and(qi == 0, hp == 0))
        def _():
            dkr_ref[...] = jnp.zeros_like(dkr_ref)

        def attend(ext):
            qr, lse = qr_ref[...], lse_ref[...]
            masks = _pair_masks(qr.shape)
            dqr_heads = []
            for j in range(2):
                hs = pl.ds(j * MLA_NOPE, MLA_NOPE)
                qn_j, qr_j = qn_ref[:, hs], jnp.where(masks[j], qr, jnp.zeros_like(qr))
                kn_j, kr, v_j = kn_ref[0:ext, hs], kr_ref[0:ext, :], v_ref[0:ext, hs]
                do_j = do_ref[:, hs]
                dob = do_j.astype(BF16)
                delta = jnp.sum(do_j * o_ref[:, hs].astype(F32), axis=1, keepdims=True)
                p = jnp.exp(_scores(qn_j, qr_j, kn_j, kr, qi * ATT_TQ) - lse[:, j * MLA_ROPE:j * MLA_ROPE + 1])
                dp = lax.dot_general(dob, v_j, _NT, preferred_element_type=F32)
                ds = (p * (dp - delta) * ATT_SCALE).astype(BF16)
                dkn_ref[0:ext, hs] += lax.dot_general(ds, qn_j, _TN, preferred_element_type=F32)
                dkr_ref[0:ext, :] += lax.dot_general(ds, qr_j, _TN, preferred_element_type=F32)
                dv_ref[0:ext, hs] += lax.dot_general(p.astype(BF16), dob, _TN, preferred_element_type=F32)
                dqn_ref[:, hs] = jnp.dot(ds, kn_j, preferred_element_type=F32).astype(dqn_ref.dtype)
                dqr_heads.append(jnp.dot(ds, kr, preferred_element_type=F32))
            dqr_ref[...] = jnp.where(masks[0], dqr_heads[0], dqr_heads[1])

        _for_key_extent(qi, seq, attend)

    return pl.pallas_call(
        body, name="attn_bwd", grid=grid, in_specs=[s_qn, s_qr, s_kn, s_kr, s_kn, s_qn, s_qr, s_qn],
        out_specs=[s_qn, s_qr, s_kn, s_kr, s_kn],
        out_shape=[jax.ShapeDtypeStruct(qn.shape, BF16)] + [jax.ShapeDtypeStruct(a.shape, F32) for a in (qr, kn, kr, v)],
        compiler_params=_cparams(),
    )(qn, qr, kn, kr, v, o, lse, do)


def _loss_head(y, target):
    t, d = y.shape
    tr = _rows(t, 512)

    def body(y_ref, t_ref, l_ref, dy_ref):
        err = y_ref[...] - t_ref[...]
        dy_ref[...] = err * (1.0 / d)
        part = 0.5 * jnp.sum(jnp.sum(err * err, axis=1, keepdims=True), axis=0, keepdims=True) * (1.0 / d)
        l_ref[...] = jnp.broadcast_to(part, l_ref.shape)

    row = pl.BlockSpec((tr, d), lambda i: (i, 0))
    parts, dy = pl.pallas_call(
        body, name="loss_head", grid=(t // tr,), in_specs=[row, row],
        out_specs=[pl.BlockSpec((8, LANE), lambda i: (i, 0)), row],
        out_shape=[jax.ShapeDtypeStruct((8 * (t // tr), LANE), F32), jax.ShapeDtypeStruct((t, d), F32)],
        compiler_params=_cparams(),
    )(y, target)
    return parts, dy


def _sum_parts(stack, name):
    n, r, c = stack.shape
    tr = _rows(r, 512)

    def body(s_ref, o_ref):
        acc = s_ref[0].astype(F32)
        for i in range(1, n):
            acc = acc + s_ref[i].astype(F32)
        o_ref[...] = acc

    return pl.pallas_call(
        body, name=name, grid=(r // tr,), in_specs=[pl.BlockSpec((n, tr, c), lambda i: (0, i, 0))],
        out_specs=pl.BlockSpec((tr, c), lambda i: (i, 0)), out_shape=jax.ShapeDtypeStruct((r, c), F32),
        compiler_params=_cparams(),
    )(stack)


def _adamw(w, g_mine, g_other, m, v, name):
    r, c = w.shape
    tr = _rows(r, 256)
    bc1 = 1.0 / (1.0 - ADAM_B1 ** ADAM_STEP)
    bc2 = 1.0 / (1.0 - ADAM_B2 ** ADAM_STEP)
    two = g_other is not None

    def body(*refs):
        if two:
            w_ref, g_ref, g2_ref, m_ref, v_ref, go_ref, d_ref, mo_ref, vo_ref = refs
            g = g_ref[...] + g2_ref[...]
        else:
            w_ref, g_ref, m_ref, v_ref, go_ref, d_ref, mo_ref, vo_ref = refs
            g = g_ref[...]
        mn = ADAM_B1 * m_ref[...] + (1.0 - ADAM_B1) * g
        vn = ADAM_B2 * v_ref[...] + (1.0 - ADAM_B2) * (g * g)
        go_ref[...] = g
        mo_ref[...] = mn
        vo_ref[...] = vn
        d_ref[...] = -ADAM_LR * ((mn * bc1) / (jnp.sqrt(vn * bc2) + ADAM_EPS) + ADAM_WD * w_ref[...])

    blk = pl.BlockSpec((tr, c), lambda i: (i, 0))
    ins = [w, g_mine] + ([g_other] if two else []) + [m, v]
    return pl.pallas_call(
        body, name=name, grid=(r // tr,), in_specs=[blk] * len(ins), out_specs=[blk] * 4,
        out_shape=[jax.ShapeDtypeStruct((r, c), F32)] * 4, compiler_params=_cparams(),
    )(*ins)


def _chip_peers():
    x, y, c = lax.axis_index("x"), lax.axis_index("y"), lax.axis_index("c")
    return (x, y, c), [(1 - x, y), (x, 1 - y), (1 - x, 1 - y)]


def _chip_exchange(ins, outs, sems, scatter, once=False):
    send_sems, recv_sems, local_sems = sems
    (x, y, c), chips = _chip_peers()
    me = 2 * x + y
    n = len(ins)

    def halved(i):
        tile = 32 // ins[i].dtype.itemsize
        return once and not scatter and ins[i].shape[0] % (2 * tile) == 0

    def half(i, core):
        h = ins[i].shape[0] // 2
        return pl.ds(pl.multiple_of(core * h, 8), h)

    def src(i, chip):
        if scatter:
            return ins[i].at[chip]
        return ins[i].at[half(i, c)] if halved(i) else ins[i]

    def slot(i, chip, core):
        return outs[i].at[chip, half(i, core)] if halved(i) else outs[i].at[chip]

    def local(i):
        return pltpu.make_async_copy(ins[i].at[me] if scatter else ins[i], outs[i].at[me], local_sems.at[i])

    def remote(i, j, piece, chip):
        px, py = chips[j]
        return pltpu.make_async_remote_copy(src_ref=src(i, piece), dst_ref=slot(i, chip, c), send_sem=send_sems.at[i, j],
                                            recv_sem=recv_sems.at[i, j], device_id=(px, py, c), device_id_type=MESH)

    def passed_on(i, j, core):
        px, py = chips[j]
        ref = slot(i, 2 * px + py, core)
        return pltpu.make_async_remote_copy(src_ref=ref, dst_ref=ref, send_sem=send_sems.at[i, 3 + j],
                                            recv_sem=recv_sems.at[i, 3 + j], device_id=(x, y, 1 - c), device_id_type=MESH)

    def start():
        for i in range(n):
            local(i).start()
            for j, (px, py) in enumerate(chips):
                remote(i, j, 2 * px + py, me).start()

    def wait():
        for i in range(n):
            for j, (px, py) in enumerate(chips):
                remote(i, j, me, 2 * px + py).wait_recv()
                if halved(i):
                    passed_on(i, j, c).start()
        for i in range(n):
            for j, (px, py) in enumerate(chips):
                if halved(i):
                    passed_on(i, j, 1 - c).wait_recv()
                    passed_on(i, j, c).wait_send()
                remote(i, j, 2 * px + py, me).wait_send()
            local(i).wait()

    return start, wait


def _exchange_sems(n):
    return [pltpu.SemaphoreType.DMA((n, 6)), pltpu.SemaphoreType.DMA((n, 6)), pltpu.SemaphoreType.DMA((n,))]


def _exchange_shapes(arrs, scatter):
    return [jax.ShapeDtypeStruct(s.shape if scatter else (N_CHIPS,) + s.shape, s.dtype) for s in arrs]


def _exchange_call(name, arrs, scatter):
    n = len(arrs)

    def body(*refs):
        start, wait = _chip_exchange(refs[:n], refs[n:2 * n], refs[2 * n:], scatter, once=True)
        start()
        wait()

    any_spec = pl.BlockSpec(memory_space=pl.ANY)
    return pl.pallas_call(body, name=name, in_specs=[any_spec] * n, out_specs=[any_spec] * n,
                          out_shape=_exchange_shapes(arrs, scatter), scratch_shapes=_exchange_sems(n))(*arrs)


def _gather_chips(shards, name="gather_chips"):
    return _exchange_call(name, shards, False)


def _scatter_chips(stacks, name="scatter_chips"):
    return _exchange_call(name, stacks, True)


def _swap_cores(arrs):
    n = len(arrs)

    def body(*refs):
        ins, outs = refs[:n], refs[n:2 * n]
        send_sems, recv_sems = refs[2 * n:]
        x, y, c = lax.axis_index("x"), lax.axis_index("y"), lax.axis_index("c")
        cps = []
        for i in range(n):
            cp = pltpu.make_async_remote_copy(src_ref=ins[i], dst_ref=outs[i], send_sem=send_sems.at[i],
                                              recv_sem=recv_sems.at[i], device_id=(x, y, 1 - c), device_id_type=MESH)
            cp.start()
            cps.append(cp)
        for cp in cps:
            cp.wait()

    any_spec = pl.BlockSpec(memory_space=pl.ANY)
    return pl.pallas_call(
        body, name="swap_cores", in_specs=[any_spec] * n, out_specs=[any_spec] * n,
        out_shape=[jax.ShapeDtypeStruct(s.shape, s.dtype) for s in arrs],
        scratch_shapes=[pltpu.SemaphoreType.DMA((n,)), pltpu.SemaphoreType.DMA((n,))],
    )(*arrs)


def _gather_all(block):
    m_per, n = block.shape

    def body(x_ref, out_ref, send_sems, recv_sems, local_sem):
        x, y, c = lax.axis_index("x"), lax.axis_index("y"), lax.axis_index("c")
        me, sibling = (x, y, c), (x, y, 1 - c)
        chips = [(1 - x, y), (x, 1 - y), (1 - x, 1 - y)]

        def rows(px, py, pc):
            return out_ref.at[pl.ds((4 * px + 2 * py + pc) * m_per, m_per), :]

        def copy(k, blk, to, src=None):
            return pltpu.make_async_remote_copy(src_ref=rows(*blk) if src is None else src, dst_ref=rows(*blk),
                                                send_sem=send_sems.at[k], recv_sem=recv_sems.at[k], device_id=to,
                                                device_id_type=MESH)

        mine = pltpu.make_async_copy(x_ref, rows(*me), local_sem)
        mine.start()
        first = [copy(0, me, sibling, src=x_ref)]
        first += [copy(1 + j, me, (*chip, c), src=x_ref) for j, chip in enumerate(chips)]
        for cp in first:
            cp.start()
        passed = [copy(4 + j, (*chip, c), sibling) for j, chip in enumerate(chips)]
        for j, chip in enumerate(chips):
            copy(1 + j, (*chip, c), me).wait_recv()
            passed[j].start()
        copy(0, sibling, me).wait_recv()
        for j, chip in enumerate(chips):
            copy(4 + j, (*chip, 1 - c), me).wait_recv()
        for cp in first + passed:
            cp.wait_send()
        mine.wait()

    return pl.pallas_call(
        body, name="gather_all", out_shape=jax.ShapeDtypeStruct((N_DEV * m_per, n), block.dtype),
        in_specs=[pl.BlockSpec(memory_space=pltpu.VMEM)], out_specs=pl.BlockSpec(memory_space=pltpu.VMEM),
        scratch_shapes=[pltpu.SemaphoreType.DMA((7,)), pltpu.SemaphoreType.DMA((7,)), pltpu.SemaphoreType.DMA],
    )(block)


def _rope_tables(positions):
    inv_freq = 1.0 / (ROPE_THETA ** (jnp.arange(0, MLA_ROPE, 2, dtype=F32) / MLA_ROPE))
    ang = positions.astype(F32).reshape(-1, 1) * inv_freq
    return jnp.tile(jnp.cos(ang), (1, 4)), jnp.tile(jnp.sin(ang), (1, 4))


def _pad_cols(a, n):
    return jnp.pad(a, ((0, 0), (0, n - a.shape[1])))


def _ffn_fwd(l, h, w, nb, seq):
    ug = _mm(h, w["ffn_up_g"][l], name=f"ffn{l}_up_g")
    uv = _mm(h, w["ffn_up_v"][l], name=f"ffn{l}_up_v")
    op = _op_conv(f"ffn{l}_conv", _f_convffn, [ug, uv], [w["ffn_cw_g"][l], w["ffn_cw_v"][l]],
                  [w["ffn_cb_g"][l], w["ffn_cb_v"][l]], nb, seq)
    act = op.fwd((BF16,))[0]
    ff = _mm(act, w["ffn_down"][l], name=f"ffn{l}_down")
    return ff, (h, op, act)


def _ffn_bwd(l, saved, dff, w, grads, dh_skip):
    h, op, act = saved
    grads[f"ffn_down{l}"] = _mm(act, dff, ta=True, name=f"ffn{l}_down_dw")
    dact = _mm(dff, w["ffn_down"][l], tb=True, name=f"ffn{l}_down_dx")
    (dug, duv), (dwg, dwv, dbg, dbv) = op.bwd([dact], [BF16, BF16])
    grads[f"ffn_up_g{l}"] = _mm(h, dug, ta=True, name=f"ffn{l}_up_g_dw")
    grads[f"ffn_up_v{l}"] = _mm(h, duv, ta=True, name=f"ffn{l}_up_v_dw")
    grads[f"ffn_cw_g{l}"], grads[f"ffn_cw_v{l}"], grads[f"ffn_cb_g{l}"], grads[f"ffn_cb_v{l}"] = dwg, dwv, dbg, dbv
    dh = _mm(dug, w["ffn_up_g"][l], tb=True, add=dh_skip, name=f"ffn{l}_up_g_dx")
    return _mm(duv, w["ffn_up_v"][l], tb=True, add=dh, name=f"ffn{l}_up_v_dx")


class _Carried:
    def __init__(self, shards, finish_weights, ffn_shards, set_ffn, grad_stacks):
        self.shards, self.finish_weights, self.ffn_shards, self.set_ffn = shards, finish_weights, ffn_shards, set_ffn
        self.grad_stacks, self.received = grad_stacks, None


def _local_step(x, positions, target, w, carried=None):
    nb, seq, d = x.shape
    t = nb * seq
    x2, tgt2 = x.reshape(t, d), target.reshape(t, d)
    cos, sin = _rope_tables(positions)
    grads = {}

    xb = x2.astype(BF16)

    z = _mm(xb, w["in_z"], name="ssd_in_z")
    raw = [_mm(xb, w[k], name="ssd_" + k) for k in ("in_x", "in_b", "in_c")]
    dt_raw = _mm(xb, w["in_dt"], name="ssd_in_dt")
    conv_ops = [_op_conv("ssd_conv_" + s, _f_convsilu, [r], [w["conv_w_" + s]], [w["conv_b_" + s]], nb, seq)
                for s, r in zip("xbc", raw)]
    xs, bm, cm = [op.fwd()[0] for op in conv_ops]
    dt_op = _op_dt("ssd_dt", dt_raw, w["dt_bias"], w["a_log"])
    dt, cum, ecum, wend = dt_op.fwd()

    def to_rows(v):
        v = jnp.swapaxes(v.reshape(nb, seq, LANE), 1, 2)[:, :SSD_H, :].reshape(nb * SSD_G, SSD_K, seq)
        return jnp.pad(v, ((0, 0), (0, SSD_KPAD - SSD_K), (0, 0)))

    def from_rows(v):
        v = v[:, :SSD_K, :].reshape(nb, SSD_H, seq)
        return jnp.swapaxes(jnp.pad(v, ((0, 0), (0, LANE - SSD_H), (0, 0))), 1, 2).reshape(t, LANE)

    ssd_rows = jnp.concatenate([to_rows(cum), to_rows(dt), to_rows(ecum), to_rows(wend)], axis=1)
    y, states, *gathered = _ssd_scan_fwd(xs, bm, cm, ssd_rows, nb, seq, carried.shards if carried else ())
    if carried:
        w = {**w, **carried.finish_weights(gathered)}
    post_op = _op_ssd_post("ssd_post", y, xs, z, w["d_exp"], w["norm_g"])
    yn = post_op.fwd((BF16,))[0]
    mix0 = _mm(yn, w["ssd_out"], name="ssd_out")
    ln0m = _op_ln("ln_mix0", x2, mix0, w["ln_mix_g"][0], w["ln_mix_b"][0])
    h0a, h0a_b = ln0m.fwd((F32, BF16))
    ff0, ffn0_saved = _ffn_fwd(0, h0a_b, w, nb, seq)
    ln0f = _op_ln("ln_ffn0", h0a, ff0, w["ln_ffn_g"][0], w["ln_ffn_b"][0])
    h1, h1_b = ln0f.fwd((F32, BF16))

    down = _mm(h1_b, w["kvq_down"], name="kvq_down")
    kvn_op = _op_rms("kv_norm", down, w["kv_norm_g"], window=(0, MLA_KV_RANK))
    ckvn = kvn_op.fwd((BF16,))[0]
    kr_op = _op_rope("k_rope", _f_rope_dup, down, cos, sin, window=(MLA_KV_RANK // LANE, 1))
    kr = kr_op.fwd((BF16,))[0]
    kn = _mm(ckvn, w["kv_up_k"], out_dtype=BF16, name="kv_up_k")
    v = _mm(ckvn, w["kv_up_v"], out_dtype=BF16, name="kv_up_v")
    qn_op = _op_rms("q_norm", down, w["q_norm_g"], window=(1, MLA_Q_RANK))
    cq = qn_op.fwd((BF16,))[0]
    qn = _mm(cq, w["q_up_n"], out_dtype=BF16, name="q_up_n")
    qr_raw = _mm(cq, w["q_up_r"], name="q_up_r")
    qr_op = _op_rope("q_rope", _f_rope, qr_raw, cos, sin)
    qr = qr_op.fwd((BF16,))[0]
    o, lse, *gathered = _attn_fwd(qn, qr, kn, kr, v, nb, seq, carried.ffn_shards if carried else ())
    if carried:
        carried.set_ffn(w, gathered)
    mix1 = _mm(o, w["attn_out"], name="attn_out")
    ln1m = _op_ln("ln_mix1", h1, mix1, w["ln_mix_g"][1], w["ln_mix_b"][1])
    h1a, h1a_b = ln1m.fwd((F32, BF16))
    ff1, ffn1_saved = _ffn_fwd(1, h1a_b, w, nb, seq)
    ln1f = _op_ln("ln_ffn1", h1a, ff1, w["ln_ffn_g"][1], w["ln_ffn_b"][1])
    h2 = ln1f.fwd()[0]

    loss_parts, dh2 = _loss_head(h2, tgt2)

    (dh1a, dff1), (grads["ln_ffn_g1"], grads["ln_ffn_b1"]) = ln1f.bwd([dh2], [F32, BF16])
    dh1a = _ffn_bwd(1, ffn1_saved, dff1, w, grads, dh1a)
    (dh1, dmix1), (grads["ln_mix_g1"], grads["ln_mix_b1"]) = ln1m.bwd([dh1a], [F32, BF16])
    grads["attn_out"] = _mm(o, dmix1, ta=True, name="attn_out_dw")
    do = _mm(dmix1, w["attn_out"], tb=True, name="attn_out_dx")
    dqn, dqr, dkn, dkr, dv = _attn_bwd(qn, qr, kn, kr, v, o, lse, do, nb, seq)
    (dqr_raw,), _ = qr_op.bwd([dqr], [BF16])
    grads["q_up_n"] = _mm(cq, dqn, ta=True, name="q_up_n_dw")
    grads["q_up_r"] = _mm(cq, dqr_raw, ta=True, name="q_up_r_dw")
    dcq = _mm(dqn, w["q_up_n"], tb=True, name="q_up_n_dx")
    dcq = _mm(dqr_raw, w["q_up_r"], tb=True, add=dcq, name="q_up_r_dx")
    (dcq_raw,), (grads["q_norm_g"],) = qn_op.bwd([dcq], [BF16])
    grads["kv_up_k"] = _mm(ckvn, dkn, ta=True, name="kv_up_k_dw")
    grads["kv_up_v"] = _mm(ckvn, dv, ta=True, name="kv_up_v_dw")
    dckvn = _mm(dkn, w["kv_up_k"], tb=True, name="kv_up_k_dx")
    dckvn = _mm(dv, w["kv_up_v"], tb=True, add=dckvn, name="kv_up_v_dx")
    (dckv,), (grads["kv_norm_g"],) = kvn_op.bwd([dckvn], [BF16])
    (dkr_in,), _ = kr_op.bwd([dkr], [BF16])
    ddown = jnp.concatenate([dckv, dkr_in, dcq_raw], axis=1)
    grads["kvq_down"] = _mm(h1_b, ddown, ta=True, name="kvq_down_dw")
    dh1 = _mm(ddown, w["kvq_down"], tb=True, add=dh1, name="kvq_down_dx")

    (dh0a, dff0), (grads["ln_ffn_g0"], grads["ln_ffn_b0"]) = ln0f.bwd([dh1], [F32, BF16])
    dh0a = _ffn_bwd(0, ffn0_saved, dff0, w, grads, dh0a)
    (dx, dmix0), (grads["ln_mix_g0"], grads["ln_mix_b0"]) = ln0m.bwd([dh0a], [F32, BF16])
    grads["ssd_out"] = _mm(yn, dmix0, ta=True, name="ssd_out_dw")
    dyn = _mm(dmix0, w["ssd_out"], tb=True, name="ssd_out_dx")
    (dy, dxs_post, dz), (grads["d_exp"], grads["norm_g"]) = post_op.bwd([dyn], [F32, F32, BF16])
    dxs, dbm, dcm, drows, *received = _ssd_scan_bwd(xs, bm, cm, ssd_rows, states, dy, dxs_post, nb, seq,
                                                    carried.grad_stacks(grads) if carried else ())
    if carried:
        carried.received = received
    d_dt, d_cum, d_ecum, d_wend = [from_rows(drows[:, o:o + SSD_KPAD]) for o in (_DT, _CUM, _ECUM, _WEND)]
    (ddt_raw,), (grads["dt_bias"], grads["a_log"]) = dt_op.bwd([d_dt, d_cum, d_ecum, d_wend], [BF16])
    draws = []
    for s, op, dout in zip("xbc", conv_ops, (dxs, dbm, dcm)):
        (dr,), (grads["conv_w_" + s], grads["conv_b_" + s]) = op.bwd([dout], [BF16])
        draws.append(dr)
    for k, dr in zip(("in_x", "in_b", "in_c"), draws):
        grads[k] = _mm(xb, dr, ta=True, name=f"ssd_{k}_dw")
        dx = _mm(dr, w[k], tb=True, add=dx, name=f"ssd_{k}_dx")
    grads["in_z"] = _mm(xb, dz, ta=True, name="ssd_in_z_dw")
    grads["in_dt"] = _mm(xb, ddt_raw, ta=True, name="ssd_in_dt_dw")
    dx = _mm(dz, w["in_z"], tb=True, add=dx, name="ssd_in_z_dx")
    dx = _mm(ddt_raw, w["in_dt"], tb=True, add=dx, name="ssd_in_dt_dx")
    return loss_parts, dx.reshape(nb, seq, d), grads


_XE, _BE, _CE = SSD_D_INNER, SSD_D_INNER + SSD_GN, SSD_D_INNER + 2 * SSD_GN
_KVR = MLA_KV_RANK + LANE


def _prep_weights(fw):
    w = {}
    f = FFN_HIDDEN
    if "ssd_in_proj" in fw:
        ip = fw["ssd_in_proj"][0]
        o = SSD_D_INNER
        w["in_z"], w["in_x"], w["in_b"], w["in_c"] = ip[:, :o], ip[:, o:o + _XE], ip[:, o + _XE:o + _BE], ip[:, o + _BE:o + _CE]
        w["in_dt"] = _pad_cols(ip[:, o + _CE:], LANE)
        cw, cb = fw["ssd_conv_w"][0], fw["ssd_conv_b"]
        for s, (lo, hi) in zip("xbc", ((0, _XE), (_XE, _BE), (_BE, _CE))):
            w["conv_w_" + s], w["conv_b_" + s] = cw[:, lo:hi], cb[:, lo:hi]
    if "ssd_dt_bias" in fw:
        w["dt_bias"], w["a_log"] = _pad_cols(fw["ssd_dt_bias"], LANE), _pad_cols(fw["ssd_A_log"], LANE)
        w["d_exp"] = jnp.repeat(fw["ssd_D"][0], SSD_P)[None, :]
        w["kv_norm_g"], w["q_norm_g"] = fw["kv_norm_g"][None, :], fw["q_norm_g"]
        w["ffn_cb_g"] = [fw["ffn_conv_b"][l:l + 1, :f] for l in range(DEPTH)]
        w["ffn_cb_v"] = [fw["ffn_conv_b"][l:l + 1, f:] for l in range(DEPTH)]
        for k in ("ln_mix_g", "ln_mix_b", "ln_ffn_g", "ln_ffn_b"):
            w[k] = [fw[k][l:l + 1] for l in range(DEPTH)]
    if "ssd_out_proj" in fw:
        w["norm_g"], w["ssd_out"] = fw["ssd_norm_g"], fw["ssd_out_proj"][0]
        kd = fw["kv_down_proj"]
        w["kvq_down"] = jnp.concatenate([_pad_cols(kd, _KVR), fw["q_down_proj"][0]], axis=1)
        w["kv_up_k"], w["kv_up_v"] = fw["kv_up_k"], fw["kv_up_v"]
        qu = fw["q_up_proj"][0].reshape(MLA_Q_RANK, MLA_H, MLA_NOPE + MLA_ROPE)
        w["q_up_n"] = qu[:, :, :MLA_NOPE].reshape(MLA_Q_RANK, MLA_H * MLA_NOPE)
        w["q_up_r"] = qu[:, :, MLA_NOPE:].reshape(MLA_Q_RANK, MLA_H * MLA_ROPE)
        w["attn_out"] = fw["attn_out_proj"][0]
        w["ffn_up_g"], w["ffn_up_v"], w["ffn_down"] = [None] * DEPTH, [None] * DEPTH, [None] * DEPTH
        for l in range(DEPTH):
            if fw["ffn_up"][l] is not None:
                _set_ffn_weights(w, l, fw["ffn_up"][l], fw["ffn_down"][l])
        w["ffn_cw_g"] = [fw["ffn_conv_w"][l][:, :f] for l in range(DEPTH)]
        w["ffn_cw_v"] = [fw["ffn_conv_w"][l][:, f:] for l in range(DEPTH)]
    return w


def _set_ffn_weights(w, l, up, down):
    w["ffn_up_g"][l], w["ffn_up_v"][l], w["ffn_down"][l] = up[:, :FFN_HIDDEN], up[:, FFN_HIDDEN:], down


def _assemble_grads(g, names):
    make = {
        "ssd_in_proj": lambda: jnp.concatenate([g["in_z"], g["in_x"], g["in_b"], g["in_c"], g["in_dt"][:, :SSD_H]], axis=1)[None],
        "ssd_conv_w": lambda: jnp.concatenate([g["conv_w_" + s] for s in "xbc"], axis=1)[None],
        "ssd_conv_b": lambda: jnp.concatenate([g["conv_b_" + s] for s in "xbc"], axis=1),
        "ssd_dt_bias": lambda: g["dt_bias"][:, :SSD_H],
        "ssd_A_log": lambda: g["a_log"][:, :SSD_H],
        "ssd_D": lambda: jnp.sum(g["d_exp"].reshape(SSD_H, SSD_P), axis=1)[None, :],
        "ssd_norm_g": lambda: g["norm_g"],
        "ssd_out_proj": lambda: g["ssd_out"][None],
        "kv_down_proj": lambda: g["kvq_down"][:, :MLA_KV_RANK + MLA_ROPE],
        "kv_norm_g": lambda: g["kv_norm_g"][0],
        "kv_up_k": lambda: g["kv_up_k"],
        "kv_up_v": lambda: g["kv_up_v"],
        "q_down_proj": lambda: g["kvq_down"][None, :, _KVR:],
        "q_norm_g": lambda: g["q_norm_g"],
        "q_up_proj": lambda: jnp.concatenate([g["q_up_n"].reshape(MLA_Q_RANK, MLA_H, MLA_NOPE),
                                              g["q_up_r"].reshape(MLA_Q_RANK, MLA_H, MLA_ROPE)], axis=2).reshape(1, MLA_Q_RANK, -1),
        "attn_out_proj": lambda: g["attn_out"][None],
        "ffn_up": lambda: jnp.stack([jnp.concatenate([g[f"ffn_up_g{l}"], g[f"ffn_up_v{l}"]], axis=1) for l in range(DEPTH)]),
        "ffn_conv_w": lambda: jnp.stack([jnp.concatenate([g[f"ffn_cw_g{l}"], g[f"ffn_cw_v{l}"]], axis=1) for l in range(DEPTH)]),
        "ffn_conv_b": lambda: jnp.concatenate([jnp.concatenate([g[f"ffn_cb_g{l}"], g[f"ffn_cb_v{l}"]], axis=1)
                                               for l in range(DEPTH)], axis=0),
        "ffn_down": lambda: jnp.stack([g[f"ffn_down{l}"] for l in range(DEPTH)]),
    }
    for k in ("ln_mix_g", "ln_mix_b", "ln_ffn_g", "ln_ffn_b"):
        make[k] = lambda k=k: jnp.concatenate([g[f"{k}{l}"] for l in range(DEPTH)], axis=0)
    return {n: make[n]() for n in names}


_WEIGHTS = ["ssd_in_proj", "ssd_conv_w", "ssd_conv_b", "ssd_dt_bias", "ssd_A_log", "ssd_D", "ssd_norm_g", "ssd_out_proj",
            "kv_down_proj", "kv_norm_g", "kv_up_k", "kv_up_v", "q_down_proj", "q_norm_g", "q_up_proj", "attn_out_proj",
            "ffn_up", "ffn_conv_w", "ffn_conv_b", "ffn_down", "ln_mix_g", "ln_mix_b", "ln_ffn_g", "ln_ffn_b"]
_COL_CUT = ["ssd_in_proj", "ssd_conv_w", "ssd_conv_b", "ssd_norm_g", "kv_up_k", "kv_up_v", "q_up_proj", "ffn_up", "ffn_conv_w"]
_ROW_CUT = ["ssd_out_proj", "kv_down_proj", "q_down_proj", "attn_out_proj", "ffn_down"]
_CUT = _COL_CUT + _ROW_CUT
_WHOLE = [n for n in _WEIGHTS if n not in _CUT]
_EARLY = ["ssd_in_proj", "ssd_conv_w", "ssd_conv_b"]
_LATE = [n for n in _CUT if n not in _EARLY]
_FFN_MATRICES = ["ffn_up", "ffn_down"]
_MXU_WEIGHTS = ["ssd_in_proj", "ssd_out_proj", "kv_down_proj", "kv_up_k", "kv_up_v", "q_down_proj", "q_up_proj",
                "attn_out_proj", "ffn_up", "ffn_down"]
_PACK_ROWS = 160


def _shard_2d(name, s):
    return s.reshape(-1, s.shape[-1])


def _unstack(name, g, shard_shape):
    if name in _COL_CUT:
        lead = shard_shape[:-1]
        return jnp.swapaxes(g, 0, 1).reshape(*lead, N_CHIPS * shard_shape[-1])
    lead, rs, c = shard_shape[:-2], shard_shape[-2], shard_shape[-1]
    n_lead = math.prod(lead)
    return jnp.swapaxes(g.reshape(N_CHIPS, n_lead, rs, c), 0, 1).reshape(*lead, N_CHIPS * rs, c)


def _stack(name, full, shard_shape):
    if name in _COL_CUT:
        cs = shard_shape[-1]
        return jnp.swapaxes(full.reshape(-1, N_CHIPS, cs), 0, 1)
    lead, rs, c = shard_shape[:-2], shard_shape[-2], shard_shape[-1]
    n_lead = math.prod(lead)
    return jnp.swapaxes(full.reshape(n_lead, N_CHIPS, rs, c), 0, 1).reshape(N_CHIPS, n_lead * rs, c)


def _pack(arrs):
    flat = jnp.concatenate([a.reshape(-1) for a in arrs])
    return jnp.pad(flat, (0, _PACK_ROWS * LANE - flat.shape[0])).reshape(_PACK_ROWS, LANE)


def _unpack(packed, like):
    flat, out, o = packed.reshape(-1), [], 0
    for a in like:
        out.append(flat[o:o + a.size].reshape(a.shape))
        o += a.size
    return out


_ARGS = ["x", "positions"] + _WEIGHTS + ["loss_target"] + ["m_" + n for n in _WEIGHTS] + ["v_" + n for n in _WEIGHTS]


def kernel(x, positions, ssd_in_proj, ssd_conv_w, ssd_conv_b, ssd_dt_bias, ssd_A_log, ssd_D, ssd_norm_g,
           ssd_out_proj, kv_down_proj, kv_norm_g, kv_up_k, kv_up_v, q_down_proj, q_norm_g, q_up_proj,
           attn_out_proj, ffn_up, ffn_conv_w, ffn_conv_b, ffn_down, ln_mix_g, ln_mix_b, ln_ffn_g, ln_ffn_b,
           loss_target, m_ssd_in_proj, m_ssd_conv_w, m_ssd_conv_b, m_ssd_dt_bias, m_ssd_A_log, m_ssd_D,
           m_ssd_norm_g, m_ssd_out_proj, m_kv_down_proj, m_kv_norm_g, m_kv_up_k, m_kv_up_v, m_q_down_proj,
           m_q_norm_g, m_q_up_proj, m_attn_out_proj, m_ffn_up, m_ffn_conv_w, m_ffn_conv_b, m_ffn_down,
           m_ln_mix_g, m_ln_mix_b, m_ln_ffn_g, m_ln_ffn_b, v_ssd_in_proj, v_ssd_conv_w, v_ssd_conv_b,
           v_ssd_dt_bias, v_ssd_A_log, v_ssd_D, v_ssd_norm_g, v_ssd_out_proj, v_kv_down_proj, v_kv_norm_g,
           v_kv_up_k, v_kv_up_v, v_q_down_proj, v_q_norm_g, v_q_up_proj, v_attn_out_proj, v_ffn_up,
           v_ffn_conv_w, v_ffn_conv_b, v_ffn_down, v_ln_mix_g, v_ln_mix_b, v_ln_ffn_g, v_ln_ffn_b):
    args = (x, positions, ssd_in_proj, ssd_conv_w, ssd_conv_b, ssd_dt_bias, ssd_A_log, ssd_D, ssd_norm_g,
            ssd_out_proj, kv_down_proj, kv_norm_g, kv_up_k, kv_up_v, q_down_proj, q_norm_g, q_up_proj,
            attn_out_proj, ffn_up, ffn_conv_w, ffn_conv_b, ffn_down, ln_mix_g, ln_mix_b, ln_ffn_g, ln_ffn_b,
            loss_target, m_ssd_in_proj, m_ssd_conv_w, m_ssd_conv_b, m_ssd_dt_bias, m_ssd_A_log, m_ssd_D,
            m_ssd_norm_g, m_ssd_out_proj, m_kv_down_proj, m_kv_norm_g, m_kv_up_k, m_kv_up_v, m_q_down_proj,
            m_q_norm_g, m_q_up_proj, m_attn_out_proj, m_ffn_up, m_ffn_conv_w, m_ffn_conv_b, m_ffn_down,
            m_ln_mix_g, m_ln_mix_b, m_ln_ffn_g, m_ln_ffn_b, v_ssd_in_proj, v_ssd_conv_w, v_ssd_conv_b,
            v_ssd_dt_bias, v_ssd_A_log, v_ssd_D, v_ssd_norm_g, v_ssd_out_proj, v_kv_down_proj, v_kv_norm_g,
            v_kv_up_k, v_kv_up_v, v_q_down_proj, v_q_norm_g, v_q_up_proj, v_attn_out_proj, v_ffn_up,
            v_ffn_conv_w, v_ffn_conv_b, v_ffn_down, v_ln_mix_g, v_ln_mix_b, v_ln_ffn_g, v_ln_ffn_b)
    a = dict(zip(_ARGS, args, strict=True))

    last = DEPTH - 1

    def layers(n, late):
        if n not in _FFN_MATRICES:
            return a[n]
        return a[n][last:] if late else a[n][:last]

    def shard(n, late=False):
        s = _shard_2d(n, layers(n, late))
        return s.astype(BF16) if n in _MXU_WEIGHTS else s

    def full_weights(names, gathered, late=False):
        fw = {n: _unstack(n, g, layers(n, late).shape) for n, g in zip(names, gathered)}
        for n in _FFN_MATRICES:
            if n in fw:
                fw[n] = [None] * last + list(fw[n]) if late else list(fw[n]) + [None]
        return fw

    def set_last_ffn(w, gathered):
        fw = full_weights(_FFN_MATRICES, gathered, late=True)
        _set_ffn_weights(w, last, fw["ffn_up"][last], fw["ffn_down"][last])

    def grad_stacks(names, pieces):
        full = _assemble_grads(pieces, names)
        return [_stack(n, full[n], a[n].shape).astype(BF16) for n in names]

    fw = full_weights(_EARLY, _gather_chips([shard(n) for n in _EARLY]))
    fw.update({n: a[n] for n in _WHOLE})
    carried = _Carried([shard(n) for n in _LATE], lambda got: _prep_weights(full_weights(_LATE, got)),
                       [shard(n, late=True) for n in _FFN_MATRICES], set_last_ffn,
                       lambda pieces: grad_stacks(_LATE, pieces))

    loss_parts, grad_x, pieces = _local_step(a["x"], a["positions"], a["loss_target"], _prep_weights(fw), carried)
    loss = lax.psum(jnp.sum(loss_parts[::8, 0]), ("x", "y", "c"))

    bufs = dict(zip(_LATE, carried.received))
    bufs.update(zip(_EARLY, _scatter_chips(grad_stacks(_EARLY, pieces))))
    sums = [_sum_parts(bufs[n], "sum_chips_" + n) for n in _CUT]
    others = _swap_cores(sums)
    full = _assemble_grads(pieces, _WHOLE)
    every = _gather_all(_pack([full[n] for n in _WHOLE]))
    g_whole = _sum_parts(every.reshape(N_DEV, _PACK_ROWS, LANE), "sum_devices")

    res = {}
    for n, s, o in zip(_CUT, sums, others):
        out = _adamw(_shard_2d(n, a[n]), s, o, _shard_2d(n, a["m_" + n]), _shard_2d(n, a["v_" + n]), "adamw_" + n)
        res[n] = [r.reshape(a[n].shape) for r in out]
    whole = [a[n] for n in _WHOLE]
    out = _adamw(_pack(whole), g_whole, None, _pack([a["m_" + n] for n in _WHOLE]), _pack([a["v_" + n] for n in _WHOLE]),
                 "adamw_whole")
    for k, n in enumerate(_WHOLE):
        res[n] = [_unpack(r, whole)[k] for r in out]
    return (loss, grad_x, *[res[n][0] for n in _WEIGHTS], *[res[n][1] for n in _WEIGHTS],
            *[res[n][2] for n in _WEIGHTS], *[res[n][3] for n in _WEIGHTS])
```
